```python
import jax, jax.numpy as jnp
from jax import lax
import numpy as np

D_MODEL = 1024
BATCH = 32
SEQ = 2048
DEPTH = 1

CHUNK = 64
Q_BLOCK = 128
HEAD_DIM = 64
N_HEADS_FOX = 8
N_HEADS_DSA = 8
IDX_HEADS = 8
IDX_DIM = 64
TOPK_MAX = 256
ROPE_THETA = 10000.0
N_GROUPS = 4
EXPERTS_PER_GROUP = 8
N_EXPERTS = N_GROUPS * EXPERTS_PER_GROUP
TOP_K_INNER = 2
D_EXPERT = 512
EPS = 1e-6

W_FOX = N_HEADS_FOX * HEAD_DIM
W_DSA = N_HEADS_DSA * HEAD_DIM
IN_SPLITS = (W_FOX, W_FOX, W_FOX, N_HEADS_FOX,
             W_DSA, HEAD_DIM, HEAD_DIM,
             IDX_HEADS * IDX_DIM, IDX_DIM, IDX_HEADS,
             D_MODEL, D_MODEL)
IN_COLS = 3 * W_FOX + N_HEADS_FOX + W_DSA + 2 * HEAD_DIM + IDX_HEADS * IDX_DIM + IDX_DIM + IDX_HEADS + 2 * D_MODEL

kernel_name = 'streaming_hybrid_fox_dsa_hmoe_block'


def rmsnorm(x, g):
    xf = x.astype(jnp.float32)
    y = xf * lax.rsqrt(jnp.mean(xf * xf, axis=-1, keepdims=True) + EPS)
    return (y * g.astype(jnp.float32)).astype(x.dtype)


def rope(x, pos):
    half = x.shape[-1] // 2
    inv = ROPE_THETA ** (-jnp.arange(half, dtype=jnp.float32) / half)
    ang = pos.astype(jnp.float32)[..., None] * inv
    cos = jnp.cos(ang)[:, :, None, :]
    sin = jnp.sin(ang)[:, :, None, :]
    x1 = x[..., :half].astype(jnp.float32)
    x2 = x[..., half:].astype(jnp.float32)
    out = jnp.concatenate([x1 * cos - x2 * sin, x2 * cos + x1 * sin], axis=-1)
    return out.astype(x.dtype)


def forgetting_attention(q, k, v, log_f):
    B, S, H, D = q.shape
    Ft = jnp.cumsum(log_f, axis=1).transpose(0, 2, 1)
    scale = D ** -0.5
    outs = []
    for start in range(0, S, Q_BLOCK):
        end = start + Q_BLOCK
        s = jnp.einsum('bqhd,bkhd->bhqk', q[:, start:end], k[:, :end],
                       preferred_element_type=jnp.float32) * scale
        s = s + Ft[:, :, start:end, None] - Ft[:, :, None, :end]
        tq = jnp.arange(start, end)[:, None]
        tk = jnp.arange(end)[None, :]
        s = jnp.where(tk <= tq, s, -jnp.inf)
        p = jax.nn.softmax(s, axis=-1)
        outs.append(jnp.einsum('bhqk,bkhd->bqhd', p.astype(v.dtype), v[:, :end]))
    return jnp.concatenate(outs, axis=1)


def dsa_attention(q, k, v, qi, ki, wi):
    B, S, H, D = q.shape
    topk = min(TOPK_MAX, S // 4)
    chunk_id = jnp.arange(S) // CHUNK
    scale = D ** -0.5
    gather = jax.vmap(lambda arr, ids: arr[ids])
    outs = []
    for start in range(0, S, Q_BLOCK):
        end = start + Q_BLOCK
        kk = min(topk, end)
        dots = jnp.einsum('bqhd,bkd->bqhk', qi[:, start:end], ki[:, :end],
                          preferred_element_type=jnp.float32)
        score = jnp.einsum('bqh,bqhk->bqk', wi[:, start:end].astype(jnp.float32), jax.nn.relu(dots))
        cq = chunk_id[start:end]
        allowed = chunk_id[:end][None, :] <= cq[:, None]
        score = jnp.where(allowed[None], score, -jnp.inf)
        _, idx = lax.top_k(score, kk)
        valid = chunk_id[idx] <= cq[None, :, None]
        kg = gather(k[:, :end], idx)
        vg = gather(v[:, :end], idx)
        s = jnp.einsum('bqhd,bqkd->bhqk', q[:, start:end], kg,
                       preferred_element_type=jnp.float32) * scale
        s = jnp.where(valid[:, None], s, -jnp.inf)
        p = jax.nn.softmax(s, axis=-1)
        outs.append(jnp.einsum('bhqk,bqkd->bqhd', p.astype(vg.dtype), vg))
    return jnp.concatenate(outs, axis=1)


def hierarchical_moe(h, w_grp, b_grp, w_exp, b_exp, w1, w3, w2):
    N, D = h.shape
    g_logits = jnp.matmul(h, w_grp).astype(jnp.float32) + b_grp.astype(jnp.float32)
    g_prob = jax.nn.softmax(g_logits, axis=-1)
    g_idx = jnp.argmax(g_logits, axis=-1)
    g_w = jnp.take_along_axis(g_prob, g_idx[:, None], axis=1)[:, 0]
    e_logits = (jnp.matmul(h, w_exp).astype(jnp.float32) + b_exp.astype(jnp.float32))
    e_logits = e_logits.reshape(N, N_GROUPS, EXPERTS_PER_GROUP)
    e_logits = jnp.take_along_axis(e_logits, g_idx[:, None, None], axis=1)[:, 0]
    e_prob = jax.nn.softmax(e_logits, axis=-1)
    top_p, top_local = lax.top_k(e_prob, TOP_K_INNER)
    top_p = top_p / jnp.sum(top_p, axis=-1, keepdims=True)
    weights = (g_w[:, None] * top_p).reshape(-1)
    expert = (g_idx[:, None] * EXPERTS_PER_GROUP + top_local).reshape(-1)
    order = jnp.argsort(expert)
    tok = order // TOP_K_INNER
    xs = h[tok]
    sizes = jnp.bincount(expert, length=N_EXPERTS).astype(jnp.int32)
    a = lax.ragged_dot(xs, w1, sizes)
    b = lax.ragged_dot(xs, w3, sizes)
    y = lax.ragged_dot(jax.nn.silu(a) * b, w2, sizes)
    y = (y * weights[order][:, None]).astype(h.dtype)
    return jax.ops.segment_sum(y, tok, num_segments=N)


def setup_inputs(seed: int = 0) -> dict:
    key = jax.random.key(seed)
    ks = jax.random.split(key, 24)
    f32 = jnp.float32
    nrm = lambda k, shape, fan: jax.random.normal(k, shape, f32) * fan ** -0.5
    L = DEPTH
    x = jax.random.normal(ks[0], (BATCH, SEQ, D_MODEL), f32)
    c = jax.random.normal(ks[1], (BATCH, D_MODEL), f32)
    offsets = jax.random.randint(ks[2], (BATCH,), 0, 64) * CHUNK
    positions = (offsets[:, None] + jnp.arange(SEQ)[None, :]).astype(jnp.int32)
    return {
        'x': x,
        'c': c,
        'positions': positions,
        'ada_w': nrm(ks[3], (L, D_MODEL, 6 * D_MODEL), D_MODEL) * 0.5,
        'ada_b': jax.random.normal(ks[4], (L, 6 * D_MODEL), f32) * 0.02,
        'norm1_g': 1.0 + 0.05 * jax.random.normal(ks[5], (L, D_MODEL), f32),
        'norm2_g': 1.0 + 0.05 * jax.random.normal(ks[6], (L, D_MODEL), f32),
        'w_in': nrm(ks[7], (L, D_MODEL, IN_COLS), D_MODEL),
        'b_fgt': jax.random.uniform(ks[8], (L, N_HEADS_FOX), f32, 1.0, 4.0),
        'b_gate': jax.random.normal(ks[9], (L, 2 * D_MODEL), f32) * 0.02,
        'qn_fox': 1.0 + 0.05 * jax.random.normal(ks[10], (L, HEAD_DIM), f32),
        'kn_fox': 1.0 + 0.05 * jax.random.normal(ks[11], (L, HEAD_DIM), f32),
        'qn_dsa': 1.0 + 0.05 * jax.random.normal(ks[12], (L, HEAD_DIM), f32),
        'kn_dsa': 1.0 + 0.05 * jax.random.normal(ks[13], (L, HEAD_DIM), f32),
        'w_proj_fox': nrm(ks[14], (L, W_FOX, D_MODEL), W_FOX),
        'w_proj_dsa': nrm(ks[15], (L, W_DSA, D_MODEL), W_DSA),
        'w_out': nrm(ks[16], (L, D_MODEL, D_MODEL), D_MODEL),
        'router_w_grp': nrm(ks[17], (L, D_MODEL, N_GROUPS), D_MODEL),
        'router_b_grp': jax.random.normal(ks[18], (L, N_GROUPS), f32) * 0.01,
        'router_w_exp': nrm(ks[19], (L, D_MODEL, N_EXPERTS), D_MODEL),
        'router_b_exp': jax.random.normal(ks[20], (L, N_EXPERTS), f32) * 0.01,
        'exp_w1': nrm(ks[21], (L, N_EXPERTS, D_MODEL, D_EXPERT), D_MODEL),
        'exp_w3': nrm(ks[22], (L, N_EXPERTS, D_MODEL, D_EXPERT), D_MODEL),
        'exp_w2': nrm(ks[23], (L, N_EXPERTS, D_EXPERT, D_MODEL), D_EXPERT),
    }


def reference(x, c, positions, ada_w, ada_b, norm1_g, norm2_g, w_in, b_fgt, b_gate,
              qn_fox, kn_fox, qn_dsa, kn_dsa, w_proj_fox, w_proj_dsa, w_out,
              router_w_grp, router_b_grp, router_w_exp, router_b_exp,
              exp_w1, exp_w3, exp_w2):
    B, S, D = x.shape
    split_at = np.cumsum(IN_SPLITS)[:-1].tolist()
    c_act = jax.nn.silu(c)
    for l in range(DEPTH):
        mod = jnp.matmul(c_act, ada_w[l]) + ada_b[l]
        sh1, sc1, gt1, sh2, sc2, gt2 = [m[:, None, :] for m in jnp.split(mod, 6, axis=-1)]

        h = rmsnorm(x, norm1_g[l]) * (1.0 + sc1) + sh1
        proj = jnp.matmul(h, w_in[l])
        (fq, fk, fv, flog, dq, dk, dv, iq, ik, iw, g_fox, g_dsa) = jnp.split(proj, split_at, axis=-1)

        fq = rmsnorm(fq.reshape(B, S, N_HEADS_FOX, HEAD_DIM), qn_fox[l])
        fk = rmsnorm(fk.reshape(B, S, N_HEADS_FOX, HEAD_DIM), kn_fox[l])
        fv = fv.reshape(B, S, N_HEADS_FOX, HEAD_DIM)
        log_f = jax.nn.log_sigmoid((flog + b_fgt[l]).astype(jnp.float32))
        out_fox = forgetting_attention(fq, fk, fv, log_f).reshape(B, S, W_FOX)

        dq = rope(rmsnorm(dq.reshape(B, S, N_HEADS_DSA, HEAD_DIM), qn_dsa[l]), positions)
        dk = rope(rmsnorm(dk.reshape(B, S, 1, HEAD_DIM), kn_dsa[l]), positions)[:, :, 0]
        iq = rope(iq.reshape(B, S, IDX_HEADS, IDX_DIM), positions)
        ik = rope(ik.reshape(B, S, 1, IDX_DIM), positions)[:, :, 0]
        out_dsa = dsa_attention(dq, dk, dv, iq, ik, iw).reshape(B, S, W_DSA)

        merged = (jax.nn.sigmoid(g_fox + b_gate[l, :D]) * jnp.matmul(out_fox, w_proj_fox[l])
                  + jax.nn.sigmoid(g_dsa + b_gate[l, D:]) * jnp.matmul(out_dsa, w_proj_dsa[l]))
        x = x + gt1 * jnp.matmul(merged, w_out[l])

        h2 = rmsnorm(x, norm2_g[l]) * (1.0 + sc2) + sh2
        y = hierarchical_moe(h2.reshape(B * S, D), router_w_grp[l], router_b_grp[l],
                             router_w_exp[l], router_b_exp[l],
                             exp_w1[l], exp_w3[l], exp_w2[l]).reshape(B, S, D)
        x = x + gt2 * y
    return x
```

```python
import functools

import jax
import jax.numpy as jnp
import numpy as np
from jax import lax
from jax.experimental import pallas as pl
from jax.experimental.pallas import tpu as pltpu

F32 = jnp.float32
BF16 = jnp.bfloat16

CHUNK = 64
HEAD_DIM = 64
N_HEADS = 8
W_HEADS = N_HEADS * HEAD_DIM
TOPK_MAX = 256
ROPE_THETA = 10000.0
N_GROUPS = 4
EXPERTS_PER_GROUP = 8
N_EXPERTS = N_GROUPS * EXPERTS_PER_GROUP
EPS = 1e-6
MASKED = -1e30

LANES = 128
VMEM_LIMIT = 56 * 1024 * 1024


def _cparams(sem):
    return pltpu.CompilerParams(dimension_semantics=sem, vmem_limit_bytes=VMEM_LIMIT)


def _mod_kernel(c_ref, w_ref, b_ref, o_ref):
    c = c_ref[...]
    ca = (c * jax.nn.sigmoid(c)).astype(BF16)
    o_ref[...] = jnp.dot(ca, w_ref[...].astype(BF16), preferred_element_type=F32) + b_ref[...]


def _modulation(c, ada_w, ada_b):
    B, D = c.shape
    n = ada_w.shape[1] // D
    return pl.pallas_call(
        _mod_kernel,
        grid=(n,),
        in_specs=[pl.BlockSpec((B, D), lambda j: (0, 0)),
                  pl.BlockSpec((D, D), lambda j: (0, j)),
                  pl.BlockSpec((1, D), lambda j: (0, j))],
        out_specs=pl.BlockSpec((B, D), lambda j: (0, j)),
        out_shape=jax.ShapeDtypeStruct((B, n * D), F32),
        compiler_params=_cparams(("arbitrary",)),
        name="adaln_mod",
    )(c, ada_w, ada_b.reshape(1, -1))


C_FQ, C_FK, C_DQ, C_IQ, C_FV = 0, 512, 1024, 1536, 2048
C_S1 = 2560
C_S2 = 2688
C_GATE = 2816
C_END = 4864


def _rope(x, cos, sin_lo, sin_hi):
    w = x.shape[-1]
    return x * cos + pltpu.roll(x, w - 32, 1) * sin_lo + pltpu.roll(x, 32, 1) * sin_hi


def _inproj_kernel(x_ref, mod_ref, g1_ref, w_ref, bd512_ref, bd128_ref, cos_ref, slo_ref, shi_ref,
                   gains_ref, kn128_ref, b2_ref, bg_ref,
                   fq_ref, fk_ref, dq_ref, iq_ref, fv_ref, s1_ref, s2_ref, gate_ref):
    x = x_ref[0]
    ms = jnp.mean(x * x, axis=-1, keepdims=True)
    h = x * lax.rsqrt(ms + EPS) * g1_ref[...]
    h = h * (1.0 + mod_ref[0, 1:2, :]) + mod_ref[0, 0:1, :]
    hb = h.astype(BF16)

    def proj(lo, hi):
        return jnp.dot(hb, w_ref[:, lo:hi], preferred_element_type=F32)

    def head_norm(y, gain):
        msq = jnp.dot((y * y).astype(BF16), bd512_ref[...], preferred_element_type=F32)
        return y * lax.rsqrt(msq + EPS) * gain

    cos1, slo1, shi1 = cos_ref[0], slo_ref[0], shi_ref[0]
    cos4 = jnp.concatenate([cos1] * 4, axis=1)
    slo4 = jnp.concatenate([slo1] * 4, axis=1)
    shi4 = jnp.concatenate([shi1] * 4, axis=1)

    fq_ref[0] = head_norm(proj(C_FQ, C_FQ + 512), gains_ref[0:1, :]).astype(BF16)
    fk_ref[0] = head_norm(proj(C_FK, C_FK + 512), gains_ref[1:2, :]).astype(BF16)
    dq = head_norm(proj(C_DQ, C_DQ + 512), gains_ref[2:3, :])
    dq_ref[0] = _rope(dq, cos4, slo4, shi4).astype(BF16)
    iq_ref[0] = _rope(proj(C_IQ, C_IQ + 512), cos4, slo4, shi4).astype(BF16)
    fv_ref[0] = proj(C_FV, C_FV + 512).astype(BF16)

    s1 = proj(C_S1, C_S1 + 128)
    msq = jnp.dot((s1 * s1).astype(BF16), bd128_ref[...], preferred_element_type=F32)
    lane = lax.broadcasted_iota(jnp.int32, s1.shape, 1)
    s1 = jnp.where(lane < HEAD_DIM, s1 * lax.rsqrt(msq + EPS) * kn128_ref[...], s1)
    s1_ref[0] = _rope(s1, cos1, slo1, shi1).astype(BF16)

    s2 = proj(C_S2, C_S2 + 128)
    z = s2 + b2_ref[...]
    logsig = jnp.minimum(z, 0.0) - jnp.log(1.0 + jnp.exp(-jnp.abs(z)))
    is_fgt = (lane >= HEAD_DIM) & (lane < HEAD_DIM + N_HEADS)
    s2_ref[0] = jnp.where(is_fgt, logsig, s2)

    gate_ref[0] = jax.nn.sigmoid(proj(C_GATE, C_END) + bg_ref[...]).astype(BF16)


def _in_projection(x, mod3, norm1_g, w_perm, bd512, bd128, cos, slo, shi, gains, kn128, b2, bg, tm):
    B, S, D = x.shape
    tok = lambda w: pl.BlockSpec((1, tm, w), lambda b, i: (b, i, 0))
    const = lambda shape: pl.BlockSpec(shape, lambda b, i: (0,) * len(shape))
    out_shapes = [jax.ShapeDtypeStruct((B, S, 512), BF16)] * 5 + [
        jax.ShapeDtypeStruct((B, S, 128), BF16),
        jax.ShapeDtypeStruct((B, S, 128), F32),
        jax.ShapeDtypeStruct((B, S, 2 * D), BF16)]
    return pl.pallas_call(
        _inproj_kernel,
        grid=(B, S // tm),
        in_specs=[tok(D),
                  pl.BlockSpec((1, 6, D), lambda b, i: (b, 0, 0)),
                  const((1, D)),
                  const(w_perm.shape),
                  const((512, 512)), const((128, 128)),
                  tok(128), tok(128), tok(128),
                  const((3, 512)), const((1, 128)), const((1, 128)), const((1, 2 * D))],
        out_specs=[tok(512)] * 5 + [tok(128), tok(128), tok(2 * D)],
        out_shape=out_shapes,
        compiler_params=_cparams(("parallel", "parallel")),
        name="in_projection",
    )(x, mod3, norm1_g, w_perm, bd512, bd128, cos, slo, shi, gains, kn128, b2, bg)


def _cumsum_kernel(x_ref, o_ref):
    x = x_ref[0]
    n = x.shape[-1]
    pos = lax.broadcasted_iota(jnp.int32, x.shape, 1)
    shift = 1
    while shift < n:
        x = x + jnp.where(pos >= shift, pltpu.roll(x, shift, 1), 0.0)
        shift *= 2
    o_ref[0] = x


def _seq_cumsum(logf_t):
    B, H, S = logf_t.shape
    return pl.pallas_call(
        _cumsum_kernel,
        grid=(B,),
        in_specs=[pl.BlockSpec((1, H, S), lambda b: (b, 0, 0))],
        out_specs=pl.BlockSpec((1, H, S), lambda b: (b, 0, 0)),
        out_shape=jax.ShapeDtypeStruct((B, H, S), F32),
        compiler_params=_cparams(("parallel",)),
        name="forget_cumsum",
    )(logf_t)


def _fox_kernel(q_ref, k_ref, v_ref, f_ref, o_ref, *, tq):
    i = pl.program_id(2)
    q = q_ref[0, 0]
    row = lax.broadcasted_iota(jnp.int32, (tq, tq), 0)
    col = lax.broadcasted_iota(jnp.int32, (tq, tq), 1)

    def step(j, carry, masked):
        m, l, acc = carry
        off = pl.multiple_of(j * tq, tq)
        k = k_ref[0, 0, pl.ds(off, tq), :]
        v = v_ref[0, 0, pl.ds(off, tq), :]
        s = lax.dot_general(q, k, (((1,), (1,)), ((), ())), preferred_element_type=F32)
        s = s - f_ref[0, 0, :, pl.ds(off, tq)]
        if masked:
            s = jnp.where(col <= row, s, MASKED)
        m_new = jnp.maximum(m, jnp.max(s, axis=-1, keepdims=True))
        p = jnp.exp(s - m_new)
        alpha = jnp.exp(m - m_new)
        l = alpha * l + jnp.sum(p, axis=-1, keepdims=True)
        acc = alpha * acc + jnp.dot(p.astype(BF16), v, preferred_element_type=F32)
        return m_new, l, acc

    init = (jnp.full((tq, 1), MASKED, F32), jnp.zeros((tq, 1), F32), jnp.zeros((tq, HEAD_DIM), F32))
    carry = lax.fori_loop(0, i, lambda j, c: step(j, c, False), init)
    m, l, acc = step(i, carry, True)
    o_ref[0, 0] = (acc / l).astype(BF16)


def _fox_attention(q, k, v, f_rows, tq):
    B, H, S, Dh = q.shape
    kv_spec = pl.BlockSpec((1, 1, S, Dh), lambda b, h, i: (b, h, 0, 0))
    return pl.pallas_call(
        functools.partial(_fox_kernel, tq=tq),
        grid=(B, H, S // tq),
        in_specs=[pl.BlockSpec((1, 1, tq, Dh), lambda b, h, i: (b, h, i, 0)),
                  kv_spec, kv_spec,
                  pl.BlockSpec((1, 1, 1, S), lambda b, h, i: (b, h, 0, 0))],
        out_specs=pl.BlockSpec((1, 1, tq, Dh), lambda b, h, i: (b, h, i, 0)),
        out_shape=jax.ShapeDtypeStruct((B, H, S, Dh), BF16),
        compiler_params=_cparams(("parallel", "parallel", "arbitrary")),
        name="fox_attention",
    )(q, k, v, f_rows)


QB = 128
KC = 128
INT_MIN = -(2 ** 31)


def _dsa_kernel(ik_ref, dk_ref, dvt_ref, iqt_ref, dqt_ref, iwt_ref, o_ref, key_ref, bias_ref, *, topk):
    i = pl.program_id(1)
    nch = i + 1
    lane_q = lax.broadcasted_iota(jnp.int32, (KC, QB), 1)
    sub_k = lax.broadcasted_iota(jnp.int32, (KC, QB), 0)
    q_chunk = (i * QB + lane_q) // CHUNK

    def score_chunk(c, _):
        off = pl.multiple_of(c * KC, KC)
        ik = ik_ref[0, pl.ds(off, KC), :]
        sc = jnp.zeros((KC, QB), F32)
        for hh in range(N_HEADS):
            d = jnp.dot(ik, iqt_ref[0, hh * HEAD_DIM:(hh + 1) * HEAD_DIM, :], preferred_element_type=F32)
            sc = sc + iwt_ref[0, hh:hh + 1, :] * jnp.maximum(d, 0.0)
        sc = sc + 0.0
        allowed = (off + sub_k) // CHUNK <= q_chunk
        bits = pltpu.bitcast(sc, jnp.int32)
        key = bits ^ ((bits >> 31) & 0x7FFFFFFF)
        key_ref[pl.ds(off, KC), :] = jnp.where(allowed, key, INT_MIN)
        return 0

    lax.fori_loop(0, nch, score_chunk, 0)

    def count(pred):
        def body(c, acc):
            off = pl.multiple_of(c * KC, KC)
            hit = pred(key_ref[pl.ds(off, KC), :], off + sub_k)
            return acc + jnp.sum(jnp.where(hit, 1.0, 0.0), axis=0, keepdims=True)
        return lax.fori_loop(0, nch, body, jnp.zeros((1, QB), F32))

    kf = jnp.float32(topk)

    n_nonneg = count(lambda k, _: k >= 0)
    thr = jnp.where(n_nonneg >= kf, 0, INT_MIN).astype(jnp.int32)
    for bit in range(30, -1, -1):
        cand = thr + jnp.int32(1 << bit)
        n = count(lambda k, _, cand=cand: k >= cand)
        thr = jnp.where(n >= kf, cand, thr)

    n_gt = count(lambda k, _: k > thr)
    n_ge = count(lambda k, _: k >= thr)
    need = kf - n_gt
    tie_excess = jnp.max(jnp.where((n_ge > kf) & (thr > INT_MIN), 1.0, 0.0)) > 0.0

    def tie_bound():
        p = jnp.zeros((1, QB), jnp.int32)
        nbits = int(np.ceil(np.log2(key_ref.shape[0]))) + 1
        for bit in range(nbits - 1, -1, -1):
            cand = p + jnp.int32(1 << bit)
            n = count(lambda k, idx, cand=cand: (k == thr) & (idx < cand))
            p = jnp.where(n < need, cand, p)
        return p

    last_tie = lax.cond(tie_excess, tie_bound, lambda: jnp.full((1, QB), 2 ** 30, jnp.int32))

    def bias_chunk(c, _):
        off = pl.multiple_of(c * KC, KC)
        k = key_ref[pl.ds(off, KC), :]
        idx = off + sub_k
        sel = ((k > thr) | ((k == thr) & (idx <= last_tie))) & (k > INT_MIN)
        bias_ref[pl.ds(off, KC), :] = jnp.where(sel, 0.0, MASKED)
        return 0

    lax.fori_loop(0, nch, bias_chunk, 0)

    for hh in range(N_HEADS):
        qt = dqt_ref[0, hh * HEAD_DIM:(hh + 1) * HEAD_DIM, :]

        def att(c, carry, qt=qt):
            m, l, acc = carry
            off = pl.multiple_of(c * KC, KC)
            s = jnp.dot(dk_ref[0, pl.ds(off, KC), :], qt, preferred_element_type=F32)
            s = s + bias_ref[pl.ds(off, KC), :]
            m_new = jnp.maximum(m, jnp.max(s, axis=0, keepdims=True))
            p = jnp.exp(s - m_new)
            alpha = jnp.exp(m - m_new)
            l = alpha * l + jnp.sum(p, axis=0, keepdims=True)
            acc = alpha * acc + jnp.dot(dvt_ref[0, :, pl.ds(off, KC)], p.astype(BF16),
                                        preferred_element_type=F32)
            return m_new, l, acc

        init = (jnp.full((1, QB), MASKED, F32), jnp.zeros((1, QB), F32), jnp.zeros((HEAD_DIM, QB), F32))
        m, l, acc = lax.fori_loop(0, nch, att, init)
        o_ref[0, hh * HEAD_DIM:(hh + 1) * HEAD_DIM, :] = (acc / l).astype(BF16)


def _dsa_attention(ik, dk, dvt, iqt, dqt, iwt, topk):
    B, S, Dh = ik.shape
    seq = pl.BlockSpec((1, S, Dh), lambda b, i: (b, 0, 0))
    qcols = lambda r: pl.BlockSpec((1, r, QB), lambda b, i: (b, 0, i))
    return pl.pallas_call(
        functools.partial(_dsa_kernel, topk=topk),
        grid=(B, S // QB),
        in_specs=[seq, seq,
                  pl.BlockSpec((1, Dh, S), lambda b, i: (b, 0, 0)),
                  qcols(W_HEADS), qcols(W_HEADS), qcols(N_HEADS)],
        out_specs=qcols(W_HEADS),
        out_shape=jax.ShapeDtypeStruct((B, W_HEADS, S), BF16),
        scratch_shapes=[pltpu.VMEM((S, QB), jnp.int32), pltpu.VMEM((S, QB), F32)],
        compiler_params=_cparams(("parallel", "arbitrary")),
        name="dsa_attention",
    )(ik, dk, dvt, iqt, dqt, iwt)


def _first(mask, lane):
    return jnp.min(jnp.where(mask, lane, LANES), axis=-1, keepdims=True)


def _post_kernel(of_ref, od_ref, gate_ref, x_ref, mod_ref, wpf_ref, wpd_ref, wo_ref, g2_ref, wr_ref, br_ref,
                 x1_ref, h2_ref, route_ref):
    D = x_ref.shape[-1]
    pf = jnp.dot(of_ref[0], wpf_ref[...], preferred_element_type=F32)
    pd = jnp.dot(od_ref[0], wpd_ref[...], preferred_element_type=F32)
    merged = gate_ref[0, :, :D].astype(F32) * pf + gate_ref[0, :, D:].astype(F32) * pd
    y = jnp.dot(merged.astype(BF16), wo_ref[...], preferred_element_type=F32)
    x1 = x_ref[0] + mod_ref[0, 2:3, :] * y
    x1_ref[0] = x1

    ms = jnp.mean(x1 * x1, axis=-1, keepdims=True)
    h2 = x1 * lax.rsqrt(ms + EPS) * g2_ref[...]
    h2 = h2 * (1.0 + mod_ref[0, 4:5, :]) + mod_ref[0, 3:4, :]
    hb = h2.astype(BF16)
    h2_ref[0] = h2

    logits = jnp.dot(hb, wr_ref[...], preferred_element_type=F32) + br_ref[...]
    lane = lax.broadcasted_iota(jnp.int32, logits.shape, 1)
    is_grp = lane < N_GROUPS
    gl = jnp.where(is_grp, logits, -jnp.inf)
    gmax = jnp.max(gl, axis=-1, keepdims=True)
    g_idx = _first(gl == gmax, lane)
    g_w = 1.0 / jnp.sum(jnp.exp(gl - gmax), axis=-1, keepdims=True)

    e_lo = N_GROUPS + g_idx * EXPERTS_PER_GROUP
    in_grp = (lane >= e_lo) & (lane < e_lo + EXPERTS_PER_GROUP)
    el = jnp.where(in_grp, logits, -jnp.inf)
    emax = jnp.max(el, axis=-1, keepdims=True)
    ee = jnp.exp(el - emax)
    prob = ee / jnp.sum(ee, axis=-1, keepdims=True)
    prob = jnp.where(in_grp, prob, -1.0)
    p0 = jnp.max(prob, axis=-1, keepdims=True)
    l0 = _first(prob == p0, lane)
    rest = jnp.where(lane == l0, -1.0, prob)
    p1 = jnp.max(rest, axis=-1, keepdims=True)
    l1 = _first(rest == p1, lane)
    psum = p0 + p1
    w0 = g_w * (p0 / psum)
    w1 = g_w * (p1 / psum)
    e0 = (l0 - N_GROUPS).astype(F32)
    e1 = (l1 - N_GROUPS).astype(F32)
    route_ref[0] = jnp.where(lane == 0, e0, jnp.where(lane == 1, e1, jnp.where(lane == 2, w0,
                             jnp.where(lane == 3, w1, 0.0))))


def _post_attention(of, od, gates, x, mod3, wpf, wpd, wo, g2, wr, br, tm):
    B, S, D = x.shape
    tok = lambda w: pl.BlockSpec((1, tm, w), lambda b, i: (b, i, 0))
    const = lambda shape: pl.BlockSpec(shape, lambda b, i: (0,) * len(shape))
    return pl.pallas_call(
        _post_kernel,
        grid=(B, S // tm),
        in_specs=[tok(W_HEADS), tok(W_HEADS), tok(2 * D), tok(D),
                  pl.BlockSpec((1, 6, D), lambda b, i: (b, 0, 0)),
                  const(wpf.shape), const(wpd.shape), const(wo.shape),
                  const((1, D)), const((D, LANES)), const((1, LANES))],
        out_specs=[tok(D), tok(D), tok(LANES)],
        out_shape=[jax.ShapeDtypeStruct((B, S, D), F32),
                   jax.ShapeDtypeStruct((B, S, D), F32),
                   jax.ShapeDtypeStruct((B, S, LANES), F32)],
        compiler_params=_cparams(("parallel", "parallel")),
        name="merge_out_router",
    )(of, od, gates, x, mod3, wpf, wpd, wo, g2, wr, br)


def _rank_kernel(route_ref, tri_ref, rank_ref, count_ref, carry_ref):
    @pl.when(pl.program_id(0) == 0)
    def _():
        carry_ref[...] = jnp.zeros_like(carry_ref)

    r = route_ref[...]
    lane = lax.broadcasted_iota(jnp.int32, r.shape, 1).astype(F32)
    hot0 = lane == r[:, 0:1]
    hot1 = lane == r[:, 1:2]
    hits = jnp.where(hot0 | hot1, 1.0, 0.0)
    incl = jnp.dot(tri_ref[...], hits.astype(BF16), preferred_element_type=F32)
    before = incl - hits + carry_ref[...]
    r0 = jnp.sum(jnp.where(hot0, before, 0.0), axis=-1, keepdims=True)
    r1 = jnp.sum(jnp.where(hot1, before, 0.0), axis=-1, keepdims=True)
    rank_ref[...] = jnp.where(lane == 0.0, r0, jnp.where(lane == 1.0, r1, 0.0))
    carry_ref[...] = carry_ref[...] + jnp.sum(hits, axis=0, keepdims=True)
    count_ref[...] = carry_ref[...]


def _expert_ranks(route, tm):
    N = route.shape[0]
    tri = jnp.asarray(np.tril(np.ones((tm, tm), np.float32)), BF16)
    return pl.pallas_call(
        _rank_kernel,
        grid=(N // tm,),
        in_specs=[pl.BlockSpec((tm, LANES), lambda i: (i, 0)),
                  pl.BlockSpec((tm, tm), lambda i: (0, 0))],
        out_specs=[pl.BlockSpec((tm, LANES), lambda i: (i, 0)),
                   pl.BlockSpec((1, LANES), lambda i: (0, 0))],
        out_shape=[jax.ShapeDtypeStruct((N, LANES), F32), jax.ShapeDtypeStruct((1, LANES), F32)],
        scratch_shapes=[pltpu.VMEM((1, LANES), F32)],
        compiler_params=_cparams(("arbitrary",)),
        name="expert_ranks",
    )(route, tri)


def _dispatch_kernel(pos_ref, h_ref, xs_in_ref, xs_ref, sem, *, tm):
    del xs_in_ref

    def copy(r, slot):
        return pltpu.make_async_copy(h_ref.at[pl.ds(r, 1), :],
                                     xs_ref.at[pl.ds(pos_ref[0, slot, r], 1), :], sem)

    def issue(r, _):
        copy(r, 0).start()
        copy(r, 1).start()
        return 0

    def drain(r, _):
        copy(r, 0).wait()
        copy(r, 1).wait()
        return 0

    lax.fori_loop(0, tm, issue, 0)
    lax.fori_loop(0, tm, drain, 0)


def _dispatch(h2, pos3, n_rows, tm):
    N, D = h2.shape
    xs0 = jnp.zeros((n_rows, D), F32)
    return pl.pallas_call(
        functools.partial(_dispatch_kernel, tm=tm),
        grid=(N // tm,),
        in_specs=[pl.BlockSpec((1, 2, tm), lambda i: (i, 0, 0), memory_space=pltpu.SMEM),
                  pl.BlockSpec((tm, D), lambda i: (i, 0)),
                  pl.BlockSpec(memory_space=pl.ANY)],
        out_specs=pl.BlockSpec(memory_space=pl.ANY),
        out_shape=jax.ShapeDtypeStruct((n_rows, D), F32),
        scratch_shapes=[pltpu.SemaphoreType.DMA(())],
        input_output_aliases={2: 0},
        compiler_params=_cparams(("arbitrary",)),
        name="moe_dispatch",
    )(pos3, h2, xs0)


def _expert_kernel(te_ref, nt_ref, xs_ref, w1_ref, w3_ref, w2_ref, y_ref):
    g = pl.program_id(0)

    @pl.when(g < nt_ref[0])
    def _():
        xb = xs_ref[...].astype(BF16)
        a = jnp.dot(xb, w1_ref[0], preferred_element_type=F32)
        b = jnp.dot(xb, w3_ref[0], preferred_element_type=F32)
        hmid = (a * jax.nn.sigmoid(a) * b).astype(BF16)
        y_ref[...] = jnp.dot(hmid, w2_ref[0], preferred_element_type=F32)

    @pl.when(g >= nt_ref[0])
    def _():
        y_ref[...] = jnp.zeros_like(y_ref)


def _experts(tile_expert, n_tiles_used, xs, w1, w3, w2, tg):
    P, D = xs.shape
    E, _, De = w1.shape
    grid_spec = pltpu.PrefetchScalarGridSpec(
        num_scalar_prefetch=2,
        grid=(P // tg,),
        in_specs=[pl.BlockSpec((tg, D), lambda g, te, nt: (g, 0)),
                  pl.BlockSpec((1, D, De), lambda g, te, nt: (te[g], 0, 0)),
                  pl.BlockSpec((1, D, De), lambda g, te, nt: (te[g], 0, 0)),
                  pl.BlockSpec((1, De, D), lambda g, te, nt: (te[g], 0, 0))],
        out_specs=pl.BlockSpec((tg, D), lambda g, te, nt: (g, 0)),
    )
    return pl.pallas_call(
        _expert_kernel,
        grid_spec=grid_spec,
        out_shape=jax.ShapeDtypeStruct((P, D), F32),
        compiler_params=_cparams(("arbitrary",)),
        name="moe_experts",
    )(tile_expert, n_tiles_used, xs, w1, w3, w2)


def _combine_kernel(pos_ref, y_ref, x1_ref, route_ref, gt_ref, o_ref, buf0, buf1, sem, *, tm):
    def copy(r, slot, buf):
        return pltpu.make_async_copy(y_ref.at[pl.ds(pos_ref[0, slot, r], 1), :],
                                     buf.at[pl.ds(r, 1), :], sem)

    def issue(r, _):
        copy(r, 0, buf0).start()
        copy(r, 1, buf1).start()
        return 0

    def drain(r, _):
        copy(r, 0, buf0).wait()
        copy(r, 1, buf1).wait()
        return 0

    lax.fori_loop(0, tm, issue, 0)
    lax.fori_loop(0, tm, drain, 0)
    w0 = route_ref[:, 2:3]
    w1 = route_ref[:, 3:4]
    y = buf0[...] * w0 + buf1[...] * w1
    o_ref[...] = x1_ref[...] + gt_ref[0] * y


def _combine(pos3, y, x1, route, gt2, tm, S):
    N, D = x1.shape
    per_b = S // tm
    return pl.pallas_call(
        functools.partial(_combine_kernel, tm=tm),
        grid=(N // tm,),
        in_specs=[pl.BlockSpec((1, 2, tm), lambda i: (i, 0, 0), memory_space=pltpu.SMEM),
                  pl.BlockSpec(memory_space=pl.ANY),
                  pl.BlockSpec((tm, D), lambda i: (i, 0)),
                  pl.BlockSpec((tm, LANES), lambda i: (i, 0)),
                  pl.BlockSpec((1, 1, D), lambda i: (i // per_b, 0, 0))],
        out_specs=pl.BlockSpec((tm, D), lambda i: (i, 0)),
        out_shape=jax.ShapeDtypeStruct((N, D), F32),
        scratch_shapes=[pltpu.VMEM((tm, D), F32), pltpu.VMEM((tm, D), F32), pltpu.SemaphoreType.DMA(())],
        compiler_params=_cparams(("arbitrary",)),
        name="moe_combine",
    )(pos3, y, x1, route, gt2)


def _rope_tables(positions):
    half = HEAD_DIM // 2
    inv = ROPE_THETA ** (-jnp.arange(half, dtype=F32) / half)
    ang = positions.astype(F32)[..., None] * inv
    cos, sin = jnp.cos(ang), jnp.sin(ang)
    zero = jnp.zeros_like(sin)
    cos128 = jnp.concatenate([cos] * 4, axis=-1)
    sin_lo = jnp.concatenate([-sin, zero] * 2, axis=-1)
    sin_hi = jnp.concatenate([zero, sin] * 2, axis=-1)
    return cos128, sin_lo, sin_hi


def _block_diag_mean(width):
    blk = np.kron(np.eye(width // HEAD_DIM, dtype=np.float32), np.full((HEAD_DIM, HEAD_DIM), 1.0 / HEAD_DIM, np.float32))
    return jnp.asarray(blk, BF16)


def _layer(x, c_mod, positions, norm1_g, norm2_g, w_in, b_fgt, b_gate, qn_fox, kn_fox, qn_dsa, kn_dsa,
           w_proj_fox, w_proj_dsa, w_out, r_w_grp, r_b_grp, r_w_exp, r_b_exp, w1, w3, w2):
    B, S, D = x.shape
    N = B * S
    topk = min(TOPK_MAX, S // 4)
    tm = min(512, S)
    scale = HEAD_DIM ** -0.5
    mod3 = c_mod.reshape(B, 6, D)

    o = np.cumsum([0, 512, 512, 512, 8, 512, 64, 64, 512, 64, 8, D, D])
    seg = lambda k: w_in[:, o[k]:o[k + 1]]
    zpad = jnp.zeros((D, LANES - HEAD_DIM - 2 * N_HEADS), F32)
    w_perm = jnp.concatenate([seg(0), seg(1), seg(4), seg(7), seg(2),
                              seg(5), seg(8),
                              seg(6), seg(3), seg(9), zpad,
                              seg(10), seg(11)], axis=1).astype(BF16)
    gains = jnp.stack([jnp.tile(qn_fox * scale, N_HEADS), jnp.tile(kn_fox, N_HEADS),
                       jnp.tile(qn_dsa * scale, N_HEADS)])
    kn128 = jnp.concatenate([kn_dsa, jnp.ones((HEAD_DIM,), F32)]).reshape(1, LANES)
    b2 = jnp.concatenate([jnp.zeros((HEAD_DIM,), F32), b_fgt,
                          jnp.zeros((LANES - HEAD_DIM - N_HEADS,), F32)]).reshape(1, LANES)
    cos128, sin_lo, sin_hi = _rope_tables(positions)

    fq, fk, dq, iq, fv, s1, s2, gates = _in_projection(
        x, mod3, norm1_g.reshape(1, D), w_perm, _block_diag_mean(512), _block_diag_mean(128),
        cos128, sin_lo, sin_hi, gains, kn128, b2, b_gate.reshape(1, 2 * D), tm)

    logf_t = jnp.transpose(s2[:, :, HEAD_DIM:HEAD_DIM + N_HEADS], (0, 2, 1))
    f_rows = _seq_cumsum(logf_t).reshape(B, N_HEADS, 1, S)
    heads = lambda a: jnp.transpose(a.reshape(B, S, N_HEADS, HEAD_DIM), (0, 2, 1, 3))
    of = _fox_attention(heads(fq), heads(fk), heads(fv), f_rows, min(256, S))
    of = jnp.transpose(of, (0, 2, 1, 3)).reshape(B, S, W_HEADS)

    dk, ik = s1[:, :, :HEAD_DIM], s1[:, :, HEAD_DIM:]
    dvt = jnp.transpose(s2[:, :, :HEAD_DIM].astype(BF16), (0, 2, 1))
    iwt = jnp.transpose(s2[:, :, HEAD_DIM + N_HEADS:HEAD_DIM + 2 * N_HEADS], (0, 2, 1))
    odt = _dsa_attention(ik, dk, dvt, jnp.transpose(iq, (0, 2, 1)), jnp.transpose(dq, (0, 2, 1)), iwt, topk)
    od = jnp.transpose(odt, (0, 2, 1))

    wr = jnp.concatenate([r_w_grp, r_w_exp, jnp.zeros((D, LANES - N_GROUPS - N_EXPERTS), F32)], axis=1).astype(BF16)
    br = jnp.concatenate([r_b_grp, r_b_exp, jnp.zeros((LANES - N_GROUPS - N_EXPERTS,), F32)]).reshape(1, LANES)
    x1, h2, route = _post_attention(of, od, gates, x, mod3, w_proj_fox.astype(BF16), w_proj_dsa.astype(BF16),
                                    w_out.astype(BF16), norm2_g.reshape(1, D), wr, br, tm)
    x1, h2, route = x1.reshape(N, D), h2.reshape(N, D), route.reshape(N, LANES)

    tg = 512 if N * 2 >= 512 * N_EXPERTS else 128
    ranks, counts = _expert_ranks(route, tm)
    counts = counts[0, :N_EXPERTS].astype(jnp.int32)
    padded = ((counts + tg - 1) // tg) * tg
    ends = jnp.cumsum(padded)
    starts = ends - padded
    e01 = route[:, :2].astype(jnp.int32)
    pos = starts[e01] + ranks[:, :2].astype(jnp.int32)
    n_rows = N * 2 + N_EXPERTS * tg
    n_tiles = n_rows // tg
    tile_expert = jnp.minimum(jnp.searchsorted(ends, jnp.arange(n_tiles, dtype=jnp.int32) * tg, side="right"),
                              N_EXPERTS - 1).astype(jnp.int32)
    n_used = (ends[-1] // tg).astype(jnp.int32).reshape(1)

    td = min(256, S)
    pos3 = jnp.transpose(pos.reshape(N // td, td, 2), (0, 2, 1))
    xs = _dispatch(h2, pos3, n_rows, td)
    y = _experts(tile_expert, n_used, xs, w1.astype(BF16), w3.astype(BF16), w2.astype(BF16), tg)
    out = _combine(pos3, y, x1, route, mod3[:, 5:6, :], td, S)
    return out.reshape(B, S, D)


def kernel(x, c, positions, ada_w, ada_b, norm1_g, norm2_g, w_in, b_fgt, b_gate, qn_fox, kn_fox, qn_dsa, kn_dsa, w_proj_fox, w_proj_dsa, w_out, router_w_grp, router_b_grp, router_w_exp, router_b_exp, exp_w1, exp_w3, exp_w2):
    for l in range(ada_w.shape[0]):
        c_mod = _modulation(c, ada_w[l], ada_b[l])
        x = _layer(x, c_mod, positions, norm1_g[l], norm2_g[l], w_in[l], b_fgt[l], b_gate[l],
                   qn_fox[l], kn_fox[l], qn_dsa[l], kn_dsa[l], w_proj_fox[l], w_proj_dsa[l], w_out[l],
                   router_w_grp[l], router_b_grp[l], router_w_exp[l], router_b_exp[l],
                   exp_w1[l], exp_w3[l], exp_w2[l])
    return x
```

```python
import functools

import jax
import jax.numpy as jnp
import numpy as np
from jax import lax
from jax.experimental import pallas as pl
from jax.experimental.pallas import tpu as pltpu

F32 = jnp.float32
BF16 = jnp.bfloat16

CHUNK = 64
HEAD_DIM = 64
N_HEADS = 8
W_HEADS = N_HEADS * HEAD_DIM
TOPK_MAX = 256
ROPE_THETA = 10000.0
N_GROUPS = 4
EXPERTS_PER_GROUP = 8
N_EXPERTS = N_GROUPS * EXPERTS_PER_GROUP
EPS = 1e-6
MASKED = -1e30

LANES = 128
VMEM_LIMIT = 56 * 1024 * 1024


def _cparams(sem):
    return pltpu.CompilerParams(dimension_semantics=sem, vmem_limit_bytes=VMEM_LIMIT)


def _mod_kernel(c_ref, w_ref, b_ref, o_ref):
    c = c_ref[...]
    ca = (c * jax.nn.sigmoid(c)).astype(BF16)
    o_ref[...] = jnp.dot(ca, w_ref[...].astype(BF16), preferred_element_type=F32) + b_ref[...]


def _modulation(c, ada_w, ada_b):
    B, D = c.shape
    n = ada_w.shape[1] // D
    return pl.pallas_call(
        _mod_kernel,
        grid=(n,),
        in_specs=[pl.BlockSpec((B, D), lambda j: (0, 0)),
                  pl.BlockSpec((D, D), lambda j: (0, j)),
                  pl.BlockSpec((1, D), lambda j: (0, j))],
        out_specs=pl.BlockSpec((B, D), lambda j: (0, j)),
        out_shape=jax.ShapeDtypeStruct((B, n * D), F32),
        compiler_params=_cparams(("arbitrary",)),
        name="adaln_mod",
    )(c, ada_w, ada_b.reshape(1, -1))


C_FQ, C_FK, C_DQ, C_IQ, C_FV = 0, 512, 1024, 1536, 2048
C_S1 = 2560
C_S2 = 2688
C_GATE = 2816
C_END = 4864


def _rope(x, cos, sin_lo, sin_hi):
    w = x.shape[-1]
    return x * cos + pltpu.roll(x, w - 32, 1) * sin_lo + pltpu.roll(x, 32, 1) * sin_hi


def _inproj_kernel(x_ref, mod_ref, g1_ref, w_ref, bd512_ref, bd128_ref, cos_ref, slo_ref, shi_ref,
                   gains_ref, kn128_ref, b2_ref, bg_ref,
                   fq_ref, fk_ref, dq_ref, iq_ref, fv_ref, s1_ref, s2_ref, gate_ref):
    x = x_ref[0]
    ms = jnp.mean(x * x, axis=-1, keepdims=True)
    h = x * lax.rsqrt(ms + EPS) * g1_ref[...]
    h = h * (1.0 + mod_ref[0, 1:2, :]) + mod_ref[0, 0:1, :]
    hb = h.astype(BF16)

    def proj(lo, hi):
        return jnp.dot(hb, w_ref[:, lo:hi], preferred_element_type=F32)

    def head_norm(y, gain):
        msq = jnp.dot((y * y).astype(BF16), bd512_ref[...], preferred_element_type=F32)
        return y * lax.rsqrt(msq + EPS) * gain

    cos1, slo1, shi1 = cos_ref[0], slo_ref[0], shi_ref[0]
    cos4 = jnp.concatenate([cos1] * 4, axis=1)
    slo4 = jnp.concatenate([slo1] * 4, axis=1)
    shi4 = jnp.concatenate([shi1] * 4, axis=1)

    fq_ref[0] = head_norm(proj(C_FQ, C_FQ + 512), gains_ref[0:1, :]).astype(BF16)
    fk_ref[0] = head_norm(proj(C_FK, C_FK + 512), gains_ref[1:2, :]).astype(BF16)
    dq = head_norm(proj(C_DQ, C_DQ + 512), gains_ref[2:3, :])
    dq_ref[0] = _rope(dq, cos4, slo4, shi4).astype(BF16)
    iq_ref[0] = _rope(proj(C_IQ, C_IQ + 512), cos4, slo4, shi4).astype(BF16)
    fv_ref[0] = proj(C_FV, C_FV + 512).astype(BF16)

    s1 = proj(C_S1, C_S1 + 128)
    msq = jnp.dot((s1 * s1).astype(BF16), bd128_ref[...], preferred_element_type=F32)
    lane = lax.broadcasted_iota(jnp.int32, s1.shape, 1)
    s1 = jnp.where(lane < HEAD_DIM, s1 * lax.rsqrt(msq + EPS) * kn128_ref[...], s1)
    s1_ref[0] = _rope(s1, cos1, slo1, shi1).astype(BF16)

    s2 = proj(C_S2, C_S2 + 128)
    z = s2 + b2_ref[...]
    logsig = jnp.minimum(z, 0.0) - jnp.log(1.0 + jnp.exp(-jnp.abs(z)))
    is_fgt = (lane >= HEAD_DIM) & (lane < HEAD_DIM + N_HEADS)
    s2_ref[0] = jnp.where(is_fgt, logsig, s2)

    gate_ref[0] = jax.nn.sigmoid(proj(C_GATE, C_END) + bg_ref[...]).astype(BF16)


def _in_projection(x, mod3, norm1_g, w_perm, bd512, bd128, cos, slo, shi, gains, kn128, b2, bg, tm):
    B, S, D = x.shape
    tok = lambda w: pl.BlockSpec((1, tm, w), lambda b, i: (b, i, 0))
    const = lambda shape: pl.BlockSpec(shape, lambda b, i: (0,) * len(shape))
    out_shapes = [jax.ShapeDtypeStruct((B, S, 512), BF16)] * 5 + [
        jax.ShapeDtypeStruct((B, S, 128), BF16),
        jax.ShapeDtypeStruct((B, S, 128), F32),
        jax.ShapeDtypeStruct((B, S, 2 * D), BF16)]
    return pl.pallas_call(
        _inproj_kernel,
        grid=(B, S // tm),
        in_specs=[tok(D),
                  pl.BlockSpec((1, 6, D), lambda b, i: (b, 0, 0)),
                  const((1, D)),
                  const(w_perm.shape),
                  const((512, 512)), const((128, 128)),
                  tok(128), tok(128), tok(128),
                  const((3, 512)), const((1, 128)), const((1, 128)), const((1, 2 * D))],
        out_specs=[tok(512)] * 5 + [tok(128), tok(128), tok(2 * D)],
        out_shape=out_shapes,
        compiler_params=_cparams(("parallel", "parallel")),
        name="in_projection",
    )(x, mod3, norm1_g, w_perm, bd512, bd128, cos, slo, shi, gains, kn128, b2, bg)


def _cumsum_kernel(x_ref, o_ref):
    x = x_ref[0]
    n = x.shape[-1]
    pos = lax.broadcasted_iota(jnp.int32, x.shape, 1)
    shift = 1
    while shift < n:
        x = x + jnp.where(pos >= shift, pltpu.roll(x, shift, 1), 0.0)
        shift *= 2
    hi = x.astype(BF16).astype(F32)
    mid = (x - hi).astype(BF16).astype(F32)
    o_ref[0, 0] = hi
    o_ref[0, 1] = mid
    o_ref[0, 2] = (x - hi - mid).astype(BF16).astype(F32)


def _seq_cumsum(logf_t):
    B, H, S = logf_t.shape
    return pl.pallas_call(
        _cumsum_kernel,
        grid=(B,),
        in_specs=[pl.BlockSpec((1, H, S), lambda b: (b, 0, 0))],
        out_specs=pl.BlockSpec((1, 3, H, S), lambda b: (b, 0, 0, 0)),
        out_shape=jax.ShapeDtypeStruct((B, 3, H, S), F32),
        compiler_params=_cparams(("parallel",)),
        name="forget_cumsum",
    )(logf_t)


KC = 256
SUB = 8


def _fold_rows(a, op):
    return op(a.reshape(a.shape[0] // SUB, SUB, a.shape[1]), axis=0)


def _softmax_pv(nch, s_ref, acc_ref, vt_at, m_all, o_ref):
    Q = o_ref.shape[-1]
    acc_ref[...] = jnp.zeros_like(acc_ref)

    def body(c, lsum):
        off = pl.multiple_of(c * KC, KC)
        new = []
        for hh in range(N_HEADS):
            p = jnp.exp(s_ref[hh, pl.ds(off, KC), :] - m_all[hh])
            new.append(lsum[hh] + _fold_rows(p, jnp.sum))
            acc_ref[hh] += jnp.dot(vt_at(hh, off), p.astype(BF16), preferred_element_type=F32)
        return tuple(new)

    lsum = lax.fori_loop(0, nch, body, tuple(jnp.zeros((SUB, Q), F32) for _ in range(N_HEADS)))
    for hh in range(N_HEADS):
        l = jnp.sum(lsum[hh], axis=0, keepdims=True)
        o_ref[0, hh * HEAD_DIM:(hh + 1) * HEAD_DIM, :] = (acc_ref[hh] / l).astype(BF16)


def _fox_kernel(ka_ref, qt_ref, vt_ref, o_ref, s_ref, acc_ref):
    i = pl.program_id(1)
    Q = o_ref.shape[-1]
    row = lax.broadcasted_iota(jnp.int32, (HEAD_DIM, Q), 0)
    aug = jnp.where(row < 3, -1.0, 0.0).astype(BF16)
    qa = [jnp.concatenate([qt_ref[0, hh * HEAD_DIM:(hh + 1) * HEAD_DIM, :], aug], axis=0)
          for hh in range(N_HEADS)]

    def scores(c, mx, bias):
        off = pl.multiple_of(c * KC, KC)
        new = []
        for hh in range(N_HEADS):
            s = jnp.dot(ka_ref[0, hh, pl.ds(off, KC), :], qa[hh], preferred_element_type=F32)
            if bias is not None:
                s = s + bias
            s_ref[hh, pl.ds(off, KC), :] = s
            new.append(jnp.maximum(mx[hh], _fold_rows(s, jnp.max)))
        return tuple(new)

    mx = tuple(jnp.full((SUB, Q), MASKED, F32) for _ in range(N_HEADS))
    mx = lax.fori_loop(0, i, lambda c, m: scores(c, m, None), mx)
    kk = lax.broadcasted_iota(jnp.int32, (KC, Q), 0)
    qq = lax.broadcasted_iota(jnp.int32, (KC, Q), 1)
    mx = scores(i, mx, jnp.where(kk <= qq, 0.0, MASKED))
    m_all = [jnp.max(m, axis=0, keepdims=True) for m in mx]
    _softmax_pv(i + 1, s_ref, acc_ref, lambda hh, off: vt_ref[0, hh * HEAD_DIM:(hh + 1) * HEAD_DIM, pl.ds(off, KC)],
                m_all, o_ref)


def _fox_attention(k_aug, qt, vt):
    B, H, S, Wa = k_aug.shape
    return pl.pallas_call(
        _fox_kernel,
        grid=(B, S // KC),
        in_specs=[pl.BlockSpec((1, H, S, Wa), lambda b, i: (b, 0, 0, 0)),
                  pl.BlockSpec((1, W_HEADS, KC), lambda b, i: (b, 0, i)),
                  pl.BlockSpec((1, W_HEADS, S), lambda b, i: (b, 0, 0))],
        out_specs=pl.BlockSpec((1, W_HEADS, KC), lambda b, i: (b, 0, i)),
        out_shape=jax.ShapeDtypeStruct((B, W_HEADS, S), BF16),
        scratch_shapes=[pltpu.VMEM((H, S, KC), F32), pltpu.VMEM((H, HEAD_DIM, KC), F32)],
        compiler_params=_cparams(("parallel", "arbitrary")),
        name="fox_attention",
    )(k_aug, qt, vt)


INT_MIN = -(2 ** 31)


def _dsa_kernel(ik_ref, dk_ref, dvt_ref, iqt_ref, dqt_ref, iwt_ref, o_ref, key_ref, s_ref, acc_ref, *, topk):
    i = pl.program_id(1)
    Q = o_ref.shape[-1]
    nch = i + 1
    lane_q = lax.broadcasted_iota(jnp.int32, (KC, Q), 1)
    sub_k = lax.broadcasted_iota(jnp.int32, (KC, Q), 0)
    q_chunk = (i * Q + lane_q) // CHUNK

    def score_chunk(c, _):
        off = pl.multiple_of(c * KC, KC)
        ik = ik_ref[0, pl.ds(off, KC), :]
        sc = jnp.zeros((KC, Q), F32)
        for hh in range(N_HEADS):
            d = jnp.dot(ik, iqt_ref[0, hh * HEAD_DIM:(hh + 1) * HEAD_DIM, :], preferred_element_type=F32)
            sc = sc + iwt_ref[0, hh:hh + 1, :] * jnp.maximum(d, 0.0)
        sc = sc + 0.0
        allowed = (off + sub_k) // CHUNK <= q_chunk
        bits = pltpu.bitcast(sc, jnp.int32)
        key = bits ^ ((bits >> 31) & 0x7FFFFFFF)
        key_ref[pl.ds(off, KC), :] = jnp.where(allowed, key, INT_MIN)
        return 0

    lax.fori_loop(0, nch, score_chunk, 0)

    def count(pred):
        def body(c, acc):
            off = pl.multiple_of(c * KC, KC)
            hit = pred(key_ref[pl.ds(off, KC), :], off + sub_k)
            return acc + _fold_rows(jnp.where(hit, 1.0, 0.0), jnp.sum)
        acc = lax.fori_loop(0, nch, body, jnp.zeros((SUB, Q), F32))
        return jnp.sum(acc, axis=0, keepdims=True)

    kf = jnp.float32(topk)

    n_nonneg = count(lambda k, _: k >= 0)
    top_half = n_nonneg >= kf
    thr = jnp.where(top_half, 0, INT_MIN).astype(jnp.int32)
    n_ge = jnp.where(top_half, n_nonneg, (nch * KC).astype(F32))
    for bit in range(30, -1, -1):
        cand = thr + jnp.int32(1 << bit)
        n = count(lambda k, _, cand=cand: k >= cand)
        take = n >= kf
        thr = jnp.where(take, cand, thr)
        n_ge = jnp.where(take, n, n_ge)

    excess = (n_ge > kf) & (thr > INT_MIN)

    @pl.when(jnp.max(jnp.where(excess, 1.0, 0.0)) > 0.0)
    def _():
        need = kf - count(lambda k, _: k > thr)
        last = jnp.zeros((1, Q), jnp.int32)
        nbits = int(np.ceil(np.log2(key_ref.shape[0]))) + 1
        for bit in range(nbits - 1, -1, -1):
            cand = last + jnp.int32(1 << bit)
            n = count(lambda k, idx, cand=cand: (k == thr) & (idx < cand))
            last = jnp.where(n < need, cand, last)

        def demote(c, _):
            off = pl.multiple_of(c * KC, KC)
            k = key_ref[pl.ds(off, KC), :]
            drop = excess & (k == thr) & (off + sub_k > last)
            key_ref[pl.ds(off, KC), :] = jnp.where(drop, thr - 1, k)
            return 0

        lax.fori_loop(0, nch, demote, 0)

    keep_from = jnp.maximum(thr, INT_MIN + 1)

    qts = [dqt_ref[0, hh * HEAD_DIM:(hh + 1) * HEAD_DIM, :] for hh in range(N_HEADS)]

    def scores(c, mx):
        off = pl.multiple_of(c * KC, KC)
        bias = jnp.where(key_ref[pl.ds(off, KC), :] >= keep_from, 0.0, MASKED)
        dk = dk_ref[0, pl.ds(off, KC), :]
        new = []
        for hh in range(N_HEADS):
            s = jnp.dot(dk, qts[hh], preferred_element_type=F32) + bias
            s_ref[hh, pl.ds(off, KC), :] = s
            new.append(jnp.maximum(mx[hh], _fold_rows(s, jnp.max)))
        return tuple(new)

    mx = lax.fori_loop(0, nch, scores, tuple(jnp.full((SUB, Q), MASKED, F32) for _ in range(N_HEADS)))
    m_all = [jnp.max(m, axis=0, keepdims=True) for m in mx]
    _softmax_pv(nch, s_ref, acc_ref, lambda hh, off: dvt_ref[0, :, pl.ds(off, KC)], m_all, o_ref)


def _dsa_attention(ik, dk, dvt, iqt, dqt, iwt, topk):
    B, S, Dh = ik.shape
    seq = pl.BlockSpec((1, S, Dh), lambda b, i: (b, 0, 0))
    qcols = lambda r: pl.BlockSpec((1, r, KC), lambda b, i: (b, 0, i))
    return pl.pallas_call(
        functools.partial(_dsa_kernel, topk=topk),
        grid=(B, S // KC),
        in_specs=[seq, seq,
                  pl.BlockSpec((1, Dh, S), lambda b, i: (b, 0, 0)),
                  qcols(W_HEADS), qcols(W_HEADS), qcols(N_HEADS)],
        out_specs=qcols(W_HEADS),
        out_shape=jax.ShapeDtypeStruct((B, W_HEADS, S), BF16),
        scratch_shapes=[pltpu.VMEM((S, KC), jnp.int32), pltpu.VMEM((N_HEADS, S, KC), F32),
                        pltpu.VMEM((N_HEADS, HEAD_DIM, KC), F32)],
        compiler_params=_cparams(("parallel", "arbitrary")),
        name="dsa_attention",
    )(ik, dk, dvt, iqt, dqt, iwt)


def _first(mask, lane):
    return jnp.min(jnp.where(mask, lane, LANES), axis=-1, keepdims=True)


def _post_kernel(of_ref, od_ref, gate_ref, x_ref, mod_ref, wpf_ref, wpd_ref, wo_ref, g2_ref, wr_ref, br_ref,
                 x1_ref, h2_ref, route_ref):
    D = x_ref.shape[-1]
    pf = jnp.dot(of_ref[0], wpf_ref[...], preferred_element_type=F32)
    pd = jnp.dot(od_ref[0], wpd_ref[...], preferred_element_type=F32)
    merged = gate_ref[0, :, :D].astype(F32) * pf + gate_ref[0, :, D:].astype(F32) * pd
    y = jnp.dot(merged.astype(BF16), wo_ref[...], preferred_element_type=F32)
    x1 = x_ref[0] + mod_ref[0, 2:3, :] * y
    x1_ref[0] = x1

    ms = jnp.mean(x1 * x1, axis=-1, keepdims=True)
    h2 = x1 * lax.rsqrt(ms + EPS) * g2_ref[...]
    h2 = h2 * (1.0 + mod_ref[0, 4:5, :]) + mod_ref[0, 3:4, :]
    hb = h2.astype(BF16)
    h2_ref[0] = h2

    logits = jnp.dot(hb, wr_ref[...], preferred_element_type=F32) + br_ref[...]
    lane = lax.broadcasted_iota(jnp.int32, logits.shape, 1)
    is_grp = lane < N_GROUPS
    gl = jnp.where(is_grp, logits, -jnp.inf)
    gmax = jnp.max(gl, axis=-1, keepdims=True)
    g_idx = _first(gl == gmax, lane)
    g_w = 1.0 / jnp.sum(jnp.exp(gl - gmax), axis=-1, keepdims=True)

    e_lo = N_GROUPS + g_idx * EXPERTS_PER_GROUP
    in_grp = (lane >= e_lo) & (lane < e_lo + EXPERTS_PER_GROUP)
    el = jnp.where(in_grp, logits, -jnp.inf)
    emax = jnp.max(el, axis=-1, keepdims=True)
    ee = jnp.exp(el - emax)
    prob = ee / jnp.sum(ee, axis=-1, keepdims=True)
    prob = jnp.where(in_grp, prob, -1.0)
    p0 = jnp.max(prob, axis=-1, keepdims=True)
    l0 = _first(prob == p0, lane)
    rest = jnp.where(lane == l0, -1.0, prob)
    p1 = jnp.max(rest, axis=-1, keepdims=True)
    l1 = _first(rest == p1, lane)
    psum = p0 + p1
    w0 = g_w * (p0 / psum)
    w1 = g_w * (p1 / psum)
    e0 = (l0 - N_GROUPS).astype(F32)
    e1 = (l1 - N_GROUPS).astype(F32)
    route_ref[0] = jnp.where(lane == 0, e0, jnp.where(lane == 1, e1, jnp.where(lane == 2, w0,
                             jnp.where(lane == 3, w1, 0.0))))


def _post_attention(of, od, gates, x, mod3, wpf, wpd, wo, g2, wr, br, tm):
    B, S, D = x.shape
    tok = lambda w: pl.BlockSpec((1, tm, w), lambda b, i: (b, i, 0))
    const = lambda shape: pl.BlockSpec(shape, lambda b, i: (0,) * len(shape))
    return pl.pallas_call(
        _post_kernel,
        grid=(B, S // tm),
        in_specs=[tok(W_HEADS), tok(W_HEADS), tok(2 * D), tok(D),
                  pl.BlockSpec((1, 6, D), lambda b, i: (b, 0, 0)),
                  const(wpf.shape), const(wpd.shape), const(wo.shape),
                  const((1, D)), const((D, LANES)), const((1, LANES))],
        out_specs=[tok(D), tok(D), tok(LANES)],
        out_shape=[jax.ShapeDtypeStruct((B, S, D), F32),
                   jax.ShapeDtypeStruct((B, S, D), F32),
                   jax.ShapeDtypeStruct((B, S, LANES), F32)],
        compiler_params=_cparams(("parallel", "parallel")),
        name="merge_out_router",
    )(of, od, gates, x, mod3, wpf, wpd, wo, g2, wr, br)


def _rank_kernel(route_ref, tri_ref, rank_ref, count_ref, carry_ref):
    @pl.when(pl.program_id(0) == 0)
    def _():
        carry_ref[...] = jnp.zeros_like(carry_ref)

    r = route_ref[...]
    lane = lax.broadcasted_iota(jnp.int32, r.shape, 1).astype(F32)
    hot0 = lane == r[:, 0:1]
    hot1 = lane == r[:, 1:2]
    hits = jnp.where(hot0 | hot1, 1.0, 0.0)
    incl = jnp.dot(tri_ref[...], hits.astype(BF16), preferred_element_type=F32)
    before = incl - hits + carry_ref[...]
    r0 = jnp.sum(jnp.where(hot0, before, 0.0), axis=-1, keepdims=True)
    r1 = jnp.sum(jnp.where(hot1, before, 0.0), axis=-1, keepdims=True)
    rank_ref[...] = jnp.where(lane == 0.0, r0, jnp.where(lane == 1.0, r1, 0.0))
    carry_ref[...] = carry_ref[...] + jnp.sum(hits, axis=0, keepdims=True)
    count_ref[...] = carry_ref[...]


def _expert_ranks(route, tm):
    N = route.shape[0]
    tri = jnp.asarray(np.tril(np.ones((tm, tm), np.float32)), BF16)
    return pl.pallas_call(
        _rank_kernel,
        grid=(N // tm,),
        in_specs=[pl.BlockSpec((tm, LANES), lambda i: (i, 0)),
                  pl.BlockSpec((tm, tm), lambda i: (0, 0))],
        out_specs=[pl.BlockSpec((tm, LANES), lambda i: (i, 0)),
                   pl.BlockSpec((1, LANES), lambda i: (0, 0))],
        out_shape=[jax.ShapeDtypeStruct((N, LANES), F32), jax.ShapeDtypeStruct((1, LANES), F32)],
        scratch_shapes=[pltpu.VMEM((1, LANES), F32)],
        compiler_params=_cparams(("arbitrary",)),
        name="expert_ranks",
    )(route, tri)


def _dispatch_kernel(pos_ref, h_ref, xs_in_ref, xs_ref, sem, *, tm):
    del xs_in_ref

    def copy(r, slot):
        return pltpu.make_async_copy(h_ref.at[pl.ds(r, 1), :],
                                     xs_ref.at[pl.ds(pos_ref[0, slot, r], 1), :], sem)

    def issue(r, _):
        copy(r, 0).start()
        copy(r, 1).start()
        return 0

    def drain(r, _):
        copy(r, 0).wait()
        copy(r, 1).wait()
        return 0

    lax.fori_loop(0, tm, issue, 0)
    lax.fori_loop(0, tm, drain, 0)


def _dispatch(h2, pos3, n_rows, tm):
    N, D = h2.shape
    xs0 = jnp.zeros((n_rows, D), F32)
    return pl.pallas_call(
        functools.partial(_dispatch_kernel, tm=tm),
        grid=(N // tm,),
        in_specs=[pl.BlockSpec((1, 2, tm), lambda i: (i, 0, 0), memory_space=pltpu.SMEM),
                  pl.BlockSpec((tm, D), lambda i: (i, 0)),
                  pl.BlockSpec(memory_space=pl.ANY)],
        out_specs=pl.BlockSpec(memory_space=pl.ANY),
        out_shape=jax.ShapeDtypeStruct((n_rows, D), F32),
        scratch_shapes=[pltpu.SemaphoreType.DMA(())],
        input_output_aliases={2: 0},
        compiler_params=_cparams(("arbitrary",)),
        name="moe_dispatch",
    )(pos3, h2, xs0)


def _expert_kernel(te_ref, nt_ref, xs_ref, w1_ref, w3_ref, w2_ref, y_ref):
    g = pl.program_id(0)

    @pl.when(g < nt_ref[0])
    def _():
        xb = xs_ref[...].astype(BF16)
        a = jnp.dot(xb, w1_ref[0], preferred_element_type=F32)
        b = jnp.dot(xb, w3_ref[0], preferred_element_type=F32)
        hmid = (a * jax.nn.sigmoid(a) * b).astype(BF16)
        y_ref[...] = jnp.dot(hmid, w2_ref[0], preferred_element_type=F32)

    @pl.when(g >= nt_ref[0])
    def _():
        y_ref[...] = jnp.zeros_like(y_ref)


def _experts(tile_expert, n_tiles_used, xs, w1, w3, w2, tg):
    P, D = xs.shape
    E, _, De = w1.shape
    grid_spec = pltpu.PrefetchScalarGridSpec(
        num_scalar_prefetch=2,
        grid=(P // tg,),
        in_specs=[pl.BlockSpec((tg, D), lambda g, te, nt: (g, 0)),
                  pl.BlockSpec((1, D, De), lambda g, te, nt: (te[g], 0, 0)),
                  pl.BlockSpec((1, D, De), lambda g, te, nt: (te[g], 0, 0)),
                  pl.BlockSpec((1, De, D), lambda g, te, nt: (te[g], 0, 0))],
        out_specs=pl.BlockSpec((tg, D), lambda g, te, nt: (g, 0)),
    )
    return pl.pallas_call(
        _expert_kernel,
        grid_spec=grid_spec,
        out_shape=jax.ShapeDtypeStruct((P, D), F32),
        compiler_params=_cparams(("arbitrary",)),
        name="moe_experts",
    )(tile_expert, n_tiles_used, xs, w1, w3, w2)


def _combine_kernel(pos_ref, y_ref, x1_ref, route_ref, gt_ref, o_ref, buf0, buf1, sem, *, tm):
    def copy(r, slot, buf):
        return pltpu.make_async_copy(y_ref.at[pl.ds(pos_ref[0, slot, r], 1), :],
                                     buf.at[pl.ds(r, 1), :], sem)

    def issue(r, _):
        copy(r, 0, buf0).start()
        copy(r, 1, buf1).start()
        return 0

    def drain(r, _):
        copy(r, 0, buf0).wait()
        copy(r, 1, buf1).wait()
        return 0

    lax.fori_loop(0, tm, issue, 0)
    lax.fori_loop(0, tm, drain, 0)
    w0 = route_ref[:, 2:3]
    w1 = route_ref[:, 3:4]
    y = buf0[...] * w0 + buf1[...] * w1
    o_ref[...] = x1_ref[...] + gt_ref[0] * y


def _combine(pos3, y, x1, route, gt2, tm, S):
    N, D = x1.shape
    per_b = S // tm
    return pl.pallas_call(
        functools.partial(_combine_kernel, tm=tm),
        grid=(N // tm,),
        in_specs=[pl.BlockSpec((1, 2, tm), lambda i: (i, 0, 0), memory_space=pltpu.SMEM),
                  pl.BlockSpec(memory_space=pl.ANY),
                  pl.BlockSpec((tm, D), lambda i: (i, 0)),
                  pl.BlockSpec((tm, LANES), lambda i: (i, 0)),
                  pl.BlockSpec((1, 1, D), lambda i: (i // per_b, 0, 0))],
        out_specs=pl.BlockSpec((tm, D), lambda i: (i, 0)),
        out_shape=jax.ShapeDtypeStruct((N, D), F32),
        scratch_shapes=[pltpu.VMEM((tm, D), F32), pltpu.VMEM((tm, D), F32), pltpu.SemaphoreType.DMA(())],
        compiler_params=_cparams(("arbitrary",)),
        name="moe_combine",
    )(pos3, y, x1, route, gt2)


def _rope_tables(positions):
    half = HEAD_DIM // 2
    inv = ROPE_THETA ** (-jnp.arange(half, dtype=F32) / half)
    ang = positions.astype(F32)[..., None] * inv
    cos, sin = jnp.cos(ang), jnp.sin(ang)
    zero = jnp.zeros_like(sin)
    cos128 = jnp.concatenate([cos] * 4, axis=-1)
    sin_lo = jnp.concatenate([-sin, zero] * 2, axis=-1)
    sin_hi = jnp.concatenate([zero, sin] * 2, axis=-1)
    return cos128, sin_lo, sin_hi


def _block_diag_mean(width):
    blk = np.kron(np.eye(width // HEAD_DIM, dtype=np.float32), np.full((HEAD_DIM, HEAD_DIM), 1.0 / HEAD_DIM, np.float32))
    return jnp.asarray(blk, BF16)


def _layer(x, c_mod, positions, norm1_g, norm2_g, w_in, b_fgt, b_gate, qn_fox, kn_fox, qn_dsa, kn_dsa,
           w_proj_fox, w_proj_dsa, w_out, r_w_grp, r_b_grp, r_w_exp, r_b_exp, w1, w3, w2):
    B, S, D = x.shape
    N = B * S
    topk = min(TOPK_MAX, S // 4)
    tm = min(512, S)
    scale = HEAD_DIM ** -0.5
    mod3 = c_mod.reshape(B, 6, D)

    o = np.cumsum([0, 512, 512, 512, 8, 512, 64, 64, 512, 64, 8, D, D])
    seg = lambda k: w_in[:, o[k]:o[k + 1]]
    zpad = jnp.zeros((D, LANES - HEAD_DIM - 2 * N_HEADS), F32)
    w_perm = jnp.concatenate([seg(0), seg(1), seg(4), seg(7), seg(2),
                              seg(5), seg(8),
                              seg(6), seg(3), seg(9), zpad,
                              seg(10), seg(11)], axis=1).astype(BF16)
    gains = jnp.stack([jnp.tile(qn_fox * scale, N_HEADS), jnp.tile(kn_fox, N_HEADS),
                       jnp.tile(qn_dsa * scale, N_HEADS)])
    kn128 = jnp.concatenate([kn_dsa, jnp.ones((HEAD_DIM,), F32)]).reshape(1, LANES)
    b2 = jnp.concatenate([jnp.zeros((HEAD_DIM,), F32), b_fgt,
                          jnp.zeros((LANES - HEAD_DIM - N_HEADS,), F32)]).reshape(1, LANES)
    cos128, sin_lo, sin_hi = _rope_tables(positions)

    fq, fk, dq, iq, fv, s1, s2, gates = _in_projection(
        x, mod3, norm1_g.reshape(1, D), w_perm, _block_diag_mean(512), _block_diag_mean(128),
        cos128, sin_lo, sin_hi, gains, kn128, b2, b_gate.reshape(1, 2 * D), tm)

    logf_t = jnp.transpose(s2[:, :, HEAD_DIM:HEAD_DIM + N_HEADS], (0, 2, 1))
    f_parts = jnp.transpose(_seq_cumsum(logf_t), (0, 2, 3, 1)).astype(BF16)
    k_heads = jnp.transpose(fk.reshape(B, S, N_HEADS, HEAD_DIM), (0, 2, 1, 3))
    k_aug = jnp.concatenate([k_heads, f_parts, jnp.zeros((B, N_HEADS, S, HEAD_DIM - 3), BF16)], axis=-1)
    oft = _fox_attention(k_aug, jnp.transpose(fq, (0, 2, 1)), jnp.transpose(fv, (0, 2, 1)))
    of = jnp.transpose(oft, (0, 2, 1))

    dk, ik = s1[:, :, :HEAD_DIM], s1[:, :, HEAD_DIM:]
    dvt = jnp.transpose(s2[:, :, :HEAD_DIM].astype(BF16), (0, 2, 1))
    iwt = jnp.transpose(s2[:, :, HEAD_DIM + N_HEADS:HEAD_DIM + 2 * N_HEADS], (0, 2, 1))
    odt = _dsa_attention(ik, dk, dvt, jnp.transpose(iq, (0, 2, 1)), jnp.transpose(dq, (0, 2, 1)), iwt, topk)
    od = jnp.transpose(odt, (0, 2, 1))

    wr = jnp.concatenate([r_w_grp, r_w_exp, jnp.zeros((D, LANES - N_GROUPS - N_EXPERTS), F32)], axis=1).astype(BF16)
    br = jnp.concatenate([r_b_grp, r_b_exp, jnp.zeros((LANES - N_GROUPS - N_EXPERTS,), F32)]).reshape(1, LANES)
    x1, h2, route = _post_attention(of, od, gates, x, mod3, w_proj_fox.astype(BF16), w_proj_dsa.astype(BF16),
                                    w_out.astype(BF16), norm2_g.reshape(1, D), wr, br, tm)
    x1, h2, route = x1.reshape(N, D), h2.reshape(N, D), route.reshape(N, LANES)

    tg = 512 if N * 2 >= 512 * N_EXPERTS else 128
    ranks, counts = _expert_ranks(route, tm)
    counts = counts[0, :N_EXPERTS].astype(jnp.int32)
    padded = ((counts + tg - 1) // tg) * tg
    ends = jnp.cumsum(padded)
    starts = ends - padded
    e01 = route[:, :2].astype(jnp.int32)
    pos = starts[e01] + ranks[:, :2].astype(jnp.int32)
    n_rows = N * 2 + N_EXPERTS * tg
    n_tiles = n_rows // tg
    tile_expert = jnp.minimum(jnp.searchsorted(ends, jnp.arange(n_tiles, dtype=jnp.int32) * tg, side="right"),
                              N_EXPERTS - 1).astype(jnp.int32)
    n_used = (ends[-1] // tg).astype(jnp.int32).reshape(1)

    td = min(256, S)
    pos3 = jnp.transpose(pos.reshape(N // td, td, 2), (0, 2, 1))
    xs = _dispatch(h2, pos3, n_rows, td)
    y = _experts(tile_expert, n_used, xs, w1.astype(BF16), w3.astype(BF16), w2.astype(BF16), tg)
    out = _combine(pos3, y, x1, route, mod3[:, 5:6, :], td, S)
    return out.reshape(B, S, D)


def kernel(x, c, positions, ada_w, ada_b, norm1_g, norm2_g, w_in, b_fgt, b_gate, qn_fox, kn_fox, qn_dsa, kn_dsa, w_proj_fox, w_proj_dsa, w_out, router_w_grp, router_b_grp, router_w_exp, router_b_exp, exp_w1, exp_w3, exp_w2):
    for l in range(ada_w.shape[0]):
        c_mod = _modulation(c, ada_w[l], ada_b[l])
        x = _layer(x, c_mod, positions, norm1_g[l], norm2_g[l], w_in[l], b_fgt[l], b_gate[l],
                   qn_fox[l], kn_fox[l], qn_dsa[l], kn_dsa[l], w_proj_fox[l], w_proj_dsa[l], w_out[l],
                   router_w_grp[l], router_b_grp[l], router_w_exp[l], router_b_exp[l],
                   exp_w1[l], exp_w3[l], exp_w2[l])
    return x
```

```python
import functools

import jax
import jax.numpy as jnp
import numpy as np
from jax import lax
from jax.experimental import pallas as pl
from jax.experimental.pallas import tpu as pltpu

F32 = jnp.float32
BF16 = jnp.bfloat16

CHUNK = 64
CHUNK_SHIFT = 6
DMA_UNROLL = 8
HEAD_DIM = 64
N_HEADS = 8
W_HEADS = N_HEADS * HEAD_DIM
TOPK_MAX = 256
ROPE_THETA = 10000.0
N_GROUPS = 4
EXPERTS_PER_GROUP = 8
N_EXPERTS = N_GROUPS * EXPERTS_PER_GROUP
EPS = 1e-6
MASKED = -1e30

LANES = 128
VMEM_LIMIT = 56 * 1024 * 1024


def _cparams(sem):
    return pltpu.CompilerParams(dimension_semantics=sem, vmem_limit_bytes=VMEM_LIMIT)


def _mod_kernel(c_ref, w_ref, b_ref, o_ref):
    c = c_ref[...]
    ca = (c * jax.nn.sigmoid(c)).astype(BF16)
    o_ref[...] = jnp.dot(ca, w_ref[...].astype(BF16), preferred_element_type=F32) + b_ref[...]


def _modulation(c, ada_w, ada_b):
    B, D = c.shape
    n = ada_w.shape[1] // D
    return pl.pallas_call(
        _mod_kernel,
        grid=(n,),
        in_specs=[pl.BlockSpec((B, D), lambda j: (0, 0)),
                  pl.BlockSpec((D, D), lambda j: (0, j)),
                  pl.BlockSpec((1, D), lambda j: (0, j))],
        out_specs=pl.BlockSpec((B, D), lambda j: (0, j)),
        out_shape=jax.ShapeDtypeStruct((B, n * D), F32),
        compiler_params=_cparams(("arbitrary",)),
        name="adaln_mod",
    )(c, ada_w, ada_b.reshape(1, -1))


C_FQ, C_FK, C_DQ, C_IQ, C_FV = 0, 512, 1024, 1536, 2048
C_S1 = 2560
C_S2 = 2688
C_GATE = 2816
C_END = 4864


def _rope(x, cos, sin_lo, sin_hi):
    w = x.shape[-1]
    return x * cos + pltpu.roll(x, w - 32, 1) * sin_lo + pltpu.roll(x, 32, 1) * sin_hi


def _inproj_kernel(x_ref, mod_ref, g1_ref, w_ref, bd512_ref, bd128_ref, cos_ref, slo_ref, shi_ref,
                   gains_ref, kn128_ref, b2_ref, bg_ref,
                   fq_ref, fk_ref, dq_ref, iq_ref, fv_ref, s1_ref, s2_ref, gate_ref):
    x = x_ref[0]
    ms = jnp.mean(x * x, axis=-1, keepdims=True)
    h = x * lax.rsqrt(ms + EPS) * g1_ref[...]
    h = h * (1.0 + mod_ref[0, 1:2, :]) + mod_ref[0, 0:1, :]
    hb = h.astype(BF16)

    def proj(lo, hi):
        return jnp.dot(hb, w_ref[:, lo:hi], preferred_element_type=F32)

    def head_norm(y, gain):
        msq = jnp.dot((y * y).astype(BF16), bd512_ref[...], preferred_element_type=F32)
        return y * lax.rsqrt(msq + EPS) * gain

    cos1, slo1, shi1 = cos_ref[0], slo_ref[0], shi_ref[0]
    cos4 = jnp.concatenate([cos1] * 4, axis=1)
    slo4 = jnp.concatenate([slo1] * 4, axis=1)
    shi4 = jnp.concatenate([shi1] * 4, axis=1)

    fq_ref[0] = head_norm(proj(C_FQ, C_FQ + 512), gains_ref[0:1, :]).astype(BF16)
    fk_ref[0] = head_norm(proj(C_FK, C_FK + 512), gains_ref[1:2, :]).astype(BF16)
    dq = head_norm(proj(C_DQ, C_DQ + 512), gains_ref[2:3, :])
    dq_ref[0] = _rope(dq, cos4, slo4, shi4).astype(BF16)
    iq_ref[0] = _rope(proj(C_IQ, C_IQ + 512), cos4, slo4, shi4).astype(BF16)
    fv_ref[0] = proj(C_FV, C_FV + 512).astype(BF16)

    s1 = proj(C_S1, C_S1 + 128)
    msq = jnp.dot((s1 * s1).astype(BF16), bd128_ref[...], preferred_element_type=F32)
    lane = lax.broadcasted_iota(jnp.int32, s1.shape, 1)
    s1 = jnp.where(lane < HEAD_DIM, s1 * lax.rsqrt(msq + EPS) * kn128_ref[...], s1)
    s1_ref[0] = _rope(s1, cos1, slo1, shi1).astype(BF16)

    s2 = proj(C_S2, C_S2 + 128)
    z = s2 + b2_ref[...]
    logsig = jnp.minimum(z, 0.0) - jnp.log(1.0 + jnp.exp(-jnp.abs(z)))
    is_fgt = (lane >= HEAD_DIM) & (lane < HEAD_DIM + N_HEADS)
    s2_ref[0] = jnp.where(is_fgt, logsig, s2)

    gate_ref[0] = jax.nn.sigmoid(proj(C_GATE, C_END) + bg_ref[...]).astype(BF16)


def _in_projection(x, mod3, norm1_g, w_perm, bd512, bd128, cos, slo, shi, gains, kn128, b2, bg, tm):
    B, S, D = x.shape
    tok = lambda w: pl.BlockSpec((1, tm, w), lambda b, i: (b, i, 0))
    const = lambda shape: pl.BlockSpec(shape, lambda b, i: (0,) * len(shape))
    out_shapes = [jax.ShapeDtypeStruct((B, S, 512), BF16)] * 5 + [
        jax.ShapeDtypeStruct((B, S, 128), BF16),
        jax.ShapeDtypeStruct((B, S, 128), F32),
        jax.ShapeDtypeStruct((B, S, 2 * D), BF16)]
    return pl.pallas_call(
        _inproj_kernel,
        grid=(B, S // tm),
        in_specs=[tok(D),
                  pl.BlockSpec((1, 6, D), lambda b, i: (b, 0, 0)),
                  const((1, D)),
                  const(w_perm.shape),
                  const((512, 512)), const((128, 128)),
                  tok(128), tok(128), tok(128),
                  const((3, 512)), const((1, 128)), const((1, 128)), const((1, 2 * D))],
        out_specs=[tok(512)] * 5 + [tok(128), tok(128), tok(2 * D)],
        out_shape=out_shapes,
        compiler_params=_cparams(("parallel", "parallel")),
        name="in_projection",
    )(x, mod3, norm1_g, w_perm, bd512, bd128, cos, slo, shi, gains, kn128, b2, bg)


def _cumsum_kernel(x_ref, o_ref):
    x = x_ref[0]
    n = x.shape[-1]
    pos = lax.broadcasted_iota(jnp.int32, x.shape, 1)
    shift = 1
    while shift < n:
        x = x + jnp.where(pos >= shift, pltpu.roll(x, shift, 1), 0.0)
        shift *= 2
    hi = x.astype(BF16).astype(F32)
    mid = (x - hi).astype(BF16).astype(F32)
    o_ref[0, 0] = hi
    o_ref[0, 1] = mid
    o_ref[0, 2] = (x - hi - mid).astype(BF16).astype(F32)


def _seq_cumsum(logf_t):
    B, H, S = logf_t.shape
    return pl.pallas_call(
        _cumsum_kernel,
        grid=(B,),
        in_specs=[pl.BlockSpec((1, H, S), lambda b: (b, 0, 0))],
        out_specs=pl.BlockSpec((1, 3, H, S), lambda b: (b, 0, 0, 0)),
        out_shape=jax.ShapeDtypeStruct((B, 3, H, S), F32),
        compiler_params=_cparams(("parallel",)),
        name="forget_cumsum",
    )(logf_t)


KC = 256
SUB = 8


def _fold_rows(a, op):
    return op(a.reshape(a.shape[0] // SUB, SUB, a.shape[1]), axis=0)


def _softmax_pv(nch, s_ref, acc_ref, vt_at, m_all, o_ref):
    Q = o_ref.shape[-1]
    acc_ref[...] = jnp.zeros_like(acc_ref)

    def body(c, lsum):
        off = pl.multiple_of(c * KC, KC)
        new = []
        for hh in range(N_HEADS):
            p = jnp.exp(s_ref[hh, pl.ds(off, KC), :] - m_all[hh])
            new.append(lsum[hh] + _fold_rows(p, jnp.sum))
            acc_ref[hh] += jnp.dot(vt_at(hh, off), p.astype(BF16), preferred_element_type=F32)
        return tuple(new)

    lsum = lax.fori_loop(0, nch, body, tuple(jnp.zeros((SUB, Q), F32) for _ in range(N_HEADS)))
    for hh in range(N_HEADS):
        l = jnp.sum(lsum[hh], axis=0, keepdims=True)
        o_ref[0, hh * HEAD_DIM:(hh + 1) * HEAD_DIM, :] = (acc_ref[hh] / l).astype(BF16)


def _fox_kernel(ka_ref, qt_ref, vt_ref, o_ref, s_ref, acc_ref):
    i = pl.program_id(1)
    Q = o_ref.shape[-1]
    row = lax.broadcasted_iota(jnp.int32, (HEAD_DIM, Q), 0)
    aug = jnp.where(row < 3, -1.0, 0.0).astype(BF16)
    qa = [jnp.concatenate([qt_ref[0, hh * HEAD_DIM:(hh + 1) * HEAD_DIM, :], aug], axis=0)
          for hh in range(N_HEADS)]

    def scores(c, mx, bias):
        off = pl.multiple_of(c * KC, KC)
        new = []
        for hh in range(N_HEADS):
            s = jnp.dot(ka_ref[0, hh, pl.ds(off, KC), :], qa[hh], preferred_element_type=F32)
            if bias is not None:
                s = s + bias
            s_ref[hh, pl.ds(off, KC), :] = s
            new.append(jnp.maximum(mx[hh], _fold_rows(s, jnp.max)))
        return tuple(new)

    mx = tuple(jnp.full((SUB, Q), MASKED, F32) for _ in range(N_HEADS))
    mx = lax.fori_loop(0, i, lambda c, m: scores(c, m, None), mx)
    kk = lax.broadcasted_iota(jnp.int32, (KC, Q), 0)
    qq = lax.broadcasted_iota(jnp.int32, (KC, Q), 1)
    mx = scores(i, mx, jnp.where(kk <= qq, 0.0, MASKED))
    m_all = [jnp.max(m, axis=0, keepdims=True) for m in mx]
    _softmax_pv(i + 1, s_ref, acc_ref, lambda hh, off: vt_ref[0, hh * HEAD_DIM:(hh + 1) * HEAD_DIM, pl.ds(off, KC)],
                m_all, o_ref)


def _fox_attention(k_aug, qt, vt):
    B, H, S, Wa = k_aug.shape
    return pl.pallas_call(
        _fox_kernel,
        grid=(B, S // KC),
        in_specs=[pl.BlockSpec((1, H, S, Wa), lambda b, i: (b, 0, 0, 0)),
                  pl.BlockSpec((1, W_HEADS, KC), lambda b, i: (b, 0, i)),
                  pl.BlockSpec((1, W_HEADS, S), lambda b, i: (b, 0, 0))],
        out_specs=pl.BlockSpec((1, W_HEADS, KC), lambda b, i: (b, 0, i)),
        out_shape=jax.ShapeDtypeStruct((B, W_HEADS, S), BF16),
        scratch_shapes=[pltpu.VMEM((H, S, KC), F32), pltpu.VMEM((H, HEAD_DIM, KC), F32)],
        compiler_params=_cparams(("parallel", "arbitrary")),
        name="fox_attention",
    )(k_aug, qt, vt)


INT_MIN = -(2 ** 31)


def _dsa_kernel(ik_ref, dk_ref, dvt_ref, iqt_ref, dqt_ref, iwt_ref, o_ref, key_ref, s_ref, acc_ref, *, topk):
    i = pl.program_id(1)
    Q = o_ref.shape[-1]
    nch = i + 1
    sub_k = lax.broadcasted_iota(jnp.int32, (KC, Q), 0)
    sub_r = lax.broadcasted_iota(jnp.int32, (CHUNK, Q), 0)
    q_chunk = (i * Q + lax.broadcasted_iota(jnp.int32, (CHUNK, Q), 1)) >> CHUNK_SHIFT
    iqts = [iqt_ref[0, hh * HEAD_DIM:(hh + 1) * HEAD_DIM, :] for hh in range(N_HEADS)]
    iws = [iwt_ref[0, hh:hh + 1, :] for hh in range(N_HEADS)]

    def score_chunk(c, _):
        for r in range(KC // CHUNK):
            off = pl.multiple_of(c * KC + r * CHUNK, CHUNK)
            ik = ik_ref[0, pl.ds(off, CHUNK), :]
            sc = jnp.zeros((CHUNK, Q), F32)
            for hh in range(N_HEADS):
                d = jnp.dot(ik, iqts[hh], preferred_element_type=F32)
                sc = sc + iws[hh] * jnp.maximum(d, 0.0)
            sc = sc + 0.0
            allowed = ((off + sub_r) >> CHUNK_SHIFT) <= q_chunk
            bits = pltpu.bitcast(sc, jnp.int32)
            key = bits ^ ((bits >> 31) & 0x7FFFFFFF)
            key_ref[pl.ds(off, CHUNK), :] = jnp.where(allowed, key, INT_MIN)
        return 0

    lax.fori_loop(0, nch, score_chunk, 0)

    def count(pred):
        def body(c, acc):
            off = pl.multiple_of(c * KC, KC)
            hit = pred(key_ref[pl.ds(off, KC), :], off + sub_k)
            return acc + _fold_rows(jnp.where(hit, 1.0, 0.0), jnp.sum)
        acc = lax.fori_loop(0, nch, body, jnp.zeros((SUB, Q), F32))
        return jnp.sum(acc, axis=0, keepdims=True)

    kf = jnp.float32(topk)

    n_nonneg = count(lambda k, _: k >= 0)
    top_half = n_nonneg >= kf
    thr = jnp.where(top_half, 0, INT_MIN).astype(jnp.int32)
    n_ge = jnp.where(top_half, n_nonneg, (nch * KC).astype(F32))
    for bit in range(30, -1, -1):
        cand = thr + jnp.int32(1 << bit)
        n = count(lambda k, _, cand=cand: k >= cand)
        take = n >= kf
        thr = jnp.where(take, cand, thr)
        n_ge = jnp.where(take, n, n_ge)

    excess = (n_ge > kf) & (thr > INT_MIN)

    @pl.when(jnp.max(jnp.where(excess, 1.0, 0.0)) > 0.0)
    def _():
        need = kf - count(lambda k, _: k > thr)
        last = jnp.zeros((1, Q), jnp.int32)
        nbits = int(np.ceil(np.log2(key_ref.shape[0]))) + 1
        for bit in range(nbits - 1, -1, -1):
            cand = last + jnp.int32(1 << bit)
            n = count(lambda k, idx, cand=cand: (k == thr) & (idx < cand))
            last = jnp.where(n < need, cand, last)

        def demote(c, _):
            off = pl.multiple_of(c * KC, KC)
            k = key_ref[pl.ds(off, KC), :]
            drop = excess & (k == thr) & (off + sub_k > last)
            key_ref[pl.ds(off, KC), :] = jnp.where(drop, thr - 1, k)
            return 0

        lax.fori_loop(0, nch, demote, 0)

    keep_from = jnp.maximum(thr, INT_MIN + 1)

    qts = [dqt_ref[0, hh * HEAD_DIM:(hh + 1) * HEAD_DIM, :] for hh in range(N_HEADS)]

    def scores(c, mx):
        off = pl.multiple_of(c * KC, KC)
        bias = jnp.where(key_ref[pl.ds(off, KC), :] >= keep_from, 0.0, MASKED)
        dk = dk_ref[0, pl.ds(off, KC), :]
        new = []
        for hh in range(N_HEADS):
            s = jnp.dot(dk, qts[hh], preferred_element_type=F32) + bias
            s_ref[hh, pl.ds(off, KC), :] = s
            new.append(jnp.maximum(mx[hh], _fold_rows(s, jnp.max)))
        return tuple(new)

    mx = lax.fori_loop(0, nch, scores, tuple(jnp.full((SUB, Q), MASKED, F32) for _ in range(N_HEADS)))
    m_all = [jnp.max(m, axis=0, keepdims=True) for m in mx]
    _softmax_pv(nch, s_ref, acc_ref, lambda hh, off: dvt_ref[0, :, pl.ds(off, KC)], m_all, o_ref)


def _dsa_attention(ik, dk, dvt, iqt, dqt, iwt, topk):
    B, S, Dh = ik.shape
    seq = pl.BlockSpec((1, S, Dh), lambda b, i: (b, 0, 0))
    qcols = lambda r: pl.BlockSpec((1, r, KC), lambda b, i: (b, 0, i))
    return pl.pallas_call(
        functools.partial(_dsa_kernel, topk=topk),
        grid=(B, S // KC),
        in_specs=[seq, seq,
                  pl.BlockSpec((1, Dh, S), lambda b, i: (b, 0, 0)),
                  qcols(W_HEADS), qcols(W_HEADS), qcols(N_HEADS)],
        out_specs=qcols(W_HEADS),
        out_shape=jax.ShapeDtypeStruct((B, W_HEADS, S), BF16),
        scratch_shapes=[pltpu.VMEM((S, KC), jnp.int32), pltpu.VMEM((N_HEADS, S, KC), F32),
                        pltpu.VMEM((N_HEADS, HEAD_DIM, KC), F32)],
        compiler_params=_cparams(("parallel", "arbitrary")),
        name="dsa_attention",
    )(ik, dk, dvt, iqt, dqt, iwt)


def _first(mask, lane):
    return jnp.min(jnp.where(mask, lane, LANES), axis=-1, keepdims=True)


def _post_kernel(of_ref, od_ref, gate_ref, x_ref, mod_ref, wpf_ref, wpd_ref, wo_ref, g2_ref, wr_ref, br_ref,
                 x1_ref, h2_ref, route_ref):
    D = x_ref.shape[-1]
    pf = jnp.dot(of_ref[0], wpf_ref[...], preferred_element_type=F32)
    pd = jnp.dot(od_ref[0], wpd_ref[...], preferred_element_type=F32)
    merged = gate_ref[0, :, :D].astype(F32) * pf + gate_ref[0, :, D:].astype(F32) * pd
    y = jnp.dot(merged.astype(BF16), wo_ref[...], preferred_element_type=F32)
    x1 = x_ref[0] + mod_ref[0, 2:3, :] * y
    x1_ref[0] = x1

    ms = jnp.mean(x1 * x1, axis=-1, keepdims=True)
    h2 = x1 * lax.rsqrt(ms + EPS) * g2_ref[...]
    h2 = h2 * (1.0 + mod_ref[0, 4:5, :]) + mod_ref[0, 3:4, :]
    hb = h2.astype(BF16)
    h2_ref[0] = h2

    logits = jnp.dot(hb, wr_ref[...], preferred_element_type=F32) + br_ref[...]
    lane = lax.broadcasted_iota(jnp.int32, logits.shape, 1)
    is_grp = lane < N_GROUPS
    gl = jnp.where(is_grp, logits, -jnp.inf)
    gmax = jnp.max(gl, axis=-1, keepdims=True)
    g_idx = _first(gl == gmax, lane)
    g_w = 1.0 / jnp.sum(jnp.exp(gl - gmax), axis=-1, keepdims=True)

    e_lo = N_GROUPS + g_idx * EXPERTS_PER_GROUP
    in_grp = (lane >= e_lo) & (lane < e_lo + EXPERTS_PER_GROUP)
    el = jnp.where(in_grp, logits, -jnp.inf)
    emax = jnp.max(el, axis=-1, keepdims=True)
    ee = jnp.exp(el - emax)
    prob = ee / jnp.sum(ee, axis=-1, keepdims=True)
    prob = jnp.where(in_grp, prob, -1.0)
    p0 = jnp.max(prob, axis=-1, keepdims=True)
    l0 = _first(prob == p0, lane)
    rest = jnp.where(lane == l0, -1.0, prob)
    p1 = jnp.max(rest, axis=-1, keepdims=True)
    l1 = _first(rest == p1, lane)
    psum = p0 + p1
    w0 = g_w * (p0 / psum)
    w1 = g_w * (p1 / psum)
    e0 = (l0 - N_GROUPS).astype(F32)
    e1 = (l1 - N_GROUPS).astype(F32)
    route_ref[0] = jnp.where(lane == 0, e0, jnp.where(lane == 1, e1, jnp.where(lane == 2, w0,
                             jnp.where(lane == 3, w1, 0.0))))


def _post_attention(of, od, gates, x, mod3, wpf, wpd, wo, g2, wr, br, tm):
    B, S, D = x.shape
    tok = lambda w: pl.BlockSpec((1, tm, w), lambda b, i: (b, i, 0))
    const = lambda shape: pl.BlockSpec(shape, lambda b, i: (0,) * len(shape))
    return pl.pallas_call(
        _post_kernel,
        grid=(B, S // tm),
        in_specs=[tok(W_HEADS), tok(W_HEADS), tok(2 * D), tok(D),
                  pl.BlockSpec((1, 6, D), lambda b, i: (b, 0, 0)),
                  const(wpf.shape), const(wpd.shape), const(wo.shape),
                  const((1, D)), const((D, LANES)), const((1, LANES))],
        out_specs=[tok(D), tok(D), tok(LANES)],
        out_shape=[jax.ShapeDtypeStruct((B, S, D), F32),
                   jax.ShapeDtypeStruct((B, S, D), F32),
                   jax.ShapeDtypeStruct((B, S, LANES), F32)],
        compiler_params=_cparams(("parallel", "parallel")),
        name="merge_out_router",
    )(of, od, gates, x, mod3, wpf, wpd, wo, g2, wr, br)


def _rank_kernel(route_ref, tri_ref, rank_ref, count_ref, carry_ref):
    @pl.when(pl.program_id(0) == 0)
    def _():
        carry_ref[...] = jnp.zeros_like(carry_ref)

    r = route_ref[...]
    lane = lax.broadcasted_iota(jnp.int32, r.shape, 1).astype(F32)
    hot0 = lane == r[:, 0:1]
    hot1 = lane == r[:, 1:2]
    hits = jnp.where(hot0 | hot1, 1.0, 0.0)
    incl = jnp.dot(tri_ref[...], hits.astype(BF16), preferred_element_type=F32)
    before = incl - hits + carry_ref[...]
    r0 = jnp.sum(jnp.where(hot0, before, 0.0), axis=-1, keepdims=True)
    r1 = jnp.sum(jnp.where(hot1, before, 0.0), axis=-1, keepdims=True)
    rank_ref[...] = jnp.where(lane == 0.0, r0, jnp.where(lane == 1.0, r1, 0.0))
    carry_ref[...] = carry_ref[...] + jnp.sum(hits, axis=0, keepdims=True)
    count_ref[...] = carry_ref[...]


def _expert_ranks(route, tm):
    N = route.shape[0]
    tri = jnp.asarray(np.tril(np.ones((tm, tm), np.float32)), BF16)
    return pl.pallas_call(
        _rank_kernel,
        grid=(N // tm,),
        in_specs=[pl.BlockSpec((tm, LANES), lambda i: (i, 0)),
                  pl.BlockSpec((tm, tm), lambda i: (0, 0))],
        out_specs=[pl.BlockSpec((tm, LANES), lambda i: (i, 0)),
                   pl.BlockSpec((1, LANES), lambda i: (0, 0))],
        out_shape=[jax.ShapeDtypeStruct((N, LANES), F32), jax.ShapeDtypeStruct((1, LANES), F32)],
        scratch_shapes=[pltpu.VMEM((1, LANES), F32)],
        compiler_params=_cparams(("arbitrary",)),
        name="expert_ranks",
    )(route, tri)


def _dispatch_kernel(pos_ref, h_ref, xs_in_ref, xs_ref, sem, *, tm):
    del xs_in_ref

    def copy(r, slot):
        return pltpu.make_async_copy(h_ref.at[pl.ds(r, 1), :],
                                     xs_ref.at[pl.ds(pos_ref[0, slot, r], 1), :], sem)

    def issue(r, _):
        copy(r, 0).start()
        copy(r, 1).start()
        return 0

    lax.fori_loop(0, tm, issue, 0, unroll=DMA_UNROLL)
    for _ in range(2):
        pltpu.make_async_copy(h_ref, xs_ref.at[pl.ds(0, tm), :], sem).wait()


def _dispatch(h2, pos3, n_rows, tm):
    N, D = h2.shape
    xs0 = jnp.zeros((n_rows, D), F32)
    return pl.pallas_call(
        functools.partial(_dispatch_kernel, tm=tm),
        grid=(N // tm,),
        in_specs=[pl.BlockSpec((1, 2, tm), lambda i: (i, 0, 0), memory_space=pltpu.SMEM),
                  pl.BlockSpec((tm, D), lambda i: (i, 0)),
                  pl.BlockSpec(memory_space=pl.ANY)],
        out_specs=pl.BlockSpec(memory_space=pl.ANY),
        out_shape=jax.ShapeDtypeStruct((n_rows, D), F32),
        scratch_shapes=[pltpu.SemaphoreType.DMA(())],
        input_output_aliases={2: 0},
        compiler_params=_cparams(("arbitrary",)),
        name="moe_dispatch",
    )(pos3, h2, xs0)


def _expert_kernel(te_ref, nt_ref, xs_ref, w1_ref, w3_ref, w2_ref, y_ref, w1b, w3b, w2b):
    g = pl.program_id(0)
    used = g < nt_ref[0]
    new_expert = (g == 0) | (te_ref[g] != te_ref[jnp.maximum(g - 1, 0)])

    @pl.when(used & new_expert)
    def _():
        w1b[...] = w1_ref[0].astype(BF16)
        w3b[...] = w3_ref[0].astype(BF16)
        w2b[...] = w2_ref[0].astype(BF16)

    @pl.when(used)
    def _():
        xb = xs_ref[...].astype(BF16)
        a = jnp.dot(xb, w1b[...], preferred_element_type=F32)
        b = jnp.dot(xb, w3b[...], preferred_element_type=F32)
        hmid = (a * jax.nn.sigmoid(a) * b).astype(BF16)
        y_ref[...] = jnp.dot(hmid, w2b[...], preferred_element_type=F32)

    @pl.when(g >= nt_ref[0])
    def _():
        y_ref[...] = jnp.zeros_like(y_ref)


def _experts(tile_expert, n_tiles_used, xs, w1, w3, w2, tg):
    P, D = xs.shape
    E, _, De = w1.shape
    grid_spec = pltpu.PrefetchScalarGridSpec(
        num_scalar_prefetch=2,
        grid=(P // tg,),
        in_specs=[pl.BlockSpec((tg, D), lambda g, te, nt: (g, 0)),
                  pl.BlockSpec((1, D, De), lambda g, te, nt: (te[g], 0, 0)),
                  pl.BlockSpec((1, D, De), lambda g, te, nt: (te[g], 0, 0)),
                  pl.BlockSpec((1, De, D), lambda g, te, nt: (te[g], 0, 0))],
        out_specs=pl.BlockSpec((tg, D), lambda g, te, nt: (g, 0)),
        scratch_shapes=[pltpu.VMEM((D, De), BF16), pltpu.VMEM((D, De), BF16), pltpu.VMEM((De, D), BF16)],
    )
    return pl.pallas_call(
        _expert_kernel,
        grid_spec=grid_spec,
        out_shape=jax.ShapeDtypeStruct((P, D), F32),
        compiler_params=_cparams(("arbitrary",)),
        name="moe_experts",
    )(tile_expert, n_tiles_used, xs, w1, w3, w2)


def _combine_kernel(pos_ref, y_ref, x1_ref, route_ref, gt_ref, o_ref, buf0, buf1, sem, *, tm):
    def copy(r, slot, buf):
        return pltpu.make_async_copy(y_ref.at[pl.ds(pos_ref[0, slot, r], 1), :],
                                     buf.at[pl.ds(r, 1), :], sem)

    def issue(r, _):
        copy(r, 0, buf0).start()
        copy(r, 1, buf1).start()
        return 0

    lax.fori_loop(0, tm, issue, 0, unroll=DMA_UNROLL)
    for buf in (buf0, buf1):
        pltpu.make_async_copy(y_ref.at[pl.ds(0, tm), :], buf, sem).wait()
    w0 = route_ref[:, 2:3]
    w1 = route_ref[:, 3:4]
    y = buf0[...] * w0 + buf1[...] * w1
    o_ref[...] = x1_ref[...] + gt_ref[0] * y


def _combine(pos3, y, x1, route, gt2, tm, S):
    N, D = x1.shape
    per_b = S // tm
    return pl.pallas_call(
        functools.partial(_combine_kernel, tm=tm),
        grid=(N // tm,),
        in_specs=[pl.BlockSpec((1, 2, tm), lambda i: (i, 0, 0), memory_space=pltpu.SMEM),
                  pl.BlockSpec(memory_space=pl.ANY),
                  pl.BlockSpec((tm, D), lambda i: (i, 0)),
                  pl.BlockSpec((tm, LANES), lambda i: (i, 0)),
                  pl.BlockSpec((1, 1, D), lambda i: (i // per_b, 0, 0))],
        out_specs=pl.BlockSpec((tm, D), lambda i: (i, 0)),
        out_shape=jax.ShapeDtypeStruct((N, D), F32),
        scratch_shapes=[pltpu.VMEM((tm, D), F32), pltpu.VMEM((tm, D), F32), pltpu.SemaphoreType.DMA(())],
        compiler_params=_cparams(("arbitrary",)),
        name="moe_combine",
    )(pos3, y, x1, route, gt2)


def _rope_tables(positions):
    half = HEAD_DIM // 2
    inv = ROPE_THETA ** (-jnp.arange(half, dtype=F32) / half)
    ang = positions.astype(F32)[..., None] * inv
    cos, sin = jnp.cos(ang), jnp.sin(ang)
    zero = jnp.zeros_like(sin)
    cos128 = jnp.concatenate([cos] * 4, axis=-1)
    sin_lo = jnp.concatenate([-sin, zero] * 2, axis=-1)
    sin_hi = jnp.concatenate([zero, sin] * 2, axis=-1)
    return cos128, sin_lo, sin_hi


def _block_diag_mean(width):
    blk = np.kron(np.eye(width // HEAD_DIM, dtype=np.float32), np.full((HEAD_DIM, HEAD_DIM), 1.0 / HEAD_DIM, np.float32))
    return jnp.asarray(blk, BF16)


def _layer(x, c_mod, positions, norm1_g, norm2_g, w_in, b_fgt, b_gate, qn_fox, kn_fox, qn_dsa, kn_dsa,
           w_proj_fox, w_proj_dsa, w_out, r_w_grp, r_b_grp, r_w_exp, r_b_exp, w1, w3, w2):
    B, S, D = x.shape
    N = B * S
    topk = min(TOPK_MAX, S // 4)
    tm = min(512, S)
    scale = HEAD_DIM ** -0.5
    mod3 = c_mod.reshape(B, 6, D)

    o = np.cumsum([0, 512, 512, 512, 8, 512, 64, 64, 512, 64, 8, D, D])
    seg = lambda k: w_in[:, o[k]:o[k + 1]]
    zpad = jnp.zeros((D, LANES - HEAD_DIM - 2 * N_HEADS), F32)
    w_perm = jnp.concatenate([seg(0), seg(1), seg(4), seg(7), seg(2),
                              seg(5), seg(8),
                              seg(6), seg(3), seg(9), zpad,
                              seg(10), seg(11)], axis=1).astype(BF16)
    gains = jnp.stack([jnp.tile(qn_fox * scale, N_HEADS), jnp.tile(kn_fox, N_HEADS),
                       jnp.tile(qn_dsa * scale, N_HEADS)])
    kn128 = jnp.concatenate([kn_dsa, jnp.ones((HEAD_DIM,), F32)]).reshape(1, LANES)
    b2 = jnp.concatenate([jnp.zeros((HEAD_DIM,), F32), b_fgt,
                          jnp.zeros((LANES - HEAD_DIM - N_HEADS,), F32)]).reshape(1, LANES)
    cos128, sin_lo, sin_hi = _rope_tables(positions)

    fq, fk, dq, iq, fv, s1, s2, gates = _in_projection(
        x, mod3, norm1_g.reshape(1, D), w_perm, _block_diag_mean(512), _block_diag_mean(128),
        cos128, sin_lo, sin_hi, gains, kn128, b2, b_gate.reshape(1, 2 * D), tm)

    logf_t = jnp.transpose(s2[:, :, HEAD_DIM:HEAD_DIM + N_HEADS], (0, 2, 1))
    f_parts = jnp.transpose(_seq_cumsum(logf_t), (0, 2, 3, 1)).astype(BF16)
    k_heads = jnp.transpose(fk.reshape(B, S, N_HEADS, HEAD_DIM), (0, 2, 1, 3))
    k_aug = jnp.concatenate([k_heads, f_parts, jnp.zeros((B, N_HEADS, S, HEAD_DIM - 3), BF16)], axis=-1)
    oft = _fox_attention(k_aug, jnp.transpose(fq, (0, 2, 1)), jnp.transpose(fv, (0, 2, 1)))
    of = jnp.transpose(oft, (0, 2, 1))

    dk, ik = s1[:, :, :HEAD_DIM], s1[:, :, HEAD_DIM:]
    dvt = jnp.transpose(s2[:, :, :HEAD_DIM].astype(BF16), (0, 2, 1))
    iwt = jnp.transpose(s2[:, :, HEAD_DIM + N_HEADS:HEAD_DIM + 2 * N_HEADS], (0, 2, 1))
    odt = _dsa_attention(ik, dk, dvt, jnp.transpose(iq, (0, 2, 1)), jnp.transpose(dq, (0, 2, 1)), iwt, topk)
    od = jnp.transpose(odt, (0, 2, 1))

    wr = jnp.concatenate([r_w_grp, r_w_exp, jnp.zeros((D, LANES - N_GROUPS - N_EXPERTS), F32)], axis=1).astype(BF16)
    br = jnp.concatenate([r_b_grp, r_b_exp, jnp.zeros((LANES - N_GROUPS - N_EXPERTS,), F32)]).reshape(1, LANES)
    x1, h2, route = _post_attention(of, od, gates, x, mod3, w_proj_fox.astype(BF16), w_proj_dsa.astype(BF16),
                                    w_out.astype(BF16), norm2_g.reshape(1, D), wr, br, tm)
    x1, h2, route = x1.reshape(N, D), h2.reshape(N, D), route.reshape(N, LANES)

    tg = 512 if N * 2 >= 512 * N_EXPERTS else 128
    ranks, counts = _expert_ranks(route, tm)
    counts = counts[0, :N_EXPERTS].astype(jnp.int32)
    padded = ((counts + tg - 1) // tg) * tg
    ends = jnp.cumsum(padded)
    starts = ends - padded
    e01 = route[:, :2].astype(jnp.int32)
    start_of = jnp.sum(jnp.where(e01[..., None] == jnp.arange(N_EXPERTS, dtype=jnp.int32), starts, 0), axis=-1)
    pos = start_of + ranks[:, :2].astype(jnp.int32)
    n_rows = N * 2 + N_EXPERTS * tg
    n_tiles = n_rows // tg
    tile_start = jnp.arange(n_tiles, dtype=jnp.int32) * tg
    tile_expert = jnp.minimum(jnp.sum((ends[None, :] <= tile_start[:, None]).astype(jnp.int32), axis=1),
                              N_EXPERTS - 1)
    n_used = (ends[-1] // tg).astype(jnp.int32).reshape(1)

    td = min(256, S)
    pos3 = jnp.transpose(pos.reshape(N // td, td, 2), (0, 2, 1))
    xs = _dispatch(h2, pos3, n_rows, td)
    y = _experts(tile_expert, n_used, xs, w1, w3, w2, tg)
    out = _combine(pos3, y, x1, route, mod3[:, 5:6, :], td, S)
    return out.reshape(B, S, D)


def kernel(x, c, positions, ada_w, ada_b, norm1_g, norm2_g, w_in, b_fgt, b_gate, qn_fox, kn_fox, qn_dsa, kn_dsa, w_proj_fox, w_proj_dsa, w_out, router_w_grp, router_b_grp, router_w_exp, router_b_exp, exp_w1, exp_w3, exp_w2):
    for l in range(ada_w.shape[0]):
        c_mod = _modulation(c, ada_w[l], ada_b[l])
        x = _layer(x, c_mod, positions, norm1_g[l], norm2_g[l], w_in[l], b_fgt[l], b_gate[l],
                   qn_fox[l], kn_fox[l], qn_dsa[l], kn_dsa[l], w_proj_fox[l], w_proj_dsa[l], w_out[l],
                   router_w_grp[l], router_b_grp[l], router_w_exp[l], router_b_exp[l],
                   exp_w1[l], exp_w3[l], exp_w2[l])
    return x
```

```python
import functools

import jax
import jax.numpy as jnp
import numpy as np
from jax import lax
from jax.experimental import pallas as pl
from jax.experimental.pallas import tpu as pltpu

F32 = jnp.float32
BF16 = jnp.bfloat16

CHUNK = 64
CHUNK_SHIFT = 6
DMA_UNROLL = 8
HEAD_DIM = 64
N_HEADS = 8
W_HEADS = N_HEADS * HEAD_DIM
TOPK_MAX = 256
ROPE_THETA = 10000.0
N_GROUPS = 4
EXPERTS_PER_GROUP = 8
N_EXPERTS = N_GROUPS * EXPERTS_PER_GROUP
EPS = 1e-6
MASKED = -1e30

LANES = 128
VMEM_LIMIT = 56 * 1024 * 1024


def _cparams(sem):
    return pltpu.CompilerParams(dimension_semantics=sem, vmem_limit_bytes=VMEM_LIMIT)


def _mod_kernel(c_ref, w_ref, b_ref, o_ref):
    c = c_ref[...]
    ca = (c * jax.nn.sigmoid(c)).astype(BF16)
    o_ref[...] = jnp.dot(ca, w_ref[...].astype(BF16), preferred_element_type=F32) + b_ref[...]


def _modulation(c, ada_w, ada_b):
    B, D = c.shape
    n = ada_w.shape[1] // D
    return pl.pallas_call(
        _mod_kernel,
        grid=(n,),
        in_specs=[pl.BlockSpec((B, D), lambda j: (0, 0)),
                  pl.BlockSpec((D, D), lambda j: (0, j)),
                  pl.BlockSpec((1, D), lambda j: (0, j))],
        out_specs=pl.BlockSpec((B, D), lambda j: (0, j)),
        out_shape=jax.ShapeDtypeStruct((B, n * D), F32),
        compiler_params=_cparams(("arbitrary",)),
        name="adaln_mod",
    )(c, ada_w, ada_b.reshape(1, -1))


C_FQ, C_FK, C_DQ, C_IQ, C_FV = 0, 512, 1024, 1536, 2048
C_S1 = 2560
C_S2 = 2688
C_GATE = 2816
C_END = 4864


def _rope(x, cos, sin_lo, sin_hi):
    w = x.shape[-1]
    return x * cos + pltpu.roll(x, w - 32, 1) * sin_lo + pltpu.roll(x, 32, 1) * sin_hi


def _inproj_kernel(x_ref, mod_ref, g1_ref, w_ref, bd512_ref, bd128_ref, cos_ref, slo_ref, shi_ref,
                   gains_ref, kn128_ref, b2_ref, bg_ref,
                   fq_ref, fk_ref, dq_ref, iq_ref, fv_ref, s1_ref, s2_ref, gate_ref):
    x = x_ref[0]
    ms = jnp.mean(x * x, axis=-1, keepdims=True)
    h = x * lax.rsqrt(ms + EPS) * g1_ref[...]
    h = h * (1.0 + mod_ref[0, 1:2, :]) + mod_ref[0, 0:1, :]
    hb = h.astype(BF16)

    def proj(lo, hi):
        return jnp.dot(hb, w_ref[:, lo:hi], preferred_element_type=F32)

    def head_norm(y, gain):
        msq = jnp.dot((y * y).astype(BF16), bd512_ref[...], preferred_element_type=F32)
        return y * lax.rsqrt(msq + EPS) * gain

    cos1, slo1, shi1 = cos_ref[0], slo_ref[0], shi_ref[0]
    cos4 = jnp.concatenate([cos1] * 4, axis=1)
    slo4 = jnp.concatenate([slo1] * 4, axis=1)
    shi4 = jnp.concatenate([shi1] * 4, axis=1)

    fq_ref[0] = head_norm(proj(C_FQ, C_FQ + 512), gains_ref[0:1, :]).astype(BF16)
    fk_ref[0] = head_norm(proj(C_FK, C_FK + 512), gains_ref[1:2, :]).astype(BF16)
    dq = head_norm(proj(C_DQ, C_DQ + 512), gains_ref[2:3, :])
    dq_ref[0] = _rope(dq, cos4, slo4, shi4).astype(BF16)
    iq_ref[0] = _rope(proj(C_IQ, C_IQ + 512), cos4, slo4, shi4).astype(BF16)
    fv_ref[0] = proj(C_FV, C_FV + 512).astype(BF16)

    s1 = proj(C_S1, C_S1 + 128)
    msq = jnp.dot((s1 * s1).astype(BF16), bd128_ref[...], preferred_element_type=F32)
    lane = lax.broadcasted_iota(jnp.int32, s1.shape, 1)
    s1 = jnp.where(lane < HEAD_DIM, s1 * lax.rsqrt(msq + EPS) * kn128_ref[...], s1)
    s1_ref[0] = _rope(s1, cos1, slo1, shi1).astype(BF16)

    s2 = proj(C_S2, C_S2 + 128)
    z = s2 + b2_ref[...]
    logsig = jnp.minimum(z, 0.0) - jnp.log(1.0 + jnp.exp(-jnp.abs(z)))
    is_fgt = (lane >= HEAD_DIM) & (lane < HEAD_DIM + N_HEADS)
    s2_ref[0] = jnp.where(is_fgt, logsig, s2)

    gate_ref[0] = jax.nn.sigmoid(proj(C_GATE, C_END) + bg_ref[...]).astype(BF16)


def _in_projection(x, mod3, norm1_g, w_perm, bd512, bd128, cos, slo, shi, gains, kn128, b2, bg, tm):
    B, S, D = x.shape
    tok = lambda w: pl.BlockSpec((1, tm, w), lambda b, i: (b, i, 0))
    const = lambda shape: pl.BlockSpec(shape, lambda b, i: (0,) * len(shape))
    out_shapes = [jax.ShapeDtypeStruct((B, S, 512), BF16)] * 5 + [
        jax.ShapeDtypeStruct((B, S, 128), BF16),
        jax.ShapeDtypeStruct((B, S, 128), F32),
        jax.ShapeDtypeStruct((B, S, 2 * D), BF16)]
    return pl.pallas_call(
        _inproj_kernel,
        grid=(B, S // tm),
        in_specs=[tok(D),
                  pl.BlockSpec((1, 6, D), lambda b, i: (b, 0, 0)),
                  const((1, D)),
                  const(w_perm.shape),
                  const((512, 512)), const((128, 128)),
                  tok(128), tok(128), tok(128),
                  const((3, 512)), const((1, 128)), const((1, 128)), const((1, 2 * D))],
        out_specs=[tok(512)] * 5 + [tok(128), tok(128), tok(2 * D)],
        out_shape=out_shapes,
        compiler_params=_cparams(("parallel", "parallel")),
        name="in_projection",
    )(x, mod3, norm1_g, w_perm, bd512, bd128, cos, slo, shi, gains, kn128, b2, bg)


def _cumsum_kernel(x_ref, o_ref):
    x = x_ref[0]
    n = x.shape[-1]
    pos = lax.broadcasted_iota(jnp.int32, x.shape, 1)
    shift = 1
    while shift < n:
        x = x + jnp.where(pos >= shift, pltpu.roll(x, shift, 1), 0.0)
        shift *= 2
    hi = x.astype(BF16).astype(F32)
    mid = (x - hi).astype(BF16).astype(F32)
    o_ref[0, 0] = hi
    o_ref[0, 1] = mid
    o_ref[0, 2] = (x - hi - mid).astype(BF16).astype(F32)


def _seq_cumsum(logf_t):
    B, H, S = logf_t.shape
    return pl.pallas_call(
        _cumsum_kernel,
        grid=(B,),
        in_specs=[pl.BlockSpec((1, H, S), lambda b: (b, 0, 0))],
        out_specs=pl.BlockSpec((1, 3, H, S), lambda b: (b, 0, 0, 0)),
        out_shape=jax.ShapeDtypeStruct((B, 3, H, S), F32),
        compiler_params=_cparams(("parallel",)),
        name="forget_cumsum",
    )(logf_t)


KC = 256
SUB = 8


def _fold_rows(a, op, ways=1):
    n = a.shape[0] // SUB
    a = a.reshape(n, SUB, a.shape[1])
    chains = [a[w] for w in range(ways)]
    for j in range(ways, n):
        chains[j % ways] = op(chains[j % ways], a[j])
    while len(chains) > 1:
        chains = [op(chains[2 * j], chains[2 * j + 1]) for j in range(len(chains) // 2)]
    return chains[0]


def _softmax_pv(nch, s_ref, acc_ref, vt_at, m_all, o_ref):
    Q = o_ref.shape[-1]
    acc_ref[...] = jnp.zeros_like(acc_ref)

    def body(c, lsum):
        off = pl.multiple_of(c * KC, KC)
        new = []
        for hh in range(N_HEADS):
            p = jnp.exp(s_ref[hh, pl.ds(off, KC), :] - m_all[hh])
            new.append(lsum[hh] + _fold_rows(p, jnp.add))
            acc_ref[hh] += jnp.dot(vt_at(hh, off), p.astype(BF16), preferred_element_type=F32)
        return tuple(new)

    lsum = lax.fori_loop(0, nch, body, tuple(jnp.zeros((SUB, Q), F32) for _ in range(N_HEADS)))
    for hh in range(N_HEADS):
        l = jnp.sum(lsum[hh], axis=0, keepdims=True)
        o_ref[0, hh * HEAD_DIM:(hh + 1) * HEAD_DIM, :] = (acc_ref[hh] / l).astype(BF16)


def _fox_kernel(ka_ref, qt_ref, vt_ref, o_ref, s_ref, acc_ref):
    i = pl.program_id(1)
    Q = o_ref.shape[-1]
    row = lax.broadcasted_iota(jnp.int32, (HEAD_DIM, Q), 0)
    aug = jnp.where(row < 3, -1.0, 0.0).astype(BF16)
    qa = [jnp.concatenate([qt_ref[0, hh * HEAD_DIM:(hh + 1) * HEAD_DIM, :], aug], axis=0)
          for hh in range(N_HEADS)]

    def scores(c, mx, bias):
        off = pl.multiple_of(c * KC, KC)
        new = []
        for hh in range(N_HEADS):
            s = jnp.dot(ka_ref[0, hh, pl.ds(off, KC), :], qa[hh], preferred_element_type=F32)
            if bias is not None:
                s = s + bias
            s_ref[hh, pl.ds(off, KC), :] = s
            new.append(jnp.maximum(mx[hh], _fold_rows(s, jnp.maximum)))
        return tuple(new)

    mx = tuple(jnp.full((SUB, Q), MASKED, F32) for _ in range(N_HEADS))
    mx = lax.fori_loop(0, i, lambda c, m: scores(c, m, None), mx)
    kk = lax.broadcasted_iota(jnp.int32, (KC, Q), 0)
    qq = lax.broadcasted_iota(jnp.int32, (KC, Q), 1)
    mx = scores(i, mx, jnp.where(kk <= qq, 0.0, MASKED))
    m_all = [jnp.max(m, axis=0, keepdims=True) for m in mx]
    _softmax_pv(i + 1, s_ref, acc_ref, lambda hh, off: vt_ref[0, hh * HEAD_DIM:(hh + 1) * HEAD_DIM, pl.ds(off, KC)],
                m_all, o_ref)


def _fox_attention(k_aug, qt, vt):
    B, H, S, Wa = k_aug.shape
    return pl.pallas_call(
        _fox_kernel,
        grid=(B, S // KC),
        in_specs=[pl.BlockSpec((1, H, S, Wa), lambda b, i: (b, 0, 0, 0)),
                  pl.BlockSpec((1, W_HEADS, KC), lambda b, i: (b, 0, i)),
                  pl.BlockSpec((1, W_HEADS, S), lambda b, i: (b, 0, 0))],
        out_specs=pl.BlockSpec((1, W_HEADS, KC), lambda b, i: (b, 0, i)),
        out_shape=jax.ShapeDtypeStruct((B, W_HEADS, S), BF16),
        scratch_shapes=[pltpu.VMEM((H, S, KC), F32), pltpu.VMEM((H, HEAD_DIM, KC), F32)],
        compiler_params=_cparams(("parallel", "arbitrary")),
        name="fox_attention",
    )(k_aug, qt, vt)


INT_MIN = -(2 ** 31)


def _dsa_kernel(ik_ref, dk_ref, dvt_ref, iqt_ref, dqt_ref, iwt_ref, o_ref, key_ref, s_ref, acc_ref, *, topk):
    i = pl.program_id(1)
    Q = o_ref.shape[-1]
    nch = i + 1
    sub_k = lax.broadcasted_iota(jnp.int32, (KC, Q), 0)
    sub_r = lax.broadcasted_iota(jnp.int32, (CHUNK, Q), 0)
    q_chunk = (i * Q + lax.broadcasted_iota(jnp.int32, (CHUNK, Q), 1)) >> CHUNK_SHIFT
    iqts = [iqt_ref[0, hh * HEAD_DIM:(hh + 1) * HEAD_DIM, :] for hh in range(N_HEADS)]
    iws = [iwt_ref[0, hh:hh + 1, :] for hh in range(N_HEADS)]

    def score_chunk(c, _):
        for r in range(KC // CHUNK):
            off = pl.multiple_of(c * KC + r * CHUNK, CHUNK)
            ik = ik_ref[0, pl.ds(off, CHUNK), :]
            sc = jnp.zeros((CHUNK, Q), F32)
            for hh in range(N_HEADS):
                d = jnp.dot(ik, iqts[hh], preferred_element_type=F32)
                sc = sc + iws[hh] * jnp.maximum(d, 0.0)
            sc = sc + 0.0
            allowed = ((off + sub_r) >> CHUNK_SHIFT) <= q_chunk
            bits = pltpu.bitcast(sc, jnp.int32)
            key = bits ^ ((bits >> 31) & 0x7FFFFFFF)
            key_ref[pl.ds(off, CHUNK), :] = jnp.where(allowed, key, INT_MIN)
        return 0

    lax.fori_loop(0, nch, score_chunk, 0)

    def count(pred):
        def body(c, acc):
            off = pl.multiple_of(c * KC, KC)
            hit = pred(key_ref[pl.ds(off, KC), :], off + sub_k)
            return acc + _fold_rows(jnp.where(hit, 1.0, 0.0), jnp.add, ways=4)
        acc = lax.fori_loop(0, nch, body, jnp.zeros((SUB, Q), F32))
        return jnp.sum(acc, axis=0, keepdims=True)

    kf = jnp.float32(topk)

    n_nonneg = count(lambda k, _: k >= 0)
    top_half = n_nonneg >= kf
    thr = jnp.where(top_half, 0, INT_MIN).astype(jnp.int32)
    n_ge = jnp.where(top_half, n_nonneg, (nch * KC).astype(F32))
    for bit in range(30, -1, -1):
        cand = thr + jnp.int32(1 << bit)
        n = count(lambda k, _, cand=cand: k >= cand)
        take = n >= kf
        thr = jnp.where(take, cand, thr)
        n_ge = jnp.where(take, n, n_ge)

    excess = (n_ge > kf) & (thr > INT_MIN)

    @pl.when(jnp.max(jnp.where(excess, 1.0, 0.0)) > 0.0)
    def _():
        need = kf - count(lambda k, _: k > thr)
        last = jnp.zeros((1, Q), jnp.int32)
        nbits = int(np.ceil(np.log2(key_ref.shape[0]))) + 1
        for bit in range(nbits - 1, -1, -1):
            cand = last + jnp.int32(1 << bit)
            n = count(lambda k, idx, cand=cand: (k == thr) & (idx < cand))
            last = jnp.where(n < need, cand, last)

        def demote(c, _):
            off = pl.multiple_of(c * KC, KC)
            k = key_ref[pl.ds(off, KC), :]
            drop = excess & (k == thr) & (off + sub_k > last)
            key_ref[pl.ds(off, KC), :] = jnp.where(drop, thr - 1, k)
            return 0

        lax.fori_loop(0, nch, demote, 0)

    keep_from = jnp.maximum(thr, INT_MIN + 1)

    qts = [dqt_ref[0, hh * HEAD_DIM:(hh + 1) * HEAD_DIM, :] for hh in range(N_HEADS)]

    def scores(c, mx):
        off = pl.multiple_of(c * KC, KC)
        bias = jnp.where(key_ref[pl.ds(off, KC), :] >= keep_from, 0.0, MASKED)
        dk = dk_ref[0, pl.ds(off, KC), :]
        new = []
        for hh in range(N_HEADS):
            s = jnp.dot(dk, qts[hh], preferred_element_type=F32) + bias
            s_ref[hh, pl.ds(off, KC), :] = s
            new.append(jnp.maximum(mx[hh], _fold_rows(s, jnp.maximum)))
        return tuple(new)

    mx = lax.fori_loop(0, nch, scores, tuple(jnp.full((SUB, Q), MASKED, F32) for _ in range(N_HEADS)))
    m_all = [jnp.max(m, axis=0, keepdims=True) for m in mx]
    _softmax_pv(nch, s_ref, acc_ref, lambda hh, off: dvt_ref[0, :, pl.ds(off, KC)], m_all, o_ref)


def _dsa_attention(ik, dk, dvt, iqt, dqt, iwt, topk):
    B, S, Dh = ik.shape
    seq = pl.BlockSpec((1, S, Dh), lambda b, i: (b, 0, 0))
    qcols = lambda r: pl.BlockSpec((1, r, KC), lambda b, i: (b, 0, i))
    return pl.pallas_call(
        functools.partial(_dsa_kernel, topk=topk),
        grid=(B, S // KC),
        in_specs=[seq, seq,
                  pl.BlockSpec((1, Dh, S), lambda b, i: (b, 0, 0)),
                  qcols(W_HEADS), qcols(W_HEADS), qcols(N_HEADS)],
        out_specs=qcols(W_HEADS),
        out_shape=jax.ShapeDtypeStruct((B, W_HEADS, S), BF16),
        scratch_shapes=[pltpu.VMEM((S, KC), jnp.int32), pltpu.VMEM((N_HEADS, S, KC), F32),
                        pltpu.VMEM((N_HEADS, HEAD_DIM, KC), F32)],
        compiler_params=_cparams(("parallel", "arbitrary")),
        name="dsa_attention",
    )(ik, dk, dvt, iqt, dqt, iwt)


def _first(mask, lane):
    return jnp.min(jnp.where(mask, lane, LANES), axis=-1, keepdims=True)


def _post_kernel(of_ref, od_ref, gate_ref, x_ref, mod_ref, wpf_ref, wpd_ref, wo_ref, g2_ref, wr_ref, br_ref,
                 x1_ref, h2_ref, route_ref):
    D = x_ref.shape[-1]
    pf = jnp.dot(of_ref[0], wpf_ref[...], preferred_element_type=F32)
    pd = jnp.dot(od_ref[0], wpd_ref[...], preferred_element_type=F32)
    merged = gate_ref[0, :, :D].astype(F32) * pf + gate_ref[0, :, D:].astype(F32) * pd
    y = jnp.dot(merged.astype(BF16), wo_ref[...], preferred_element_type=F32)
    x1 = x_ref[0] + mod_ref[0, 2:3, :] * y
    x1_ref[0] = x1

    ms = jnp.mean(x1 * x1, axis=-1, keepdims=True)
    h2 = x1 * lax.rsqrt(ms + EPS) * g2_ref[...]
    h2 = h2 * (1.0 + mod_ref[0, 4:5, :]) + mod_ref[0, 3:4, :]
    hb = h2.astype(BF16)
    h2_ref[0] = h2

    logits = jnp.dot(hb, wr_ref[...], preferred_element_type=F32) + br_ref[...]
    lane = lax.broadcasted_iota(jnp.int32, logits.shape, 1)
    is_grp = lane < N_GROUPS
    gl = jnp.where(is_grp, logits, -jnp.inf)
    gmax = jnp.max(gl, axis=-1, keepdims=True)
    g_idx = _first(gl == gmax, lane)
    g_w = 1.0 / jnp.sum(jnp.exp(gl - gmax), axis=-1, keepdims=True)

    e_lo = N_GROUPS + g_idx * EXPERTS_PER_GROUP
    in_grp = (lane >= e_lo) & (lane < e_lo + EXPERTS_PER_GROUP)
    el = jnp.where(in_grp, logits, -jnp.inf)
    emax = jnp.max(el, axis=-1, keepdims=True)
    ee = jnp.exp(el - emax)
    prob = ee / jnp.sum(ee, axis=-1, keepdims=True)
    prob = jnp.where(in_grp, prob, -1.0)
    p0 = jnp.max(prob, axis=-1, keepdims=True)
    l0 = _first(prob == p0, lane)
    rest = jnp.where(lane == l0, -1.0, prob)
    p1 = jnp.max(rest, axis=-1, keepdims=True)
    l1 = _first(rest == p1, lane)
    psum = p0 + p1
    w0 = g_w * (p0 / psum)
    w1 = g_w * (p1 / psum)
    e0 = (l0 - N_GROUPS).astype(F32)
    e1 = (l1 - N_GROUPS).astype(F32)
    route_ref[0] = jnp.where(lane == 0, e0, jnp.where(lane == 1, e1, jnp.where(lane == 2, w0,
                             jnp.where(lane == 3, w1, 0.0))))


def _post_attention(of, od, gates, x, mod3, wpf, wpd, wo, g2, wr, br, tm):
    B, S, D = x.shape
    tok = lambda w: pl.BlockSpec((1, tm, w), lambda b, i: (b, i, 0))
    const = lambda shape: pl.BlockSpec(shape, lambda b, i: (0,) * len(shape))
    return pl.pallas_call(
        _post_kernel,
        grid=(B, S // tm),
        in_specs=[tok(W_HEADS), tok(W_HEADS), tok(2 * D), tok(D),
                  pl.BlockSpec((1, 6, D), lambda b, i: (b, 0, 0)),
                  const(wpf.shape), const(wpd.shape), const(wo.shape),
                  const((1, D)), const((D, LANES)), const((1, LANES))],
        out_specs=[tok(D), tok(D), tok(LANES)],
        out_shape=[jax.ShapeDtypeStruct((B, S, D), F32),
                   jax.ShapeDtypeStruct((B, S, D), F32),
                   jax.ShapeDtypeStruct((B, S, LANES), F32)],
        compiler_params=_cparams(("parallel", "parallel")),
        name="merge_out_router",
    )(of, od, gates, x, mod3, wpf, wpd, wo, g2, wr, br)


def _rank_kernel(route_ref, tri_ref, rank_ref, count_ref, carry_ref):
    @pl.when(pl.program_id(0) == 0)
    def _():
        carry_ref[...] = jnp.zeros_like(carry_ref)

    r = route_ref[...]
    lane = lax.broadcasted_iota(jnp.int32, r.shape, 1).astype(F32)
    hot0 = lane == r[:, 0:1]
    hot1 = lane == r[:, 1:2]
    hits = jnp.where(hot0 | hot1, 1.0, 0.0)
    incl = jnp.dot(tri_ref[...], hits.astype(BF16), preferred_element_type=F32)
    before = incl - hits + carry_ref[...]
    r0 = jnp.sum(jnp.where(hot0, before, 0.0), axis=-1, keepdims=True)
    r1 = jnp.sum(jnp.where(hot1, before, 0.0), axis=-1, keepdims=True)
    rank_ref[...] = jnp.where(lane == 0.0, r0, jnp.where(lane == 1.0, r1, 0.0))
    carry_ref[...] = carry_ref[...] + jnp.sum(hits, axis=0, keepdims=True)
    count_ref[...] = carry_ref[...]


def _expert_ranks(route, tm):
    N = route.shape[0]
    tri = jnp.asarray(np.tril(np.ones((tm, tm), np.float32)), BF16)
    return pl.pallas_call(
        _rank_kernel,
        grid=(N // tm,),
        in_specs=[pl.BlockSpec((tm, LANES), lambda i: (i, 0)),
                  pl.BlockSpec((tm, tm), lambda i: (0, 0))],
        out_specs=[pl.BlockSpec((tm, LANES), lambda i: (i, 0)),
                   pl.BlockSpec((1, LANES), lambda i: (0, 0))],
        out_shape=[jax.ShapeDtypeStruct((N, LANES), F32), jax.ShapeDtypeStruct((1, LANES), F32)],
        scratch_shapes=[pltpu.VMEM((1, LANES), F32)],
        compiler_params=_cparams(("arbitrary",)),
        name="expert_ranks",
    )(route, tri)


def _dispatch_kernel(zstart_ref, zon_ref, nt_ref, pos_ref, h_ref, xs_ref, zbuf, sem, zsem, *, tm, tg):
    @pl.when(pl.program_id(0) == 0)
    def _():
        zbuf[...] = jnp.zeros_like(zbuf)

        def zero_tile(start):
            return pltpu.make_async_copy(zbuf, xs_ref.at[pl.ds(pl.multiple_of(start, tg), tg), :], zsem)

        n_tiles = xs_ref.shape[0] // tg
        for e in range(N_EXPERTS):
            pl.when(zon_ref[e] > 0)(lambda e=e: zero_tile(zstart_ref[e]).start())
        lax.fori_loop(nt_ref[0], n_tiles, lambda t, _: (zero_tile(t * tg).start(), 0)[1], 0)
        for e in range(N_EXPERTS):
            pl.when(zon_ref[e] > 0)(lambda e=e: zero_tile(zstart_ref[e]).wait())
        lax.fori_loop(nt_ref[0], n_tiles, lambda t, _: (zero_tile(t * tg).wait(), 0)[1], 0)

    def copy(r, slot):
        return pltpu.make_async_copy(h_ref.at[pl.ds(r, 1), :],
                                     xs_ref.at[pl.ds(pos_ref[0, slot, r], 1), :], sem)

    def issue(r, _):
        copy(r, 0).start()
        copy(r, 1).start()
        return 0

    lax.fori_loop(0, tm, issue, 0, unroll=DMA_UNROLL)
    for _ in range(2):
        pltpu.make_async_copy(h_ref, xs_ref.at[pl.ds(0, tm), :], sem).wait()


def _dispatch(h2, pos3, last_tile_start, has_rows, n_tiles_used, n_rows, tm, tg):
    N, D = h2.shape
    grid_spec = pltpu.PrefetchScalarGridSpec(
        num_scalar_prefetch=3,
        grid=(N // tm,),
        in_specs=[pl.BlockSpec((1, 2, tm), lambda i, zs, zo, nt: (i, 0, 0), memory_space=pltpu.SMEM),
                  pl.BlockSpec((tm, D), lambda i, zs, zo, nt: (i, 0))],
        out_specs=pl.BlockSpec(memory_space=pl.ANY),
        scratch_shapes=[pltpu.VMEM((tg, D), F32), pltpu.SemaphoreType.DMA(()), pltpu.SemaphoreType.DMA(())],
    )
    return pl.pallas_call(
        functools.partial(_dispatch_kernel, tm=tm, tg=tg),
        grid_spec=grid_spec,
        out_shape=jax.ShapeDtypeStruct((n_rows, D), F32),
        compiler_params=_cparams(("arbitrary",)),
        name="moe_dispatch",
    )(last_tile_start, has_rows, n_tiles_used, pos3, h2)


def _expert_kernel(te_ref, nt_ref, xs_ref, w1_ref, w3_ref, w2_ref, y_ref, w1b, w3b, w2b):
    g = pl.program_id(0)
    used = g < nt_ref[0]
    new_expert = (g == 0) | (te_ref[g] != te_ref[jnp.maximum(g - 1, 0)])

    @pl.when(used & new_expert)
    def _():
        w1b[...] = w1_ref[0].astype(BF16)
        w3b[...] = w3_ref[0].astype(BF16)
        w2b[...] = w2_ref[0].astype(BF16)

    @pl.when(used)
    def _():
        xb = xs_ref[...].astype(BF16)
        a = jnp.dot(xb, w1b[...], preferred_element_type=F32)
        b = jnp.dot(xb, w3b[...], preferred_element_type=F32)
        hmid = (a * jax.nn.sigmoid(a) * b).astype(BF16)
        y_ref[...] = jnp.dot(hmid, w2b[...], preferred_element_type=F32)

    @pl.when(jnp.logical_not(used))
    def _():
        y_ref[...] = jnp.zeros_like(y_ref)


def _experts(tile_expert, n_tiles_used, xs, w1, w3, w2, tg):
    P, D = xs.shape
    E, _, De = w1.shape
    row_tile = lambda g, te, nt: (jnp.minimum(g, nt[0] - 1), 0)
    grid_spec = pltpu.PrefetchScalarGridSpec(
        num_scalar_prefetch=2,
        grid=(P // tg,),
        in_specs=[pl.BlockSpec((tg, D), row_tile),
                  pl.BlockSpec((1, D, De), lambda g, te, nt: (te[g], 0, 0)),
                  pl.BlockSpec((1, D, De), lambda g, te, nt: (te[g], 0, 0)),
                  pl.BlockSpec((1, De, D), lambda g, te, nt: (te[g], 0, 0))],
        out_specs=pl.BlockSpec((tg, D), lambda g, te, nt: (g, 0)),
        scratch_shapes=[pltpu.VMEM((D, De), BF16), pltpu.VMEM((D, De), BF16), pltpu.VMEM((De, D), BF16)],
    )
    return pl.pallas_call(
        _expert_kernel,
        grid_spec=grid_spec,
        out_shape=jax.ShapeDtypeStruct((P, D), F32),
        compiler_params=_cparams(("arbitrary",)),
        name="moe_experts",
    )(tile_expert, n_tiles_used, xs, w1, w3, w2)


def _combine_kernel(pos_ref, y_ref, x1_ref, route_ref, gt_ref, o_ref, buf0, buf1, sem, *, tm):
    def copy(r, slot, buf):
        return pltpu.make_async_copy(y_ref.at[pl.ds(pos_ref[0, slot, r], 1), :],
                                     buf.at[pl.ds(r, 1), :], sem)

    def issue(r, _):
        copy(r, 0, buf0).start()
        copy(r, 1, buf1).start()
        return 0

    lax.fori_loop(0, tm, issue, 0, unroll=DMA_UNROLL)
    for buf in (buf0, buf1):
        pltpu.make_async_copy(y_ref.at[pl.ds(0, tm), :], buf, sem).wait()
    w0 = route_ref[:, 2:3]
    w1 = route_ref[:, 3:4]
    y = buf0[...] * w0 + buf1[...] * w1
    o_ref[...] = x1_ref[...] + gt_ref[0] * y


def _combine(pos3, y, x1, route, gt2, tm, S):
    N, D = x1.shape
    per_b = S // tm
    return pl.pallas_call(
        functools.partial(_combine_kernel, tm=tm),
        grid=(N // tm,),
        in_specs=[pl.BlockSpec((1, 2, tm), lambda i: (i, 0, 0), memory_space=pltpu.SMEM),
                  pl.BlockSpec(memory_space=pl.ANY),
                  pl.BlockSpec((tm, D), lambda i: (i, 0)),
                  pl.BlockSpec((tm, LANES), lambda i: (i, 0)),
                  pl.BlockSpec((1, 1, D), lambda i: (i // per_b, 0, 0))],
        out_specs=pl.BlockSpec((tm, D), lambda i: (i, 0)),
        out_shape=jax.ShapeDtypeStruct((N, D), F32),
        scratch_shapes=[pltpu.VMEM((tm, D), F32), pltpu.VMEM((tm, D), F32), pltpu.SemaphoreType.DMA(())],
        compiler_params=_cparams(("arbitrary",)),
        name="moe_combine",
    )(pos3, y, x1, route, gt2)


def _rope_tables(positions):
    half = HEAD_DIM // 2
    inv = ROPE_THETA ** (-jnp.arange(half, dtype=F32) / half)
    ang = positions.astype(F32)[..., None] * inv
    cos, sin = jnp.cos(ang), jnp.sin(ang)
    zero = jnp.zeros_like(sin)
    cos128 = jnp.concatenate([cos] * 4, axis=-1)
    sin_lo = jnp.concatenate([-sin, zero] * 2, axis=-1)
    sin_hi = jnp.concatenate([zero, sin] * 2, axis=-1)
    return cos128, sin_lo, sin_hi


def _block_diag_mean(width):
    blk = np.kron(np.eye(width // HEAD_DIM, dtype=np.float32), np.full((HEAD_DIM, HEAD_DIM), 1.0 / HEAD_DIM, np.float32))
    return jnp.asarray(blk, BF16)


def _layer(x, c_mod, positions, norm1_g, norm2_g, w_in, b_fgt, b_gate, qn_fox, kn_fox, qn_dsa, kn_dsa,
           w_proj_fox, w_proj_dsa, w_out, r_w_grp, r_b_grp, r_w_exp, r_b_exp, w1, w3, w2):
    B, S, D = x.shape
    N = B * S
    topk = min(TOPK_MAX, S // 4)
    tm = min(512, S)
    scale = HEAD_DIM ** -0.5
    mod3 = c_mod.reshape(B, 6, D)

    o = np.cumsum([0, 512, 512, 512, 8, 512, 64, 64, 512, 64, 8, D, D])
    seg = lambda k: w_in[:, o[k]:o[k + 1]]
    zpad = jnp.zeros((D, LANES - HEAD_DIM - 2 * N_HEADS), F32)
    w_perm = jnp.concatenate([seg(0), seg(1), seg(4), seg(7), seg(2),
                              seg(5), seg(8),
                              seg(6), seg(3), seg(9), zpad,
                              seg(10), seg(11)], axis=1).astype(BF16)
    gains = jnp.stack([jnp.tile(qn_fox * scale, N_HEADS), jnp.tile(kn_fox, N_HEADS),
                       jnp.tile(qn_dsa * scale, N_HEADS)])
    kn128 = jnp.concatenate([kn_dsa, jnp.ones((HEAD_DIM,), F32)]).reshape(1, LANES)
    b2 = jnp.concatenate([jnp.zeros((HEAD_DIM,), F32), b_fgt,
                          jnp.zeros((LANES - HEAD_DIM - N_HEADS,), F32)]).reshape(1, LANES)
    cos128, sin_lo, sin_hi = _rope_tables(positions)

    fq, fk, dq, iq, fv, s1, s2, gates = _in_projection(
        x, mod3, norm1_g.reshape(1, D), w_perm, _block_diag_mean(512), _block_diag_mean(128),
        cos128, sin_lo, sin_hi, gains, kn128, b2, b_gate.reshape(1, 2 * D), tm)

    logf_t = jnp.transpose(s2[:, :, HEAD_DIM:HEAD_DIM + N_HEADS], (0, 2, 1))
    f_parts = jnp.transpose(_seq_cumsum(logf_t), (0, 2, 3, 1)).astype(BF16)
    k_heads = jnp.transpose(fk.reshape(B, S, N_HEADS, HEAD_DIM), (0, 2, 1, 3))
    k_aug = jnp.concatenate([k_heads, f_parts, jnp.zeros((B, N_HEADS, S, HEAD_DIM - 3), BF16)], axis=-1)
    oft = _fox_attention(k_aug, jnp.transpose(fq, (0, 2, 1)), jnp.transpose(fv, (0, 2, 1)))
    of = jnp.transpose(oft, (0, 2, 1))

    dk, ik = s1[:, :, :HEAD_DIM], s1[:, :, HEAD_DIM:]
    dvt = jnp.transpose(s2[:, :, :HEAD_DIM].astype(BF16), (0, 2, 1))
    iwt = jnp.transpose(s2[:, :, HEAD_DIM + N_HEADS:HEAD_DIM + 2 * N_HEADS], (0, 2, 1))
    odt = _dsa_attention(ik, dk, dvt, jnp.transpose(iq, (0, 2, 1)), jnp.transpose(dq, (0, 2, 1)), iwt, topk)
    od = jnp.transpose(odt, (0, 2, 1))

    wr = jnp.concatenate([r_w_grp, r_w_exp, jnp.zeros((D, LANES - N_GROUPS - N_EXPERTS), F32)], axis=1).astype(BF16)
    br = jnp.concatenate([r_b_grp, r_b_exp, jnp.zeros((LANES - N_GROUPS - N_EXPERTS,), F32)]).reshape(1, LANES)
    x1, h2, route = _post_attention(of, od, gates, x, mod3, w_proj_fox.astype(BF16), w_proj_dsa.astype(BF16),
                                    w_out.astype(BF16), norm2_g.reshape(1, D), wr, br, tm)
    x1, h2, route = x1.reshape(N, D), h2.reshape(N, D), route.reshape(N, LANES)

    tg = 512 if N * 2 >= 512 * N_EXPERTS else 128
    ranks, counts = _expert_ranks(route, tm)
    counts = counts[0, :N_EXPERTS].astype(jnp.int32)
    padded = ((counts + tg - 1) // tg) * tg
    ends = jnp.cumsum(padded)
    starts = ends - padded
    e01 = route[:, :2].astype(jnp.int32)
    start_of = jnp.sum(jnp.where(e01[..., None] == jnp.arange(N_EXPERTS, dtype=jnp.int32), starts, 0), axis=-1)
    pos = start_of + ranks[:, :2].astype(jnp.int32)
    n_rows = N * 2 + N_EXPERTS * tg
    n_tiles = n_rows // tg
    tile_start = jnp.arange(n_tiles, dtype=jnp.int32) * tg
    tile_expert = jnp.minimum(jnp.sum((ends[None, :] <= tile_start[:, None]).astype(jnp.int32), axis=1),
                              N_EXPERTS - 1)
    n_used = (ends[-1] // tg).astype(jnp.int32).reshape(1)

    td = min(256, S)
    pos3 = jnp.transpose(pos.reshape(N // td, td, 2), (0, 2, 1))
    xs = _dispatch(h2, pos3, jnp.maximum(ends - tg, 0).astype(jnp.int32), (padded > 0).astype(jnp.int32),
                   n_used, n_rows, td, tg)
    y = _experts(tile_expert, n_used, xs, w1, w3, w2, tg)
    out = _combine(pos3, y, x1, route, mod3[:, 5:6, :], td, S)
    return out.reshape(B, S, D)


def kernel(x, c, positions, ada_w, ada_b, norm1_g, norm2_g, w_in, b_fgt, b_gate, qn_fox, kn_fox, qn_dsa, kn_dsa, w_proj_fox, w_proj_dsa, w_out, router_w_grp, router_b_grp, router_w_exp, router_b_exp, exp_w1, exp_w3, exp_w2):
    for l in range(ada_w.shape[0]):
        c_mod = _modulation(c, ada_w[l], ada_b[l])
        x = _layer(x, c_mod, positions, norm1_g[l], norm2_g[l], w_in[l], b_fgt[l], b_gate[l],
                   qn_fox[l], kn_fox[l], qn_dsa[l], kn_dsa[l], w_proj_fox[l], w_proj_dsa[l], w_out[l],
                   router_w_grp[l], router_b_grp[l], router_w_exp[l], router_b_exp[l],
                   exp_w1[l], exp_w3[l], exp_w2[l])
    return x
```

```python
import functools

import jax
import jax.numpy as jnp
import numpy as np
from jax import lax
from jax.experimental import pallas as pl
from jax.experimental.pallas import tpu as pltpu

F32 = jnp.float32
BF16 = jnp.bfloat16

CHUNK = 64
CHUNK_SHIFT = 6
DMA_UNROLL = 8
HEAD_DIM = 64
N_HEADS = 8
W_HEADS = N_HEADS * HEAD_DIM
TOPK_MAX = 256
ROPE_THETA = 10000.0
N_GROUPS = 4
EXPERTS_PER_GROUP = 8
N_EXPERTS = N_GROUPS * EXPERTS_PER_GROUP
EPS = 1e-6
MASKED = -1e30

LANES = 128
VMEM_LIMIT = 56 * 1024 * 1024


def _cparams(sem):
    return pltpu.CompilerParams(dimension_semantics=sem, vmem_limit_bytes=VMEM_LIMIT)


def _mod_kernel(c_ref, w_ref, b_ref, o_ref):
    c = c_ref[...]
    ca = (c * jax.nn.sigmoid(c)).astype(BF16)
    o_ref[...] = jnp.dot(ca, w_ref[...].astype(BF16), preferred_element_type=F32) + b_ref[...]


def _modulation(c, ada_w, ada_b):
    B, D = c.shape
    n = ada_w.shape[1] // D
    return pl.pallas_call(
        _mod_kernel,
        grid=(n,),
        in_specs=[pl.BlockSpec((B, D), lambda j: (0, 0)),
                  pl.BlockSpec((D, D), lambda j: (0, j)),
                  pl.BlockSpec((1, D), lambda j: (0, j))],
        out_specs=pl.BlockSpec((B, D), lambda j: (0, j)),
        out_shape=jax.ShapeDtypeStruct((B, n * D), F32),
        compiler_params=_cparams(("arbitrary",)),
        name="adaln_mod",
    )(c, ada_w, ada_b.reshape(1, -1))


C_FQ, C_FK, C_DQ, C_IQ, C_FV = 0, 512, 1024, 1536, 2048
C_S1 = 2560
C_S2 = 2688
C_GATE = 2816
C_END = 4864


def _rope(x, cos, sin_lo, sin_hi):
    w = x.shape[-1]
    return x * cos + pltpu.roll(x, w - 32, 1) * sin_lo + pltpu.roll(x, 32, 1) * sin_hi


def _inproj_kernel(x_ref, mod_ref, g1_ref, w_ref, bd512_ref, bd128_ref, cos_ref, slo_ref, shi_ref,
                   gains_ref, kn128_ref, b2_ref, bg_ref,
                   fq_ref, fk_ref, dq_ref, iq_ref, fv_ref, s1_ref, s2_ref, gate_ref):
    x = x_ref[0]
    ms = jnp.mean(x * x, axis=-1, keepdims=True)
    h = x * lax.rsqrt(ms + EPS) * g1_ref[...]
    h = h * (1.0 + mod_ref[0, 1:2, :]) + mod_ref[0, 0:1, :]
    hb = h.astype(BF16)

    def proj(lo, hi):
        return jnp.dot(hb, w_ref[:, lo:hi], preferred_element_type=F32)

    def head_norm(y, gain):
        msq = jnp.dot((y * y).astype(BF16), bd512_ref[...], preferred_element_type=F32)
        return y * lax.rsqrt(msq + EPS) * gain

    cos1, slo1, shi1 = cos_ref[0], slo_ref[0], shi_ref[0]
    cos4 = jnp.concatenate([cos1] * 4, axis=1)
    slo4 = jnp.concatenate([slo1] * 4, axis=1)
    shi4 = jnp.concatenate([shi1] * 4, axis=1)

    fq_ref[0] = head_norm(proj(C_FQ, C_FQ + 512), gains_ref[0:1, :]).astype(BF16)
    fk_ref[0] = head_norm(proj(C_FK, C_FK + 512), gains_ref[1:2, :]).astype(BF16)
    dq = head_norm(proj(C_DQ, C_DQ + 512), gains_ref[2:3, :])
    dq_ref[0] = _rope(dq, cos4, slo4, shi4).astype(BF16)
    iq_ref[0] = _rope(proj(C_IQ, C_IQ + 512), cos4, slo4, shi4).astype(BF16)
    fv_ref[0] = proj(C_FV, C_FV + 512).astype(BF16)

    s1 = proj(C_S1, C_S1 + 128)
    msq = jnp.dot((s1 * s1).astype(BF16), bd128_ref[...], preferred_element_type=F32)
    lane = lax.broadcasted_iota(jnp.int32, s1.shape, 1)
    s1 = jnp.where(lane < HEAD_DIM, s1 * lax.rsqrt(msq + EPS) * kn128_ref[...], s1)
    s1_ref[0] = _rope(s1, cos1, slo1, shi1).astype(BF16)

    s2 = proj(C_S2, C_S2 + 128)
    z = s2 + b2_ref[...]
    logsig = jnp.minimum(z, 0.0) - jnp.log(1.0 + jnp.exp(-jnp.abs(z)))
    is_fgt = (lane >= HEAD_DIM) & (lane < HEAD_DIM + N_HEADS)
    s2_ref[0] = jnp.where(is_fgt, logsig, s2)

    gate_ref[0] = jax.nn.sigmoid(proj(C_GATE, C_END) + bg_ref[...]).astype(BF16)


def _in_projection(x, mod3, norm1_g, w_perm, bd512, bd128, cos, slo, shi, gains, kn128, b2, bg, tm):
    B, S, D = x.shape
    tok = lambda w: pl.BlockSpec((1, tm, w), lambda b, i: (b, i, 0))
    const = lambda shape: pl.BlockSpec(shape, lambda b, i: (0,) * len(shape))
    out_shapes = [jax.ShapeDtypeStruct((B, S, 512), BF16)] * 5 + [
        jax.ShapeDtypeStruct((B, S, 128), BF16),
        jax.ShapeDtypeStruct((B, S, 128), F32),
        jax.ShapeDtypeStruct((B, S, 2 * D), BF16)]
    return pl.pallas_call(
        _inproj_kernel,
        grid=(B, S // tm),
        in_specs=[tok(D),
                  pl.BlockSpec((1, 6, D), lambda b, i: (b, 0, 0)),
                  const((1, D)),
                  const(w_perm.shape),
                  const((512, 512)), const((128, 128)),
                  tok(128), tok(128), tok(128),
                  const((3, 512)), const((1, 128)), const((1, 128)), const((1, 2 * D))],
        out_specs=[tok(512)] * 5 + [tok(128), tok(128), tok(2 * D)],
        out_shape=out_shapes,
        compiler_params=_cparams(("parallel", "parallel")),
        name="in_projection",
    )(x, mod3, norm1_g, w_perm, bd512, bd128, cos, slo, shi, gains, kn128, b2, bg)


def _cumsum_kernel(x_ref, o_ref):
    x = x_ref[0]
    n = x.shape[-1]
    pos = lax.broadcasted_iota(jnp.int32, x.shape, 1)
    shift = 1
    while shift < n:
        x = x + jnp.where(pos >= shift, pltpu.roll(x, shift, 1), 0.0)
        shift *= 2
    hi = x.astype(BF16).astype(F32)
    mid = (x - hi).astype(BF16).astype(F32)
    o_ref[0, 0] = hi
    o_ref[0, 1] = mid
    o_ref[0, 2] = (x - hi - mid).astype(BF16).astype(F32)


def _seq_cumsum(logf_t):
    B, H, S = logf_t.shape
    return pl.pallas_call(
        _cumsum_kernel,
        grid=(B,),
        in_specs=[pl.BlockSpec((1, H, S), lambda b: (b, 0, 0))],
        out_specs=pl.BlockSpec((1, 3, H, S), lambda b: (b, 0, 0, 0)),
        out_shape=jax.ShapeDtypeStruct((B, 3, H, S), F32),
        compiler_params=_cparams(("parallel",)),
        name="forget_cumsum",
    )(logf_t)


KC = 256
SUB = 8


def _fold_rows(a, op, ways=1):
    n = a.shape[0] // SUB
    a = a.reshape(n, SUB, a.shape[1])
    chains = [a[w] for w in range(ways)]
    for j in range(ways, n):
        chains[j % ways] = op(chains[j % ways], a[j])
    while len(chains) > 1:
        chains = [op(chains[2 * j], chains[2 * j + 1]) for j in range(len(chains) // 2)]
    return chains[0]


def _softmax_pv(nch, s_ref, acc_ref, vt_at, m_all, o_ref):
    Q = o_ref.shape[-1]
    acc_ref[...] = jnp.zeros_like(acc_ref)

    def body(c, lsum):
        off = pl.multiple_of(c * KC, KC)
        new = []
        for hh in range(N_HEADS):
            p = jnp.exp(s_ref[hh, pl.ds(off, KC), :] - m_all[hh])
            new.append(lsum[hh] + _fold_rows(p, jnp.add))
            acc_ref[hh] += jnp.dot(vt_at(hh, off), p.astype(BF16), preferred_element_type=F32)
        return tuple(new)

    lsum = lax.fori_loop(0, nch, body, tuple(jnp.zeros((SUB, Q), F32) for _ in range(N_HEADS)))
    for hh in range(N_HEADS):
        l = jnp.sum(lsum[hh], axis=0, keepdims=True)
        o_ref[0, hh * HEAD_DIM:(hh + 1) * HEAD_DIM, :] = (acc_ref[hh] / l).astype(BF16)


def _fox_kernel(ka_ref, qt_ref, vt_ref, o_ref, s_ref, acc_ref):
    i = pl.program_id(1)
    Q = o_ref.shape[-1]
    row = lax.broadcasted_iota(jnp.int32, (HEAD_DIM, Q), 0)
    aug = jnp.where(row < 3, -1.0, 0.0).astype(BF16)
    qa = [jnp.concatenate([qt_ref[0, hh * HEAD_DIM:(hh + 1) * HEAD_DIM, :], aug], axis=0)
          for hh in range(N_HEADS)]

    def scores(c, mx, bias):
        off = pl.multiple_of(c * KC, KC)
        new = []
        for hh in range(N_HEADS):
            s = jnp.dot(ka_ref[0, hh, pl.ds(off, KC), :], qa[hh], preferred_element_type=F32)
            if bias is not None:
                s = s + bias
            s_ref[hh, pl.ds(off, KC), :] = s
            new.append(jnp.maximum(mx[hh], _fold_rows(s, jnp.maximum)))
        return tuple(new)

    mx = tuple(jnp.full((SUB, Q), MASKED, F32) for _ in range(N_HEADS))
    mx = lax.fori_loop(0, i, lambda c, m: scores(c, m, None), mx)
    kk = lax.broadcasted_iota(jnp.int32, (KC, Q), 0)
    qq = lax.broadcasted_iota(jnp.int32, (KC, Q), 1)
    mx = scores(i, mx, jnp.where(kk <= qq, 0.0, MASKED))
    m_all = [jnp.max(m, axis=0, keepdims=True) for m in mx]
    _softmax_pv(i + 1, s_ref, acc_ref, lambda hh, off: vt_ref[0, hh * HEAD_DIM:(hh + 1) * HEAD_DIM, pl.ds(off, KC)],
                m_all, o_ref)


def _fox_attention(k_aug, qt, vt):
    B, H, S, Wa = k_aug.shape
    return pl.pallas_call(
        _fox_kernel,
        grid=(B, S // KC),
        in_specs=[pl.BlockSpec((1, H, S, Wa), lambda b, i: (b, 0, 0, 0)),
                  pl.BlockSpec((1, W_HEADS, KC), lambda b, i: (b, 0, i)),
                  pl.BlockSpec((1, W_HEADS, S), lambda b, i: (b, 0, 0))],
        out_specs=pl.BlockSpec((1, W_HEADS, KC), lambda b, i: (b, 0, i)),
        out_shape=jax.ShapeDtypeStruct((B, W_HEADS, S), BF16),
        scratch_shapes=[pltpu.VMEM((H, S, KC), F32), pltpu.VMEM((H, HEAD_DIM, KC), F32)],
        compiler_params=_cparams(("parallel", "arbitrary")),
        name="fox_attention",
    )(k_aug, qt, vt)


INT_MIN = -(2 ** 31)
FIELD = 10


def _dsa_kernel(ik_ref, dk_ref, dvt_ref, iqt_ref, dqt_ref, iwt_ref, o_ref, key_ref, s_ref, acc_ref, *, topk):
    i = pl.program_id(1)
    Q = o_ref.shape[-1]
    nch = i + 1
    sub_k = lax.broadcasted_iota(jnp.int32, (KC, Q), 0)
    sub_r = lax.broadcasted_iota(jnp.int32, (CHUNK, Q), 0)
    q_chunk = (i * Q + lax.broadcasted_iota(jnp.int32, (CHUNK, Q), 1)) >> CHUNK_SHIFT
    iqts = [iqt_ref[0, hh * HEAD_DIM:(hh + 1) * HEAD_DIM, :] for hh in range(N_HEADS)]
    iws = [iwt_ref[0, hh:hh + 1, :] for hh in range(N_HEADS)]

    def score_chunk(c, _):
        for r in range(KC // CHUNK):
            off = pl.multiple_of(c * KC + r * CHUNK, CHUNK)
            ik = ik_ref[0, pl.ds(off, CHUNK), :]
            sc = jnp.zeros((CHUNK, Q), F32)
            for hh in range(N_HEADS):
                d = jnp.dot(ik, iqts[hh], preferred_element_type=F32)
                sc = sc + iws[hh] * jnp.maximum(d, 0.0)
            sc = sc + 0.0
            allowed = ((off + sub_r) >> CHUNK_SHIFT) <= q_chunk
            bits = pltpu.bitcast(sc, jnp.int32)
            key = bits ^ ((bits >> 31) & 0x7FFFFFFF)
            key_ref[pl.ds(off, CHUNK), :] = jnp.where(allowed, key, INT_MIN)
        return 0

    lax.fori_loop(0, nch, score_chunk, 0)

    def count(pred):
        def body(c, acc):
            off = pl.multiple_of(c * KC, KC)
            hit = pred(key_ref[pl.ds(off, KC), :], off + sub_k)
            return acc + _fold_rows(jnp.where(hit, 1.0, 0.0), jnp.add, ways=4)
        acc = lax.fori_loop(0, nch, body, jnp.zeros((SUB, Q), F32))
        return jnp.sum(acc, axis=0, keepdims=True)

    kf = jnp.float32(topk)

    n_nonneg = count(lambda k, _: k >= 0)
    top_half = n_nonneg >= kf
    thr = jnp.where(top_half, 0, INT_MIN).astype(jnp.int32)
    n_ge = jnp.where(top_half, n_nonneg, (nch * KC).astype(F32))
    def count3(c1, c2, c3):
        def body(c, acc):
            off = pl.multiple_of(c * KC, KC)
            k = key_ref[pl.ds(off, KC), :]
            code = jnp.where(k >= c3, 1 << (2 * FIELD), jnp.where(k >= c2, 1 << FIELD, jnp.where(k >= c1, 1, 0)))
            return acc + _fold_rows(code, jnp.add, ways=4)
        acc = lax.fori_loop(0, nch, body, jnp.zeros((SUB, Q), jnp.int32))
        low = (1 << FIELD) - 1
        b1 = jnp.sum((acc & low).astype(F32), axis=0, keepdims=True)
        b2 = jnp.sum(((acc >> FIELD) & low).astype(F32), axis=0, keepdims=True)
        n3 = jnp.sum((acc >> (2 * FIELD)).astype(F32), axis=0, keepdims=True)
        return n3 + b2 + b1, n3 + b2, n3

    assert key_ref.shape[0] // SUB < (1 << FIELD)
    for bit in range(29, 0, -2):
        step = 1 << bit
        c1, c2, c3 = thr + jnp.int32(step), thr + jnp.int32(2 * step), thr + jnp.int32(3 * step)
        n1, n2, n3 = count3(c1, c2, c3)
        t1, t2, t3 = n1 >= kf, n2 >= kf, n3 >= kf
        thr = jnp.where(t3, c3, jnp.where(t2, c2, jnp.where(t1, c1, thr)))
        n_ge = jnp.where(t3, n3, jnp.where(t2, n2, jnp.where(t1, n1, n_ge)))
    cand = thr + jnp.int32(1)
    n = count(lambda k, _: k >= cand)
    take = n >= kf
    thr = jnp.where(take, cand, thr)
    n_ge = jnp.where(take, n, n_ge)

    excess = (n_ge > kf) & (thr > INT_MIN)

    @pl.when(jnp.max(jnp.where(excess, 1.0, 0.0)) > 0.0)
    def _():
        need = kf - count(lambda k, _: k > thr)
        last = jnp.zeros((1, Q), jnp.int32)
        nbits = int(np.ceil(np.log2(key_ref.shape[0]))) + 1
        for bit in range(nbits - 1, -1, -1):
            cand = last + jnp.int32(1 << bit)
            n = count(lambda k, idx, cand=cand: (k == thr) & (idx < cand))
            last = jnp.where(n < need, cand, last)

        def demote(c, _):
            off = pl.multiple_of(c * KC, KC)
            k = key_ref[pl.ds(off, KC), :]
            drop = excess & (k == thr) & (off + sub_k > last)
            key_ref[pl.ds(off, KC), :] = jnp.where(drop, thr - 1, k)
            return 0

        lax.fori_loop(0, nch, demote, 0)

    keep_from = jnp.maximum(thr, INT_MIN + 1)

    qts = [dqt_ref[0, hh * HEAD_DIM:(hh + 1) * HEAD_DIM, :] for hh in range(N_HEADS)]

    def scores(c, mx):
        off = pl.multiple_of(c * KC, KC)
        bias = jnp.where(key_ref[pl.ds(off, KC), :] >= keep_from, 0.0, MASKED)
        dk = dk_ref[0, pl.ds(off, KC), :]
        new = []
        for hh in range(N_HEADS):
            s = jnp.dot(dk, qts[hh], preferred_element_type=F32) + bias
            s_ref[hh, pl.ds(off, KC), :] = s
            new.append(jnp.maximum(mx[hh], _fold_rows(s, jnp.maximum)))
        return tuple(new)

    mx = lax.fori_loop(0, nch, scores, tuple(jnp.full((SUB, Q), MASKED, F32) for _ in range(N_HEADS)))
    m_all = [jnp.max(m, axis=0, keepdims=True) for m in mx]
    _softmax_pv(nch, s_ref, acc_ref, lambda hh, off: dvt_ref[0, :, pl.ds(off, KC)], m_all, o_ref)


def _dsa_attention(ik, dk, dvt, iqt, dqt, iwt, topk):
    B, S, Dh = ik.shape
    seq = pl.BlockSpec((1, S, Dh), lambda b, i: (b, 0, 0))
    qcols = lambda r: pl.BlockSpec((1, r, KC), lambda b, i: (b, 0, i))
    return pl.pallas_call(
        functools.partial(_dsa_kernel, topk=topk),
        grid=(B, S // KC),
        in_specs=[seq, seq,
                  pl.BlockSpec((1, Dh, S), lambda b, i: (b, 0, 0)),
                  qcols(W_HEADS), qcols(W_HEADS), qcols(N_HEADS)],
        out_specs=qcols(W_HEADS),
        out_shape=jax.ShapeDtypeStruct((B, W_HEADS, S), BF16),
        scratch_shapes=[pltpu.VMEM((S, KC), jnp.int32), pltpu.VMEM((N_HEADS, S, KC), F32),
                        pltpu.VMEM((N_HEADS, HEAD_DIM, KC), F32)],
        compiler_params=_cparams(("parallel", "arbitrary")),
        name="dsa_attention",
    )(ik, dk, dvt, iqt, dqt, iwt)


def _first(mask, lane):
    return jnp.min(jnp.where(mask, lane, LANES), axis=-1, keepdims=True)


def _post_kernel(of_ref, od_ref, gate_ref, x_ref, mod_ref, wpf_ref, wpd_ref, wo_ref, g2_ref, wr_ref, br_ref,
                 x1_ref, h2_ref, route_ref):
    D = x_ref.shape[-1]
    pf = jnp.dot(of_ref[0], wpf_ref[...], preferred_element_type=F32)
    pd = jnp.dot(od_ref[0], wpd_ref[...], preferred_element_type=F32)
    merged = gate_ref[0, :, :D].astype(F32) * pf + gate_ref[0, :, D:].astype(F32) * pd
    y = jnp.dot(merged.astype(BF16), wo_ref[...], preferred_element_type=F32)
    x1 = x_ref[0] + mod_ref[0, 2:3, :] * y
    x1_ref[0] = x1

    ms = jnp.mean(x1 * x1, axis=-1, keepdims=True)
    h2 = x1 * lax.rsqrt(ms + EPS) * g2_ref[...]
    h2 = h2 * (1.0 + mod_ref[0, 4:5, :]) + mod_ref[0, 3:4, :]
    hb = h2.astype(BF16)
    h2_ref[0] = h2

    logits = jnp.dot(hb, wr_ref[...], preferred_element_type=F32) + br_ref[...]
    lane = lax.broadcasted_iota(jnp.int32, logits.shape, 1)
    is_grp = lane < N_GROUPS
    gl = jnp.where(is_grp, logits, -jnp.inf)
    gmax = jnp.max(gl, axis=-1, keepdims=True)
    g_idx = _first(gl == gmax, lane)
    g_w = 1.0 / jnp.sum(jnp.exp(gl - gmax), axis=-1, keepdims=True)

    e_lo = N_GROUPS + g_idx * EXPERTS_PER_GROUP
    in_grp = (lane >= e_lo) & (lane < e_lo + EXPERTS_PER_GROUP)
    el = jnp.where(in_grp, logits, -jnp.inf)
    emax = jnp.max(el, axis=-1, keepdims=True)
    ee = jnp.exp(el - emax)
    prob = ee / jnp.sum(ee, axis=-1, keepdims=True)
    prob = jnp.where(in_grp, prob, -1.0)
    p0 = jnp.max(prob, axis=-1, keepdims=True)
    l0 = _first(prob == p0, lane)
    rest = jnp.where(lane == l0, -1.0, prob)
    p1 = jnp.max(rest, axis=-1, keepdims=True)
    l1 = _first(rest == p1, lane)
    psum = p0 + p1
    w0 = g_w * (p0 / psum)
    w1 = g_w * (p1 / psum)
    e0 = (l0 - N_GROUPS).astype(F32)
    e1 = (l1 - N_GROUPS).astype(F32)
    route_ref[0] = jnp.where(lane == 0, e0, jnp.where(lane == 1, e1, jnp.where(lane == 2, w0,
                             jnp.where(lane == 3, w1, 0.0))))


def _post_attention(of, od, gates, x, mod3, wpf, wpd, wo, g2, wr, br, tm):
    B, S, D = x.shape
    tok = lambda w: pl.BlockSpec((1, tm, w), lambda b, i: (b, i, 0))
    const = lambda shape: pl.BlockSpec(shape, lambda b, i: (0,) * len(shape))
    return pl.pallas_call(
        _post_kernel,
        grid=(B, S // tm),
        in_specs=[tok(W_HEADS), tok(W_HEADS), tok(2 * D), tok(D),
                  pl.BlockSpec((1, 6, D), lambda b, i: (b, 0, 0)),
                  const(wpf.shape), const(wpd.shape), const(wo.shape),
                  const((1, D)), const((D, LANES)), const((1, LANES))],
        out_specs=[tok(D), tok(D), tok(LANES)],
        out_shape=[jax.ShapeDtypeStruct((B, S, D), F32),
                   jax.ShapeDtypeStruct((B, S, D), F32),
                   jax.ShapeDtypeStruct((B, S, LANES), F32)],
        compiler_params=_cparams(("parallel", "parallel")),
        name="merge_out_router",
    )(of, od, gates, x, mod3, wpf, wpd, wo, g2, wr, br)


def _rank_kernel(route_ref, tri_ref, rank_ref, count_ref, carry_ref):
    @pl.when(pl.program_id(0) == 0)
    def _():
        carry_ref[...] = jnp.zeros_like(carry_ref)

    r = route_ref[...]
    lane = lax.broadcasted_iota(jnp.int32, r.shape, 1).astype(F32)
    hot0 = lane == r[:, 0:1]
    hot1 = lane == r[:, 1:2]
    hits = jnp.where(hot0 | hot1, 1.0, 0.0)
    incl = jnp.dot(tri_ref[...], hits.astype(BF16), preferred_element_type=F32)
    before = incl - hits + carry_ref[...]
    r0 = jnp.sum(jnp.where(hot0, before, 0.0), axis=-1, keepdims=True)
    r1 = jnp.sum(jnp.where(hot1, before, 0.0), axis=-1, keepdims=True)
    rank_ref[...] = jnp.where(lane == 0.0, r0, jnp.where(lane == 1.0, r1, 0.0))
    carry_ref[...] = carry_ref[...] + jnp.sum(hits, axis=0, keepdims=True)
    count_ref[...] = carry_ref[...]


def _expert_ranks(route, tm):
    N = route.shape[0]
    tri = jnp.asarray(np.tril(np.ones((tm, tm), np.float32)), BF16)
    return pl.pallas_call(
        _rank_kernel,
        grid=(N // tm,),
        in_specs=[pl.BlockSpec((tm, LANES), lambda i: (i, 0)),
                  pl.BlockSpec((tm, tm), lambda i: (0, 0))],
        out_specs=[pl.BlockSpec((tm, LANES), lambda i: (i, 0)),
                   pl.BlockSpec((1, LANES), lambda i: (0, 0))],
        out_shape=[jax.ShapeDtypeStruct((N, LANES), F32), jax.ShapeDtypeStruct((1, LANES), F32)],
        scratch_shapes=[pltpu.VMEM((1, LANES), F32)],
        compiler_params=_cparams(("arbitrary",)),
        name="expert_ranks",
    )(route, tri)


def _dispatch_kernel(zstart_ref, zon_ref, nt_ref, pos_ref, h_ref, xs_ref, zbuf, sem, zsem, *, tm, tg):
    @pl.when(pl.program_id(0) == 0)
    def _():
        zbuf[...] = jnp.zeros_like(zbuf)

        def zero_tile(start):
            return pltpu.make_async_copy(zbuf, xs_ref.at[pl.ds(pl.multiple_of(start, tg), tg), :], zsem)

        n_tiles = xs_ref.shape[0] // tg
        for e in range(N_EXPERTS):
            pl.when(zon_ref[e] > 0)(lambda e=e: zero_tile(zstart_ref[e]).start())
        lax.fori_loop(nt_ref[0], n_tiles, lambda t, _: (zero_tile(t * tg).start(), 0)[1], 0)
        for e in range(N_EXPERTS):
            pl.when(zon_ref[e] > 0)(lambda e=e: zero_tile(zstart_ref[e]).wait())
        lax.fori_loop(nt_ref[0], n_tiles, lambda t, _: (zero_tile(t * tg).wait(), 0)[1], 0)

    def copy(r, slot):
        return pltpu.make_async_copy(h_ref.at[pl.ds(r, 1), :],
                                     xs_ref.at[pl.ds(pos_ref[0, slot, r], 1), :], sem)

    def issue(r, _):
        copy(r, 0).start()
        copy(r, 1).start()
        return 0

    lax.fori_loop(0, tm, issue, 0, unroll=DMA_UNROLL)
    for _ in range(2):
        pltpu.make_async_copy(h_ref, xs_ref.at[pl.ds(0, tm), :], sem).wait()


def _dispatch(h2, pos3, last_tile_start, has_rows, n_tiles_used, n_rows, tm, tg):
    N, D = h2.shape
    grid_spec = pltpu.PrefetchScalarGridSpec(
        num_scalar_prefetch=3,
        grid=(N // tm,),
        in_specs=[pl.BlockSpec((1, 2, tm), lambda i, zs, zo, nt: (i, 0, 0), memory_space=pltpu.SMEM),
                  pl.BlockSpec((tm, D), lambda i, zs, zo, nt: (i, 0))],
        out_specs=pl.BlockSpec(memory_space=pl.ANY),
        scratch_shapes=[pltpu.VMEM((tg, D), F32), pltpu.SemaphoreType.DMA(()), pltpu.SemaphoreType.DMA(())],
    )
    return pl.pallas_call(
        functools.partial(_dispatch_kernel, tm=tm, tg=tg),
        grid_spec=grid_spec,
        out_shape=jax.ShapeDtypeStruct((n_rows, D), F32),
        compiler_params=_cparams(("arbitrary",)),
        name="moe_dispatch",
    )(last_tile_start, has_rows, n_tiles_used, pos3, h2)


def _expert_kernel(te_ref, nt_ref, xs_ref, w1_ref, w3_ref, w2_ref, y_ref, w1b, w3b, w2b):
    g = pl.program_id(0)
    used = g < nt_ref[0]
    new_expert = (g == 0) | (te_ref[g] != te_ref[jnp.maximum(g - 1, 0)])

    @pl.when(used & new_expert)
    def _():
        w1b[...] = w1_ref[0].astype(BF16)
        w3b[...] = w3_ref[0].astype(BF16)
        w2b[...] = w2_ref[0].astype(BF16)

    @pl.when(used)
    def _():
        xb = xs_ref[...].astype(BF16)
        a = jnp.dot(xb, w1b[...], preferred_element_type=F32)
        b = jnp.dot(xb, w3b[...], preferred_element_type=F32)
        hmid = (a * jax.nn.sigmoid(a) * b).astype(BF16)
        y_ref[...] = jnp.dot(hmid, w2b[...], preferred_element_type=F32)

    @pl.when(jnp.logical_not(used))
    def _():
        y_ref[...] = jnp.zeros_like(y_ref)


def _experts(tile_expert, n_tiles_used, xs, w1, w3, w2, tg):
    P, D = xs.shape
    E, _, De = w1.shape
    row_tile = lambda g, te, nt: (jnp.minimum(g, nt[0] - 1), 0)
    grid_spec = pltpu.PrefetchScalarGridSpec(
        num_scalar_prefetch=2,
        grid=(P // tg,),
        in_specs=[pl.BlockSpec((tg, D), row_tile),
                  pl.BlockSpec((1, D, De), lambda g, te, nt: (te[g], 0, 0)),
                  pl.BlockSpec((1, D, De), lambda g, te, nt: (te[g], 0, 0)),
                  pl.BlockSpec((1, De, D), lambda g, te, nt: (te[g], 0, 0))],
        out_specs=pl.BlockSpec((tg, D), lambda g, te, nt: (g, 0)),
        scratch_shapes=[pltpu.VMEM((D, De), BF16), pltpu.VMEM((D, De), BF16), pltpu.VMEM((De, D), BF16)],
    )
    return pl.pallas_call(
        _expert_kernel,
        grid_spec=grid_spec,
        out_shape=jax.ShapeDtypeStruct((P, D), F32),
        compiler_params=_cparams(("arbitrary",)),
        name="moe_experts",
    )(tile_expert, n_tiles_used, xs, w1, w3, w2)


def _combine_kernel(pos_ref, y_ref, x1_ref, route_ref, gt_ref, o_ref, buf0, buf1, sem, *, tm):
    def copy(r, slot, buf):
        return pltpu.make_async_copy(y_ref.at[pl.ds(pos_ref[0, slot, r], 1), :],
                                     buf.at[pl.ds(r, 1), :], sem)

    def issue(r, _):
        copy(r, 0, buf0).start()
        copy(r, 1, buf1).start()
        return 0

    lax.fori_loop(0, tm, issue, 0, unroll=DMA_UNROLL)
    for buf in (buf0, buf1):
        pltpu.make_async_copy(y_ref.at[pl.ds(0, tm), :], buf, sem).wait()
    w0 = route_ref[:, 2:3]
    w1 = route_ref[:, 3:4]
    y = buf0[...] * w0 + buf1[...] * w1
    o_ref[...] = x1_ref[...] + gt_ref[0] * y


def _combine(pos3, y, x1, route, gt2, tm, S):
    N, D = x1.shape
    per_b = S // tm
    return pl.pallas_call(
        functools.partial(_combine_kernel, tm=tm),
        grid=(N // tm,),
        in_specs=[pl.BlockSpec((1, 2, tm), lambda i: (i, 0, 0), memory_space=pltpu.SMEM),
                  pl.BlockSpec(memory_space=pl.ANY),
                  pl.BlockSpec((tm, D), lambda i: (i, 0)),
                  pl.BlockSpec((tm, LANES), lambda i: (i, 0)),
                  pl.BlockSpec((1, 1, D), lambda i: (i // per_b, 0, 0))],
        out_specs=pl.BlockSpec((tm, D), lambda i: (i, 0)),
        out_shape=jax.ShapeDtypeStruct((N, D), F32),
        scratch_shapes=[pltpu.VMEM((tm, D), F32), pltpu.VMEM((tm, D), F32), pltpu.SemaphoreType.DMA(())],
        compiler_params=_cparams(("arbitrary",)),
        name="moe_combine",
    )(pos3, y, x1, route, gt2)


def _rope_tables(positions):
    half = HEAD_DIM // 2
    inv = ROPE_THETA ** (-jnp.arange(half, dtype=F32) / half)
    ang = positions.astype(F32)[..., None] * inv
    cos, sin = jnp.cos(ang), jnp.sin(ang)
    zero = jnp.zeros_like(sin)
    cos128 = jnp.concatenate([cos] * 4, axis=-1)
    sin_lo = jnp.concatenate([-sin, zero] * 2, axis=-1)
    sin_hi = jnp.concatenate([zero, sin] * 2, axis=-1)
    return cos128, sin_lo, sin_hi


def _block_diag_mean(width):
    blk = np.kron(np.eye(width // HEAD_DIM, dtype=np.float32), np.full((HEAD_DIM, HEAD_DIM), 1.0 / HEAD_DIM, np.float32))
    return jnp.asarray(blk, BF16)


def _layer(x, c_mod, positions, norm1_g, norm2_g, w_in, b_fgt, b_gate, qn_fox, kn_fox, qn_dsa, kn_dsa,
           w_proj_fox, w_proj_dsa, w_out, r_w_grp, r_b_grp, r_w_exp, r_b_exp, w1, w3, w2):
    B, S, D = x.shape
    N = B * S
    topk = min(TOPK_MAX, S // 4)
    tm = min(512, S)
    scale = HEAD_DIM ** -0.5
    mod3 = c_mod.reshape(B, 6, D)

    o = np.cumsum([0, 512, 512, 512, 8, 512, 64, 64, 512, 64, 8, D, D])
    seg = lambda k: w_in[:, o[k]:o[k + 1]]
    zpad = jnp.zeros((D, LANES - HEAD_DIM - 2 * N_HEADS), F32)
    w_perm = jnp.concatenate([seg(0), seg(1), seg(4), seg(7), seg(2),
                              seg(5), seg(8),
                              seg(6), seg(3), seg(9), zpad,
                              seg(10), seg(11)], axis=1).astype(BF16)
    gains = jnp.stack([jnp.tile(qn_fox * scale, N_HEADS), jnp.tile(kn_fox, N_HEADS),
                       jnp.tile(qn_dsa * scale, N_HEADS)])
    kn128 = jnp.concatenate([kn_dsa, jnp.ones((HEAD_DIM,), F32)]).reshape(1, LANES)
    b2 = jnp.concatenate([jnp.zeros((HEAD_DIM,), F32), b_fgt,
                          jnp.zeros((LANES - HEAD_DIM - N_HEADS,), F32)]).reshape(1, LANES)
    cos128, sin_lo, sin_hi = _rope_tables(positions)

    fq, fk, dq, iq, fv, s1, s2, gates = _in_projection(
        x, mod3, norm1_g.reshape(1, D), w_perm, _block_diag_mean(512), _block_diag_mean(128),
        cos128, sin_lo, sin_hi, gains, kn128, b2, b_gate.reshape(1, 2 * D), tm)

    logf_t = jnp.transpose(s2[:, :, HEAD_DIM:HEAD_DIM + N_HEADS], (0, 2, 1))
    f_parts = jnp.transpose(_seq_cumsum(logf_t), (0, 2, 3, 1)).astype(BF16)
    k_heads = jnp.transpose(fk.reshape(B, S, N_HEADS, HEAD_DIM), (0, 2, 1, 3))
    k_aug = jnp.concatenate([k_heads, f_parts, jnp.zeros((B, N_HEADS, S, HEAD_DIM - 3), BF16)], axis=-1)
    oft = _fox_attention(k_aug, jnp.transpose(fq, (0, 2, 1)), jnp.transpose(fv, (0, 2, 1)))
    of = jnp.transpose(oft, (0, 2, 1))

    dk, ik = s1[:, :, :HEAD_DIM], s1[:, :, HEAD_DIM:]
    dvt = jnp.transpose(s2[:, :, :HEAD_DIM].astype(BF16), (0, 2, 1))
    iwt = jnp.transpose(s2[:, :, HEAD_DIM + N_HEADS:HEAD_DIM + 2 * N_HEADS], (0, 2, 1))
    odt = _dsa_attention(ik, dk, dvt, jnp.transpose(iq, (0, 2, 1)), jnp.transpose(dq, (0, 2, 1)), iwt, topk)
    od = jnp.transpose(odt, (0, 2, 1))

    wr = jnp.concatenate([r_w_grp, r_w_exp, jnp.zeros((D, LANES - N_GROUPS - N_EXPERTS), F32)], axis=1).astype(BF16)
    br = jnp.concatenate([r_b_grp, r_b_exp, jnp.zeros((LANES - N_GROUPS - N_EXPERTS,), F32)]).reshape(1, LANES)
    x1, h2, route = _post_attention(of, od, gates, x, mod3, w_proj_fox.astype(BF16), w_proj_dsa.astype(BF16),
                                    w_out.astype(BF16), norm2_g.reshape(1, D), wr, br, tm)
    x1, h2, route = x1.reshape(N, D), h2.reshape(N, D), route.reshape(N, LANES)

    tg = 512 if N * 2 >= 512 * N_EXPERTS else 128
    ranks, counts = _expert_ranks(route, tm)
    counts = counts[0, :N_EXPERTS].astype(jnp.int32)
    padded = ((counts + tg - 1) // tg) * tg
    ends = jnp.cumsum(padded)
    starts = ends - padded
    e01 = route[:, :2].astype(jnp.int32)
    start_of = jnp.sum(jnp.where(e01[..., None] == jnp.arange(N_EXPERTS, dtype=jnp.int32), starts, 0), axis=-1)
    pos = start_of + ranks[:, :2].astype(jnp.int32)
    n_rows = N * 2 + N_EXPERTS * tg
    n_tiles = n_rows // tg
    tile_start = jnp.arange(n_tiles, dtype=jnp.int32) * tg
    tile_expert = jnp.minimum(jnp.sum((ends[None, :] <= tile_start[:, None]).astype(jnp.int32), axis=1),
                              N_EXPERTS - 1)
    n_used = (ends[-1] // tg).astype(jnp.int32).reshape(1)

    td = min(256, S)
    pos3 = jnp.transpose(pos.reshape(N // td, td, 2), (0, 2, 1))
    xs = _dispatch(h2, pos3, jnp.maximum(ends - tg, 0).astype(jnp.int32), (padded > 0).astype(jnp.int32),
                   n_used, n_rows, td, tg)
    y = _experts(tile_expert, n_used, xs, w1, w3, w2, tg)
    out = _combine(pos3, y, x1, route, mod3[:, 5:6, :], td, S)
    return out.reshape(B, S, D)


def kernel(x, c, positions, ada_w, ada_b, norm1_g, norm2_g, w_in, b_fgt, b_gate, qn_fox, kn_fox, qn_dsa, kn_dsa, w_proj_fox, w_proj_dsa, w_out, router_w_grp, router_b_grp, router_w_exp, router_b_exp, exp_w1, exp_w3, exp_w2):
    for l in range(ada_w.shape[0]):
        c_mod = _modulation(c, ada_w[l], ada_b[l])
        x = _layer(x, c_mod, positions, norm1_g[l], norm2_g[l], w_in[l], b_fgt[l], b_gate[l],
                   qn_fox[l], kn_fox[l], qn_dsa[l], kn_dsa[l], w_proj_fox[l], w_proj_dsa[l], w_out[l],
                   router_w_grp[l], router_b_grp[l], router_w_exp[l], router_b_exp[l],
                   exp_w1[l], exp_w3[l], exp_w2[l])
    return x
```

```python
import functools

import jax
import jax.numpy as jnp
import numpy as np
from jax import lax
from jax.experimental import pallas as pl
from jax.experimental.pallas import tpu as pltpu

F32 = jnp.float32
BF16 = jnp.bfloat16

CHUNK = 64
CHUNK_SHIFT = 6
DMA_UNROLL = 8
HEAD_DIM = 64
N_HEADS = 8
W_HEADS = N_HEADS * HEAD_DIM
TOPK_MAX = 256
ROPE_THETA = 10000.0
N_GROUPS = 4
EXPERTS_PER_GROUP = 8
N_EXPERTS = N_GROUPS * EXPERTS_PER_GROUP
EPS = 1e-6
MASKED = -1e30

LANES = 128
VMEM_LIMIT = 56 * 1024 * 1024


def _cparams(sem):
    return pltpu.CompilerParams(dimension_semantics=sem, vmem_limit_bytes=VMEM_LIMIT)


def _mod_kernel(c_ref, w_ref, b_ref, o_ref):
    c = c_ref[...]
    ca = (c * jax.nn.sigmoid(c)).astype(BF16)
    o_ref[...] = jnp.dot(ca, w_ref[...].astype(BF16), preferred_element_type=F32) + b_ref[...]


def _modulation(c, ada_w, ada_b):
    B, D = c.shape
    n = ada_w.shape[1] // D
    return pl.pallas_call(
        _mod_kernel,
        grid=(n,),
        in_specs=[pl.BlockSpec((B, D), lambda j: (0, 0)),
                  pl.BlockSpec((D, D), lambda j: (0, j)),
                  pl.BlockSpec((1, D), lambda j: (0, j))],
        out_specs=pl.BlockSpec((B, D), lambda j: (0, j)),
        out_shape=jax.ShapeDtypeStruct((B, n * D), F32),
        compiler_params=_cparams(("arbitrary",)),
        name="adaln_mod",
    )(c, ada_w, ada_b.reshape(1, -1))


C_FQ, C_FK, C_DQ, C_IQ, C_FV = 0, 512, 1024, 1536, 2048
C_S1 = 2560
C_S2 = 2688
C_GATE = 2816
C_END = 4864


def _rope(x, cos, sin_lo, sin_hi):
    w = x.shape[-1]
    return x * cos + pltpu.roll(x, w - 32, 1) * sin_lo + pltpu.roll(x, 32, 1) * sin_hi


def _inproj_kernel(x_ref, mod_ref, g1_ref, w_ref, bd512_ref, bd128_ref, cos_ref, slo_ref, shi_ref,
                   gains_ref, kn128_ref, b2_ref, bg_ref,
                   fq_ref, fk_ref, dq_ref, iq_ref, fv_ref, s1_ref, s2_ref, gate_ref):
    x = x_ref[0]
    ms = jnp.mean(x * x, axis=-1, keepdims=True)
    h = x * lax.rsqrt(ms + EPS) * g1_ref[...]
    h = h * (1.0 + mod_ref[0, 1:2, :]) + mod_ref[0, 0:1, :]
    hb = h.astype(BF16)

    def proj(lo, hi):
        return jnp.dot(hb, w_ref[:, lo:hi], preferred_element_type=F32)

    def head_norm(y, gain):
        msq = jnp.dot((y * y).astype(BF16), bd512_ref[...], preferred_element_type=F32)
        return y * lax.rsqrt(msq + EPS) * gain

    cos1, slo1, shi1 = cos_ref[0], slo_ref[0], shi_ref[0]
    cos4 = jnp.concatenate([cos1] * 4, axis=1)
    slo4 = jnp.concatenate([slo1] * 4, axis=1)
    shi4 = jnp.concatenate([shi1] * 4, axis=1)

    fq_ref[0] = head_norm(proj(C_FQ, C_FQ + 512), gains_ref[0:1, :]).astype(BF16)
    fk_ref[0] = head_norm(proj(C_FK, C_FK + 512), gains_ref[1:2, :]).astype(BF16)
    dq = head_norm(proj(C_DQ, C_DQ + 512), gains_ref[2:3, :])
    dq_ref[0] = _rope(dq, cos4, slo4, shi4).astype(BF16)
    iq_ref[0] = _rope(proj(C_IQ, C_IQ + 512), cos4, slo4, shi4).astype(BF16)
    fv_ref[0] = proj(C_FV, C_FV + 512).astype(BF16)

    s1 = proj(C_S1, C_S1 + 128)
    msq = jnp.dot((s1 * s1).astype(BF16), bd128_ref[...], preferred_element_type=F32)
    lane = lax.broadcasted_iota(jnp.int32, s1.shape, 1)
    s1 = jnp.where(lane < HEAD_DIM, s1 * lax.rsqrt(msq + EPS) * kn128_ref[...], s1)
    s1_ref[0] = _rope(s1, cos1, slo1, shi1).astype(BF16)

    s2 = proj(C_S2, C_S2 + 128)
    z = s2 + b2_ref[...]
    logsig = jnp.minimum(z, 0.0) - jnp.log(1.0 + jnp.exp(-jnp.abs(z)))
    is_fgt = (lane >= HEAD_DIM) & (lane < HEAD_DIM + N_HEADS)
    s2_ref[0] = jnp.where(is_fgt, logsig, s2)

    gate_ref[0] = jax.nn.sigmoid(proj(C_GATE, C_END) + bg_ref[...]).astype(BF16)


def _in_projection(x, mod3, norm1_g, w_perm, bd512, bd128, cos, slo, shi, gains, kn128, b2, bg, tm):
    B, S, D = x.shape
    tok = lambda w: pl.BlockSpec((1, tm, w), lambda b, i: (b, i, 0))
    const = lambda shape: pl.BlockSpec(shape, lambda b, i: (0,) * len(shape))
    out_shapes = [jax.ShapeDtypeStruct((B, S, 512), BF16)] * 5 + [
        jax.ShapeDtypeStruct((B, S, 128), BF16),
        jax.ShapeDtypeStruct((B, S, 128), F32),
        jax.ShapeDtypeStruct((B, S, 2 * D), BF16)]
    return pl.pallas_call(
        _inproj_kernel,
        grid=(B, S // tm),
        in_specs=[tok(D),
                  pl.BlockSpec((1, 6, D), lambda b, i: (b, 0, 0)),
                  const((1, D)),
                  const(w_perm.shape),
                  const((512, 512)), const((128, 128)),
                  tok(128), tok(128), tok(128),
                  const((3, 512)), const((1, 128)), const((1, 128)), const((1, 2 * D))],
        out_specs=[tok(512)] * 5 + [tok(128), tok(128), tok(2 * D)],
        out_shape=out_shapes,
        compiler_params=_cparams(("parallel", "parallel")),
        name="in_projection",
    )(x, mod3, norm1_g, w_perm, bd512, bd128, cos, slo, shi, gains, kn128, b2, bg)


def _cumsum_kernel(x_ref, o_ref):
    x = x_ref[0]
    n = x.shape[-1]
    pos = lax.broadcasted_iota(jnp.int32, x.shape, 1)
    shift = 1
    while shift < n:
        x = x + jnp.where(pos >= shift, pltpu.roll(x, shift, 1), 0.0)
        shift *= 2
    hi = x.astype(BF16).astype(F32)
    mid = (x - hi).astype(BF16).astype(F32)
    o_ref[0, 0] = hi
    o_ref[0, 1] = mid
    o_ref[0, 2] = (x - hi - mid).astype(BF16).astype(F32)


def _seq_cumsum(logf_t):
    B, H, S = logf_t.shape
    return pl.pallas_call(
        _cumsum_kernel,
        grid=(B,),
        in_specs=[pl.BlockSpec((1, H, S), lambda b: (b, 0, 0))],
        out_specs=pl.BlockSpec((1, 3, H, S), lambda b: (b, 0, 0, 0)),
        out_shape=jax.ShapeDtypeStruct((B, 3, H, S), F32),
        compiler_params=_cparams(("parallel",)),
        name="forget_cumsum",
    )(logf_t)


KC = 256
SUB = 8


def _fold_rows(a, op, ways=1):
    n = a.shape[0] // SUB
    a = a.reshape(n, SUB, a.shape[1])
    chains = [a[w] for w in range(ways)]
    for j in range(ways, n):
        chains[j % ways] = op(chains[j % ways], a[j])
    while len(chains) > 1:
        chains = [op(chains[2 * j], chains[2 * j + 1]) for j in range(len(chains) // 2)]
    return chains[0]


def _softmax_pv(nch, s_ref, acc_ref, vt_at, m_all, o_ref):
    Q = o_ref.shape[-1]
    acc_ref[...] = jnp.zeros_like(acc_ref)

    def body(c, lsum):
        off = pl.multiple_of(c * KC, KC)
        new = []
        for hh in range(N_HEADS):
            p = jnp.exp(s_ref[hh, pl.ds(off, KC), :] - m_all[hh])
            new.append(lsum[hh] + _fold_rows(p, jnp.add))
            acc_ref[hh] += jnp.dot(vt_at(hh, off), p.astype(BF16), preferred_element_type=F32)
        return tuple(new)

    lsum = lax.fori_loop(0, nch, body, tuple(jnp.zeros((SUB, Q), F32) for _ in range(N_HEADS)))
    for hh in range(N_HEADS):
        l = jnp.sum(lsum[hh], axis=0, keepdims=True)
        o_ref[0, hh * HEAD_DIM:(hh + 1) * HEAD_DIM, :] = (acc_ref[hh] / l).astype(BF16)


def _fox_kernel(ka_ref, qt_ref, vt_ref, o_ref, s_ref, acc_ref):
    i = pl.program_id(1)
    Q = o_ref.shape[-1]
    row = lax.broadcasted_iota(jnp.int32, (HEAD_DIM, Q), 0)
    aug = jnp.where(row < 3, -1.0, 0.0).astype(BF16)
    qa = [jnp.concatenate([qt_ref[0, hh * HEAD_DIM:(hh + 1) * HEAD_DIM, :], aug], axis=0)
          for hh in range(N_HEADS)]

    def scores(c, mx, bias):
        off = pl.multiple_of(c * KC, KC)
        new = []
        for hh in range(N_HEADS):
            s = jnp.dot(ka_ref[0, hh, pl.ds(off, KC), :], qa[hh], preferred_element_type=F32)
            if bias is not None:
                s = s + bias
            s_ref[hh, pl.ds(off, KC), :] = s
            new.append(jnp.maximum(mx[hh], _fold_rows(s, jnp.maximum)))
        return tuple(new)

    mx = tuple(jnp.full((SUB, Q), MASKED, F32) for _ in range(N_HEADS))
    mx = lax.fori_loop(0, i, lambda c, m: scores(c, m, None), mx)
    kk = lax.broadcasted_iota(jnp.int32, (KC, Q), 0)
    qq = lax.broadcasted_iota(jnp.int32, (KC, Q), 1)
    mx = scores(i, mx, jnp.where(kk <= qq, 0.0, MASKED))
    m_all = [jnp.max(m, axis=0, keepdims=True) for m in mx]
    _softmax_pv(i + 1, s_ref, acc_ref, lambda hh, off: vt_ref[0, hh * HEAD_DIM:(hh + 1) * HEAD_DIM, pl.ds(off, KC)],
                m_all, o_ref)


def _fox_attention(k_aug, qt, vt):
    B, H, S, Wa = k_aug.shape
    return pl.pallas_call(
        _fox_kernel,
        grid=(B, S // KC),
        in_specs=[pl.BlockSpec((1, H, S, Wa), lambda b, i: (b, 0, 0, 0)),
                  pl.BlockSpec((1, W_HEADS, KC), lambda b, i: (b, 0, i)),
                  pl.BlockSpec((1, W_HEADS, S), lambda b, i: (b, 0, 0))],
        out_specs=pl.BlockSpec((1, W_HEADS, KC), lambda b, i: (b, 0, i)),
        out_shape=jax.ShapeDtypeStruct((B, W_HEADS, S), BF16),
        scratch_shapes=[pltpu.VMEM((H, S, KC), F32), pltpu.VMEM((H, HEAD_DIM, KC), F32)],
        compiler_params=_cparams(("parallel", "arbitrary")),
        name="fox_attention",
    )(k_aug, qt, vt)


INT_MIN = -(2 ** 31)
FIELD = 10


def _dsa_kernel(ik_ref, dk_ref, dvt_ref, iqt_ref, dqt_ref, iwt_ref, o_ref, key_ref, s_ref, acc_ref, *, topk):
    i = pl.program_id(1)
    Q = o_ref.shape[-1]
    nch = i + 1
    sub_k = lax.broadcasted_iota(jnp.int32, (KC, Q), 0)
    sub_r = lax.broadcasted_iota(jnp.int32, (CHUNK, Q), 0)
    q_chunk = (i * Q + lax.broadcasted_iota(jnp.int32, (CHUNK, Q), 1)) >> CHUNK_SHIFT
    iqts = [iqt_ref[0, hh * HEAD_DIM:(hh + 1) * HEAD_DIM, :] for hh in range(N_HEADS)]
    iws = [iwt_ref[0, hh:hh + 1, :] for hh in range(N_HEADS)]

    def score_chunk(c, _):
        for r in range(KC // CHUNK):
            off = pl.multiple_of(c * KC + r * CHUNK, CHUNK)
            ik = ik_ref[0, pl.ds(off, CHUNK), :]
            sc = jnp.zeros((CHUNK, Q), F32)
            for hh in range(N_HEADS):
                d = jnp.dot(ik, iqts[hh], preferred_element_type=F32)
                sc = sc + iws[hh] * jnp.maximum(d, 0.0)
            sc = sc + 0.0
            allowed = ((off + sub_r) >> CHUNK_SHIFT) <= q_chunk
            bits = pltpu.bitcast(sc, jnp.int32)
            key = bits ^ ((bits >> 31) & 0x7FFFFFFF)
            key_ref[pl.ds(off, CHUNK), :] = jnp.where(allowed, key, INT_MIN)
        return 0

    lax.fori_loop(0, nch, score_chunk, 0)

    def count(pred):
        def body(c, acc):
            off = pl.multiple_of(c * KC, KC)
            hit = pred(key_ref[pl.ds(off, KC), :], off + sub_k)
            return acc + _fold_rows(jnp.where(hit, 1.0, 0.0), jnp.add, ways=4)
        acc = lax.fori_loop(0, nch, body, jnp.zeros((SUB, Q), F32))
        return jnp.sum(acc, axis=0, keepdims=True)

    kf = jnp.float32(topk)

    n_nonneg = count(lambda k, _: k >= 0)
    top_half = n_nonneg >= kf
    thr = jnp.where(top_half, 0, INT_MIN).astype(jnp.int32)
    n_ge = jnp.where(top_half, n_nonneg, (nch * KC).astype(F32))
    def descend(j, carry):
        thr, n_ge = carry
        cand = thr + (jnp.int32(1) << (30 - j))
        n = count(lambda k, _: k >= cand)
        take = n >= kf
        return jnp.where(take, cand, thr), jnp.where(take, n, n_ge)

    thr, n_ge = lax.fori_loop(0, 31, descend, (thr, n_ge))

    excess = (n_ge > kf) & (thr > INT_MIN)

    @pl.when(jnp.max(jnp.where(excess, 1.0, 0.0)) > 0.0)
    def _():
        need = kf - count(lambda k, _: k > thr)
        nbits = int(np.ceil(np.log2(key_ref.shape[0]))) + 1

        def bound(j, last):
            cand = last + (jnp.int32(1) << (nbits - 1 - j))
            n = count(lambda k, idx: (k == thr) & (idx < cand))
            return jnp.where(n < need, cand, last)

        last = lax.fori_loop(0, nbits, bound, jnp.zeros((1, Q), jnp.int32))

        def demote(c, _):
            off = pl.multiple_of(c * KC, KC)
            k = key_ref[pl.ds(off, KC), :]
            drop = excess & (k == thr) & (off + sub_k > last)
            key_ref[pl.ds(off, KC), :] = jnp.where(drop, thr - 1, k)
            return 0

        lax.fori_loop(0, nch, demote, 0)

    keep_from = jnp.maximum(thr, INT_MIN + 1)

    qts = [dqt_ref[0, hh * HEAD_DIM:(hh + 1) * HEAD_DIM, :] for hh in range(N_HEADS)]

    def scores(c, mx):
        off = pl.multiple_of(c * KC, KC)
        bias = jnp.where(key_ref[pl.ds(off, KC), :] >= keep_from, 0.0, MASKED)
        dk = dk_ref[0, pl.ds(off, KC), :]
        new = []
        for hh in range(N_HEADS):
            s = jnp.dot(dk, qts[hh], preferred_element_type=F32) + bias
            s_ref[hh, pl.ds(off, KC), :] = s
            new.append(jnp.maximum(mx[hh], _fold_rows(s, jnp.maximum)))
        return tuple(new)

    mx = lax.fori_loop(0, nch, scores, tuple(jnp.full((SUB, Q), MASKED, F32) for _ in range(N_HEADS)))
    m_all = [jnp.max(m, axis=0, keepdims=True) for m in mx]
    _softmax_pv(nch, s_ref, acc_ref, lambda hh, off: dvt_ref[0, :, pl.ds(off, KC)], m_all, o_ref)


def _dsa_attention(ik, dk, dvt, iqt, dqt, iwt, topk):
    B, S, Dh = ik.shape
    seq = pl.BlockSpec((1, S, Dh), lambda b, i: (b, 0, 0))
    qcols = lambda r: pl.BlockSpec((1, r, KC), lambda b, i: (b, 0, i))
    return pl.pallas_call(
        functools.partial(_dsa_kernel, topk=topk),
        grid=(B, S // KC),
        in_specs=[seq, seq,
                  pl.BlockSpec((1, Dh, S), lambda b, i: (b, 0, 0)),
                  qcols(W_HEADS), qcols(W_HEADS), qcols(N_HEADS)],
        out_specs=qcols(W_HEADS),
        out_shape=jax.ShapeDtypeStruct((B, W_HEADS, S), BF16),
        scratch_shapes=[pltpu.VMEM((S, KC), jnp.int32), pltpu.VMEM((N_HEADS, S, KC), F32),
                        pltpu.VMEM((N_HEADS, HEAD_DIM, KC), F32)],
        compiler_params=_cparams(("parallel", "arbitrary")),
        name="dsa_attention",
    )(ik, dk, dvt, iqt, dqt, iwt)


def _first(mask, lane):
    return jnp.min(jnp.where(mask, lane, LANES), axis=-1, keepdims=True)


def _post_kernel(of_ref, od_ref, gate_ref, x_ref, mod_ref, wpf_ref, wpd_ref, wo_ref, g2_ref, wr_ref, br_ref,
                 x1_ref, h2_ref, route_ref):
    D = x_ref.shape[-1]
    pf = jnp.dot(of_ref[0], wpf_ref[...], preferred_element_type=F32)
    pd = jnp.dot(od_ref[0], wpd_ref[...], preferred_element_type=F32)
    merged = gate_ref[0, :, :D].astype(F32) * pf + gate_ref[0, :, D:].astype(F32) * pd
    y = jnp.dot(merged.astype(BF16), wo_ref[...], preferred_element_type=F32)
    x1 = x_ref[0] + mod_ref[0, 2:3, :] * y
    x1_ref[0] = x1

    ms = jnp.mean(x1 * x1, axis=-1, keepdims=True)
    h2 = x1 * lax.rsqrt(ms + EPS) * g2_ref[...]
    h2 = h2 * (1.0 + mod_ref[0, 4:5, :]) + mod_ref[0, 3:4, :]
    hb = h2.astype(BF16)
    h2_ref[0] = h2

    logits = jnp.dot(hb, wr_ref[...], preferred_element_type=F32) + br_ref[...]
    lane = lax.broadcasted_iota(jnp.int32, logits.shape, 1)
    is_grp = lane < N_GROUPS
    gl = jnp.where(is_grp, logits, -jnp.inf)
    gmax = jnp.max(gl, axis=-1, keepdims=True)
    g_idx = _first(gl == gmax, lane)
    g_w = 1.0 / jnp.sum(jnp.exp(gl - gmax), axis=-1, keepdims=True)

    e_lo = N_GROUPS + g_idx * EXPERTS_PER_GROUP
    in_grp = (lane >= e_lo) & (lane < e_lo + EXPERTS_PER_GROUP)
    el = jnp.where(in_grp, logits, -jnp.inf)
    emax = jnp.max(el, axis=-1, keepdims=True)
    ee = jnp.exp(el - emax)
    prob = ee / jnp.sum(ee, axis=-1, keepdims=True)
    prob = jnp.where(in_grp, prob, -1.0)
    p0 = jnp.max(prob, axis=-1, keepdims=True)
    l0 = _first(prob == p0, lane)
    rest = jnp.where(lane == l0, -1.0, prob)
    p1 = jnp.max(rest, axis=-1, keepdims=True)
    l1 = _first(rest == p1, lane)
    psum = p0 + p1
    w0 = g_w * (p0 / psum)
    w1 = g_w * (p1 / psum)
    e0 = (l0 - N_GROUPS).astype(F32)
    e1 = (l1 - N_GROUPS).astype(F32)
    route_ref[0] = jnp.where(lane == 0, e0, jnp.where(lane == 1, e1, jnp.where(lane == 2, w0,
                             jnp.where(lane == 3, w1, 0.0))))


def _post_attention(of, od, gates, x, mod3, wpf, wpd, wo, g2, wr, br, tm):
    B, S, D = x.shape
    tok = lambda w: pl.BlockSpec((1, tm, w), lambda b, i: (b, i, 0))
    const = lambda shape: pl.BlockSpec(shape, lambda b, i: (0,) * len(shape))
    return pl.pallas_call(
        _post_kernel,
        grid=(B, S // tm),
        in_specs=[tok(W_HEADS), tok(W_HEADS), tok(2 * D), tok(D),
                  pl.BlockSpec((1, 6, D), lambda b, i: (b, 0, 0)),
                  const(wpf.shape), const(wpd.shape), const(wo.shape),
                  const((1, D)), const((D, LANES)), const((1, LANES))],
        out_specs=[tok(D), tok(D), tok(LANES)],
        out_shape=[jax.ShapeDtypeStruct((B, S, D), F32),
                   jax.ShapeDtypeStruct((B, S, D), F32),
                   jax.ShapeDtypeStruct((B, S, LANES), F32)],
        compiler_params=_cparams(("parallel", "parallel")),
        name="merge_out_router",
    )(of, od, gates, x, mod3, wpf, wpd, wo, g2, wr, br)


def _rank_kernel(route_ref, tri_ref, rank_ref, count_ref, carry_ref):
    @pl.when(pl.program_id(0) == 0)
    def _():
        carry_ref[...] = jnp.zeros_like(carry_ref)

    r = route_ref[...]
    lane = lax.broadcasted_iota(jnp.int32, r.shape, 1).astype(F32)
    hot0 = lane == r[:, 0:1]
    hot1 = lane == r[:, 1:2]
    hits = jnp.where(hot0 | hot1, 1.0, 0.0)
    incl = jnp.dot(tri_ref[...], hits.astype(BF16), preferred_element_type=F32)
    before = incl - hits + carry_ref[...]
    r0 = jnp.sum(jnp.where(hot0, before, 0.0), axis=-1, keepdims=True)
    r1 = jnp.sum(jnp.where(hot1, before, 0.0), axis=-1, keepdims=True)
    rank_ref[...] = jnp.where(lane == 0.0, r0, jnp.where(lane == 1.0, r1, 0.0))
    carry_ref[...] = carry_ref[...] + jnp.sum(hits, axis=0, keepdims=True)
    count_ref[...] = carry_ref[...]


def _expert_ranks(route, tm):
    N = route.shape[0]
    tri = jnp.asarray(np.tril(np.ones((tm, tm), np.float32)), BF16)
    return pl.pallas_call(
        _rank_kernel,
        grid=(N // tm,),
        in_specs=[pl.BlockSpec((tm, LANES), lambda i: (i, 0)),
                  pl.BlockSpec((tm, tm), lambda i: (0, 0))],
        out_specs=[pl.BlockSpec((tm, LANES), lambda i: (i, 0)),
                   pl.BlockSpec((1, LANES), lambda i: (0, 0))],
        out_shape=[jax.ShapeDtypeStruct((N, LANES), F32), jax.ShapeDtypeStruct((1, LANES), F32)],
        scratch_shapes=[pltpu.VMEM((1, LANES), F32)],
        compiler_params=_cparams(("arbitrary",)),
        name="expert_ranks",
    )(route, tri)


def _dispatch_kernel(zstart_ref, zon_ref, nt_ref, pos_ref, h_ref, xs_ref, zbuf, sem, zsem, *, tm, tg):
    @pl.when(pl.program_id(0) == 0)
    def _():
        zbuf[...] = jnp.zeros_like(zbuf)

        def zero_tile(start):
            return pltpu.make_async_copy(zbuf, xs_ref.at[pl.ds(pl.multiple_of(start, tg), tg), :], zsem)

        n_tiles = xs_ref.shape[0] // tg
        for e in range(N_EXPERTS):
            pl.when(zon_ref[e] > 0)(lambda e=e: zero_tile(zstart_ref[e]).start())
        lax.fori_loop(nt_ref[0], n_tiles, lambda t, _: (zero_tile(t * tg).start(), 0)[1], 0)
        for e in range(N_EXPERTS):
            pl.when(zon_ref[e] > 0)(lambda e=e: zero_tile(zstart_ref[e]).wait())
        lax.fori_loop(nt_ref[0], n_tiles, lambda t, _: (zero_tile(t * tg).wait(), 0)[1], 0)

    def copy(r, slot):
        return pltpu.make_async_copy(h_ref.at[pl.ds(r, 1), :],
                                     xs_ref.at[pl.ds(pos_ref[0, slot, r], 1), :], sem)

    def issue(r, _):
        copy(r, 0).start()
        copy(r, 1).start()
        return 0

    lax.fori_loop(0, tm, issue, 0, unroll=DMA_UNROLL)
    for _ in range(2):
        pltpu.make_async_copy(h_ref, xs_ref.at[pl.ds(0, tm), :], sem).wait()


def _dispatch(h2, pos3, last_tile_start, has_rows, n_tiles_used, n_rows, tm, tg):
    N, D = h2.shape
    grid_spec = pltpu.PrefetchScalarGridSpec(
        num_scalar_prefetch=3,
        grid=(N // tm,),
        in_specs=[pl.BlockSpec((1, 2, tm), lambda i, zs, zo, nt: (i, 0, 0), memory_space=pltpu.SMEM),
                  pl.BlockSpec((tm, D), lambda i, zs, zo, nt: (i, 0))],
        out_specs=pl.BlockSpec(memory_space=pl.ANY),
        scratch_shapes=[pltpu.VMEM((tg, D), F32), pltpu.SemaphoreType.DMA(()), pltpu.SemaphoreType.DMA(())],
    )
    return pl.pallas_call(
        functools.partial(_dispatch_kernel, tm=tm, tg=tg),
        grid_spec=grid_spec,
        out_shape=jax.ShapeDtypeStruct((n_rows, D), F32),
        compiler_params=_cparams(("arbitrary",)),
        name="moe_dispatch",
    )(last_tile_start, has_rows, n_tiles_used, pos3, h2)


def _expert_kernel(te_ref, nt_ref, xs_ref, w1_ref, w3_ref, w2_ref, y_ref, w1b, w3b, w2b):
    g = pl.program_id(0)
    used = g < nt_ref[0]
    new_expert = (g == 0) | (te_ref[g] != te_ref[jnp.maximum(g - 1, 0)])

    @pl.when(used & new_expert)
    def _():
        w1b[...] = w1_ref[0].astype(BF16)
        w3b[...] = w3_ref[0].astype(BF16)
        w2b[...] = w2_ref[0].astype(BF16)

    @pl.when(used)
    def _():
        xb = xs_ref[...].astype(BF16)
        a = jnp.dot(xb, w1b[...], preferred_element_type=F32)
        b = jnp.dot(xb, w3b[...], preferred_element_type=F32)
        hmid = (a * jax.nn.sigmoid(a) * b).astype(BF16)
        y_ref[...] = jnp.dot(hmid, w2b[...], preferred_element_type=F32)

    @pl.when(jnp.logical_not(used))
    def _():
        y_ref[...] = jnp.zeros_like(y_ref)


def _experts(tile_expert, n_tiles_used, xs, w1, w3, w2, tg):
    P, D = xs.shape
    E, _, De = w1.shape
    row_tile = lambda g, te, nt: (jnp.minimum(g, nt[0] - 1), 0)
    grid_spec = pltpu.PrefetchScalarGridSpec(
        num_scalar_prefetch=2,
        grid=(P // tg,),
        in_specs=[pl.BlockSpec((tg, D), row_tile),
                  pl.BlockSpec((1, D, De), lambda g, te, nt: (te[g], 0, 0)),
                  pl.BlockSpec((1, D, De), lambda g, te, nt: (te[g], 0, 0)),
                  pl.BlockSpec((1, De, D), lambda g, te, nt: (te[g], 0, 0))],
        out_specs=pl.BlockSpec((tg, D), lambda g, te, nt: (g, 0)),
        scratch_shapes=[pltpu.VMEM((D, De), BF16), pltpu.VMEM((D, De), BF16), pltpu.VMEM((De, D), BF16)],
    )
    return pl.pallas_call(
        _expert_kernel,
        grid_spec=grid_spec,
        out_shape=jax.ShapeDtypeStruct((P, D), F32),
        compiler_params=_cparams(("arbitrary",)),
        name="moe_experts",
    )(tile_expert, n_tiles_used, xs, w1, w3, w2)


def _combine_kernel(pos_ref, y_ref, x1_ref, route_ref, gt_ref, o_ref, buf0, buf1, sem, *, tm):
    def copy(r, slot, buf):
        return pltpu.make_async_copy(y_ref.at[pl.ds(pos_ref[0, slot, r], 1), :],
                                     buf.at[pl.ds(r, 1), :], sem)

    def issue(r, _):
        copy(r, 0, buf0).start()
        copy(r, 1, buf1).start()
        return 0

    lax.fori_loop(0, tm, issue, 0, unroll=DMA_UNROLL)
    for buf in (buf0, buf1):
        pltpu.make_async_copy(y_ref.at[pl.ds(0, tm), :], buf, sem).wait()
    w0 = route_ref[:, 2:3]
    w1 = route_ref[:, 3:4]
    y = buf0[...] * w0 + buf1[...] * w1
    o_ref[...] = x1_ref[...] + gt_ref[0] * y


def _combine(pos3, y, x1, route, gt2, tm, S):
    N, D = x1.shape
    per_b = S // tm
    return pl.pallas_call(
        functools.partial(_combine_kernel, tm=tm),
        grid=(N // tm,),
        in_specs=[pl.BlockSpec((1, 2, tm), lambda i: (i, 0, 0), memory_space=pltpu.SMEM),
                  pl.BlockSpec(memory_space=pl.ANY),
                  pl.BlockSpec((tm, D), lambda i: (i, 0)),
                  pl.BlockSpec((tm, LANES), lambda i: (i, 0)),
                  pl.BlockSpec((1, 1, D), lambda i: (i // per_b, 0, 0))],
        out_specs=pl.BlockSpec((tm, D), lambda i: (i, 0)),
        out_shape=jax.ShapeDtypeStruct((N, D), F32),
        scratch_shapes=[pltpu.VMEM((tm, D), F32), pltpu.VMEM((tm, D), F32), pltpu.SemaphoreType.DMA(())],
        compiler_params=_cparams(("arbitrary",)),
        name="moe_combine",
    )(pos3, y, x1, route, gt2)


def _rope_tables(positions):
    half = HEAD_DIM // 2
    inv = ROPE_THETA ** (-jnp.arange(half, dtype=F32) / half)
    ang = positions.astype(F32)[..., None] * inv
    cos, sin = jnp.cos(ang), jnp.sin(ang)
    zero = jnp.zeros_like(sin)
    cos128 = jnp.concatenate([cos] * 4, axis=-1)
    sin_lo = jnp.concatenate([-sin, zero] * 2, axis=-1)
    sin_hi = jnp.concatenate([zero, sin] * 2, axis=-1)
    return cos128, sin_lo, sin_hi


def _block_diag_mean(width):
    blk = np.kron(np.eye(width // HEAD_DIM, dtype=np.float32), np.full((HEAD_DIM, HEAD_DIM), 1.0 / HEAD_DIM, np.float32))
    return jnp.asarray(blk, BF16)


def _layer(x, c_mod, positions, norm1_g, norm2_g, w_in, b_fgt, b_gate, qn_fox, kn_fox, qn_dsa, kn_dsa,
           w_proj_fox, w_proj_dsa, w_out, r_w_grp, r_b_grp, r_w_exp, r_b_exp, w1, w3, w2):
    B, S, D = x.shape
    N = B * S
    topk = min(TOPK_MAX, S // 4)
    tm = min(512, S)
    scale = HEAD_DIM ** -0.5
    mod3 = c_mod.reshape(B, 6, D)

    o = np.cumsum([0, 512, 512, 512, 8, 512, 64, 64, 512, 64, 8, D, D])
    seg = lambda k: w_in[:, o[k]:o[k + 1]]
    zpad = jnp.zeros((D, LANES - HEAD_DIM - 2 * N_HEADS), F32)
    w_perm = jnp.concatenate([seg(0), seg(1), seg(4), seg(7), seg(2),
                              seg(5), seg(8),
                              seg(6), seg(3), seg(9), zpad,
                              seg(10), seg(11)], axis=1).astype(BF16)
    gains = jnp.stack([jnp.tile(qn_fox * scale, N_HEADS), jnp.tile(kn_fox, N_HEADS),
                       jnp.tile(qn_dsa * scale, N_HEADS)])
    kn128 = jnp.concatenate([kn_dsa, jnp.ones((HEAD_DIM,), F32)]).reshape(1, LANES)
    b2 = jnp.concatenate([jnp.zeros((HEAD_DIM,), F32), b_fgt,
                          jnp.zeros((LANES - HEAD_DIM - N_HEADS,), F32)]).reshape(1, LANES)
    cos128, sin_lo, sin_hi = _rope_tables(positions)

    fq, fk, dq, iq, fv, s1, s2, gates = _in_projection(
        x, mod3, norm1_g.reshape(1, D), w_perm, _block_diag_mean(512), _block_diag_mean(128),
        cos128, sin_lo, sin_hi, gains, kn128, b2, b_gate.reshape(1, 2 * D), tm)

    logf_t = jnp.transpose(s2[:, :, HEAD_DIM:HEAD_DIM + N_HEADS], (0, 2, 1))
    f_parts = jnp.transpose(_seq_cumsum(logf_t), (0, 2, 3, 1)).astype(BF16)
    k_heads = jnp.transpose(fk.reshape(B, S, N_HEADS, HEAD_DIM), (0, 2, 1, 3))
    k_aug = jnp.concatenate([k_heads, f_parts, jnp.zeros((B, N_HEADS, S, HEAD_DIM - 3), BF16)], axis=-1)
    oft = _fox_attention(k_aug, jnp.transpose(fq, (0, 2, 1)), jnp.transpose(fv, (0, 2, 1)))
    of = jnp.transpose(oft, (0, 2, 1))

    dk, ik = s1[:, :, :HEAD_DIM], s1[:, :, HEAD_DIM:]
    dvt = jnp.transpose(s2[:, :, :HEAD_DIM].astype(BF16), (0, 2, 1))
    iwt = jnp.transpose(s2[:, :, HEAD_DIM + N_HEADS:HEAD_DIM + 2 * N_HEADS], (0, 2, 1))
    odt = _dsa_attention(ik, dk, dvt, jnp.transpose(iq, (0, 2, 1)), jnp.transpose(dq, (0, 2, 1)), iwt, topk)
    od = jnp.transpose(odt, (0, 2, 1))

    wr = jnp.concatenate([r_w_grp, r_w_exp, jnp.zeros((D, LANES - N_GROUPS - N_EXPERTS), F32)], axis=1).astype(BF16)
    br = jnp.concatenate([r_b_grp, r_b_exp, jnp.zeros((LANES - N_GROUPS - N_EXPERTS,), F32)]).reshape(1, LANES)
    x1, h2, route = _post_attention(of, od, gates, x, mod3, w_proj_fox.astype(BF16), w_proj_dsa.astype(BF16),
                                    w_out.astype(BF16), norm2_g.reshape(1, D), wr, br, tm)
    x1, h2, route = x1.reshape(N, D), h2.reshape(N, D), route.reshape(N, LANES)

    tg = 512 if N * 2 >= 512 * N_EXPERTS else 128
    ranks, counts = _expert_ranks(route, tm)
    counts = counts[0, :N_EXPERTS].astype(jnp.int32)
    padded = ((counts + tg - 1) // tg) * tg
    ends = jnp.cumsum(padded)
    starts = ends - padded
    e01 = route[:, :2].astype(jnp.int32)
    start_of = jnp.sum(jnp.where(e01[..., None] == jnp.arange(N_EXPERTS, dtype=jnp.int32), starts, 0), axis=-1)
    pos = start_of + ranks[:, :2].astype(jnp.int32)
    n_rows = N * 2 + N_EXPERTS * tg
    n_tiles = n_rows // tg
    tile_start = jnp.arange(n_tiles, dtype=jnp.int32) * tg
    tile_expert = jnp.minimum(jnp.sum((ends[None, :] <= tile_start[:, None]).astype(jnp.int32), axis=1),
                              N_EXPERTS - 1)
    n_used = (ends[-1] // tg).astype(jnp.int32).reshape(1)

    td = min(256, S)
    pos3 = jnp.transpose(pos.reshape(N // td, td, 2), (0, 2, 1))
    xs = _dispatch(h2, pos3, jnp.maximum(ends - tg, 0).astype(jnp.int32), (padded > 0).astype(jnp.int32),
                   n_used, n_rows, td, tg)
    y = _experts(tile_expert, n_used, xs, w1, w3, w2, tg)
    out = _combine(pos3, y, x1, route, mod3[:, 5:6, :], td, S)
    return out.reshape(B, S, D)


def kernel(x, c, positions, ada_w, ada_b, norm1_g, norm2_g, w_in, b_fgt, b_gate, qn_fox, kn_fox, qn_dsa, kn_dsa, w_proj_fox, w_proj_dsa, w_out, router_w_grp, router_b_grp, router_w_exp, router_b_exp, exp_w1, exp_w3, exp_w2):
    for l in range(ada_w.shape[0]):
        c_mod = _modulation(c, ada_w[l], ada_b[l])
        x = _layer(x, c_mod, positions, norm1_g[l], norm2_g[l], w_in[l], b_fgt[l], b_gate[l],
                   qn_fox[l], kn_fox[l], qn_dsa[l], kn_dsa[l], w_proj_fox[l], w_proj_dsa[l], w_out[l],
                   router_w_grp[l], router_b_grp[l], router_w_exp[l], router_b_exp[l],
                   exp_w1[l], exp_w3[l], exp_w2[l])
    return x
```

```python
import functools

import jax
import jax.numpy as jnp
import numpy as np
from jax import lax
from jax.experimental import pallas as pl
from jax.experimental.pallas import tpu as pltpu

F32 = jnp.float32
BF16 = jnp.bfloat16

CHUNK = 64
CHUNK_SHIFT = 6
DMA_UNROLL = 8
HEAD_DIM = 64
N_HEADS = 8
W_HEADS = N_HEADS * HEAD_DIM
TOPK_MAX = 256
ROPE_THETA = 10000.0
N_GROUPS = 4
EXPERTS_PER_GROUP = 8
N_EXPERTS = N_GROUPS * EXPERTS_PER_GROUP
EPS = 1e-6
MASKED = -1e30

LANES = 128
VMEM_LIMIT = 56 * 1024 * 1024


def _cparams(sem):
    return pltpu.CompilerParams(dimension_semantics=sem, vmem_limit_bytes=VMEM_LIMIT)


def _mod_kernel(c_ref, w_ref, b_ref, o_ref):
    c = c_ref[...]
    ca = (c * jax.nn.sigmoid(c)).astype(BF16)
    o_ref[...] = jnp.dot(ca, w_ref[...].astype(BF16), preferred_element_type=F32) + b_ref[...]


def _modulation(c, ada_w, ada_b):
    B, D = c.shape
    n = ada_w.shape[1] // D
    return pl.pallas_call(
        _mod_kernel,
        grid=(n,),
        in_specs=[pl.BlockSpec((B, D), lambda j: (0, 0)),
                  pl.BlockSpec((D, D), lambda j: (0, j)),
                  pl.BlockSpec((1, D), lambda j: (0, j))],
        out_specs=pl.BlockSpec((B, D), lambda j: (0, j)),
        out_shape=jax.ShapeDtypeStruct((B, n * D), F32),
        compiler_params=_cparams(("arbitrary",)),
        name="adaln_mod",
    )(c, ada_w, ada_b.reshape(1, -1))


R_FQ, R_DQ, R_IQ, R_FV = 0, 512, 1024, 1536
R_KK = 2048
R_S2 = 2176
R_END = 2304
HALF = HEAD_DIM // 2


def _inproj_kernel(x_ref, mod_ref, g1_ref, wtok_ref, wt_ref, bd512_ref, cos_ref, sin_ref,
                   gk_ref, gcol_ref, bf_ref, bg_ref,
                   fk_ref, gate_ref, fqt_ref, dqt_ref, iqt_ref, fvt_ref, kkt_ref, dvt_ref, lf_ref, iwt_ref):
    x = x_ref[0]
    ms = jnp.mean(x * x, axis=-1, keepdims=True)
    h = x * lax.rsqrt(ms + EPS) * g1_ref[...]
    h = h * (1.0 + mod_ref[0, 1:2, :]) + mod_ref[0, 0:1, :]
    hb = h.astype(BF16)
    D = x.shape[-1]
    cos, sin = cos_ref[0], sin_ref[0]

    def proj_t(lo, hi):
        return lax.dot_general(wt_ref[lo:hi, :], hb, (((1,), (1,)), ((), ())), preferred_element_type=F32)

    def norm_t(yh, gain):
        msq = jnp.mean(yh * yh, axis=0, keepdims=True)
        return yh * lax.rsqrt(msq + EPS) * gain

    def rope_store(ref, lo, yh):
        x1, x2 = yh[:HALF], yh[HALF:]
        ref[0, lo:lo + HALF, :] = (x1 * cos - x2 * sin).astype(ref.dtype)
        ref[0, lo + HALF:lo + HEAD_DIM, :] = (x2 * cos + x1 * sin).astype(ref.dtype)

    fq = proj_t(R_FQ, R_FQ + W_HEADS)
    dq = proj_t(R_DQ, R_DQ + W_HEADS)
    iq = proj_t(R_IQ, R_IQ + W_HEADS)
    for hh in range(N_HEADS):
        lo = hh * HEAD_DIM
        fqt_ref[0, lo:lo + HEAD_DIM, :] = norm_t(fq[lo:lo + HEAD_DIM], gcol_ref[0]).astype(BF16)
        rope_store(dqt_ref, lo, norm_t(dq[lo:lo + HEAD_DIM], gcol_ref[1]))
        rope_store(iqt_ref, lo, iq[lo:lo + HEAD_DIM])
    fvt_ref[0] = proj_t(R_FV, R_FV + W_HEADS).astype(BF16)

    kk = proj_t(R_KK, R_KK + 2 * HEAD_DIM)
    rope_store(kkt_ref, 0, norm_t(kk[:HEAD_DIM], gcol_ref[2]))
    rope_store(kkt_ref, HEAD_DIM, kk[HEAD_DIM:])

    s2 = proj_t(R_S2, R_S2 + LANES)
    dvt_ref[0] = s2[:HEAD_DIM].astype(BF16)
    z = s2[HEAD_DIM:HEAD_DIM + N_HEADS] + bf_ref[...]
    lf_ref[0] = jnp.minimum(z, 0.0) - jnp.log(1.0 + jnp.exp(-jnp.abs(z)))
    iwt_ref[0] = s2[HEAD_DIM + N_HEADS:HEAD_DIM + 2 * N_HEADS]

    fk = jnp.dot(hb, wtok_ref[:, :W_HEADS], preferred_element_type=F32)
    msq = jnp.dot((fk * fk).astype(BF16), bd512_ref[...], preferred_element_type=F32)
    fk = (fk * lax.rsqrt(msq + EPS) * gk_ref[...]).astype(BF16)
    for hh in range(N_HEADS):
        fk_ref[0, hh] = fk[:, hh * HEAD_DIM:(hh + 1) * HEAD_DIM]
    g = jnp.dot(hb, wtok_ref[:, W_HEADS:], preferred_element_type=F32)
    gate_ref[0] = jax.nn.sigmoid(g + bg_ref[...]).astype(BF16)


def _in_projection(x, mod3, norm1_g, w_tok, w_t, bd512, cos_t, sin_t, gk, gcol, bf, bg, tm):
    B, S, D = x.shape
    tok = lambda w: pl.BlockSpec((1, tm, w), lambda b, i: (b, i, 0))
    feat = lambda r: pl.BlockSpec((1, r, tm), lambda b, i: (b, 0, i))
    const = lambda shape: pl.BlockSpec(shape, lambda b, i: (0,) * len(shape))
    out_shapes = [jax.ShapeDtypeStruct((B, N_HEADS, S, HEAD_DIM), BF16),
                  jax.ShapeDtypeStruct((B, S, 2 * D), BF16)] + \
                 [jax.ShapeDtypeStruct((B, W_HEADS, S), BF16)] * 4 + \
                 [jax.ShapeDtypeStruct((B, 2 * HEAD_DIM, S), BF16),
                  jax.ShapeDtypeStruct((B, HEAD_DIM, S), BF16),
                  jax.ShapeDtypeStruct((B, N_HEADS, S), F32),
                  jax.ShapeDtypeStruct((B, N_HEADS, S), F32)]
    return pl.pallas_call(
        _inproj_kernel,
        grid=(B, S // tm),
        in_specs=[tok(D),
                  pl.BlockSpec((1, 6, D), lambda b, i: (b, 0, 0)),
                  const((1, D)),
                  const(w_tok.shape), const(w_t.shape), const((W_HEADS, W_HEADS)),
                  feat(HALF), feat(HALF),
                  const((1, W_HEADS)), const((3, HEAD_DIM, 1)), const((N_HEADS, 1)), const((1, 2 * D))],
        out_specs=[pl.BlockSpec((1, N_HEADS, tm, HEAD_DIM), lambda b, i: (b, 0, i, 0)), tok(2 * D),
                   feat(W_HEADS), feat(W_HEADS), feat(W_HEADS), feat(W_HEADS),
                   feat(2 * HEAD_DIM), feat(HEAD_DIM), feat(N_HEADS), feat(N_HEADS)],
        out_shape=out_shapes,
        compiler_params=_cparams(("parallel", "parallel")),
        name="in_projection",
    )(x, mod3, norm1_g, w_tok, w_t, bd512, cos_t, sin_t, gk, gcol, bf, bg)


def _cumsum_kernel(x_ref, o_ref):
    x = x_ref[0]
    n = x.shape[-1]
    pos = lax.broadcasted_iota(jnp.int32, x.shape, 1)
    shift = 1
    while shift < n:
        x = x + jnp.where(pos >= shift, pltpu.roll(x, shift, 1), 0.0)
        shift *= 2
    o_ref[0] = x


def _seq_cumsum(logf_t):
    B, H, S = logf_t.shape
    return pl.pallas_call(
        _cumsum_kernel,
        grid=(B,),
        in_specs=[pl.BlockSpec((1, H, S), lambda b: (b, 0, 0))],
        out_specs=pl.BlockSpec((1, H, S), lambda b: (b, 0, 0)),
        out_shape=jax.ShapeDtypeStruct((B, H, S), F32),
        compiler_params=_cparams(("parallel",)),
        name="forget_cumsum",
    )(logf_t)


KC = 256
SUB = 8


def _fold_rows(a, op, ways=1):
    n = a.shape[0] // SUB
    a = a.reshape(n, SUB, a.shape[1])
    chains = [a[w] for w in range(ways)]
    for j in range(ways, n):
        chains[j % ways] = op(chains[j % ways], a[j])
    while len(chains) > 1:
        chains = [op(chains[2 * j], chains[2 * j + 1]) for j in range(len(chains) // 2)]
    return chains[0]


def _softmax_pv(nch, s_ref, acc_ref, vt_at, m_all, o_ref):
    Q = o_ref.shape[1]
    acc_ref[...] = jnp.zeros_like(acc_ref)

    def body(c, lsum):
        off = pl.multiple_of(c * KC, KC)
        new = []
        for hh in range(N_HEADS):
            p = jnp.exp(s_ref[hh, pl.ds(off, KC), :] - m_all[hh])
            new.append(lsum[hh] + _fold_rows(p, jnp.add))
            acc_ref[hh] += jnp.dot(vt_at(hh, off), p.astype(BF16), preferred_element_type=F32)
        return tuple(new)

    lsum = lax.fori_loop(0, nch, body, tuple(jnp.zeros((SUB, Q), F32) for _ in range(N_HEADS)))
    for hh in range(N_HEADS):
        acc_ref[hh] = acc_ref[hh] / jnp.sum(lsum[hh], axis=0, keepdims=True)
    out_t = acc_ref[...].reshape(N_HEADS * HEAD_DIM, Q)
    o_ref[0] = out_t.T.astype(BF16)


def _fox_kernel(k_ref, f_ref, qt_ref, vt_ref, o_ref, s_ref, acc_ref):
    i = pl.program_id(1)
    Q = o_ref.shape[1]
    qts = [qt_ref[0, hh * HEAD_DIM:(hh + 1) * HEAD_DIM, :] for hh in range(N_HEADS)]

    def scores(c, mx, bias):
        off = pl.multiple_of(c * KC, KC)
        new = []
        for hh in range(N_HEADS):
            s = jnp.dot(k_ref[0, hh, pl.ds(off, KC), :], qts[hh], preferred_element_type=F32)
            s = s - f_ref[0, pl.ds(off, KC), hh:hh + 1]
            if bias is not None:
                s = s + bias
            s_ref[hh, pl.ds(off, KC), :] = s
            new.append(jnp.maximum(mx[hh], _fold_rows(s, jnp.maximum)))
        return tuple(new)

    mx = tuple(jnp.full((SUB, Q), MASKED, F32) for _ in range(N_HEADS))
    mx = lax.fori_loop(0, i, lambda c, m: scores(c, m, None), mx)
    kk = lax.broadcasted_iota(jnp.int32, (KC, Q), 0)
    qq = lax.broadcasted_iota(jnp.int32, (KC, Q), 1)
    mx = scores(i, mx, jnp.where(kk <= qq, 0.0, MASKED))
    m_all = [jnp.max(m, axis=0, keepdims=True) for m in mx]
    _softmax_pv(i + 1, s_ref, acc_ref, lambda hh, off: vt_ref[0, hh * HEAD_DIM:(hh + 1) * HEAD_DIM, pl.ds(off, KC)],
                m_all, o_ref)


def _fox_attention(k_heads, f_tok, qt, vt):
    B, H, S, Dh = k_heads.shape
    return pl.pallas_call(
        _fox_kernel,
        grid=(B, S // KC),
        in_specs=[pl.BlockSpec((1, H, S, Dh), lambda b, i: (b, 0, 0, 0)),
                  pl.BlockSpec((1, S, H), lambda b, i: (b, 0, 0)),
                  pl.BlockSpec((1, W_HEADS, KC), lambda b, i: (b, 0, i)),
                  pl.BlockSpec((1, W_HEADS, S), lambda b, i: (b, 0, 0))],
        out_specs=pl.BlockSpec((1, KC, W_HEADS), lambda b, i: (b, i, 0)),
        out_shape=jax.ShapeDtypeStruct((B, S, W_HEADS), BF16),
        scratch_shapes=[pltpu.VMEM((H, S, KC), F32), pltpu.VMEM((H, HEAD_DIM, KC), F32)],
        compiler_params=_cparams(("parallel", "arbitrary")),
        name="fox_attention",
    )(k_heads, f_tok, qt, vt)


INT_MIN = -(2 ** 31)
FIELD = 10


def _dsa_kernel(kkt_ref, dvt_ref, iqt_ref, dqt_ref, iwt_ref, o_ref, key_ref, s_ref, acc_ref, dk_ref, ik_ref,
                *, topk):
    i = pl.program_id(1)
    Q = o_ref.shape[1]
    nch = i + 1

    @pl.when(i == 0)
    def _():
        def to_rows(c, _):
            off = pl.multiple_of(c * KC, KC)
            rows = kkt_ref[0, :, pl.ds(off, KC)].astype(F32).T
            dk_ref[pl.ds(off, KC), :] = rows[:, :HEAD_DIM].astype(BF16)
            ik_ref[pl.ds(off, KC), :] = rows[:, HEAD_DIM:].astype(BF16)
            return 0
        lax.fori_loop(0, kkt_ref.shape[-1] // KC, to_rows, 0)

    sub_k = lax.broadcasted_iota(jnp.int32, (KC, Q), 0)
    sub_r = lax.broadcasted_iota(jnp.int32, (CHUNK, Q), 0)
    q_chunk = (i * Q + lax.broadcasted_iota(jnp.int32, (CHUNK, Q), 1)) >> CHUNK_SHIFT
    iqts = [iqt_ref[0, hh * HEAD_DIM:(hh + 1) * HEAD_DIM, :] for hh in range(N_HEADS)]
    iws = [iwt_ref[0, hh:hh + 1, :] for hh in range(N_HEADS)]

    def score_chunk(c, _):
        for r in range(KC // CHUNK):
            off = pl.multiple_of(c * KC + r * CHUNK, CHUNK)
            ik = ik_ref[pl.ds(off, CHUNK), :]
            sc = jnp.zeros((CHUNK, Q), F32)
            for hh in range(N_HEADS):
                d = jnp.dot(ik, iqts[hh], preferred_element_type=F32)
                sc = sc + iws[hh] * jnp.maximum(d, 0.0)
            sc = sc + 0.0
            allowed = ((off + sub_r) >> CHUNK_SHIFT) <= q_chunk
            bits = pltpu.bitcast(sc, jnp.int32)
            key = bits ^ ((bits >> 31) & 0x7FFFFFFF)
            key_ref[pl.ds(off, CHUNK), :] = jnp.where(allowed, key, INT_MIN)
        return 0

    lax.fori_loop(0, nch, score_chunk, 0)

    def count(pred):
        def body(c, acc):
            off = pl.multiple_of(c * KC, KC)
            hit = pred(key_ref[pl.ds(off, KC), :], off + sub_k)
            return acc + _fold_rows(jnp.where(hit, 1.0, 0.0), jnp.add, ways=4)
        acc = lax.fori_loop(0, nch, body, jnp.zeros((SUB, Q), F32))
        return jnp.sum(acc, axis=0, keepdims=True)

    kf = jnp.float32(topk)

    n_nonneg = count(lambda k, _: k >= 0)
    top_half = n_nonneg >= kf
    thr = jnp.where(top_half, 0, INT_MIN).astype(jnp.int32)
    n_ge = jnp.where(top_half, n_nonneg, (nch * KC).astype(F32))

    def descend(j, carry):
        thr, n_ge = carry
        cand = thr + (jnp.int32(1) << (30 - j))
        n = count(lambda k, _: k >= cand)
        take = n >= kf
        return jnp.where(take, cand, thr), jnp.where(take, n, n_ge)

    thr, n_ge = lax.fori_loop(0, 31, descend, (thr, n_ge))

    excess = (n_ge > kf) & (thr > INT_MIN)

    @pl.when(jnp.max(jnp.where(excess, 1.0, 0.0)) > 0.0)
    def _():
        need = kf - count(lambda k, _: k > thr)
        nbits = int(np.ceil(np.log2(key_ref.shape[0]))) + 1

        def bound(j, last):
            cand = last + (jnp.int32(1) << (nbits - 1 - j))
            n = count(lambda k, idx: (k == thr) & (idx < cand))
            return jnp.where(n < need, cand, last)

        last = lax.fori_loop(0, nbits, bound, jnp.zeros((1, Q), jnp.int32))

        def demote(c, _):
            off = pl.multiple_of(c * KC, KC)
            k = key_ref[pl.ds(off, KC), :]
            drop = excess & (k == thr) & (off + sub_k > last)
            key_ref[pl.ds(off, KC), :] = jnp.where(drop, thr - 1, k)
            return 0

        lax.fori_loop(0, nch, demote, 0)

    keep_from = jnp.maximum(thr, INT_MIN + 1)

    qts = [dqt_ref[0, hh * HEAD_DIM:(hh + 1) * HEAD_DIM, :] for hh in range(N_HEADS)]

    def scores(c, mx):
        off = pl.multiple_of(c * KC, KC)
        bias = jnp.where(key_ref[pl.ds(off, KC), :] >= keep_from, 0.0, MASKED)
        dk = dk_ref[pl.ds(off, KC), :]
        new = []
        for hh in range(N_HEADS):
            s = jnp.dot(dk, qts[hh], preferred_element_type=F32) + bias
            s_ref[hh, pl.ds(off, KC), :] = s
            new.append(jnp.maximum(mx[hh], _fold_rows(s, jnp.maximum)))
        return tuple(new)

    mx = lax.fori_loop(0, nch, scores, tuple(jnp.full((SUB, Q), MASKED, F32) for _ in range(N_HEADS)))
    m_all = [jnp.max(m, axis=0, keepdims=True) for m in mx]
    _softmax_pv(nch, s_ref, acc_ref, lambda hh, off: dvt_ref[0, :, pl.ds(off, KC)], m_all, o_ref)


def _dsa_attention(kkt, dvt, iqt, dqt, iwt, topk):
    B, Dh, S = dvt.shape
    rows = lambda r: pl.BlockSpec((1, r, S), lambda b, i: (b, 0, 0))
    qcols = lambda r: pl.BlockSpec((1, r, KC), lambda b, i: (b, 0, i))
    return pl.pallas_call(
        functools.partial(_dsa_kernel, topk=topk),
        grid=(B, S // KC),
        in_specs=[rows(2 * Dh), rows(Dh), qcols(W_HEADS), qcols(W_HEADS), qcols(N_HEADS)],
        out_specs=pl.BlockSpec((1, KC, W_HEADS), lambda b, i: (b, i, 0)),
        out_shape=jax.ShapeDtypeStruct((B, S, W_HEADS), BF16),
        scratch_shapes=[pltpu.VMEM((S, KC), jnp.int32), pltpu.VMEM((N_HEADS, S, KC), F32),
                        pltpu.VMEM((N_HEADS, HEAD_DIM, KC), F32),
                        pltpu.VMEM((S, Dh), BF16), pltpu.VMEM((S, Dh), BF16)],
        compiler_params=_cparams(("parallel", "arbitrary")),
        name="dsa_attention",
    )(kkt, dvt, iqt, dqt, iwt)


def _first(mask, lane):
    return jnp.min(jnp.where(mask, lane, LANES), axis=-1, keepdims=True)


def _post_kernel(of_ref, od_ref, gate_ref, x_ref, mod_ref, wpf_ref, wpd_ref, wo_ref, g2_ref, wr_ref, br_ref,
                 x1_ref, h2_ref, route_ref):
    D = x_ref.shape[-1]
    pf = jnp.dot(of_ref[0], wpf_ref[...], preferred_element_type=F32)
    pd = jnp.dot(od_ref[0], wpd_ref[...], preferred_element_type=F32)
    merged = gate_ref[0, :, :D].astype(F32) * pf + gate_ref[0, :, D:].astype(F32) * pd
    y = jnp.dot(merged.astype(BF16), wo_ref[...], preferred_element_type=F32)
    x1 = x_ref[0] + mod_ref[0, 2:3, :] * y
    x1_ref[0] = x1

    ms = jnp.mean(x1 * x1, axis=-1, keepdims=True)
    h2 = x1 * lax.rsqrt(ms + EPS) * g2_ref[...]
    h2 = h2 * (1.0 + mod_ref[0, 4:5, :]) + mod_ref[0, 3:4, :]
    hb = h2.astype(BF16)
    h2_ref[0] = h2

    logits = jnp.dot(hb, wr_ref[...], preferred_element_type=F32) + br_ref[...]
    lane = lax.broadcasted_iota(jnp.int32, logits.shape, 1)
    is_grp = lane < N_GROUPS
    gl = jnp.where(is_grp, logits, -jnp.inf)
    gmax = jnp.max(gl, axis=-1, keepdims=True)
    g_idx = _first(gl == gmax, lane)
    g_w = 1.0 / jnp.sum(jnp.exp(gl - gmax), axis=-1, keepdims=True)

    e_lo = N_GROUPS + g_idx * EXPERTS_PER_GROUP
    in_grp = (lane >= e_lo) & (lane < e_lo + EXPERTS_PER_GROUP)
    el = jnp.where(in_grp, logits, -jnp.inf)
    emax = jnp.max(el, axis=-1, keepdims=True)
    ee = jnp.exp(el - emax)
    prob = ee / jnp.sum(ee, axis=-1, keepdims=True)
    prob = jnp.where(in_grp, prob, -1.0)
    p0 = jnp.max(prob, axis=-1, keepdims=True)
    l0 = _first(prob == p0, lane)
    rest = jnp.where(lane == l0, -1.0, prob)
    p1 = jnp.max(rest, axis=-1, keepdims=True)
    l1 = _first(rest == p1, lane)
    psum = p0 + p1
    w0 = g_w * (p0 / psum)
    w1 = g_w * (p1 / psum)
    e0 = (l0 - N_GROUPS).astype(F32)
    e1 = (l1 - N_GROUPS).astype(F32)
    route_ref[0] = jnp.where(lane == 0, e0, jnp.where(lane == 1, e1, jnp.where(lane == 2, w0,
                             jnp.where(lane == 3, w1, 0.0))))


def _post_attention(of, od, gates, x, mod3, wpf, wpd, wo, g2, wr, br, tm):
    B, S, D = x.shape
    tok = lambda w: pl.BlockSpec((1, tm, w), lambda b, i: (b, i, 0))
    const = lambda shape: pl.BlockSpec(shape, lambda b, i: (0,) * len(shape))
    return pl.pallas_call(
        _post_kernel,
        grid=(B, S // tm),
        in_specs=[tok(W_HEADS), tok(W_HEADS), tok(2 * D), tok(D),
                  pl.BlockSpec((1, 6, D), lambda b, i: (b, 0, 0)),
                  const(wpf.shape), const(wpd.shape), const(wo.shape),
                  const((1, D)), const((D, LANES)), const((1, LANES))],
        out_specs=[tok(D), tok(D), tok(LANES)],
        out_shape=[jax.ShapeDtypeStruct((B, S, D), F32),
                   jax.ShapeDtypeStruct((B, S, D), F32),
                   jax.ShapeDtypeStruct((B, S, LANES), F32)],
        compiler_params=_cparams(("parallel", "parallel")),
        name="merge_out_router",
    )(of, od, gates, x, mod3, wpf, wpd, wo, g2, wr, br)


def _rank_kernel(route_ref, tri_ref, rank_ref, count_ref, carry_ref):
    @pl.when(pl.program_id(0) == 0)
    def _():
        carry_ref[...] = jnp.zeros_like(carry_ref)

    r = route_ref[...]
    lane = lax.broadcasted_iota(jnp.int32, r.shape, 1).astype(F32)
    hot0 = lane == r[:, 0:1]
    hot1 = lane == r[:, 1:2]
    hits = jnp.where(hot0 | hot1, 1.0, 0.0)
    incl = jnp.dot(tri_ref[...], hits.astype(BF16), preferred_element_type=F32)
    before = incl - hits + carry_ref[...]
    r0 = jnp.sum(jnp.where(hot0, before, 0.0), axis=-1, keepdims=True)
    r1 = jnp.sum(jnp.where(hot1, before, 0.0), axis=-1, keepdims=True)
    rank_ref[...] = jnp.where(lane == 0.0, r0, jnp.where(lane == 1.0, r1, 0.0))
    carry_ref[...] = carry_ref[...] + jnp.sum(hits, axis=0, keepdims=True)
    count_ref[...] = carry_ref[...]


def _expert_ranks(route, tm):
    N = route.shape[0]
    tri = jnp.asarray(np.tril(np.ones((tm, tm), np.float32)), BF16)
    return pl.pallas_call(
        _rank_kernel,
        grid=(N // tm,),
        in_specs=[pl.BlockSpec((tm, LANES), lambda i: (i, 0)),
                  pl.BlockSpec((tm, tm), lambda i: (0, 0))],
        out_specs=[pl.BlockSpec((tm, LANES), lambda i: (i, 0)),
                   pl.BlockSpec((1, LANES), lambda i: (0, 0))],
        out_shape=[jax.ShapeDtypeStruct((N, LANES), F32), jax.ShapeDtypeStruct((1, LANES), F32)],
        scratch_shapes=[pltpu.VMEM((1, LANES), F32)],
        compiler_params=_cparams(("arbitrary",)),
        name="expert_ranks",
    )(route, tri)


def _dispatch_kernel(zstart_ref, zon_ref, nt_ref, pos_ref, h_ref, xs_ref, zbuf, sem, zsem, *, tm, tg):
    @pl.when(pl.program_id(0) == 0)
    def _():
        zbuf[...] = jnp.zeros_like(zbuf)

        def zero_tile(start):
            return pltpu.make_async_copy(zbuf, xs_ref.at[pl.ds(pl.multiple_of(start, tg), tg), :], zsem)

        n_tiles = xs_ref.shape[0] // tg
        for e in range(N_EXPERTS):
            pl.when(zon_ref[e] > 0)(lambda e=e: zero_tile(zstart_ref[e]).start())
        lax.fori_loop(nt_ref[0], n_tiles, lambda t, _: (zero_tile(t * tg).start(), 0)[1], 0)
        for e in range(N_EXPERTS):
            pl.when(zon_ref[e] > 0)(lambda e=e: zero_tile(zstart_ref[e]).wait())
        lax.fori_loop(nt_ref[0], n_tiles, lambda t, _: (zero_tile(t * tg).wait(), 0)[1], 0)

    def copy(r, slot):
        return pltpu.make_async_copy(h_ref.at[pl.ds(r, 1), :],
                                     xs_ref.at[pl.ds(pos_ref[0, slot, r], 1), :], sem)

    def issue(r, _):
        copy(r, 0).start()
        copy(r, 1).start()
        return 0

    lax.fori_loop(0, tm, issue, 0, unroll=DMA_UNROLL)
    for _ in range(2):
        pltpu.make_async_copy(h_ref, xs_ref.at[pl.ds(0, tm), :], sem).wait()


def _dispatch(h2, pos3, last_tile_start, has_rows, n_tiles_used, n_rows, tm, tg):
    N, D = h2.shape
    grid_spec = pltpu.PrefetchScalarGridSpec(
        num_scalar_prefetch=3,
        grid=(N // tm,),
        in_specs=[pl.BlockSpec((1, 2, tm), lambda i, zs, zo, nt: (i, 0, 0), memory_space=pltpu.SMEM),
                  pl.BlockSpec((tm, D), lambda i, zs, zo, nt: (i, 0))],
        out_specs=pl.BlockSpec(memory_space=pl.ANY),
        scratch_shapes=[pltpu.VMEM((tg, D), F32), pltpu.SemaphoreType.DMA(()), pltpu.SemaphoreType.DMA(())],
    )
    return pl.pallas_call(
        functools.partial(_dispatch_kernel, tm=tm, tg=tg),
        grid_spec=grid_spec,
        out_shape=jax.ShapeDtypeStruct((n_rows, D), F32),
        compiler_params=_cparams(("arbitrary",)),
        name="moe_dispatch",
    )(last_tile_start, has_rows, n_tiles_used, pos3, h2)


def _expert_kernel(te_ref, nt_ref, xs_ref, w1_ref, w3_ref, w2_ref, y_ref, w1b, w3b, w2b):
    g = pl.program_id(0)
    used = g < nt_ref[0]
    new_expert = (g == 0) | (te_ref[g] != te_ref[jnp.maximum(g - 1, 0)])

    @pl.when(used & new_expert)
    def _():
        w1b[...] = w1_ref[0].astype(BF16)
        w3b[...] = w3_ref[0].astype(BF16)
        w2b[...] = w2_ref[0].astype(BF16)

    @pl.when(used)
    def _():
        xb = xs_ref[...].astype(BF16)
        a = jnp.dot(xb, w1b[...], preferred_element_type=F32)
        b = jnp.dot(xb, w3b[...], preferred_element_type=F32)
        hmid = (a * jax.nn.sigmoid(a) * b).astype(BF16)
        y_ref[...] = jnp.dot(hmid, w2b[...], preferred_element_type=F32)

    @pl.when(jnp.logical_not(used))
    def _():
        y_ref[...] = jnp.zeros_like(y_ref)


def _experts(tile_expert, n_tiles_used, xs, w1, w3, w2, tg):
    P, D = xs.shape
    E, _, De = w1.shape
    row_tile = lambda g, te, nt: (jnp.minimum(g, nt[0] - 1), 0)
    grid_spec = pltpu.PrefetchScalarGridSpec(
        num_scalar_prefetch=2,
        grid=(P // tg,),
        in_specs=[pl.BlockSpec((tg, D), row_tile),
                  pl.BlockSpec((1, D, De), lambda g, te, nt: (te[g], 0, 0)),
                  pl.BlockSpec((1, D, De), lambda g, te, nt: (te[g], 0, 0)),
                  pl.BlockSpec((1, De, D), lambda g, te, nt: (te[g], 0, 0))],
        out_specs=pl.BlockSpec((tg, D), lambda g, te, nt: (g, 0)),
        scratch_shapes=[pltpu.VMEM((D, De), BF16), pltpu.VMEM((D, De), BF16), pltpu.VMEM((De, D), BF16)],
    )
    return pl.pallas_call(
        _expert_kernel,
        grid_spec=grid_spec,
        out_shape=jax.ShapeDtypeStruct((P, D), F32),
        compiler_params=_cparams(("arbitrary",)),
        name="moe_experts",
    )(tile_expert, n_tiles_used, xs, w1, w3, w2)


def _combine_kernel(pos_ref, y_ref, x1_ref, route_ref, gt_ref, o_ref, buf0, buf1, sem, *, tm):
    def copy(r, slot, buf):
        return pltpu.make_async_copy(y_ref.at[pl.ds(pos_ref[0, slot, r], 1), :],
                                     buf.at[pl.ds(r, 1), :], sem)

    def issue(r, _):
        copy(r, 0, buf0).start()
        copy(r, 1, buf1).start()
        return 0

    lax.fori_loop(0, tm, issue, 0, unroll=DMA_UNROLL)
    for buf in (buf0, buf1):
        pltpu.make_async_copy(y_ref.at[pl.ds(0, tm), :], buf, sem).wait()
    w0 = route_ref[:, 2:3]
    w1 = route_ref[:, 3:4]
    y = buf0[...] * w0 + buf1[...] * w1
    o_ref[...] = x1_ref[...] + gt_ref[0] * y


def _combine(pos3, y, x1, route, gt2, tm, S):
    N, D = x1.shape
    per_b = S // tm
    return pl.pallas_call(
        functools.partial(_combine_kernel, tm=tm),
        grid=(N // tm,),
        in_specs=[pl.BlockSpec((1, 2, tm), lambda i: (i, 0, 0), memory_space=pltpu.SMEM),
                  pl.BlockSpec(memory_space=pl.ANY),
                  pl.BlockSpec((tm, D), lambda i: (i, 0)),
                  pl.BlockSpec((tm, LANES), lambda i: (i, 0)),
                  pl.BlockSpec((1, 1, D), lambda i: (i // per_b, 0, 0))],
        out_specs=pl.BlockSpec((tm, D), lambda i: (i, 0)),
        out_shape=jax.ShapeDtypeStruct((N, D), F32),
        scratch_shapes=[pltpu.VMEM((tm, D), F32), pltpu.VMEM((tm, D), F32), pltpu.SemaphoreType.DMA(())],
        compiler_params=_cparams(("arbitrary",)),
        name="moe_combine",
    )(pos3, y, x1, route, gt2)


def _rope_tables(positions):
    inv = ROPE_THETA ** (-jnp.arange(HALF, dtype=F32) / HALF)
    ang = positions.astype(F32)[:, None, :] * inv[None, :, None]
    return jnp.cos(ang), jnp.sin(ang)


def _block_diag_mean(width):
    blk = np.kron(np.eye(width // HEAD_DIM, dtype=np.float32), np.full((HEAD_DIM, HEAD_DIM), 1.0 / HEAD_DIM, np.float32))
    return jnp.asarray(blk, BF16)


def _layer(x, c_mod, positions, norm1_g, norm2_g, w_in, b_fgt, b_gate, qn_fox, kn_fox, qn_dsa, kn_dsa,
           w_proj_fox, w_proj_dsa, w_out, r_w_grp, r_b_grp, r_w_exp, r_b_exp, w1, w3, w2):
    B, S, D = x.shape
    N = B * S
    topk = min(TOPK_MAX, S // 4)
    tm = min(512, S)
    scale = HEAD_DIM ** -0.5
    mod3 = c_mod.reshape(B, 6, D)

    o = np.cumsum([0, 512, 512, 512, 8, 512, 64, 64, 512, 64, 8, D, D])
    seg = lambda k: w_in[:, o[k]:o[k + 1]]
    zpad = jnp.zeros((D, LANES - HEAD_DIM - 2 * N_HEADS), F32)
    w_tok = jnp.concatenate([seg(1), seg(10), seg(11)], axis=1).astype(BF16)
    w_t = jnp.concatenate([seg(0), seg(4), seg(7), seg(2),
                           seg(5), seg(8),
                           seg(6), seg(3), seg(9), zpad], axis=1).T.astype(BF16)
    gcol = jnp.stack([qn_fox * scale, qn_dsa * scale, kn_dsa]).reshape(3, HEAD_DIM, 1)
    cos_t, sin_t = _rope_tables(positions)

    k_heads, gates, fqt, dqt, iqt, fvt, kkt, dvt, logf_t, iwt = _in_projection(
        x, mod3, norm1_g.reshape(1, D), w_tok, w_t, _block_diag_mean(W_HEADS), cos_t, sin_t,
        jnp.tile(kn_fox, N_HEADS).reshape(1, W_HEADS), gcol, b_fgt.reshape(N_HEADS, 1),
        b_gate.reshape(1, 2 * D), tm)

    f_tok = jnp.transpose(_seq_cumsum(logf_t), (0, 2, 1))
    of = _fox_attention(k_heads, f_tok, fqt, fvt)
    od = _dsa_attention(kkt, dvt, iqt, dqt, iwt, topk)

    wr = jnp.concatenate([r_w_grp, r_w_exp, jnp.zeros((D, LANES - N_GROUPS - N_EXPERTS), F32)], axis=1).astype(BF16)
    br = jnp.concatenate([r_b_grp, r_b_exp, jnp.zeros((LANES - N_GROUPS - N_EXPERTS,), F32)]).reshape(1, LANES)
    x1, h2, route = _post_attention(of, od, gates, x, mod3, w_proj_fox.astype(BF16), w_proj_dsa.astype(BF16),
                                    w_out.astype(BF16), norm2_g.reshape(1, D), wr, br, tm)
    x1, h2, route = x1.reshape(N, D), h2.reshape(N, D), route.reshape(N, LANES)

    tg = 512 if N * 2 >= 512 * N_EXPERTS else 128
    ranks, counts = _expert_ranks(route, tm)
    counts = counts[0, :N_EXPERTS].astype(jnp.int32)
    padded = ((counts + tg - 1) // tg) * tg
    ends = jnp.cumsum(padded)
    starts = ends - padded
    e01 = route[:, :2].astype(jnp.int32)
    start_of = jnp.sum(jnp.where(e01[..., None] == jnp.arange(N_EXPERTS, dtype=jnp.int32), starts, 0), axis=-1)
    pos = start_of + ranks[:, :2].astype(jnp.int32)
    n_rows = N * 2 + N_EXPERTS * tg
    n_tiles = n_rows // tg
    tile_start = jnp.arange(n_tiles, dtype=jnp.int32) * tg
    tile_expert = jnp.minimum(jnp.sum((ends[None, :] <= tile_start[:, None]).astype(jnp.int32), axis=1),
                              N_EXPERTS - 1)
    n_used = (ends[-1] // tg).astype(jnp.int32).reshape(1)

    td = min(256, S)
    pos3 = jnp.transpose(pos.reshape(N // td, td, 2), (0, 2, 1))
    xs = _dispatch(h2, pos3, jnp.maximum(ends - tg, 0).astype(jnp.int32), (padded > 0).astype(jnp.int32),
                   n_used, n_rows, td, tg)
    y = _experts(tile_expert, n_used, xs, w1, w3, w2, tg)
    out = _combine(pos3, y, x1, route, mod3[:, 5:6, :], td, S)
    return out.reshape(B, S, D)


def kernel(x, c, positions, ada_w, ada_b, norm1_g, norm2_g, w_in, b_fgt, b_gate, qn_fox, kn_fox, qn_dsa, kn_dsa, w_proj_fox, w_proj_dsa, w_out, router_w_grp, router_b_grp, router_w_exp, router_b_exp, exp_w1, exp_w3, exp_w2):
    for l in range(ada_w.shape[0]):
        c_mod = _modulation(c, ada_w[l], ada_b[l])
        x = _layer(x, c_mod, positions, norm1_g[l], norm2_g[l], w_in[l], b_fgt[l], b_gate[l],
                   qn_fox[l], kn_fox[l], qn_dsa[l], kn_dsa[l], w_proj_fox[l], w_proj_dsa[l], w_out[l],
                   router_w_grp[l], router_b_grp[l], router_w_exp[l], router_b_exp[l],
                   exp_w1[l], exp_w3[l], exp_w2[l])
    return x
```

```python
import functools

import jax
import jax.numpy as jnp
import numpy as np
from jax import lax
from jax.experimental import pallas as pl
from jax.experimental.pallas import tpu as pltpu

F32 = jnp.float32
BF16 = jnp.bfloat16

CHUNK = 64
CHUNK_SHIFT = 6
DMA_UNROLL = 8
HEAD_DIM = 64
N_HEADS = 8
W_HEADS = N_HEADS * HEAD_DIM
TOPK_MAX = 256
ROPE_THETA = 10000.0
N_GROUPS = 4
EXPERTS_PER_GROUP = 8
N_EXPERTS = N_GROUPS * EXPERTS_PER_GROUP
EPS = 1e-6
MASKED = -1e30

LANES = 128
VMEM_LIMIT = 56 * 1024 * 1024


def _cparams(sem):
    return pltpu.CompilerParams(dimension_semantics=sem, vmem_limit_bytes=VMEM_LIMIT)


def _mod_kernel(c_ref, w_ref, b_ref, o_ref):
    c = c_ref[...]
    ca = (c * jax.nn.sigmoid(c)).astype(BF16)
    o_ref[...] = jnp.dot(ca, w_ref[...].astype(BF16), preferred_element_type=F32) + b_ref[...]


def _modulation(c, ada_w, ada_b):
    B, D = c.shape
    n = ada_w.shape[1] // D
    return pl.pallas_call(
        _mod_kernel,
        grid=(n,),
        in_specs=[pl.BlockSpec((B, D), lambda j: (0, 0)),
                  pl.BlockSpec((D, D), lambda j: (0, j)),
                  pl.BlockSpec((1, D), lambda j: (0, j))],
        out_specs=pl.BlockSpec((B, D), lambda j: (0, j)),
        out_shape=jax.ShapeDtypeStruct((B, n * D), F32),
        compiler_params=_cparams(("arbitrary",)),
        name="adaln_mod",
    )(c, ada_w, ada_b.reshape(1, -1))


R_FQ, R_DQ, R_IQ, R_FV = 0, 512, 1024, 1536
R_KK = 2048
R_S2 = 2176
R_END = 2304
HALF = HEAD_DIM // 2


def _inproj_kernel(x_ref, mod_ref, g1_ref, wtok_ref, wt_ref, bd512_ref, cos_ref, sin_ref,
                   gk_ref, gcol_ref, bf_ref, bg_ref,
                   fk_ref, gate_ref, fqt_ref, dqt_ref, iqt_ref, fvt_ref, kkt_ref, dvt_ref, lf_ref, iwt_ref):
    x = x_ref[0]
    ms = jnp.mean(x * x, axis=-1, keepdims=True)
    h = x * lax.rsqrt(ms + EPS) * g1_ref[...]
    h = h * (1.0 + mod_ref[0, 1:2, :]) + mod_ref[0, 0:1, :]
    hb = h.astype(BF16)
    D = x.shape[-1]
    cos, sin = cos_ref[0], sin_ref[0]

    def proj_t(lo, hi):
        return lax.dot_general(wt_ref[lo:hi, :], hb, (((1,), (1,)), ((), ())), preferred_element_type=F32)

    def norm_t(yh, gain):
        msq = jnp.mean(yh * yh, axis=0, keepdims=True)
        return yh * lax.rsqrt(msq + EPS) * gain

    def rope_store(ref, lo, yh):
        x1, x2 = yh[:HALF], yh[HALF:]
        ref[0, lo:lo + HALF, :] = (x1 * cos - x2 * sin).astype(ref.dtype)
        ref[0, lo + HALF:lo + HEAD_DIM, :] = (x2 * cos + x1 * sin).astype(ref.dtype)

    fq = proj_t(R_FQ, R_FQ + W_HEADS)
    dq = proj_t(R_DQ, R_DQ + W_HEADS)
    iq = proj_t(R_IQ, R_IQ + W_HEADS)
    for hh in range(N_HEADS):
        lo = hh * HEAD_DIM
        fqt_ref[0, lo:lo + HEAD_DIM, :] = norm_t(fq[lo:lo + HEAD_DIM], gcol_ref[0]).astype(BF16)
        rope_store(dqt_ref, lo, norm_t(dq[lo:lo + HEAD_DIM], gcol_ref[1]))
        rope_store(iqt_ref, lo, iq[lo:lo + HEAD_DIM])
    fvt_ref[0] = proj_t(R_FV, R_FV + W_HEADS).astype(BF16)

    kk = proj_t(R_KK, R_KK + 2 * HEAD_DIM)
    rope_store(kkt_ref, 0, norm_t(kk[:HEAD_DIM], gcol_ref[2]))
    rope_store(kkt_ref, HEAD_DIM, kk[HEAD_DIM:])

    s2 = proj_t(R_S2, R_S2 + LANES)
    dvt_ref[0] = s2[:HEAD_DIM].astype(BF16)
    z = s2[HEAD_DIM:HEAD_DIM + N_HEADS] + bf_ref[...]
    lf_ref[0] = jnp.minimum(z, 0.0) - jnp.log(1.0 + jnp.exp(-jnp.abs(z)))
    iwt_ref[0] = s2[HEAD_DIM + N_HEADS:HEAD_DIM + 2 * N_HEADS]

    fk = jnp.dot(hb, wtok_ref[:, :W_HEADS], preferred_element_type=F32)
    msq = jnp.dot((fk * fk).astype(BF16), bd512_ref[...], preferred_element_type=F32)
    fk = (fk * lax.rsqrt(msq + EPS) * gk_ref[...]).astype(BF16)
    for hh in range(N_HEADS):
        fk_ref[0, hh] = fk[:, hh * HEAD_DIM:(hh + 1) * HEAD_DIM]
    g = jnp.dot(hb, wtok_ref[:, W_HEADS:], preferred_element_type=F32)
    gate_ref[0] = jax.nn.sigmoid(g + bg_ref[...]).astype(BF16)


def _in_projection(x, mod3, norm1_g, w_tok, w_t, bd512, cos_t, sin_t, gk, gcol, bf, bg, tm):
    B, S, D = x.shape
    tok = lambda w: pl.BlockSpec((1, tm, w), lambda b, i: (b, i, 0))
    feat = lambda r: pl.BlockSpec((1, r, tm), lambda b, i: (b, 0, i))
    const = lambda shape: pl.BlockSpec(shape, lambda b, i: (0,) * len(shape))
    out_shapes = [jax.ShapeDtypeStruct((B, N_HEADS, S, HEAD_DIM), BF16),
                  jax.ShapeDtypeStruct((B, S, 2 * D), BF16)] + \
                 [jax.ShapeDtypeStruct((B, W_HEADS, S), BF16)] * 4 + \
                 [jax.ShapeDtypeStruct((B, 2 * HEAD_DIM, S), BF16),
                  jax.ShapeDtypeStruct((B, HEAD_DIM, S), BF16),
                  jax.ShapeDtypeStruct((B, N_HEADS, S), F32),
                  jax.ShapeDtypeStruct((B, N_HEADS, S), F32)]
    return pl.pallas_call(
        _inproj_kernel,
        grid=(B, S // tm),
        in_specs=[tok(D),
                  pl.BlockSpec((1, 6, D), lambda b, i: (b, 0, 0)),
                  const((1, D)),
                  const(w_tok.shape), const(w_t.shape), const((W_HEADS, W_HEADS)),
                  feat(HALF), feat(HALF),
                  const((1, W_HEADS)), const((3, HEAD_DIM, 1)), const((N_HEADS, 1)), const((1, 2 * D))],
        out_specs=[pl.BlockSpec((1, N_HEADS, tm, HEAD_DIM), lambda b, i: (b, 0, i, 0)), tok(2 * D),
                   feat(W_HEADS), feat(W_HEADS), feat(W_HEADS), feat(W_HEADS),
                   feat(2 * HEAD_DIM), feat(HEAD_DIM), feat(N_HEADS), feat(N_HEADS)],
        out_shape=out_shapes,
        compiler_params=_cparams(("parallel", "parallel")),
        name="in_projection",
    )(x, mod3, norm1_g, w_tok, w_t, bd512, cos_t, sin_t, gk, gcol, bf, bg)


def _cumsum_kernel(x_ref, o_ref):
    x = x_ref[0]
    n = x.shape[-1]
    pos = lax.broadcasted_iota(jnp.int32, x.shape, 1)
    shift = 1
    while shift < n:
        x = x + jnp.where(pos >= shift, pltpu.roll(x, shift, 1), 0.0)
        shift *= 2
    o_ref[0] = x


def _seq_cumsum(logf_t):
    B, H, S = logf_t.shape
    return pl.pallas_call(
        _cumsum_kernel,
        grid=(B,),
        in_specs=[pl.BlockSpec((1, H, S), lambda b: (b, 0, 0))],
        out_specs=pl.BlockSpec((1, H, S), lambda b: (b, 0, 0)),
        out_shape=jax.ShapeDtypeStruct((B, H, S), F32),
        compiler_params=_cparams(("parallel",)),
        name="forget_cumsum",
    )(logf_t)


KC = 256
SUB = 8


def _fold_rows(a, op, ways=1):
    n = a.shape[0] // SUB
    a = a.reshape(n, SUB, a.shape[1])
    chains = [a[w] for w in range(ways)]
    for j in range(ways, n):
        chains[j % ways] = op(chains[j % ways], a[j])
    while len(chains) > 1:
        chains = [op(chains[2 * j], chains[2 * j + 1]) for j in range(len(chains) // 2)]
    return chains[0]


def _softmax_pv(nch, s_ref, acc_ref, vt_at, m_all, o_ref):
    Q = o_ref.shape[1]
    acc_ref[...] = jnp.zeros_like(acc_ref)

    def body(c, lsum):
        off = pl.multiple_of(c * KC, KC)
        new = []
        for hh in range(N_HEADS):
            p = jnp.exp(s_ref[hh, pl.ds(off, KC), :] - m_all[hh])
            new.append(lsum[hh] + _fold_rows(p, jnp.add))
            acc_ref[hh] += jnp.dot(vt_at(hh, off), p.astype(BF16), preferred_element_type=F32)
        return tuple(new)

    lsum = lax.fori_loop(0, nch, body, tuple(jnp.zeros((SUB, Q), F32) for _ in range(N_HEADS)))
    for hh in range(N_HEADS):
        acc_ref[hh] = acc_ref[hh] / jnp.sum(lsum[hh], axis=0, keepdims=True)
    out_t = acc_ref[...].reshape(N_HEADS * HEAD_DIM, Q)
    o_ref[0] = out_t.T.astype(BF16)


def _fox_kernel(k_ref, f_ref, qt_ref, vt_ref, o_ref, s_ref, acc_ref):
    i = pl.program_id(1)
    Q = o_ref.shape[1]
    qts = [qt_ref[0, hh * HEAD_DIM:(hh + 1) * HEAD_DIM, :] for hh in range(N_HEADS)]

    def scores(c, mx, bias):
        off = pl.multiple_of(c * KC, KC)
        new = []
        for hh in range(N_HEADS):
            s = jnp.dot(k_ref[0, hh, pl.ds(off, KC), :], qts[hh], preferred_element_type=F32)
            s = s - f_ref[0, pl.ds(off, KC), hh:hh + 1]
            if bias is not None:
                s = s + bias
            s_ref[hh, pl.ds(off, KC), :] = s
            new.append(jnp.maximum(mx[hh], _fold_rows(s, jnp.maximum)))
        return tuple(new)

    mx = tuple(jnp.full((SUB, Q), MASKED, F32) for _ in range(N_HEADS))
    mx = lax.fori_loop(0, i, lambda c, m: scores(c, m, None), mx)
    kk = lax.broadcasted_iota(jnp.int32, (KC, Q), 0)
    qq = lax.broadcasted_iota(jnp.int32, (KC, Q), 1)
    mx = scores(i, mx, jnp.where(kk <= qq, 0.0, MASKED))
    m_all = [jnp.max(m, axis=0, keepdims=True) for m in mx]
    _softmax_pv(i + 1, s_ref, acc_ref, lambda hh, off: vt_ref[0, hh * HEAD_DIM:(hh + 1) * HEAD_DIM, pl.ds(off, KC)],
                m_all, o_ref)


def _fox_attention(k_heads, f_tok, qt, vt):
    B, H, S, Dh = k_heads.shape
    return pl.pallas_call(
        _fox_kernel,
        grid=(B, S // KC),
        in_specs=[pl.BlockSpec((1, H, S, Dh), lambda b, i: (b, 0, 0, 0)),
                  pl.BlockSpec((1, S, H), lambda b, i: (b, 0, 0)),
                  pl.BlockSpec((1, W_HEADS, KC), lambda b, i: (b, 0, i)),
                  pl.BlockSpec((1, W_HEADS, S), lambda b, i: (b, 0, 0))],
        out_specs=pl.BlockSpec((1, KC, W_HEADS), lambda b, i: (b, i, 0)),
        out_shape=jax.ShapeDtypeStruct((B, S, W_HEADS), BF16),
        scratch_shapes=[pltpu.VMEM((H, S, KC), F32), pltpu.VMEM((H, HEAD_DIM, KC), F32)],
        compiler_params=_cparams(("parallel", "arbitrary")),
        name="fox_attention",
    )(k_heads, f_tok, qt, vt)


INT_MIN = -(2 ** 31)
KEY_NEG_INF = INT_MIN + 0x7FFFFF
HI16 = -(2 ** 16)
PACK = 16


def _dsa_kernel(kkt_ref, dvt_ref, iqt_ref, dqt_ref, iwt_ref, o_ref, key_ref, hi_ref, s_ref, acc_ref, dk_ref, ik_ref,
                *, topk):
    i = pl.program_id(1)
    Q = o_ref.shape[1]
    nch = i + 1

    @pl.when(i == 0)
    def _():
        def to_rows(c, _):
            off = pl.multiple_of(c * KC, KC)
            rows = kkt_ref[0, :, pl.ds(off, KC)].astype(F32).T
            dk_ref[pl.ds(off, KC), :] = rows[:, :HEAD_DIM].astype(BF16)
            ik_ref[pl.ds(off, KC), :] = rows[:, HEAD_DIM:].astype(BF16)
            return 0
        lax.fori_loop(0, kkt_ref.shape[-1] // KC, to_rows, 0)

    sub_k = lax.broadcasted_iota(jnp.int32, (KC, Q), 0)
    sub_r = lax.broadcasted_iota(jnp.int32, (CHUNK, Q), 0)
    q_chunk = (i * Q + lax.broadcasted_iota(jnp.int32, (CHUNK, Q), 1)) >> CHUNK_SHIFT
    iqts = [iqt_ref[0, hh * HEAD_DIM:(hh + 1) * HEAD_DIM, :] for hh in range(N_HEADS)]
    iws = [iwt_ref[0, hh:hh + 1, :] for hh in range(N_HEADS)]

    def score_chunk(c, _):
        for r in range(KC // CHUNK):
            off = pl.multiple_of(c * KC + r * CHUNK, CHUNK)
            ik = ik_ref[pl.ds(off, CHUNK), :]
            sc = jnp.zeros((CHUNK, Q), F32)
            for hh in range(N_HEADS):
                d = jnp.dot(ik, iqts[hh], preferred_element_type=F32)
                sc = sc + iws[hh] * jnp.maximum(d, 0.0)
            sc = sc + 0.0
            allowed = ((off + sub_r) >> CHUNK_SHIFT) <= q_chunk
            bits = pltpu.bitcast(jnp.where(allowed, sc, -jnp.inf), jnp.int32)
            key = bits ^ ((bits >> 31) & 0x7FFFFFFF)
            key_ref[pl.ds(off, CHUNK), :] = key
            hi_ref[pl.ds(off, CHUNK), :] = (key >> 16).astype(jnp.int16)
        return 0

    lax.fori_loop(0, nch, score_chunk, 0)

    def count(pred):
        def body(c, acc):
            off = pl.multiple_of(c * KC, KC)
            hit = pred(key_ref[pl.ds(off, KC), :], off + sub_k)
            return acc + _fold_rows(jnp.where(hit, 1.0, 0.0), jnp.add, ways=4)
        acc = lax.fori_loop(0, nch, body, jnp.zeros((SUB, Q), F32))
        return jnp.sum(acc, axis=0, keepdims=True)

    one, zero = jnp.ones((), BF16), jnp.zeros((), BF16)

    def count_hi(cand):
        c16 = jnp.broadcast_to(cand >> 16, (PACK, Q)).astype(jnp.int16)

        def body(c, acc):
            off = pl.multiple_of(c * KC, KC)
            kb = hi_ref[pl.ds(off, KC), :].reshape(KC // PACK, PACK, Q)
            hit = jnp.where(kb >= c16[None], one, zero)
            parts = [hit[w] for w in range(4)]
            for j in range(4, KC // PACK):
                parts[j % 4] = parts[j % 4] + hit[j]
            return acc + ((parts[0] + parts[1]) + (parts[2] + parts[3]))
        acc = lax.fori_loop(0, nch, body, jnp.zeros((PACK, Q), BF16))
        return jnp.sum(acc.astype(F32), axis=0, keepdims=True)

    def count_lo(cand):
        return count(lambda k, _: k >= cand)

    assert key_ref.shape[0] // PACK <= 256

    kf = jnp.float32(topk)

    n_nonneg = count_hi(jnp.zeros((1, Q), jnp.int32))
    top_half = n_nonneg >= kf
    thr = jnp.where(top_half, 0, INT_MIN).astype(jnp.int32)
    n_ge = jnp.where(top_half, n_nonneg, (nch * KC).astype(F32))

    def descend(counter, top_bit):
        def step(j, carry):
            thr, n_ge = carry
            cand = thr + (jnp.int32(1) << (top_bit - j))
            n = counter(cand)
            take = n >= kf
            return jnp.where(take, cand, thr), jnp.where(take, n, n_ge)
        return step

    thr, n_ge = lax.fori_loop(0, 15, descend(count_hi, 30), (thr, n_ge))
    thr, n_ge = lax.fori_loop(0, 16, descend(count_lo, 15), (thr, n_ge))

    excess = (n_ge > kf) & (thr > KEY_NEG_INF)

    @pl.when(jnp.max(jnp.where(excess, 1.0, 0.0)) > 0.0)
    def _():
        need = kf - count(lambda k, _: k > thr)
        nbits = int(np.ceil(np.log2(key_ref.shape[0]))) + 1

        def bound(j, last):
            cand = last + (jnp.int32(1) << (nbits - 1 - j))
            n = count(lambda k, idx: (k == thr) & (idx < cand))
            return jnp.where(n < need, cand, last)

        last = lax.fori_loop(0, nbits, bound, jnp.zeros((1, Q), jnp.int32))

        def demote(c, _):
            off = pl.multiple_of(c * KC, KC)
            k = key_ref[pl.ds(off, KC), :]
            drop = excess & (k == thr) & (off + sub_k > last)
            key_ref[pl.ds(off, KC), :] = jnp.where(drop, thr - 1, k)
            return 0

        lax.fori_loop(0, nch, demote, 0)

    keep_from = jnp.maximum(thr, KEY_NEG_INF + 1)

    qts = [dqt_ref[0, hh * HEAD_DIM:(hh + 1) * HEAD_DIM, :] for hh in range(N_HEADS)]

    def scores(c, mx):
        off = pl.multiple_of(c * KC, KC)
        bias = jnp.where(key_ref[pl.ds(off, KC), :] >= keep_from, 0.0, MASKED)
        dk = dk_ref[pl.ds(off, KC), :]
        new = []
        for hh in range(N_HEADS):
            s = jnp.dot(dk, qts[hh], preferred_element_type=F32) + bias
            s_ref[hh, pl.ds(off, KC), :] = s
            new.append(jnp.maximum(mx[hh], _fold_rows(s, jnp.maximum)))
        return tuple(new)

    mx = lax.fori_loop(0, nch, scores, tuple(jnp.full((SUB, Q), MASKED, F32) for _ in range(N_HEADS)))
    m_all = [jnp.max(m, axis=0, keepdims=True) for m in mx]
    _softmax_pv(nch, s_ref, acc_ref, lambda hh, off: dvt_ref[0, :, pl.ds(off, KC)], m_all, o_ref)


def _dsa_attention(kkt, dvt, iqt, dqt, iwt, topk):
    B, Dh, S = dvt.shape
    rows = lambda r: pl.BlockSpec((1, r, S), lambda b, i: (b, 0, 0))
    qcols = lambda r: pl.BlockSpec((1, r, KC), lambda b, i: (b, 0, i))
    return pl.pallas_call(
        functools.partial(_dsa_kernel, topk=topk),
        grid=(B, S // KC),
        in_specs=[rows(2 * Dh), rows(Dh), qcols(W_HEADS), qcols(W_HEADS), qcols(N_HEADS)],
        out_specs=pl.BlockSpec((1, KC, W_HEADS), lambda b, i: (b, i, 0)),
        out_shape=jax.ShapeDtypeStruct((B, S, W_HEADS), BF16),
        scratch_shapes=[pltpu.VMEM((S, KC), jnp.int32), pltpu.VMEM((S, KC), jnp.int16),
                        pltpu.VMEM((N_HEADS, S, KC), F32), pltpu.VMEM((N_HEADS, HEAD_DIM, KC), F32),
                        pltpu.VMEM((S, Dh), BF16), pltpu.VMEM((S, Dh), BF16)],
        compiler_params=_cparams(("parallel", "arbitrary")),
        name="dsa_attention",
    )(kkt, dvt, iqt, dqt, iwt)


def _first(mask, lane):
    return jnp.min(jnp.where(mask, lane, LANES), axis=-1, keepdims=True)


def _post_kernel(of_ref, od_ref, gate_ref, x_ref, mod_ref, wpf_ref, wpd_ref, wo_ref, g2_ref, wr_ref, br_ref,
                 x1_ref, h2_ref, route_ref):
    D = x_ref.shape[-1]
    pf = jnp.dot(of_ref[0], wpf_ref[...], preferred_element_type=F32)
    pd = jnp.dot(od_ref[0], wpd_ref[...], preferred_element_type=F32)
    merged = gate_ref[0, :, :D].astype(F32) * pf + gate_ref[0, :, D:].astype(F32) * pd
    y = jnp.dot(merged.astype(BF16), wo_ref[...], preferred_element_type=F32)
    x1 = x_ref[0] + mod_ref[0, 2:3, :] * y
    x1_ref[0] = x1

    ms = jnp.mean(x1 * x1, axis=-1, keepdims=True)
    h2 = x1 * lax.rsqrt(ms + EPS) * g2_ref[...]
    h2 = h2 * (1.0 + mod_ref[0, 4:5, :]) + mod_ref[0, 3:4, :]
    hb = h2.astype(BF16)
    h2_ref[0] = h2

    logits = jnp.dot(hb, wr_ref[...], preferred_element_type=F32) + br_ref[...]
    lane = lax.broadcasted_iota(jnp.int32, logits.shape, 1)
    is_grp = lane < N_GROUPS
    gl = jnp.where(is_grp, logits, -jnp.inf)
    gmax = jnp.max(gl, axis=-1, keepdims=True)
    g_idx = _first(gl == gmax, lane)
    g_w = 1.0 / jnp.sum(jnp.exp(gl - gmax), axis=-1, keepdims=True)

    e_lo = N_GROUPS + g_idx * EXPERTS_PER_GROUP
    in_grp = (lane >= e_lo) & (lane < e_lo + EXPERTS_PER_GROUP)
    el = jnp.where(in_grp, logits, -jnp.inf)
    emax = jnp.max(el, axis=-1, keepdims=True)
    ee = jnp.exp(el - emax)
    prob = ee / jnp.sum(ee, axis=-1, keepdims=True)
    prob = jnp.where(in_grp, prob, -1.0)
    p0 = jnp.max(prob, axis=-1, keepdims=True)
    l0 = _first(prob == p0, lane)
    rest = jnp.where(lane == l0, -1.0, prob)
    p1 = jnp.max(rest, axis=-1, keepdims=True)
    l1 = _first(rest == p1, lane)
    psum = p0 + p1
    w0 = g_w * (p0 / psum)
    w1 = g_w * (p1 / psum)
    e0 = (l0 - N_GROUPS).astype(F32)
    e1 = (l1 - N_GROUPS).astype(F32)
    route_ref[0] = jnp.where(lane == 0, e0, jnp.where(lane == 1, e1, jnp.where(lane == 2, w0,
                             jnp.where(lane == 3, w1, 0.0))))


def _post_attention(of, od, gates, x, mod3, wpf, wpd, wo, g2, wr, br, tm):
    B, S, D = x.shape
    tok = lambda w: pl.BlockSpec((1, tm, w), lambda b, i: (b, i, 0))
    const = lambda shape: pl.BlockSpec(shape, lambda b, i: (0,) * len(shape))
    return pl.pallas_call(
        _post_kernel,
        grid=(B, S // tm),
        in_specs=[tok(W_HEADS), tok(W_HEADS), tok(2 * D), tok(D),
                  pl.BlockSpec((1, 6, D), lambda b, i: (b, 0, 0)),
                  const(wpf.shape), const(wpd.shape), const(wo.shape),
                  const((1, D)), const((D, LANES)), const((1, LANES))],
        out_specs=[tok(D), tok(D), tok(LANES)],
        out_shape=[jax.ShapeDtypeStruct((B, S, D), F32),
                   jax.ShapeDtypeStruct((B, S, D), F32),
                   jax.ShapeDtypeStruct((B, S, LANES), F32)],
        compiler_params=_cparams(("parallel", "parallel")),
        name="merge_out_router",
    )(of, od, gates, x, mod3, wpf, wpd, wo, g2, wr, br)


def _rank_kernel(route_ref, tri_ref, rank_ref, count_ref, carry_ref):
    @pl.when(pl.program_id(0) == 0)
    def _():
        carry_ref[...] = jnp.zeros_like(carry_ref)

    r = route_ref[...]
    lane = lax.broadcasted_iota(jnp.int32, r.shape, 1).astype(F32)
    hot0 = lane == r[:, 0:1]
    hot1 = lane == r[:, 1:2]
    hits = jnp.where(hot0 | hot1, 1.0, 0.0)
    incl = jnp.dot(tri_ref[...], hits.astype(BF16), preferred_element_type=F32)
    before = incl - hits + carry_ref[...]
    r0 = jnp.sum(jnp.where(hot0, before, 0.0), axis=-1, keepdims=True)
    r1 = jnp.sum(jnp.where(hot1, before, 0.0), axis=-1, keepdims=True)
    rank_ref[...] = jnp.where(lane == 0.0, r0, jnp.where(lane == 1.0, r1, 0.0))
    carry_ref[...] = carry_ref[...] + jnp.sum(hits, axis=0, keepdims=True)
    count_ref[...] = carry_ref[...]


def _expert_ranks(route, tm):
    N = route.shape[0]
    tri = jnp.asarray(np.tril(np.ones((tm, tm), np.float32)), BF16)
    return pl.pallas_call(
        _rank_kernel,
        grid=(N // tm,),
        in_specs=[pl.BlockSpec((tm, LANES), lambda i: (i, 0)),
                  pl.BlockSpec((tm, tm), lambda i: (0, 0))],
        out_specs=[pl.BlockSpec((tm, LANES), lambda i: (i, 0)),
                   pl.BlockSpec((1, LANES), lambda i: (0, 0))],
        out_shape=[jax.ShapeDtypeStruct((N, LANES), F32), jax.ShapeDtypeStruct((1, LANES), F32)],
        scratch_shapes=[pltpu.VMEM((1, LANES), F32)],
        compiler_params=_cparams(("arbitrary",)),
        name="expert_ranks",
    )(route, tri)


def _dispatch_kernel(zstart_ref, zon_ref, nt_ref, pos_ref, h_ref, xs_ref, zbuf, sem, zsem, *, tm, tg):
    @pl.when(pl.program_id(0) == 0)
    def _():
        zbuf[...] = jnp.zeros_like(zbuf)

        def zero_tile(start):
            return pltpu.make_async_copy(zbuf, xs_ref.at[pl.ds(pl.multiple_of(start, tg), tg), :], zsem)

        n_tiles = xs_ref.shape[0] // tg
        for e in range(N_EXPERTS):
            pl.when(zon_ref[e] > 0)(lambda e=e: zero_tile(zstart_ref[e]).start())
        lax.fori_loop(nt_ref[0], n_tiles, lambda t, _: (zero_tile(t * tg).start(), 0)[1], 0)
        for e in range(N_EXPERTS):
            pl.when(zon_ref[e] > 0)(lambda e=e: zero_tile(zstart_ref[e]).wait())
        lax.fori_loop(nt_ref[0], n_tiles, lambda t, _: (zero_tile(t * tg).wait(), 0)[1], 0)

    def copy(r, slot):
        return pltpu.make_async_copy(h_ref.at[pl.ds(r, 1), :],
                                     xs_ref.at[pl.ds(pos_ref[0, slot, r], 1), :], sem)

    def issue(r, _):
        copy(r, 0).start()
        copy(r, 1).start()
        return 0

    lax.fori_loop(0, tm, issue, 0, unroll=DMA_UNROLL)
    for _ in range(2):
        pltpu.make_async_copy(h_ref, xs_ref.at[pl.ds(0, tm), :], sem).wait()


def _dispatch(h2, pos3, last_tile_start, has_rows, n_tiles_used, n_rows, tm, tg):
    N, D = h2.shape
    grid_spec = pltpu.PrefetchScalarGridSpec(
        num_scalar_prefetch=3,
        grid=(N // tm,),
        in_specs=[pl.BlockSpec((1, 2, tm), lambda i, zs, zo, nt: (i, 0, 0), memory_space=pltpu.SMEM),
                  pl.BlockSpec((tm, D), lambda i, zs, zo, nt: (i, 0))],
        out_specs=pl.BlockSpec(memory_space=pl.ANY),
        scratch_shapes=[pltpu.VMEM((tg, D), F32), pltpu.SemaphoreType.DMA(()), pltpu.SemaphoreType.DMA(())],
    )
    return pl.pallas_call(
        functools.partial(_dispatch_kernel, tm=tm, tg=tg),
        grid_spec=grid_spec,
        out_shape=jax.ShapeDtypeStruct((n_rows, D), F32),
        compiler_params=_cparams(("arbitrary",)),
        name="moe_dispatch",
    )(last_tile_start, has_rows, n_tiles_used, pos3, h2)


def _expert_kernel(te_ref, nt_ref, xs_ref, w1_ref, w3_ref, w2_ref, y_ref, w1b, w3b, w2b):
    g = pl.program_id(0)
    used = g < nt_ref[0]
    new_expert = (g == 0) | (te_ref[g] != te_ref[jnp.maximum(g - 1, 0)])

    @pl.when(used & new_expert)
    def _():
        w1b[...] = w1_ref[0].astype(BF16)
        w3b[...] = w3_ref[0].astype(BF16)
        w2b[...] = w2_ref[0].astype(BF16)

    @pl.when(used)
    def _():
        xb = xs_ref[...].astype(BF16)
        a = jnp.dot(xb, w1b[...], preferred_element_type=F32)
        b = jnp.dot(xb, w3b[...], preferred_element_type=F32)
        hmid = (a * jax.nn.sigmoid(a) * b).astype(BF16)
        y_ref[...] = jnp.dot(hmid, w2b[...], preferred_element_type=F32)

    @pl.when(jnp.logical_not(used))
    def _():
        y_ref[...] = jnp.zeros_like(y_ref)


def _experts(tile_expert, n_tiles_used, xs, w1, w3, w2, tg):
    P, D = xs.shape
    E, _, De = w1.shape
    row_tile = lambda g, te, nt: (jnp.minimum(g, nt[0] - 1), 0)
    grid_spec = pltpu.PrefetchScalarGridSpec(
        num_scalar_prefetch=2,
        grid=(P // tg,),
        in_specs=[pl.BlockSpec((tg, D), row_tile),
                  pl.BlockSpec((1, D, De), lambda g, te, nt: (te[g], 0, 0)),
                  pl.BlockSpec((1, D, De), lambda g, te, nt: (te[g], 0, 0)),
                  pl.BlockSpec((1, De, D), lambda g, te, nt: (te[g], 0, 0))],
        out_specs=pl.BlockSpec((tg, D), lambda g, te, nt: (g, 0)),
        scratch_shapes=[pltpu.VMEM((D, De), BF16), pltpu.VMEM((D, De), BF16), pltpu.VMEM((De, D), BF16)],
    )
    return pl.pallas_call(
        _expert_kernel,
        grid_spec=grid_spec,
        out_shape=jax.ShapeDtypeStruct((P, D), F32),
        compiler_params=_cparams(("arbitrary",)),
        name="moe_experts",
    )(tile_expert, n_tiles_used, xs, w1, w3, w2)


def _combine_kernel(pos_ref, y_ref, x1_ref, route_ref, gt_ref, o_ref, buf0, buf1, sem, *, tm):
    def copy(r, slot, buf):
        return pltpu.make_async_copy(y_ref.at[pl.ds(pos_ref[0, slot, r], 1), :],
                                     buf.at[pl.ds(r, 1), :], sem)

    def issue(r, _):
        copy(r, 0, buf0).start()
        copy(r, 1, buf1).start()
        return 0

    lax.fori_loop(0, tm, issue, 0, unroll=DMA_UNROLL)
    for buf in (buf0, buf1):
        pltpu.make_async_copy(y_ref.at[pl.ds(0, tm), :], buf, sem).wait()
    w0 = route_ref[:, 2:3]
    w1 = route_ref[:, 3:4]
    y = buf0[...] * w0 + buf1[...] * w1
    o_ref[...] = x1_ref[...] + gt_ref[0] * y


def _combine(pos3, y, x1, route, gt2, tm, S):
    N, D = x1.shape
    per_b = S // tm
    return pl.pallas_call(
        functools.partial(_combine_kernel, tm=tm),
        grid=(N // tm,),
        in_specs=[pl.BlockSpec((1, 2, tm), lambda i: (i, 0, 0), memory_space=pltpu.SMEM),
                  pl.BlockSpec(memory_space=pl.ANY),
                  pl.BlockSpec((tm, D), lambda i: (i, 0)),
                  pl.BlockSpec((tm, LANES), lambda i: (i, 0)),
                  pl.BlockSpec((1, 1, D), lambda i: (i // per_b, 0, 0))],
        out_specs=pl.BlockSpec((tm, D), lambda i: (i, 0)),
        out_shape=jax.ShapeDtypeStruct((N, D), F32),
        scratch_shapes=[pltpu.VMEM((tm, D), F32), pltpu.VMEM((tm, D), F32), pltpu.SemaphoreType.DMA(())],
        compiler_params=_cparams(("arbitrary",)),
        name="moe_combine",
    )(pos3, y, x1, route, gt2)


def _rope_tables(positions):
    inv = ROPE_THETA ** (-jnp.arange(HALF, dtype=F32) / HALF)
    ang = positions.astype(F32)[:, None, :] * inv[None, :, None]
    return jnp.cos(ang), jnp.sin(ang)


def _block_diag_mean(width):
    blk = np.kron(np.eye(width // HEAD_DIM, dtype=np.float32), np.full((HEAD_DIM, HEAD_DIM), 1.0 / HEAD_DIM, np.float32))
    return jnp.asarray(blk, BF16)


def _layer(x, c_mod, positions, norm1_g, norm2_g, w_in, b_fgt, b_gate, qn_fox, kn_fox, qn_dsa, kn_dsa,
           w_proj_fox, w_proj_dsa, w_out, r_w_grp, r_b_grp, r_w_exp, r_b_exp, w1, w3, w2):
    B, S, D = x.shape
    N = B * S
    topk = min(TOPK_MAX, S // 4)
    tm = min(512, S)
    scale = HEAD_DIM ** -0.5
    mod3 = c_mod.reshape(B, 6, D)

    o = np.cumsum([0, 512, 512, 512, 8, 512, 64, 64, 512, 64, 8, D, D])
    seg = lambda k: w_in[:, o[k]:o[k + 1]]
    zpad = jnp.zeros((D, LANES - HEAD_DIM - 2 * N_HEADS), F32)
    w_tok = jnp.concatenate([seg(1), seg(10), seg(11)], axis=1).astype(BF16)
    w_t = jnp.concatenate([seg(0), seg(4), seg(7), seg(2),
                           seg(5), seg(8),
                           seg(6), seg(3), seg(9), zpad], axis=1).T.astype(BF16)
    gcol = jnp.stack([qn_fox * scale, qn_dsa * scale, kn_dsa]).reshape(3, HEAD_DIM, 1)
    cos_t, sin_t = _rope_tables(positions)

    k_heads, gates, fqt, dqt, iqt, fvt, kkt, dvt, logf_t, iwt = _in_projection(
        x, mod3, norm1_g.reshape(1, D), w_tok, w_t, _block_diag_mean(W_HEADS), cos_t, sin_t,
        jnp.tile(kn_fox, N_HEADS).reshape(1, W_HEADS), gcol, b_fgt.reshape(N_HEADS, 1),
        b_gate.reshape(1, 2 * D), tm)

    f_tok = jnp.transpose(_seq_cumsum(logf_t), (0, 2, 1))
    of = _fox_attention(k_heads, f_tok, fqt, fvt)
    od = _dsa_attention(kkt, dvt, iqt, dqt, iwt, topk)

    wr = jnp.concatenate([r_w_grp, r_w_exp, jnp.zeros((D, LANES - N_GROUPS - N_EXPERTS), F32)], axis=1).astype(BF16)
    br = jnp.concatenate([r_b_grp, r_b_exp, jnp.zeros((LANES - N_GROUPS - N_EXPERTS,), F32)]).reshape(1, LANES)
    x1, h2, route = _post_attention(of, od, gates, x, mod3, w_proj_fox.astype(BF16), w_proj_dsa.astype(BF16),
                                    w_out.astype(BF16), norm2_g.reshape(1, D), wr, br, tm)
    x1, h2, route = x1.reshape(N, D), h2.reshape(N, D), route.reshape(N, LANES)

    tg = 512 if N * 2 >= 512 * N_EXPERTS else 128
    ranks, counts = _expert_ranks(route, tm)
    counts = counts[0, :N_EXPERTS].astype(jnp.int32)
    padded = ((counts + tg - 1) // tg) * tg
    ends = jnp.cumsum(padded)
    starts = ends - padded
    e01 = route[:, :2].astype(jnp.int32)
    start_of = jnp.sum(jnp.where(e01[..., None] == jnp.arange(N_EXPERTS, dtype=jnp.int32), starts, 0), axis=-1)
    pos = start_of + ranks[:, :2].astype(jnp.int32)
    n_rows = N * 2 + N_EXPERTS * tg
    n_tiles = n_rows // tg
    tile_start = jnp.arange(n_tiles, dtype=jnp.int32) * tg
    tile_expert = jnp.minimum(jnp.sum((ends[None, :] <= tile_start[:, None]).astype(jnp.int32), axis=1),
                              N_EXPERTS - 1)
    n_used = (ends[-1] // tg).astype(jnp.int32).reshape(1)

    td = min(256, S)
    pos3 = jnp.transpose(pos.reshape(N // td, td, 2), (0, 2, 1))
    xs = _dispatch(h2, pos3, jnp.maximum(ends - tg, 0).astype(jnp.int32), (padded > 0).astype(jnp.int32),
                   n_used, n_rows, td, tg)
    y = _experts(tile_expert, n_used, xs, w1, w3, w2, tg)
    out = _combine(pos3, y, x1, route, mod3[:, 5:6, :], td, S)
    return out.reshape(B, S, D)


def kernel(x, c, positions, ada_w, ada_b, norm1_g, norm2_g, w_in, b_fgt, b_gate, qn_fox, kn_fox, qn_dsa, kn_dsa, w_proj_fox, w_proj_dsa, w_out, router_w_grp, router_b_grp, router_w_exp, router_b_exp, exp_w1, exp_w3, exp_w2):
    for l in range(ada_w.shape[0]):
        c_mod = _modulation(c, ada_w[l], ada_b[l])
        x = _layer(x, c_mod, positions, norm1_g[l], norm2_g[l], w_in[l], b_fgt[l], b_gate[l],
                   qn_fox[l], kn_fox[l], qn_dsa[l], kn_dsa[l], w_proj_fox[l], w_proj_dsa[l], w_out[l],
                   router_w_grp[l], router_b_grp[l], router_w_exp[l], router_b_exp[l],
                   exp_w1[l], exp_w3[l], exp_w2[l])
    return x
```

```python
import functools

import jax
import jax.numpy as jnp
import numpy as np
from jax import lax
from jax.experimental import pallas as pl
from jax.experimental.pallas import tpu as pltpu

F32 = jnp.float32
BF16 = jnp.bfloat16

CHUNK = 64
CHUNK_SHIFT = 6
DMA_UNROLL = 8
HEAD_DIM = 64
N_HEADS = 8
W_HEADS = N_HEADS * HEAD_DIM
TOPK_MAX = 256
ROPE_THETA = 10000.0
N_GROUPS = 4
EXPERTS_PER_GROUP = 8
N_EXPERTS = N_GROUPS * EXPERTS_PER_GROUP
EPS = 1e-6
MASKED = -1e30

LANES = 128
VMEM_LIMIT = 56 * 1024 * 1024


def _cparams(sem):
    return pltpu.CompilerParams(dimension_semantics=sem, vmem_limit_bytes=VMEM_LIMIT)


def _mod_kernel(c_ref, w_ref, b_ref, o_ref):
    c = c_ref[...]
    ca = (c * jax.nn.sigmoid(c)).astype(BF16)
    o_ref[...] = jnp.dot(ca, w_ref[...].astype(BF16), preferred_element_type=F32) + b_ref[...]


def _modulation(c, ada_w, ada_b):
    B, D = c.shape
    n = ada_w.shape[1] // D
    return pl.pallas_call(
        _mod_kernel,
        grid=(n,),
        in_specs=[pl.BlockSpec((B, D), lambda j: (0, 0)),
                  pl.BlockSpec((D, D), lambda j: (0, j)),
                  pl.BlockSpec((1, D), lambda j: (0, j))],
        out_specs=pl.BlockSpec((B, D), lambda j: (0, j)),
        out_shape=jax.ShapeDtypeStruct((B, n * D), F32),
        compiler_params=_cparams(("arbitrary",)),
        name="adaln_mod",
    )(c, ada_w, ada_b.reshape(1, -1))


R_FQ, R_DQ, R_IQ, R_FV = 0, 512, 1024, 1536
R_KK = 2048
R_S2 = 2176
R_END = 2304
HALF = HEAD_DIM // 2


def _inproj_kernel(x_ref, mod_ref, g1_ref, wtok_ref, wt_ref, bd512_ref, cos_ref, sin_ref,
                   gk_ref, gcol_ref, bf_ref, bg_ref,
                   fk_ref, gate_ref, fqt_ref, dqt_ref, iqt_ref, fvt_ref, kkt_ref, dvt_ref, lf_ref, iwt_ref):
    x = x_ref[0]
    ms = jnp.mean(x * x, axis=-1, keepdims=True)
    h = x * lax.rsqrt(ms + EPS) * g1_ref[...]
    h = h * (1.0 + mod_ref[0, 1:2, :]) + mod_ref[0, 0:1, :]
    hb = h.astype(BF16)
    D = x.shape[-1]
    cos, sin = cos_ref[0], sin_ref[0]

    def proj_t(lo, hi):
        return lax.dot_general(wt_ref[lo:hi, :], hb, (((1,), (1,)), ((), ())), preferred_element_type=F32)

    def norm_t(yh, gain):
        msq = jnp.mean(yh * yh, axis=0, keepdims=True)
        return yh * lax.rsqrt(msq + EPS) * gain

    def rope_store(ref, lo, yh):
        x1, x2 = yh[:HALF], yh[HALF:]
        ref[0, lo:lo + HALF, :] = (x1 * cos - x2 * sin).astype(ref.dtype)
        ref[0, lo + HALF:lo + HEAD_DIM, :] = (x2 * cos + x1 * sin).astype(ref.dtype)

    fq = proj_t(R_FQ, R_FQ + W_HEADS)
    dq = proj_t(R_DQ, R_DQ + W_HEADS)
    iq = proj_t(R_IQ, R_IQ + W_HEADS)
    for hh in range(N_HEADS):
        lo = hh * HEAD_DIM
        fqt_ref[0, lo:lo + HEAD_DIM, :] = norm_t(fq[lo:lo + HEAD_DIM], gcol_ref[0]).astype(BF16)
        rope_store(dqt_ref, lo, norm_t(dq[lo:lo + HEAD_DIM], gcol_ref[1]))
        rope_store(iqt_ref, lo, iq[lo:lo + HEAD_DIM])
    fvt_ref[0] = proj_t(R_FV, R_FV + W_HEADS).astype(BF16)

    kk = proj_t(R_KK, R_KK + 2 * HEAD_DIM)
    rope_store(kkt_ref, 0, norm_t(kk[:HEAD_DIM], gcol_ref[2]))
    rope_store(kkt_ref, HEAD_DIM, kk[HEAD_DIM:])

    s2 = proj_t(R_S2, R_S2 + LANES)
    dvt_ref[0] = s2[:HEAD_DIM].astype(BF16)
    z = s2[HEAD_DIM:HEAD_DIM + N_HEADS] + bf_ref[...]
    lf_ref[0] = jnp.minimum(z, 0.0) - jnp.log(1.0 + jnp.exp(-jnp.abs(z)))
    iwt_ref[0] = s2[HEAD_DIM + N_HEADS:HEAD_DIM + 2 * N_HEADS]

    fk = jnp.dot(hb, wtok_ref[:, :W_HEADS], preferred_element_type=F32)
    msq = jnp.dot((fk * fk).astype(BF16), bd512_ref[...], preferred_element_type=F32)
    fk = (fk * lax.rsqrt(msq + EPS) * gk_ref[...]).astype(BF16)
    for hh in range(N_HEADS):
        fk_ref[0, hh] = fk[:, hh * HEAD_DIM:(hh + 1) * HEAD_DIM]
    g = jnp.dot(hb, wtok_ref[:, W_HEADS:], preferred_element_type=F32)
    gate_ref[0] = jax.nn.sigmoid(g + bg_ref[...]).astype(BF16)


def _in_projection(x, mod3, norm1_g, w_tok, w_t, bd512, cos_t, sin_t, gk, gcol, bf, bg, tm):
    B, S, D = x.shape
    tok = lambda w: pl.BlockSpec((1, tm, w), lambda b, i: (b, i, 0))
    feat = lambda r: pl.BlockSpec((1, r, tm), lambda b, i: (b, 0, i))
    const = lambda shape: pl.BlockSpec(shape, lambda b, i: (0,) * len(shape))
    out_shapes = [jax.ShapeDtypeStruct((B, N_HEADS, S, HEAD_DIM), BF16),
                  jax.ShapeDtypeStruct((B, S, 2 * D), BF16)] + \
                 [jax.ShapeDtypeStruct((B, W_HEADS, S), BF16)] * 4 + \
                 [jax.ShapeDtypeStruct((B, 2 * HEAD_DIM, S), BF16),
                  jax.ShapeDtypeStruct((B, HEAD_DIM, S), BF16),
                  jax.ShapeDtypeStruct((B, N_HEADS, S), F32),
                  jax.ShapeDtypeStruct((B, N_HEADS, S), F32)]
    return pl.pallas_call(
        _inproj_kernel,
        grid=(B, S // tm),
        in_specs=[tok(D),
                  pl.BlockSpec((1, 6, D), lambda b, i: (b, 0, 0)),
                  const((1, D)),
                  const(w_tok.shape), const(w_t.shape), const((W_HEADS, W_HEADS)),
                  feat(HALF), feat(HALF),
                  const((1, W_HEADS)), const((3, HEAD_DIM, 1)), const((N_HEADS, 1)), const((1, 2 * D))],
        out_specs=[pl.BlockSpec((1, N_HEADS, tm, HEAD_DIM), lambda b, i: (b, 0, i, 0)), tok(2 * D),
                   feat(W_HEADS), feat(W_HEADS), feat(W_HEADS), feat(W_HEADS),
                   feat(2 * HEAD_DIM), feat(HEAD_DIM), feat(N_HEADS), feat(N_HEADS)],
        out_shape=out_shapes,
        compiler_params=_cparams(("parallel", "parallel")),
        name="in_projection",
    )(x, mod3, norm1_g, w_tok, w_t, bd512, cos_t, sin_t, gk, gcol, bf, bg)


def _cumsum_kernel(x_ref, o_ref):
    x = x_ref[0]
    n = x.shape[-1]
    pos = lax.broadcasted_iota(jnp.int32, x.shape, 1)
    shift = 1
    while shift < n:
        x = x + jnp.where(pos >= shift, pltpu.roll(x, shift, 1), 0.0)
        shift *= 2
    o_ref[0] = x


def _seq_cumsum(logf_t):
    B, H, S = logf_t.shape
    return pl.pallas_call(
        _cumsum_kernel,
        grid=(B,),
        in_specs=[pl.BlockSpec((1, H, S), lambda b: (b, 0, 0))],
        out_specs=pl.BlockSpec((1, H, S), lambda b: (b, 0, 0)),
        out_shape=jax.ShapeDtypeStruct((B, H, S), F32),
        compiler_params=_cparams(("parallel",)),
        name="forget_cumsum",
    )(logf_t)


KC = 256
SUB = 8


def _fold_rows(a, op, ways=1):
    n = a.shape[0] // SUB
    a = a.reshape(n, SUB, a.shape[1])
    chains = [a[w] for w in range(ways)]
    for j in range(ways, n):
        chains[j % ways] = op(chains[j % ways], a[j])
    while len(chains) > 1:
        chains = [op(chains[2 * j], chains[2 * j + 1]) for j in range(len(chains) // 2)]
    return chains[0]


def _softmax_pv(nch, s_ref, acc_ref, vt_at, m_all, o_ref):
    Q = o_ref.shape[1]
    acc_ref[...] = jnp.zeros_like(acc_ref)

    def body(c, lsum):
        off = pl.multiple_of(c * KC, KC)
        new = []
        for hh in range(N_HEADS):
            p = jnp.exp(s_ref[hh, pl.ds(off, KC), :] - m_all[hh])
            new.append(lsum[hh] + _fold_rows(p, jnp.add))
            acc_ref[hh] += jnp.dot(vt_at(hh, off), p.astype(BF16), preferred_element_type=F32)
        return tuple(new)

    lsum = lax.fori_loop(0, nch, body, tuple(jnp.zeros((SUB, Q), F32) for _ in range(N_HEADS)))
    for hh in range(N_HEADS):
        acc_ref[hh] = acc_ref[hh] / jnp.sum(lsum[hh], axis=0, keepdims=True)
    out_t = acc_ref[...].reshape(N_HEADS * HEAD_DIM, Q)
    o_ref[0] = out_t.T.astype(BF16)


def _fox_kernel(k_ref, f_ref, qt_ref, vt_ref, o_ref, s_ref, acc_ref):
    i = pl.program_id(1)
    Q = o_ref.shape[1]
    qts = [qt_ref[0, hh * HEAD_DIM:(hh + 1) * HEAD_DIM, :] for hh in range(N_HEADS)]

    def scores(c, mx, bias):
        off = pl.multiple_of(c * KC, KC)
        new = []
        for hh in range(N_HEADS):
            s = jnp.dot(k_ref[0, hh, pl.ds(off, KC), :], qts[hh], preferred_element_type=F32)
            s = s - f_ref[0, pl.ds(off, KC), hh:hh + 1]
            if bias is not None:
                s = s + bias
            s_ref[hh, pl.ds(off, KC), :] = s
            new.append(jnp.maximum(mx[hh], _fold_rows(s, jnp.maximum)))
        return tuple(new)

    mx = tuple(jnp.full((SUB, Q), MASKED, F32) for _ in range(N_HEADS))
    mx = lax.fori_loop(0, i, lambda c, m: scores(c, m, None), mx)
    kk = lax.broadcasted_iota(jnp.int32, (KC, Q), 0)
    qq = lax.broadcasted_iota(jnp.int32, (KC, Q), 1)
    mx = scores(i, mx, jnp.where(kk <= qq, 0.0, MASKED))
    m_all = [jnp.max(m, axis=0, keepdims=True) for m in mx]
    _softmax_pv(i + 1, s_ref, acc_ref, lambda hh, off: vt_ref[0, hh * HEAD_DIM:(hh + 1) * HEAD_DIM, pl.ds(off, KC)],
                m_all, o_ref)


def _fox_attention(k_heads, f_tok, qt, vt):
    B, H, S, Dh = k_heads.shape
    return pl.pallas_call(
        _fox_kernel,
        grid=(B, S // KC),
        in_specs=[pl.BlockSpec((1, H, S, Dh), lambda b, i: (b, 0, 0, 0)),
                  pl.BlockSpec((1, S, H), lambda b, i: (b, 0, 0)),
                  pl.BlockSpec((1, W_HEADS, KC), lambda b, i: (b, 0, i)),
                  pl.BlockSpec((1, W_HEADS, S), lambda b, i: (b, 0, 0))],
        out_specs=pl.BlockSpec((1, KC, W_HEADS), lambda b, i: (b, i, 0)),
        out_shape=jax.ShapeDtypeStruct((B, S, W_HEADS), BF16),
        scratch_shapes=[pltpu.VMEM((H, S, KC), F32), pltpu.VMEM((H, HEAD_DIM, KC), F32)],
        compiler_params=_cparams(("parallel", "arbitrary")),
        name="fox_attention",
    )(k_heads, f_tok, qt, vt)


INT_MIN = -(2 ** 31)
KEY_NEG_INF = INT_MIN + 0x7FFFFF
HI16 = -(2 ** 16)
PACK = 16


def _dsa_kernel(kkt_ref, dvt_ref, iqt_ref, dqt_ref, iwt_ref, o_ref, key_ref, hi_ref, s_ref, acc_ref, dk_ref, ik_ref,
                *, topk):
    i = pl.program_id(1)
    Q = o_ref.shape[1]
    nch = i + 1

    @pl.when(i == 0)
    def _():
        def to_rows(c, _):
            off = pl.multiple_of(c * KC, KC)
            rows = kkt_ref[0, :, pl.ds(off, KC)].astype(F32).T
            dk_ref[pl.ds(off, KC), :] = rows[:, :HEAD_DIM].astype(BF16)
            ik_ref[pl.ds(off, KC), :] = rows[:, HEAD_DIM:].astype(BF16)
            return 0
        lax.fori_loop(0, kkt_ref.shape[-1] // KC, to_rows, 0)

    sub_k = lax.broadcasted_iota(jnp.int32, (KC, Q), 0)
    sub_r = lax.broadcasted_iota(jnp.int32, (CHUNK, Q), 0)
    q_chunk = (i * Q + lax.broadcasted_iota(jnp.int32, (CHUNK, Q), 1)) >> CHUNK_SHIFT
    iqts = [iqt_ref[0, hh * HEAD_DIM:(hh + 1) * HEAD_DIM, :] for hh in range(N_HEADS)]
    iws = [iwt_ref[0, hh:hh + 1, :] for hh in range(N_HEADS)]

    def score_chunk(c, _):
        for r in range(KC // CHUNK):
            off = pl.multiple_of(c * KC + r * CHUNK, CHUNK)
            ik = ik_ref[pl.ds(off, CHUNK), :]
            sc = jnp.zeros((CHUNK, Q), F32)
            for hh in range(N_HEADS):
                d = jnp.dot(ik, iqts[hh], preferred_element_type=F32)
                sc = sc + iws[hh] * jnp.maximum(d, 0.0)
            sc = sc + 0.0
            allowed = ((off + sub_r) >> CHUNK_SHIFT) <= q_chunk
            bits = pltpu.bitcast(jnp.where(allowed, sc, -jnp.inf), jnp.int32)
            key = bits ^ ((bits >> 31) & 0x7FFFFFFF)
            key_ref[pl.ds(off, CHUNK), :] = key
            hi_ref[pl.ds(off, CHUNK), :] = (key >> 16).astype(jnp.int16)
        return 0

    lax.fori_loop(0, nch, score_chunk, 0)

    def sweep(n, body, init):
        if isinstance(n, int):
            acc = init
            for c in range(n):
                acc = body(c * KC, acc)
            return acc
        return lax.fori_loop(0, n, lambda c, acc: body(pl.multiple_of(c * KC, KC), acc), init)

    def count(pred, n=nch):
        def body(off, acc):
            hit = pred(key_ref[pl.ds(off, KC), :], off + sub_k)
            return acc + _fold_rows(jnp.where(hit, 1.0, 0.0), jnp.add, ways=4)
        return jnp.sum(sweep(n, body, jnp.zeros((SUB, Q), F32)), axis=0, keepdims=True)

    one, zero = jnp.ones((), BF16), jnp.zeros((), BF16)

    def count_hi(cand, n):
        c16 = jnp.broadcast_to(cand >> 16, (PACK, Q)).astype(jnp.int16)

        def body(off, acc):
            kb = hi_ref[pl.ds(off, KC), :].reshape(KC // PACK, PACK, Q)
            hit = jnp.where(kb >= c16[None], one, zero)
            parts = [hit[w] for w in range(4)]
            for j in range(4, KC // PACK):
                parts[j % 4] = parts[j % 4] + hit[j]
            return acc + ((parts[0] + parts[1]) + (parts[2] + parts[3]))
        acc = sweep(n, body, jnp.zeros((PACK, Q), BF16))
        return jnp.sum(acc.astype(F32), axis=0, keepdims=True)

    assert key_ref.shape[0] // PACK <= 256
    kf = jnp.float32(topk)

    def descent(n):
        n_nonneg = count_hi(jnp.zeros((1, Q), jnp.int32), n)
        top_half = n_nonneg >= kf
        thr = jnp.where(top_half, 0, INT_MIN).astype(jnp.int32)
        n_ge = jnp.where(top_half, n_nonneg, jnp.float32(n * KC))

        def descend(counter, top_bit):
            def step(j, carry):
                thr, n_ge = carry
                cand = thr + (jnp.int32(1) << (top_bit - j))
                cnt = counter(cand)
                take = cnt >= kf
                return jnp.where(take, cand, thr), jnp.where(take, cnt, n_ge)
            return step

        carry = lax.fori_loop(0, 15, descend(lambda c: count_hi(c, n), 30), (thr, n_ge))
        return lax.fori_loop(0, 16, descend(lambda c: count(lambda k, _: k >= c, n), 15), carry)

    thr, n_ge = lax.switch(i, [functools.partial(descent, n) for n in range(1, key_ref.shape[0] // KC + 1)])

    excess = (n_ge > kf) & (thr > KEY_NEG_INF)

    @pl.when(jnp.max(jnp.where(excess, 1.0, 0.0)) > 0.0)
    def _():
        need = kf - count(lambda k, _: k > thr)
        nbits = int(np.ceil(np.log2(key_ref.shape[0]))) + 1

        def bound(j, last):
            cand = last + (jnp.int32(1) << (nbits - 1 - j))
            n = count(lambda k, idx: (k == thr) & (idx < cand))
            return jnp.where(n < need, cand, last)

        last = lax.fori_loop(0, nbits, bound, jnp.zeros((1, Q), jnp.int32))

        def demote(c, _):
            off = pl.multiple_of(c * KC, KC)
            k = key_ref[pl.ds(off, KC), :]
            drop = excess & (k == thr) & (off + sub_k > last)
            key_ref[pl.ds(off, KC), :] = jnp.where(drop, thr - 1, k)
            return 0

        lax.fori_loop(0, nch, demote, 0)

    keep_from = jnp.maximum(thr, KEY_NEG_INF + 1)

    qts = [dqt_ref[0, hh * HEAD_DIM:(hh + 1) * HEAD_DIM, :] for hh in range(N_HEADS)]

    def scores(c, mx):
        off = pl.multiple_of(c * KC, KC)
        bias = jnp.where(key_ref[pl.ds(off, KC), :] >= keep_from, 0.0, MASKED)
        dk = dk_ref[pl.ds(off, KC), :]
        new = []
        for hh in range(N_HEADS):
            s = jnp.dot(dk, qts[hh], preferred_element_type=F32) + bias
            s_ref[hh, pl.ds(off, KC), :] = s
            new.append(jnp.maximum(mx[hh], _fold_rows(s, jnp.maximum)))
        return tuple(new)

    mx = lax.fori_loop(0, nch, scores, tuple(jnp.full((SUB, Q), MASKED, F32) for _ in range(N_HEADS)))
    m_all = [jnp.max(m, axis=0, keepdims=True) for m in mx]
    _softmax_pv(nch, s_ref, acc_ref, lambda hh, off: dvt_ref[0, :, pl.ds(off, KC)], m_all, o_ref)


def _dsa_attention(kkt, dvt, iqt, dqt, iwt, topk):
    B, Dh, S = dvt.shape
    rows = lambda r: pl.BlockSpec((1, r, S), lambda b, i: (b, 0, 0))
    qcols = lambda r: pl.BlockSpec((1, r, KC), lambda b, i: (b, 0, i))
    return pl.pallas_call(
        functools.partial(_dsa_kernel, topk=topk),
        grid=(B, S // KC),
        in_specs=[rows(2 * Dh), rows(Dh), qcols(W_HEADS), qcols(W_HEADS), qcols(N_HEADS)],
        out_specs=pl.BlockSpec((1, KC, W_HEADS), lambda b, i: (b, i, 0)),
        out_shape=jax.ShapeDtypeStruct((B, S, W_HEADS), BF16),
        scratch_shapes=[pltpu.VMEM((S, KC), jnp.int32), pltpu.VMEM((S, KC), jnp.int16),
                        pltpu.VMEM((N_HEADS, S, KC), F32), pltpu.VMEM((N_HEADS, HEAD_DIM, KC), F32),
                        pltpu.VMEM((S, Dh), BF16), pltpu.VMEM((S, Dh), BF16)],
        compiler_params=_cparams(("parallel", "arbitrary")),
        name="dsa_attention",
    )(kkt, dvt, iqt, dqt, iwt)


def _first(mask, lane):
    return jnp.min(jnp.where(mask, lane, LANES), axis=-1, keepdims=True)


def _post_kernel(of_ref, od_ref, gate_ref, x_ref, mod_ref, wpf_ref, wpd_ref, wo_ref, g2_ref, wr_ref, br_ref,
                 x1_ref, h2_ref, route_ref):
    D = x_ref.shape[-1]
    pf = jnp.dot(of_ref[0], wpf_ref[...], preferred_element_type=F32)
    pd = jnp.dot(od_ref[0], wpd_ref[...], preferred_element_type=F32)
    merged = gate_ref[0, :, :D].astype(F32) * pf + gate_ref[0, :, D:].astype(F32) * pd
    y = jnp.dot(merged.astype(BF16), wo_ref[...], preferred_element_type=F32)
    x1 = x_ref[0] + mod_ref[0, 2:3, :] * y
    x1_ref[0] = x1

    ms = jnp.mean(x1 * x1, axis=-1, keepdims=True)
    h2 = x1 * lax.rsqrt(ms + EPS) * g2_ref[...]
    h2 = h2 * (1.0 + mod_ref[0, 4:5, :]) + mod_ref[0, 3:4, :]
    hb = h2.astype(BF16)
    h2_ref[0] = h2

    logits = jnp.dot(hb, wr_ref[...], preferred_element_type=F32) + br_ref[...]
    lane = lax.broadcasted_iota(jnp.int32, logits.shape, 1)
    is_grp = lane < N_GROUPS
    gl = jnp.where(is_grp, logits, -jnp.inf)
    gmax = jnp.max(gl, axis=-1, keepdims=True)
    g_idx = _first(gl == gmax, lane)
    g_w = 1.0 / jnp.sum(jnp.exp(gl - gmax), axis=-1, keepdims=True)

    e_lo = N_GROUPS + g_idx * EXPERTS_PER_GROUP
    in_grp = (lane >= e_lo) & (lane < e_lo + EXPERTS_PER_GROUP)
    el = jnp.where(in_grp, logits, -jnp.inf)
    emax = jnp.max(el, axis=-1, keepdims=True)
    ee = jnp.exp(el - emax)
    prob = ee / jnp.sum(ee, axis=-1, keepdims=True)
    prob = jnp.where(in_grp, prob, -1.0)
    p0 = jnp.max(prob, axis=-1, keepdims=True)
    l0 = _first(prob == p0, lane)
    rest = jnp.where(lane == l0, -1.0, prob)
    p1 = jnp.max(rest, axis=-1, keepdims=True)
    l1 = _first(rest == p1, lane)
    psum = p0 + p1
    w0 = g_w * (p0 / psum)
    w1 = g_w * (p1 / psum)
    e0 = (l0 - N_GROUPS).astype(F32)
    e1 = (l1 - N_GROUPS).astype(F32)
    route_ref[0] = jnp.where(lane == 0, e0, jnp.where(lane == 1, e1, jnp.where(lane == 2, w0,
                             jnp.where(lane == 3, w1, 0.0))))


def _post_attention(of, od, gates, x, mod3, wpf, wpd, wo, g2, wr, br, tm):
    B, S, D = x.shape
    tok = lambda w: pl.BlockSpec((1, tm, w), lambda b, i: (b, i, 0))
    const = lambda shape: pl.BlockSpec(shape, lambda b, i: (0,) * len(shape))
    return pl.pallas_call(
        _post_kernel,
        grid=(B, S // tm),
        in_specs=[tok(W_HEADS), tok(W_HEADS), tok(2 * D), tok(D),
                  pl.BlockSpec((1, 6, D), lambda b, i: (b, 0, 0)),
                  const(wpf.shape), const(wpd.shape), const(wo.shape),
                  const((1, D)), const((D, LANES)), const((1, LANES))],
        out_specs=[tok(D), tok(D), tok(LANES)],
        out_shape=[jax.ShapeDtypeStruct((B, S, D), F32),
                   jax.ShapeDtypeStruct((B, S, D), F32),
                   jax.ShapeDtypeStruct((B, S, LANES), F32)],
        compiler_params=_cparams(("parallel", "parallel")),
        name="merge_out_router",
    )(of, od, gates, x, mod3, wpf, wpd, wo, g2, wr, br)


def _rank_kernel(route_ref, tri_ref, rank_ref, count_ref, carry_ref):
    @pl.when(pl.program_id(0) == 0)
    def _():
        carry_ref[...] = jnp.zeros_like(carry_ref)

    r = route_ref[...]
    lane = lax.broadcasted_iota(jnp.int32, r.shape, 1).astype(F32)
    hot0 = lane == r[:, 0:1]
    hot1 = lane == r[:, 1:2]
    hits = jnp.where(hot0 | hot1, 1.0, 0.0)
    incl = jnp.dot(tri_ref[...], hits.astype(BF16), preferred_element_type=F32)
    before = incl - hits + carry_ref[...]
    r0 = jnp.sum(jnp.where(hot0, before, 0.0), axis=-1, keepdims=True)
    r1 = jnp.sum(jnp.where(hot1, before, 0.0), axis=-1, keepdims=True)
    rank_ref[...] = jnp.where(lane == 0.0, r0, jnp.where(lane == 1.0, r1, 0.0))
    carry_ref[...] = carry_ref[...] + jnp.sum(hits, axis=0, keepdims=True)
    count_ref[...] = carry_ref[...]


def _expert_ranks(route, tm):
    N = route.shape[0]
    tri = jnp.asarray(np.tril(np.ones((tm, tm), np.float32)), BF16)
    return pl.pallas_call(
        _rank_kernel,
        grid=(N // tm,),
        in_specs=[pl.BlockSpec((tm, LANES), lambda i: (i, 0)),
                  pl.BlockSpec((tm, tm), lambda i: (0, 0))],
        out_specs=[pl.BlockSpec((tm, LANES), lambda i: (i, 0)),
                   pl.BlockSpec((1, LANES), lambda i: (0, 0))],
        out_shape=[jax.ShapeDtypeStruct((N, LANES), F32), jax.ShapeDtypeStruct((1, LANES), F32)],
        scratch_shapes=[pltpu.VMEM((1, LANES), F32)],
        compiler_params=_cparams(("arbitrary",)),
        name="expert_ranks",
    )(route, tri)


def _dispatch_kernel(zstart_ref, zon_ref, nt_ref, pos_ref, h_ref, xs_ref, zbuf, sem, zsem, *, tm, tg):
    @pl.when(pl.program_id(0) == 0)
    def _():
        zbuf[...] = jnp.zeros_like(zbuf)

        def zero_tile(start):
            return pltpu.make_async_copy(zbuf, xs_ref.at[pl.ds(pl.multiple_of(start, tg), tg), :], zsem)

        n_tiles = xs_ref.shape[0] // tg
        for e in range(N_EXPERTS):
            pl.when(zon_ref[e] > 0)(lambda e=e: zero_tile(zstart_ref[e]).start())
        lax.fori_loop(nt_ref[0], n_tiles, lambda t, _: (zero_tile(t * tg).start(), 0)[1], 0)
        for e in range(N_EXPERTS):
            pl.when(zon_ref[e] > 0)(lambda e=e: zero_tile(zstart_ref[e]).wait())
        lax.fori_loop(nt_ref[0], n_tiles, lambda t, _: (zero_tile(t * tg).wait(), 0)[1], 0)

    def copy(r, slot):
        return pltpu.make_async_copy(h_ref.at[pl.ds(r, 1), :],
                                     xs_ref.at[pl.ds(pos_ref[0, slot, r], 1), :], sem)

    def issue(r, _):
        copy(r, 0).start()
        copy(r, 1).start()
        return 0

    lax.fori_loop(0, tm, issue, 0, unroll=DMA_UNROLL)
    for _ in range(2):
        pltpu.make_async_copy(h_ref, xs_ref.at[pl.ds(0, tm), :], sem).wait()


def _dispatch(h2, pos3, last_tile_start, has_rows, n_tiles_used, n_rows, tm, tg):
    N, D = h2.shape
    grid_spec = pltpu.PrefetchScalarGridSpec(
        num_scalar_prefetch=3,
        grid=(N // tm,),
        in_specs=[pl.BlockSpec((1, 2, tm), lambda i, zs, zo, nt: (i, 0, 0), memory_space=pltpu.SMEM),
                  pl.BlockSpec((tm, D), lambda i, zs, zo, nt: (i, 0))],
        out_specs=pl.BlockSpec(memory_space=pl.ANY),
        scratch_shapes=[pltpu.VMEM((tg, D), F32), pltpu.SemaphoreType.DMA(()), pltpu.SemaphoreType.DMA(())],
    )
    return pl.pallas_call(
        functools.partial(_dispatch_kernel, tm=tm, tg=tg),
        grid_spec=grid_spec,
        out_shape=jax.ShapeDtypeStruct((n_rows, D), F32),
        compiler_params=_cparams(("arbitrary",)),
        name="moe_dispatch",
    )(last_tile_start, has_rows, n_tiles_used, pos3, h2)


def _expert_kernel(te_ref, nt_ref, xs_ref, w1_ref, w3_ref, w2_ref, y_ref, w1b, w3b, w2b):
    g = pl.program_id(0)
    used = g < nt_ref[0]
    new_expert = (g == 0) | (te_ref[g] != te_ref[jnp.maximum(g - 1, 0)])

    @pl.when(used & new_expert)
    def _():
        w1b[...] = w1_ref[0].astype(BF16)
        w3b[...] = w3_ref[0].astype(BF16)
        w2b[...] = w2_ref[0].astype(BF16)

    @pl.when(used)
    def _():
        xb = xs_ref[...].astype(BF16)
        a = jnp.dot(xb, w1b[...], preferred_element_type=F32)
        b = jnp.dot(xb, w3b[...], preferred_element_type=F32)
        hmid = (a * jax.nn.sigmoid(a) * b).astype(BF16)
        y_ref[...] = jnp.dot(hmid, w2b[...], preferred_element_type=F32)

    @pl.when(jnp.logical_not(used))
    def _():
        y_ref[...] = jnp.zeros_like(y_ref)


def _experts(tile_expert, n_tiles_used, xs, w1, w3, w2, tg):
    P, D = xs.shape
    E, _, De = w1.shape
    row_tile = lambda g, te, nt: (jnp.minimum(g, nt[0] - 1), 0)
    grid_spec = pltpu.PrefetchScalarGridSpec(
        num_scalar_prefetch=2,
        grid=(P // tg,),
        in_specs=[pl.BlockSpec((tg, D), row_tile),
                  pl.BlockSpec((1, D, De), lambda g, te, nt: (te[g], 0, 0)),
                  pl.BlockSpec((1, D, De), lambda g, te, nt: (te[g], 0, 0)),
                  pl.BlockSpec((1, De, D), lambda g, te, nt: (te[g], 0, 0))],
        out_specs=pl.BlockSpec((tg, D), lambda g, te, nt: (g, 0)),
        scratch_shapes=[pltpu.VMEM((D, De), BF16), pltpu.VMEM((D, De), BF16), pltpu.VMEM((De, D), BF16)],
    )
    return pl.pallas_call(
        _expert_kernel,
        grid_spec=grid_spec,
        out_shape=jax.ShapeDtypeStruct((P, D), F32),
        compiler_params=_cparams(("arbitrary",)),
        name="moe_experts",
    )(tile_expert, n_tiles_used, xs, w1, w3, w2)


def _combine_kernel(pos_ref, y_ref, x1_ref, route_ref, gt_ref, o_ref, buf0, buf1, sem, *, tm):
    def copy(r, slot, buf):
        return pltpu.make_async_copy(y_ref.at[pl.ds(pos_ref[0, slot, r], 1), :],
                                     buf.at[pl.ds(r, 1), :], sem)

    def issue(r, _):
        copy(r, 0, buf0).start()
        copy(r, 1, buf1).start()
        return 0

    lax.fori_loop(0, tm, issue, 0, unroll=DMA_UNROLL)
    for buf in (buf0, buf1):
        pltpu.make_async_copy(y_ref.at[pl.ds(0, tm), :], buf, sem).wait()
    w0 = route_ref[:, 2:3]
    w1 = route_ref[:, 3:4]
    y = buf0[...] * w0 + buf1[...] * w1
    o_ref[...] = x1_ref[...] + gt_ref[0] * y


def _combine(pos3, y, x1, route, gt2, tm, S):
    N, D = x1.shape
    per_b = S // tm
    return pl.pallas_call(
        functools.partial(_combine_kernel, tm=tm),
        grid=(N // tm,),
        in_specs=[pl.BlockSpec((1, 2, tm), lambda i: (i, 0, 0), memory_space=pltpu.SMEM),
                  pl.BlockSpec(memory_space=pl.ANY),
                  pl.BlockSpec((tm, D), lambda i: (i, 0)),
                  pl.BlockSpec((tm, LANES), lambda i: (i, 0)),
                  pl.BlockSpec((1, 1, D), lambda i: (i // per_b, 0, 0))],
        out_specs=pl.BlockSpec((tm, D), lambda i: (i, 0)),
        out_shape=jax.ShapeDtypeStruct((N, D), F32),
        scratch_shapes=[pltpu.VMEM((tm, D), F32), pltpu.VMEM((tm, D), F32), pltpu.SemaphoreType.DMA(())],
        compiler_params=_cparams(("arbitrary",)),
        name="moe_combine",
    )(pos3, y, x1, route, gt2)


def _rope_tables(positions):
    inv = ROPE_THETA ** (-jnp.arange(HALF, dtype=F32) / HALF)
    ang = positions.astype(F32)[:, None, :] * inv[None, :, None]
    return jnp.cos(ang), jnp.sin(ang)


def _block_diag_mean(width):
    blk = np.kron(np.eye(width // HEAD_DIM, dtype=np.float32), np.full((HEAD_DIM, HEAD_DIM), 1.0 / HEAD_DIM, np.float32))
    return jnp.asarray(blk, BF16)


def _layer(x, c_mod, positions, norm1_g, norm2_g, w_in, b_fgt, b_gate, qn_fox, kn_fox, qn_dsa, kn_dsa,
           w_proj_fox, w_proj_dsa, w_out, r_w_grp, r_b_grp, r_w_exp, r_b_exp, w1, w3, w2):
    B, S, D = x.shape
    N = B * S
    topk = min(TOPK_MAX, S // 4)
    tm = min(512, S)
    scale = HEAD_DIM ** -0.5
    mod3 = c_mod.reshape(B, 6, D)

    o = np.cumsum([0, 512, 512, 512, 8, 512, 64, 64, 512, 64, 8, D, D])
    seg = lambda k: w_in[:, o[k]:o[k + 1]]
    zpad = jnp.zeros((D, LANES - HEAD_DIM - 2 * N_HEADS), F32)
    w_tok = jnp.concatenate([seg(1), seg(10), seg(11)], axis=1).astype(BF16)
    w_t = jnp.concatenate([seg(0), seg(4), seg(7), seg(2),
                           seg(5), seg(8),
                           seg(6), seg(3), seg(9), zpad], axis=1).T.astype(BF16)
    gcol = jnp.stack([qn_fox * scale, qn_dsa * scale, kn_dsa]).reshape(3, HEAD_DIM, 1)
    cos_t, sin_t = _rope_tables(positions)

    k_heads, gates, fqt, dqt, iqt, fvt, kkt, dvt, logf_t, iwt = _in_projection(
        x, mod3, norm1_g.reshape(1, D), w_tok, w_t, _block_diag_mean(W_HEADS), cos_t, sin_t,
        jnp.tile(kn_fox, N_HEADS).reshape(1, W_HEADS), gcol, b_fgt.reshape(N_HEADS, 1),
        b_gate.reshape(1, 2 * D), tm)

    f_tok = jnp.transpose(_seq_cumsum(logf_t), (0, 2, 1))
    of = _fox_attention(k_heads, f_tok, fqt, fvt)
    od = _dsa_attention(kkt, dvt, iqt, dqt, iwt, topk)

    wr = jnp.concatenate([r_w_grp, r_w_exp, jnp.zeros((D, LANES - N_GROUPS - N_EXPERTS), F32)], axis=1).astype(BF16)
    br = jnp.concatenate([r_b_grp, r_b_exp, jnp.zeros((LANES - N_GROUPS - N_EXPERTS,), F32)]).reshape(1, LANES)
    x1, h2, route = _post_attention(of, od, gates, x, mod3, w_proj_fox.astype(BF16), w_proj_dsa.astype(BF16),
                                    w_out.astype(BF16), norm2_g.reshape(1, D), wr, br, tm)
    x1, h2, route = x1.reshape(N, D), h2.reshape(N, D), route.reshape(N, LANES)

    tg = 512 if N * 2 >= 512 * N_EXPERTS else 128
    ranks, counts = _expert_ranks(route, tm)
    counts = counts[0, :N_EXPERTS].astype(jnp.int32)
    padded = ((counts + tg - 1) // tg) * tg
    ends = jnp.cumsum(padded)
    starts = ends - padded
    e01 = route[:, :2].astype(jnp.int32)
    start_of = jnp.sum(jnp.where(e01[..., None] == jnp.arange(N_EXPERTS, dtype=jnp.int32), starts, 0), axis=-1)
    pos = start_of + ranks[:, :2].astype(jnp.int32)
    n_rows = N * 2 + N_EXPERTS * tg
    n_tiles = n_rows // tg
    tile_start = jnp.arange(n_tiles, dtype=jnp.int32) * tg
    tile_expert = jnp.minimum(jnp.sum((ends[None, :] <= tile_start[:, None]).astype(jnp.int32), axis=1),
                              N_EXPERTS - 1)
    n_used = (ends[-1] // tg).astype(jnp.int32).reshape(1)

    td = min(256, S)
    pos3 = jnp.transpose(pos.reshape(N // td, td, 2), (0, 2, 1))
    xs = _dispatch(h2, pos3, jnp.maximum(ends - tg, 0).astype(jnp.int32), (padded > 0).astype(jnp.int32),
                   n_used, n_rows, td, tg)
    y = _experts(tile_expert, n_used, xs, w1, w3, w2, tg)
    out = _combine(pos3, y, x1, route, mod3[:, 5:6, :], td, S)
    return out.reshape(B, S, D)


def kernel(x, c, positions, ada_w, ada_b, norm1_g, norm2_g, w_in, b_fgt, b_gate, qn_fox, kn_fox, qn_dsa, kn_dsa, w_proj_fox, w_proj_dsa, w_out, router_w_grp, router_b_grp, router_w_exp, router_b_exp, exp_w1, exp_w3, exp_w2):
    for l in range(ada_w.shape[0]):
        c_mod = _modulation(c, ada_w[l], ada_b[l])
        x = _layer(x, c_mod, positions, norm1_g[l], norm2_g[l], w_in[l], b_fgt[l], b_gate[l],
                   qn_fox[l], kn_fox[l], qn_dsa[l], kn_dsa[l], w_proj_fox[l], w_proj_dsa[l], w_out[l],
                   router_w_grp[l], router_b_grp[l], router_w_exp[l], router_b_exp[l],
                   exp_w1[l], exp_w3[l], exp_w2[l])
    return x
```

```python
import functools

import jax
import jax.numpy as jnp
import numpy as np
from jax import lax
from jax.experimental import pallas as pl
from jax.experimental.pallas import tpu as pltpu
from jax.experimental.pallas import tpu_sc as plsc

F32 = jnp.float32
BF16 = jnp.bfloat16

CHUNK = 64
CHUNK_SHIFT = 6
DMA_UNROLL = 8
HEAD_DIM = 64
N_HEADS = 8
W_HEADS = N_HEADS * HEAD_DIM
TOPK_MAX = 256
ROPE_THETA = 10000.0
N_GROUPS = 4
EXPERTS_PER_GROUP = 8
N_EXPERTS = N_GROUPS * EXPERTS_PER_GROUP
EPS = 1e-6
LOG2E = 1.4426950408889634
MASKED = -1e30

LANES = 128
VMEM_LIMIT = 56 * 1024 * 1024


def _cparams(sem):
    return pltpu.CompilerParams(dimension_semantics=sem, vmem_limit_bytes=VMEM_LIMIT)


def _mod_kernel(c_ref, w_ref, b_ref, o_ref):
    c = c_ref[...]
    ca = (c * jax.nn.sigmoid(c)).astype(BF16)
    o_ref[...] = jnp.dot(ca, w_ref[...].astype(BF16), preferred_element_type=F32) + b_ref[...]


def _modulation(c, ada_w, ada_b):
    B, D = c.shape
    n = ada_w.shape[1] // D
    return pl.pallas_call(
        _mod_kernel,
        grid=(n,),
        in_specs=[pl.BlockSpec((B, D), lambda j: (0, 0)),
                  pl.BlockSpec((D, D), lambda j: (0, j)),
                  pl.BlockSpec((1, D), lambda j: (0, j))],
        out_specs=pl.BlockSpec((B, D), lambda j: (0, j)),
        out_shape=jax.ShapeDtypeStruct((B, n * D), F32),
        compiler_params=_cparams(("arbitrary",)),
        name="adaln_mod",
    )(c, ada_w, ada_b.reshape(1, -1))


R_FQ, R_DQ, R_IQ, R_FV = 0, 512, 1024, 1536
R_KK = 2048
R_S2 = 2176
R_END = 2304
HALF = HEAD_DIM // 2


def _inproj_kernel(x_ref, mod_ref, g1_ref, wtok_ref, wt_ref, bd512_ref, cos_ref, sin_ref,
                   gk_ref, gcol_ref, bf_ref, bg_ref,
                   fk_ref, gate_ref, fqt_ref, dqt_ref, iqt_ref, fvt_ref, kkt_ref, dvt_ref, lf_ref, iwt_ref):
    x = x_ref[0]
    ms = jnp.mean(x * x, axis=-1, keepdims=True)
    h = x * lax.rsqrt(ms + EPS) * g1_ref[...]
    h = h * (1.0 + mod_ref[0, 1:2, :]) + mod_ref[0, 0:1, :]
    hb = h.astype(BF16)
    D = x.shape[-1]
    cos, sin = cos_ref[0], sin_ref[0]

    def proj_t(lo, hi):
        return lax.dot_general(wt_ref[lo:hi, :], hb, (((1,), (1,)), ((), ())), preferred_element_type=F32)

    def norm_t(yh, gain):
        msq = jnp.mean(yh * yh, axis=0, keepdims=True)
        return yh * lax.rsqrt(msq + EPS) * gain

    def rope_store(ref, lo, yh):
        x1, x2 = yh[:HALF], yh[HALF:]
        ref[0, lo:lo + HALF, :] = (x1 * cos - x2 * sin).astype(ref.dtype)
        ref[0, lo + HALF:lo + HEAD_DIM, :] = (x2 * cos + x1 * sin).astype(ref.dtype)

    fq = proj_t(R_FQ, R_FQ + W_HEADS)
    dq = proj_t(R_DQ, R_DQ + W_HEADS)
    iq = proj_t(R_IQ, R_IQ + W_HEADS)
    for hh in range(N_HEADS):
        lo = hh * HEAD_DIM
        fqt_ref[0, lo:lo + HEAD_DIM, :] = norm_t(fq[lo:lo + HEAD_DIM], gcol_ref[0]).astype(BF16)
        rope_store(dqt_ref, lo, norm_t(dq[lo:lo + HEAD_DIM], gcol_ref[1]))
        rope_store(iqt_ref, lo, iq[lo:lo + HEAD_DIM])
    fvt_ref[0] = proj_t(R_FV, R_FV + W_HEADS).astype(BF16)

    kk = proj_t(R_KK, R_KK + 2 * HEAD_DIM)
    rope_store(kkt_ref, 0, norm_t(kk[:HEAD_DIM], gcol_ref[2]))
    rope_store(kkt_ref, HEAD_DIM, kk[HEAD_DIM:])

    s2 = proj_t(R_S2, R_S2 + LANES)
    dvt_ref[0] = s2[:HEAD_DIM].astype(BF16)
    z = s2[HEAD_DIM:HEAD_DIM + N_HEADS] + bf_ref[...]
    lf_ref[0] = jnp.minimum(z, 0.0) - jnp.log(1.0 + jnp.exp(-jnp.abs(z)))
    iwt_ref[0] = s2[HEAD_DIM + N_HEADS:HEAD_DIM + 2 * N_HEADS]

    fk = jnp.dot(hb, wtok_ref[:, :W_HEADS], preferred_element_type=F32)
    msq = jnp.dot((fk * fk).astype(BF16), bd512_ref[...], preferred_element_type=F32)
    fk = (fk * lax.rsqrt(msq + EPS) * gk_ref[...]).astype(BF16)
    for hh in range(N_HEADS):
        fk_ref[0, hh] = fk[:, hh * HEAD_DIM:(hh + 1) * HEAD_DIM]
    g = jnp.dot(hb, wtok_ref[:, W_HEADS:], preferred_element_type=F32)
    gate_ref[0] = jax.nn.sigmoid(g + bg_ref[...]).astype(BF16)


def _in_projection(x, mod3, norm1_g, w_tok, w_t, bd512, cos_t, sin_t, gk, gcol, bf, bg, tm):
    B, S, D = x.shape
    tok = lambda w: pl.BlockSpec((1, tm, w), lambda b, i: (b, i, 0))
    feat = lambda r: pl.BlockSpec((1, r, tm), lambda b, i: (b, 0, i))
    const = lambda shape: pl.BlockSpec(shape, lambda b, i: (0,) * len(shape))
    out_shapes = [jax.ShapeDtypeStruct((B, N_HEADS, S, HEAD_DIM), BF16),
                  jax.ShapeDtypeStruct((B, S, 2 * D), BF16)] + \
                 [jax.ShapeDtypeStruct((B, W_HEADS, S), BF16)] * 4 + \
                 [jax.ShapeDtypeStruct((B, 2 * HEAD_DIM, S), BF16),
                  jax.ShapeDtypeStruct((B, HEAD_DIM, S), BF16),
                  jax.ShapeDtypeStruct((B, N_HEADS, S), F32),
                  jax.ShapeDtypeStruct((B, N_HEADS, S), F32)]
    return pl.pallas_call(
        _inproj_kernel,
        grid=(B, S // tm),
        in_specs=[tok(D),
                  pl.BlockSpec((1, 6, D), lambda b, i: (b, 0, 0)),
                  const((1, D)),
                  const(w_tok.shape), const(w_t.shape), const((W_HEADS, W_HEADS)),
                  feat(HALF), feat(HALF),
                  const((1, W_HEADS)), const((3, HEAD_DIM, 1)), const((N_HEADS, 1)), const((1, 2 * D))],
        out_specs=[pl.BlockSpec((1, N_HEADS, tm, HEAD_DIM), lambda b, i: (b, 0, i, 0)), tok(2 * D),
                   feat(W_HEADS), feat(W_HEADS), feat(W_HEADS), feat(W_HEADS),
                   feat(2 * HEAD_DIM), feat(HEAD_DIM), feat(N_HEADS), feat(N_HEADS)],
        out_shape=out_shapes,
        compiler_params=_cparams(("parallel", "parallel")),
        name="in_projection",
    )(x, mod3, norm1_g, w_tok, w_t, bd512, cos_t, sin_t, gk, gcol, bf, bg)


def _cumsum_kernel(x_ref, o_ref):
    x = x_ref[0]
    n = x.shape[-1]
    pos = lax.broadcasted_iota(jnp.int32, x.shape, 1)
    shift = 1
    while shift < n:
        x = x + jnp.where(pos >= shift, pltpu.roll(x, shift, 1), 0.0)
        shift *= 2
    o_ref[0] = x


def _seq_cumsum(logf_t):
    B, H, S = logf_t.shape
    return pl.pallas_call(
        _cumsum_kernel,
        grid=(B,),
        in_specs=[pl.BlockSpec((1, H, S), lambda b: (b, 0, 0))],
        out_specs=pl.BlockSpec((1, H, S), lambda b: (b, 0, 0)),
        out_shape=jax.ShapeDtypeStruct((B, H, S), F32),
        compiler_params=_cparams(("parallel",)),
        name="forget_cumsum",
    )(logf_t)


KC = 256
SUB = 8


def _fold_rows(a, op, ways=1):
    n = a.shape[0] // SUB
    a = a.reshape(n, SUB, a.shape[1])
    chains = [a[w] for w in range(ways)]
    for j in range(ways, n):
        chains[j % ways] = op(chains[j % ways], a[j])
    while len(chains) > 1:
        chains = [op(chains[2 * j], chains[2 * j + 1]) for j in range(len(chains) // 2)]
    return chains[0]


def _softmax_pv(nch, s_ref, acc_ref, vt_at, m_all, o_ref):
    Q = o_ref.shape[1]
    acc_ref[...] = jnp.zeros_like(acc_ref)

    def body(c, lsum):
        off = pl.multiple_of(c * KC, KC)
        new = []
        for hh in range(N_HEADS):
            p = jnp.exp2(s_ref[hh, pl.ds(off, KC), :] - m_all[hh])
            new.append(lsum[hh] + _fold_rows(p, jnp.add))
            acc_ref[hh] += jnp.dot(vt_at(hh, off), p.astype(BF16), preferred_element_type=F32)
        return tuple(new)

    lsum = lax.fori_loop(0, nch, body, tuple(jnp.zeros((SUB, Q), F32) for _ in range(N_HEADS)))
    for hh in range(N_HEADS):
        acc_ref[hh] = acc_ref[hh] / jnp.sum(lsum[hh], axis=0, keepdims=True)
    out_t = acc_ref[...].reshape(N_HEADS * HEAD_DIM, Q)
    o_ref[0] = out_t.T.astype(BF16)


def _fox_kernel(k_ref, f_ref, qt_ref, vt_ref, o_ref, s_ref, acc_ref):
    i = pl.program_id(1)
    Q = o_ref.shape[1]
    qts = [qt_ref[0, hh * HEAD_DIM:(hh + 1) * HEAD_DIM, :] for hh in range(N_HEADS)]

    def scores(c, mx, bias):
        off = pl.multiple_of(c * KC, KC)
        new = []
        for hh in range(N_HEADS):
            s = jnp.dot(k_ref[0, hh, pl.ds(off, KC), :], qts[hh], preferred_element_type=F32)
            s = s - f_ref[0, pl.ds(off, KC), hh:hh + 1]
            if bias is not None:
                s = s + bias
            s_ref[hh, pl.ds(off, KC), :] = s
            new.append(jnp.maximum(mx[hh], _fold_rows(s, jnp.maximum)))
        return tuple(new)

    mx = tuple(jnp.full((SUB, Q), MASKED, F32) for _ in range(N_HEADS))
    mx = lax.fori_loop(0, i, lambda c, m: scores(c, m, None), mx)
    kk = lax.broadcasted_iota(jnp.int32, (KC, Q), 0)
    qq = lax.broadcasted_iota(jnp.int32, (KC, Q), 1)
    mx = scores(i, mx, jnp.where(kk <= qq, 0.0, MASKED))
    m_all = [jnp.max(m, axis=0, keepdims=True) for m in mx]
    _softmax_pv(i + 1, s_ref, acc_ref, lambda hh, off: vt_ref[0, hh * HEAD_DIM:(hh + 1) * HEAD_DIM, pl.ds(off, KC)],
                m_all, o_ref)


def _fox_attention(k_heads, f_tok, qt, vt):
    B, H, S, Dh = k_heads.shape
    return pl.pallas_call(
        _fox_kernel,
        grid=(B, S // KC),
        in_specs=[pl.BlockSpec((1, H, S, Dh), lambda b, i: (b, 0, 0, 0)),
                  pl.BlockSpec((1, S, H), lambda b, i: (b, 0, 0)),
                  pl.BlockSpec((1, W_HEADS, KC), lambda b, i: (b, 0, i)),
                  pl.BlockSpec((1, W_HEADS, S), lambda b, i: (b, 0, 0))],
        out_specs=pl.BlockSpec((1, KC, W_HEADS), lambda b, i: (b, i, 0)),
        out_shape=jax.ShapeDtypeStruct((B, S, W_HEADS), BF16),
        scratch_shapes=[pltpu.VMEM((H, S, KC), F32), pltpu.VMEM((H, HEAD_DIM, KC), F32)],
        compiler_params=_cparams(("parallel", "arbitrary")),
        name="fox_attention",
    )(k_heads, f_tok, qt, vt)


INT_MIN = -(2 ** 31)
KEY_NEG_INF = INT_MIN + 0x7FFFFF
HI16 = -(2 ** 16)
PACK = 16


def _dsa_kernel(kkt_ref, dvt_ref, iqt_ref, dqt_ref, iwt_ref, o_ref, key_ref, hi_ref, s_ref, acc_ref, dk_ref, ik_ref,
                *, topk):
    i = pl.program_id(1)
    Q = o_ref.shape[1]
    nch = i + 1

    @pl.when(i == 0)
    def _():
        def to_rows(c, _):
            off = pl.multiple_of(c * KC, KC)
            rows = kkt_ref[0, :, pl.ds(off, KC)].astype(F32).T
            dk_ref[pl.ds(off, KC), :] = rows[:, :HEAD_DIM].astype(BF16)
            ik_ref[pl.ds(off, KC), :] = rows[:, HEAD_DIM:].astype(BF16)
            return 0
        lax.fori_loop(0, kkt_ref.shape[-1] // KC, to_rows, 0)

    sub_k = lax.broadcasted_iota(jnp.int32, (KC, Q), 0)
    sub_r = lax.broadcasted_iota(jnp.int32, (CHUNK, Q), 0)
    q_chunk = (i * Q + lax.broadcasted_iota(jnp.int32, (CHUNK, Q), 1)) >> CHUNK_SHIFT
    iqts = [iqt_ref[0, hh * HEAD_DIM:(hh + 1) * HEAD_DIM, :] for hh in range(N_HEADS)]
    iws = [iwt_ref[0, hh:hh + 1, :] for hh in range(N_HEADS)]

    def score_chunk(c, _):
        for r in range(KC // CHUNK):
            off = pl.multiple_of(c * KC + r * CHUNK, CHUNK)
            ik = ik_ref[pl.ds(off, CHUNK), :]
            sc = jnp.zeros((CHUNK, Q), F32)
            for hh in range(N_HEADS):
                d = jnp.dot(ik, iqts[hh], preferred_element_type=F32)
                sc = sc + iws[hh] * jnp.maximum(d, 0.0)
            sc = sc + 0.0
            allowed = ((off + sub_r) >> CHUNK_SHIFT) <= q_chunk
            bits = pltpu.bitcast(jnp.where(allowed, sc, -jnp.inf), jnp.int32)
            key = bits ^ ((bits >> 31) & 0x7FFFFFFF)
            key_ref[pl.ds(off, CHUNK), :] = key
            hi_ref[pl.ds(off, CHUNK), :] = (key >> 16).astype(jnp.int16)
        return 0

    lax.fori_loop(0, nch, score_chunk, 0)

    def sweep(n, body, init):
        if isinstance(n, int):
            acc = init
            for c in range(n):
                acc = body(c * KC, acc)
            return acc
        return lax.fori_loop(0, n, lambda c, acc: body(pl.multiple_of(c * KC, KC), acc), init)

    def count(pred, n=nch):
        def body(off, acc):
            hit = pred(key_ref[pl.ds(off, KC), :], off + sub_k)
            return acc + _fold_rows(jnp.where(hit, 1.0, 0.0), jnp.add, ways=4)
        return jnp.sum(sweep(n, body, jnp.zeros((SUB, Q), F32)), axis=0, keepdims=True)

    one, zero = jnp.ones((), BF16), jnp.zeros((), BF16)

    def count_hi(cand, n):
        c16 = jnp.broadcast_to(cand >> 16, (PACK, Q)).astype(jnp.int16)

        def body(off, acc):
            kb = hi_ref[pl.ds(off, KC), :].reshape(KC // PACK, PACK, Q)
            hit = jnp.where(kb >= c16[None], one, zero)
            parts = [hit[w] for w in range(4)]
            for j in range(4, KC // PACK):
                parts[j % 4] = parts[j % 4] + hit[j]
            return acc + ((parts[0] + parts[1]) + (parts[2] + parts[3]))
        acc = sweep(n, body, jnp.zeros((PACK, Q), BF16))
        return jnp.sum(acc.astype(F32), axis=0, keepdims=True)

    assert key_ref.shape[0] // PACK <= 256
    kf = jnp.float32(topk)

    def descent(n):
        n_nonneg = count_hi(jnp.zeros((1, Q), jnp.int32), n)
        top_half = n_nonneg >= kf
        thr = jnp.where(top_half, 0, INT_MIN).astype(jnp.int32)
        n_ge = jnp.where(top_half, n_nonneg, jnp.float32(n * KC))

        def descend(counter, top_bit):
            def step(j, carry):
                thr, n_ge = carry
                cand = thr + (jnp.int32(1) << (top_bit - j))
                cnt = counter(cand)
                take = cnt >= kf
                return jnp.where(take, cand, thr), jnp.where(take, cnt, n_ge)
            return step

        carry = lax.fori_loop(0, 15, descend(lambda c: count_hi(c, n), 30), (thr, n_ge))
        return lax.fori_loop(0, 16, descend(lambda c: count(lambda k, _: k >= c, n), 15), carry)

    thr, n_ge = lax.switch(i, [functools.partial(descent, n) for n in range(1, key_ref.shape[0] // KC + 1)])

    excess = (n_ge > kf) & (thr > KEY_NEG_INF)

    @pl.when(jnp.max(jnp.where(excess, 1.0, 0.0)) > 0.0)
    def _():
        need = kf - count(lambda k, _: k > thr)
        nbits = int(np.ceil(np.log2(key_ref.shape[0]))) + 1

        def bound(j, last):
            cand = last + (jnp.int32(1) << (nbits - 1 - j))
            n = count(lambda k, idx: (k == thr) & (idx < cand))
            return jnp.where(n < need, cand, last)

        last = lax.fori_loop(0, nbits, bound, jnp.zeros((1, Q), jnp.int32))

        def demote(c, _):
            off = pl.multiple_of(c * KC, KC)
            k = key_ref[pl.ds(off, KC), :]
            drop = excess & (k == thr) & (off + sub_k > last)
            key_ref[pl.ds(off, KC), :] = jnp.where(drop, thr - 1, k)
            return 0

        lax.fori_loop(0, nch, demote, 0)

    keep_from = jnp.maximum(thr, KEY_NEG_INF + 1)

    qts = [dqt_ref[0, hh * HEAD_DIM:(hh + 1) * HEAD_DIM, :] for hh in range(N_HEADS)]

    def scores(c, mx):
        off = pl.multiple_of(c * KC, KC)
        bias = jnp.where(key_ref[pl.ds(off, KC), :] >= keep_from, 0.0, MASKED)
        dk = dk_ref[pl.ds(off, KC), :]
        new = []
        for hh in range(N_HEADS):
            s = jnp.dot(dk, qts[hh], preferred_element_type=F32) + bias
            s_ref[hh, pl.ds(off, KC), :] = s
            new.append(jnp.maximum(mx[hh], _fold_rows(s, jnp.maximum)))
        return tuple(new)

    mx = lax.fori_loop(0, nch, scores, tuple(jnp.full((SUB, Q), MASKED, F32) for _ in range(N_HEADS)))
    m_all = [jnp.max(m, axis=0, keepdims=True) for m in mx]
    _softmax_pv(nch, s_ref, acc_ref, lambda hh, off: dvt_ref[0, :, pl.ds(off, KC)], m_all, o_ref)


def _dsa_attention(kkt, dvt, iqt, dqt, iwt, topk):
    B, Dh, S = dvt.shape
    rows = lambda r: pl.BlockSpec((1, r, S), lambda b, i: (b, 0, 0))
    qcols = lambda r: pl.BlockSpec((1, r, KC), lambda b, i: (b, 0, i))
    return pl.pallas_call(
        functools.partial(_dsa_kernel, topk=topk),
        grid=(B, S // KC),
        in_specs=[rows(2 * Dh), rows(Dh), qcols(W_HEADS), qcols(W_HEADS), qcols(N_HEADS)],
        out_specs=pl.BlockSpec((1, KC, W_HEADS), lambda b, i: (b, i, 0)),
        out_shape=jax.ShapeDtypeStruct((B, S, W_HEADS), BF16),
        scratch_shapes=[pltpu.VMEM((S, KC), jnp.int32), pltpu.VMEM((S, KC), jnp.int16),
                        pltpu.VMEM((N_HEADS, S, KC), F32), pltpu.VMEM((N_HEADS, HEAD_DIM, KC), F32),
                        pltpu.VMEM((S, Dh), BF16), pltpu.VMEM((S, Dh), BF16)],
        compiler_params=_cparams(("parallel", "arbitrary")),
        name="dsa_attention",
    )(kkt, dvt, iqt, dqt, iwt)


def _first(mask, lane):
    return jnp.min(jnp.where(mask, lane, LANES), axis=-1, keepdims=True)


def _post_kernel(of_ref, od_ref, gate_ref, x_ref, mod_ref, wpf_ref, wpd_ref, wo_ref, g2_ref, wr_ref, br_ref,
                 x1_ref, h2_ref, route_ref):
    D = x_ref.shape[-1]
    pf = jnp.dot(of_ref[0], wpf_ref[...], preferred_element_type=F32)
    pd = jnp.dot(od_ref[0], wpd_ref[...], preferred_element_type=F32)
    merged = gate_ref[0, :, :D].astype(F32) * pf + gate_ref[0, :, D:].astype(F32) * pd
    y = jnp.dot(merged.astype(BF16), wo_ref[...], preferred_element_type=F32)
    x1 = x_ref[0] + mod_ref[0, 2:3, :] * y
    x1_ref[0] = x1

    ms = jnp.mean(x1 * x1, axis=-1, keepdims=True)
    h2 = x1 * lax.rsqrt(ms + EPS) * g2_ref[...]
    h2 = h2 * (1.0 + mod_ref[0, 4:5, :]) + mod_ref[0, 3:4, :]
    hb = h2.astype(BF16)
    h2_ref[0] = h2

    logits = jnp.dot(hb, wr_ref[...], preferred_element_type=F32) + br_ref[...]
    lane = lax.broadcasted_iota(jnp.int32, logits.shape, 1)
    is_grp = lane < N_GROUPS
    gl = jnp.where(is_grp, logits, -jnp.inf)
    gmax = jnp.max(gl, axis=-1, keepdims=True)
    g_idx = _first(gl == gmax, lane)
    g_w = 1.0 / jnp.sum(jnp.exp(gl - gmax), axis=-1, keepdims=True)

    e_lo = N_GROUPS + g_idx * EXPERTS_PER_GROUP
    in_grp = (lane >= e_lo) & (lane < e_lo + EXPERTS_PER_GROUP)
    el = jnp.where(in_grp, logits, -jnp.inf)
    emax = jnp.max(el, axis=-1, keepdims=True)
    ee = jnp.exp(el - emax)
    prob = ee / jnp.sum(ee, axis=-1, keepdims=True)
    prob = jnp.where(in_grp, prob, -1.0)
    p0 = jnp.max(prob, axis=-1, keepdims=True)
    l0 = _first(prob == p0, lane)
    rest = jnp.where(lane == l0, -1.0, prob)
    p1 = jnp.max(rest, axis=-1, keepdims=True)
    l1 = _first(rest == p1, lane)
    psum = p0 + p1
    w0 = g_w * (p0 / psum)
    w1 = g_w * (p1 / psum)
    e0 = (l0 - N_GROUPS).astype(F32)
    e1 = (l1 - N_GROUPS).astype(F32)
    route_ref[0] = jnp.where(lane == 0, e0, jnp.where(lane == 1, e1, jnp.where(lane == 2, w0,
                             jnp.where(lane == 3, w1, 0.0))))


def _post_attention(of, od, gates, x, mod3, wpf, wpd, wo, g2, wr, br, tm):
    B, S, D = x.shape
    tok = lambda w: pl.BlockSpec((1, tm, w), lambda b, i: (b, i, 0))
    const = lambda shape: pl.BlockSpec(shape, lambda b, i: (0,) * len(shape))
    return pl.pallas_call(
        _post_kernel,
        grid=(B, S // tm),
        in_specs=[tok(W_HEADS), tok(W_HEADS), tok(2 * D), tok(D),
                  pl.BlockSpec((1, 6, D), lambda b, i: (b, 0, 0)),
                  const(wpf.shape), const(wpd.shape), const(wo.shape),
                  const((1, D)), const((D, LANES)), const((1, LANES))],
        out_specs=[tok(D), tok(D), tok(LANES)],
        out_shape=[jax.ShapeDtypeStruct((B, S, D), F32),
                   jax.ShapeDtypeStruct((B, S, D), F32),
                   jax.ShapeDtypeStruct((B, S, LANES), F32)],
        compiler_params=_cparams(("parallel", "parallel")),
        name="merge_out_router",
    )(of, od, gates, x, mod3, wpf, wpd, wo, g2, wr, br)


def _rank_kernel(route_ref, tri_ref, rank_ref, count_ref, carry_ref):
    @pl.when(pl.program_id(0) == 0)
    def _():
        carry_ref[...] = jnp.zeros_like(carry_ref)

    r = route_ref[...]
    lane = lax.broadcasted_iota(jnp.int32, r.shape, 1).astype(F32)
    hot0 = lane == r[:, 0:1]
    hot1 = lane == r[:, 1:2]
    hits = jnp.where(hot0 | hot1, 1.0, 0.0)
    incl = jnp.dot(tri_ref[...], hits.astype(BF16), preferred_element_type=F32)
    before = incl - hits + carry_ref[...]
    r0 = jnp.sum(jnp.where(hot0, before, 0.0), axis=-1, keepdims=True)
    r1 = jnp.sum(jnp.where(hot1, before, 0.0), axis=-1, keepdims=True)
    rank_ref[...] = jnp.where(lane == 0.0, r0, jnp.where(lane == 1.0, r1, 0.0))
    carry_ref[...] = carry_ref[...] + jnp.sum(hits, axis=0, keepdims=True)
    count_ref[...] = carry_ref[...]


def _expert_ranks(route, tm):
    N = route.shape[0]
    tri = jnp.asarray(np.tril(np.ones((tm, tm), np.float32)), BF16)
    return pl.pallas_call(
        _rank_kernel,
        grid=(N // tm,),
        in_specs=[pl.BlockSpec((tm, LANES), lambda i: (i, 0)),
                  pl.BlockSpec((tm, tm), lambda i: (0, 0))],
        out_specs=[pl.BlockSpec((tm, LANES), lambda i: (i, 0)),
                   pl.BlockSpec((1, LANES), lambda i: (0, 0))],
        out_shape=[jax.ShapeDtypeStruct((N, LANES), F32), jax.ShapeDtypeStruct((1, LANES), F32)],
        scratch_shapes=[pltpu.VMEM((1, LANES), F32)],
        compiler_params=_cparams(("arbitrary",)),
        name="expert_ranks",
    )(route, tri)


def _dispatch_kernel(zstart_ref, zon_ref, nt_ref, pos_ref, h_ref, xs_ref, zbuf, sem, zsem, *, tm, tg):
    @pl.when(pl.program_id(0) == 0)
    def _():
        zbuf[...] = jnp.zeros_like(zbuf)

        def zero_tile(start):
            return pltpu.make_async_copy(zbuf, xs_ref.at[pl.ds(pl.multiple_of(start, tg), tg), :], zsem)

        n_tiles = xs_ref.shape[0] // tg
        for e in range(N_EXPERTS):
            pl.when(zon_ref[e] > 0)(lambda e=e: zero_tile(zstart_ref[e]).start())
        lax.fori_loop(nt_ref[0], n_tiles, lambda t, _: (zero_tile(t * tg).start(), 0)[1], 0)
        for e in range(N_EXPERTS):
            pl.when(zon_ref[e] > 0)(lambda e=e: zero_tile(zstart_ref[e]).wait())
        lax.fori_loop(nt_ref[0], n_tiles, lambda t, _: (zero_tile(t * tg).wait(), 0)[1], 0)

    def copy(r, slot):
        return pltpu.make_async_copy(h_ref.at[pl.ds(r, 1), :],
                                     xs_ref.at[pl.ds(pos_ref[0, slot, r], 1), :], sem)

    def issue(r, _):
        copy(r, 0).start()
        copy(r, 1).start()
        return 0

    lax.fori_loop(0, tm, issue, 0, unroll=DMA_UNROLL)
    for _ in range(2):
        pltpu.make_async_copy(h_ref, xs_ref.at[pl.ds(0, tm), :], sem).wait()


def _dispatch(h2, pos3, last_tile_start, has_rows, n_tiles_used, n_rows, tm, tg):
    N, D = h2.shape
    grid_spec = pltpu.PrefetchScalarGridSpec(
        num_scalar_prefetch=3,
        grid=(N // tm,),
        in_specs=[pl.BlockSpec((1, 2, tm), lambda i, zs, zo, nt: (i, 0, 0), memory_space=pltpu.SMEM),
                  pl.BlockSpec((tm, D), lambda i, zs, zo, nt: (i, 0))],
        out_specs=pl.BlockSpec(memory_space=pl.ANY),
        scratch_shapes=[pltpu.VMEM((tg, D), F32), pltpu.SemaphoreType.DMA(()), pltpu.SemaphoreType.DMA(())],
    )
    return pl.pallas_call(
        functools.partial(_dispatch_kernel, tm=tm, tg=tg),
        grid_spec=grid_spec,
        out_shape=jax.ShapeDtypeStruct((n_rows, D), F32),
        compiler_params=_cparams(("arbitrary",)),
        name="moe_dispatch",
    )(last_tile_start, has_rows, n_tiles_used, pos3, h2)


def _expert_kernel(te_ref, nt_ref, xs_ref, w1_ref, w3_ref, w2_ref, y_ref, w1b, w3b, w2b):
    g = pl.program_id(0)
    used = g < nt_ref[0]
    new_expert = (g == 0) | (te_ref[g] != te_ref[jnp.maximum(g - 1, 0)])

    @pl.when(used & new_expert)
    def _():
        w1b[...] = w1_ref[0].astype(BF16)
        w3b[...] = w3_ref[0].astype(BF16)
        w2b[...] = w2_ref[0].astype(BF16)

    @pl.when(used)
    def _():
        xb = xs_ref[...].astype(BF16)
        a = jnp.dot(xb, w1b[...], preferred_element_type=F32)
        b = jnp.dot(xb, w3b[...], preferred_element_type=F32)
        hmid = (a * jax.nn.sigmoid(a) * b).astype(BF16)
        y_ref[...] = jnp.dot(hmid, w2b[...], preferred_element_type=F32)

    @pl.when(jnp.logical_not(used))
    def _():
        y_ref[...] = jnp.zeros_like(y_ref)


def _experts(tile_expert, n_tiles_used, xs, w1, w3, w2, tg):
    P, D = xs.shape
    E, _, De = w1.shape
    row_tile = lambda g, te, nt: (jnp.minimum(g, nt[0] - 1), 0)
    grid_spec = pltpu.PrefetchScalarGridSpec(
        num_scalar_prefetch=2,
        grid=(P // tg,),
        in_specs=[pl.BlockSpec((tg, D), row_tile),
                  pl.BlockSpec((1, D, De), lambda g, te, nt: (te[g], 0, 0)),
                  pl.BlockSpec((1, D, De), lambda g, te, nt: (te[g], 0, 0)),
                  pl.BlockSpec((1, De, D), lambda g, te, nt: (te[g], 0, 0))],
        out_specs=pl.BlockSpec((tg, D), lambda g, te, nt: (g, 0)),
        scratch_shapes=[pltpu.VMEM((D, De), BF16), pltpu.VMEM((D, De), BF16), pltpu.VMEM((De, D), BF16)],
    )
    return pl.pallas_call(
        _expert_kernel,
        grid_spec=grid_spec,
        out_shape=jax.ShapeDtypeStruct((P, D), F32),
        compiler_params=_cparams(("arbitrary",)),
        name="moe_experts",
    )(tile_expert, n_tiles_used, xs, w1, w3, w2)


def _combine_kernel(pos_ref, y_ref, x1_ref, route_ref, gt_ref, o_ref, buf0, buf1, sem, *, tm):
    def copy(r, slot, buf):
        return pltpu.make_async_copy(y_ref.at[pl.ds(pos_ref[0, slot, r], 1), :],
                                     buf.at[pl.ds(r, 1), :], sem)

    def issue(r, _):
        copy(r, 0, buf0).start()
        copy(r, 1, buf1).start()
        return 0

    lax.fori_loop(0, tm, issue, 0, unroll=DMA_UNROLL)
    for buf in (buf0, buf1):
        pltpu.make_async_copy(y_ref.at[pl.ds(0, tm), :], buf, sem).wait()
    w0 = route_ref[:, 2:3]
    w1 = route_ref[:, 3:4]
    y = buf0[...] * w0 + buf1[...] * w1
    o_ref[...] = x1_ref[...] + gt_ref[0] * y


def _combine(pos3, y, x1, route, gt2, tm, S):
    N, D = x1.shape
    per_b = S // tm
    return pl.pallas_call(
        functools.partial(_combine_kernel, tm=tm),
        grid=(N // tm,),
        in_specs=[pl.BlockSpec((1, 2, tm), lambda i: (i, 0, 0), memory_space=pltpu.SMEM),
                  pl.BlockSpec(memory_space=pl.ANY),
                  pl.BlockSpec((tm, D), lambda i: (i, 0)),
                  pl.BlockSpec((tm, LANES), lambda i: (i, 0)),
                  pl.BlockSpec((1, 1, D), lambda i: (i // per_b, 0, 0))],
        out_specs=pl.BlockSpec((tm, D), lambda i: (i, 0)),
        out_shape=jax.ShapeDtypeStruct((N, D), F32),
        scratch_shapes=[pltpu.VMEM((tm, D), F32), pltpu.VMEM((tm, D), F32), pltpu.SemaphoreType.DMA(())],
        compiler_params=_cparams(("arbitrary",)),
        name="moe_combine",
    )(pos3, y, x1, route, gt2)


SC_CORES = 2
SC_SUBCORES = 16
SC_WINDOW = 64


def _sc_row_gather(table, idx):
    M, = idx.shape
    D = table.shape[1]
    workers = SC_CORES * SC_SUBCORES
    per_worker = M // workers
    assert per_worker * workers == M and per_worker % SC_WINDOW == 0
    mesh = plsc.VectorSubcoreMesh(core_axis_name="c", subcore_axis_name="s",
                                  num_cores=SC_CORES, num_subcores=SC_SUBCORES)

    @functools.partial(
        pl.kernel, mesh=mesh,
        out_type=jax.ShapeDtypeStruct((M, D), table.dtype),
        scratch_types=[pltpu.VMEM((SC_WINDOW,), jnp.int32), pltpu.VMEM((SC_WINDOW, D), table.dtype),
                       pltpu.SemaphoreType.DMA],
        name="sc_row_gather")
    def gather(table_hbm, idx_hbm, out_hbm, idx_v, rows_v, sem):
        wid = lax.axis_index("s") * SC_CORES + lax.axis_index("c")

        def window(j, _):
            base = pl.multiple_of(wid * per_worker + j * SC_WINDOW, SC_WINDOW)
            pltpu.sync_copy(idx_hbm.at[pl.ds(base, SC_WINDOW)], idx_v)
            pltpu.async_copy(table_hbm.at[idx_v], rows_v, sem).wait()
            pltpu.sync_copy(rows_v, out_hbm.at[pl.ds(base, SC_WINDOW)])
            return 0

        lax.fori_loop(0, per_worker // SC_WINDOW, window, 0)

    return gather(table, idx)


def _combine_rows_kernel(yg_ref, x1_ref, route_ref, gt_ref, o_ref):
    D = x1_ref.shape[-1]
    y = yg_ref[:, :D] * route_ref[:, 2:3] + yg_ref[:, D:] * route_ref[:, 3:4]
    o_ref[...] = x1_ref[...] + gt_ref[0] * y


def _combine_rows(yg, x1, route, gt2, tm, S):
    N, D = x1.shape
    per_b = S // tm
    return pl.pallas_call(
        _combine_rows_kernel,
        grid=(N // tm,),
        in_specs=[pl.BlockSpec((tm, 2 * D), lambda i: (i, 0)),
                  pl.BlockSpec((tm, D), lambda i: (i, 0)),
                  pl.BlockSpec((tm, LANES), lambda i: (i, 0)),
                  pl.BlockSpec((1, 1, D), lambda i: (i // per_b, 0, 0))],
        out_specs=pl.BlockSpec((tm, D), lambda i: (i, 0)),
        out_shape=jax.ShapeDtypeStruct((N, D), F32),
        compiler_params=_cparams(("parallel",)),
        name="moe_combine",
    )(yg, x1, route, gt2)


def _rope_tables(positions):
    inv = ROPE_THETA ** (-jnp.arange(HALF, dtype=F32) / HALF)
    ang = positions.astype(F32)[:, None, :] * inv[None, :, None]
    return jnp.cos(ang), jnp.sin(ang)


def _block_diag_mean(width):
    blk = np.kron(np.eye(width // HEAD_DIM, dtype=np.float32), np.full((HEAD_DIM, HEAD_DIM), 1.0 / HEAD_DIM, np.float32))
    return jnp.asarray(blk, BF16)


def _layer(x, c_mod, positions, norm1_g, norm2_g, w_in, b_fgt, b_gate, qn_fox, kn_fox, qn_dsa, kn_dsa,
           w_proj_fox, w_proj_dsa, w_out, r_w_grp, r_b_grp, r_w_exp, r_b_exp, w1, w3, w2):
    B, S, D = x.shape
    N = B * S
    topk = min(TOPK_MAX, S // 4)
    tm = min(512, S)
    scale = HEAD_DIM ** -0.5
    mod3 = c_mod.reshape(B, 6, D)

    o = np.cumsum([0, 512, 512, 512, 8, 512, 64, 64, 512, 64, 8, D, D])
    seg = lambda k: w_in[:, o[k]:o[k + 1]]
    zpad = jnp.zeros((D, LANES - HEAD_DIM - 2 * N_HEADS), F32)
    w_tok = jnp.concatenate([seg(1), seg(10), seg(11)], axis=1).astype(BF16)
    w_t = jnp.concatenate([seg(0), seg(4), seg(7), seg(2),
                           seg(5), seg(8),
                           seg(6), seg(3), seg(9), zpad], axis=1).T.astype(BF16)
    gcol = jnp.stack([qn_fox * (scale * LOG2E), qn_dsa * (scale * LOG2E), kn_dsa]).reshape(3, HEAD_DIM, 1)
    cos_t, sin_t = _rope_tables(positions)

    k_heads, gates, fqt, dqt, iqt, fvt, kkt, dvt, logf_t, iwt = _in_projection(
        x, mod3, norm1_g.reshape(1, D), w_tok, w_t, _block_diag_mean(W_HEADS), cos_t, sin_t,
        jnp.tile(kn_fox, N_HEADS).reshape(1, W_HEADS), gcol, b_fgt.reshape(N_HEADS, 1),
        b_gate.reshape(1, 2 * D), tm)

    f_tok = jnp.transpose(_seq_cumsum(logf_t), (0, 2, 1)) * LOG2E
    of = _fox_attention(k_heads, f_tok, fqt, fvt)
    od = _dsa_attention(kkt, dvt, iqt, dqt, iwt, topk)

    wr = jnp.concatenate([r_w_grp, r_w_exp, jnp.zeros((D, LANES - N_GROUPS - N_EXPERTS), F32)], axis=1).astype(BF16)
    br = jnp.concatenate([r_b_grp, r_b_exp, jnp.zeros((LANES - N_GROUPS - N_EXPERTS,), F32)]).reshape(1, LANES)
    x1, h2, route = _post_attention(of, od, gates, x, mod3, w_proj_fox.astype(BF16), w_proj_dsa.astype(BF16),
                                    w_out.astype(BF16), norm2_g.reshape(1, D), wr, br, tm)
    x1, h2, route = x1.reshape(N, D), h2.reshape(N, D), route.reshape(N, LANES)

    tg = 512 if N * 2 >= 512 * N_EXPERTS else 128
    ranks, counts = _expert_ranks(route, tm)
    counts = counts[0, :N_EXPERTS].astype(jnp.int32)
    padded = ((counts + tg - 1) // tg) * tg
    ends = jnp.cumsum(padded)
    starts = ends - padded
    e01 = route[:, :2].astype(jnp.int32)
    start_of = jnp.sum(jnp.where(e01[..., None] == jnp.arange(N_EXPERTS, dtype=jnp.int32), starts, 0), axis=-1)
    pos = start_of + ranks[:, :2].astype(jnp.int32)
    n_rows = N * 2 + N_EXPERTS * tg
    n_tiles = n_rows // tg
    tile_start = jnp.arange(n_tiles, dtype=jnp.int32) * tg
    tile_expert = jnp.minimum(jnp.sum((ends[None, :] <= tile_start[:, None]).astype(jnp.int32), axis=1),
                              N_EXPERTS - 1)
    n_used = (ends[-1] // tg).astype(jnp.int32).reshape(1)

    td = min(256, S)
    pos3 = jnp.transpose(pos.reshape(N // td, td, 2), (0, 2, 1))
    xs = _dispatch(h2, pos3, jnp.maximum(ends - tg, 0).astype(jnp.int32), (padded > 0).astype(jnp.int32),
                   n_used, n_rows, td, tg)
    y = _experts(tile_expert, n_used, xs, w1, w3, w2, tg)
    yg = _sc_row_gather(y, pos.reshape(N * 2)).reshape(N, 2 * D)
    out = _combine_rows(yg, x1, route, mod3[:, 5:6, :], min(512, S), S)
    return out.reshape(B, S, D)


def kernel(x, c, positions, ada_w, ada_b, norm1_g, norm2_g, w_in, b_fgt, b_gate, qn_fox, kn_fox, qn_dsa, kn_dsa, w_proj_fox, w_proj_dsa, w_out, router_w_grp, router_b_grp, router_w_exp, router_b_exp, exp_w1, exp_w3, exp_w2):
    for l in range(ada_w.shape[0]):
        c_mod = _modulation(c, ada_w[l], ada_b[l])
        x = _layer(x, c_mod, positions, norm1_g[l], norm2_g[l], w_in[l], b_fgt[l], b_gate[l],
                   qn_fox[l], kn_fox[l], qn_dsa[l], kn_dsa[l], w_proj_fox[l], w_proj_dsa[l], w_out[l],
                   router_w_grp[l], router_b_grp[l], router_w_exp[l], router_b_exp[l],
                   exp_w1[l], exp_w3[l], exp_w2[l])
    return x
```

```python
import functools

import jax
import jax.numpy as jnp
import numpy as np
from jax import lax
from jax.experimental import pallas as pl
from jax.experimental.pallas import tpu as pltpu
from jax.experimental.pallas import tpu_sc as plsc

F32 = jnp.float32
BF16 = jnp.bfloat16

CHUNK = 64
CHUNK_SHIFT = 6
DMA_UNROLL = 8
HEAD_DIM = 64
N_HEADS = 8
W_HEADS = N_HEADS * HEAD_DIM
TOPK_MAX = 256
ROPE_THETA = 10000.0
N_GROUPS = 4
EXPERTS_PER_GROUP = 8
N_EXPERTS = N_GROUPS * EXPERTS_PER_GROUP
EPS = 1e-6
LOG2E = 1.4426950408889634
MASKED = -1e30

LANES = 128
VMEM_LIMIT = 56 * 1024 * 1024


def _cparams(sem):
    return pltpu.CompilerParams(dimension_semantics=sem, vmem_limit_bytes=VMEM_LIMIT)


def _mod_kernel(c_ref, w_ref, b_ref, o_ref):
    c = c_ref[...]
    ca = (c * jax.nn.sigmoid(c)).astype(BF16)
    o_ref[...] = jnp.dot(ca, w_ref[...].astype(BF16), preferred_element_type=F32) + b_ref[...]


def _modulation(c, ada_w, ada_b):
    B, D = c.shape
    n = ada_w.shape[1] // D
    return pl.pallas_call(
        _mod_kernel,
        grid=(n,),
        in_specs=[pl.BlockSpec((B, D), lambda j: (0, 0)),
                  pl.BlockSpec((D, D), lambda j: (0, j)),
                  pl.BlockSpec((1, D), lambda j: (0, j))],
        out_specs=pl.BlockSpec((B, D), lambda j: (0, j)),
        out_shape=jax.ShapeDtypeStruct((B, n * D), F32),
        compiler_params=_cparams(("arbitrary",)),
        name="adaln_mod",
    )(c, ada_w, ada_b.reshape(1, -1))


R_FQ, R_DQ, R_IQ, R_FV = 0, 512, 1024, 1536
R_KK = 2048
R_S2 = 2176
R_END = 2304
HALF = HEAD_DIM // 2


def _inproj_kernel(x_ref, mod_ref, g1_ref, wtok_ref, wt_ref, bd512_ref, cos_ref, sin_ref,
                   gk_ref, gcol_ref, bf_ref, bg_ref,
                   fk_ref, gate_ref, fqt_ref, dqt_ref, iqt_ref, fvt_ref, kkt_ref, dvt_ref, lf_ref, iwt_ref):
    x = x_ref[0]
    ms = jnp.mean(x * x, axis=-1, keepdims=True)
    h = x * lax.rsqrt(ms + EPS) * g1_ref[...]
    h = h * (1.0 + mod_ref[0, 1:2, :]) + mod_ref[0, 0:1, :]
    hb = h.astype(BF16)
    D = x.shape[-1]
    cos, sin = cos_ref[0], sin_ref[0]

    def proj_t(lo, hi):
        return lax.dot_general(wt_ref[lo:hi, :], hb, (((1,), (1,)), ((), ())), preferred_element_type=F32)

    def norm_t(yh, gain):
        msq = jnp.mean(yh * yh, axis=0, keepdims=True)
        return yh * lax.rsqrt(msq + EPS) * gain

    def rope_store(ref, lo, yh):
        x1, x2 = yh[:HALF], yh[HALF:]
        ref[0, lo:lo + HALF, :] = (x1 * cos - x2 * sin).astype(ref.dtype)
        ref[0, lo + HALF:lo + HEAD_DIM, :] = (x2 * cos + x1 * sin).astype(ref.dtype)

    fq = proj_t(R_FQ, R_FQ + W_HEADS)
    dq = proj_t(R_DQ, R_DQ + W_HEADS)
    iq = proj_t(R_IQ, R_IQ + W_HEADS)
    for hh in range(N_HEADS):
        lo = hh * HEAD_DIM
        fqt_ref[0, lo:lo + HEAD_DIM, :] = norm_t(fq[lo:lo + HEAD_DIM], gcol_ref[0]).astype(BF16)
        rope_store(dqt_ref, lo, norm_t(dq[lo:lo + HEAD_DIM], gcol_ref[1]))
        rope_store(iqt_ref, lo, iq[lo:lo + HEAD_DIM])
    fvt_ref[0] = proj_t(R_FV, R_FV + W_HEADS).astype(BF16)

    kk = proj_t(R_KK, R_KK + 2 * HEAD_DIM)
    rope_store(kkt_ref, 0, norm_t(kk[:HEAD_DIM], gcol_ref[2]))
    rope_store(kkt_ref, HEAD_DIM, kk[HEAD_DIM:])

    s2 = proj_t(R_S2, R_S2 + LANES)
    dvt_ref[0] = s2[:HEAD_DIM].astype(BF16)
    z = s2[HEAD_DIM:HEAD_DIM + N_HEADS] + bf_ref[...]
    lf_ref[0] = jnp.minimum(z, 0.0) - jnp.log(1.0 + jnp.exp(-jnp.abs(z)))
    iwt_ref[0] = s2[HEAD_DIM + N_HEADS:HEAD_DIM + 2 * N_HEADS]

    fk = jnp.dot(hb, wtok_ref[:, :W_HEADS], preferred_element_type=F32)
    msq = jnp.dot((fk * fk).astype(BF16), bd512_ref[...], preferred_element_type=F32)
    fk = (fk * lax.rsqrt(msq + EPS) * gk_ref[...]).astype(BF16)
    for hh in range(N_HEADS):
        fk_ref[0, hh] = fk[:, hh * HEAD_DIM:(hh + 1) * HEAD_DIM]
    g = jnp.dot(hb, wtok_ref[:, W_HEADS:], preferred_element_type=F32)
    gate_ref[0] = jax.nn.sigmoid(g + bg_ref[...]).astype(BF16)


def _in_projection(x, mod3, norm1_g, w_tok, w_t, bd512, cos_t, sin_t, gk, gcol, bf, bg, tm):
    B, S, D = x.shape
    tok = lambda w: pl.BlockSpec((1, tm, w), lambda b, i: (b, i, 0))
    feat = lambda r: pl.BlockSpec((1, r, tm), lambda b, i: (b, 0, i))
    const = lambda shape: pl.BlockSpec(shape, lambda b, i: (0,) * len(shape))
    out_shapes = [jax.ShapeDtypeStruct((B, N_HEADS, S, HEAD_DIM), BF16),
                  jax.ShapeDtypeStruct((B, S, 2 * D), BF16)] + \
                 [jax.ShapeDtypeStruct((B, W_HEADS, S), BF16)] * 4 + \
                 [jax.ShapeDtypeStruct((B, 2 * HEAD_DIM, S), BF16),
                  jax.ShapeDtypeStruct((B, HEAD_DIM, S), BF16),
                  jax.ShapeDtypeStruct((B, N_HEADS, S), F32),
                  jax.ShapeDtypeStruct((B, N_HEADS, S), F32)]
    return pl.pallas_call(
        _inproj_kernel,
        grid=(B, S // tm),
        in_specs=[tok(D),
                  pl.BlockSpec((1, 6, D), lambda b, i: (b, 0, 0)),
                  const((1, D)),
                  const(w_tok.shape), const(w_t.shape), const((W_HEADS, W_HEADS)),
                  feat(HALF), feat(HALF),
                  const((1, W_HEADS)), const((3, HEAD_DIM, 1)), const((N_HEADS, 1)), const((1, 2 * D))],
        out_specs=[pl.BlockSpec((1, N_HEADS, tm, HEAD_DIM), lambda b, i: (b, 0, i, 0)), tok(2 * D),
                   feat(W_HEADS), feat(W_HEADS), feat(W_HEADS), feat(W_HEADS),
                   feat(2 * HEAD_DIM), feat(HEAD_DIM), feat(N_HEADS), feat(N_HEADS)],
        out_shape=out_shapes,
        compiler_params=_cparams(("parallel", "parallel")),
        name="in_projection",
    )(x, mod3, norm1_g, w_tok, w_t, bd512, cos_t, sin_t, gk, gcol, bf, bg)


def _cumsum_kernel(x_ref, o_ref):
    x = x_ref[0]
    n = x.shape[-1]
    pos = lax.broadcasted_iota(jnp.int32, x.shape, 1)
    shift = 1
    while shift < n:
        x = x + jnp.where(pos >= shift, pltpu.roll(x, shift, 1), 0.0)
        shift *= 2
    o_ref[0] = x


def _seq_cumsum(logf_t):
    B, H, S = logf_t.shape
    return pl.pallas_call(
        _cumsum_kernel,
        grid=(B,),
        in_specs=[pl.BlockSpec((1, H, S), lambda b: (b, 0, 0))],
        out_specs=pl.BlockSpec((1, H, S), lambda b: (b, 0, 0)),
        out_shape=jax.ShapeDtypeStruct((B, H, S), F32),
        compiler_params=_cparams(("parallel",)),
        name="forget_cumsum",
    )(logf_t)


KC = 256
SUB = 8


def _fold_rows(a, op, ways=1):
    n = a.shape[0] // SUB
    a = a.reshape(n, SUB, a.shape[1])
    chains = [a[w] for w in range(ways)]
    for j in range(ways, n):
        chains[j % ways] = op(chains[j % ways], a[j])
    while len(chains) > 1:
        chains = [op(chains[2 * j], chains[2 * j + 1]) for j in range(len(chains) // 2)]
    return chains[0]


def _softmax_pv(nch, s_ref, acc_ref, vt_at, m_all, o_ref):
    Q = o_ref.shape[1]
    acc_ref[...] = jnp.zeros_like(acc_ref)

    def body(c, lsum):
        off = pl.multiple_of(c * KC, KC)
        new = []
        for hh in range(N_HEADS):
            p = jnp.exp2(s_ref[hh, pl.ds(off, KC), :] - m_all[hh])
            new.append(lsum[hh] + _fold_rows(p, jnp.add))
            acc_ref[hh] += jnp.dot(vt_at(hh, off), p.astype(BF16), preferred_element_type=F32)
        return tuple(new)

    lsum = lax.fori_loop(0, nch, body, tuple(jnp.zeros((SUB, Q), F32) for _ in range(N_HEADS)))
    for hh in range(N_HEADS):
        acc_ref[hh] = acc_ref[hh] / jnp.sum(lsum[hh], axis=0, keepdims=True)
    out_t = acc_ref[...].reshape(N_HEADS * HEAD_DIM, Q)
    o_ref[0] = out_t.T.astype(BF16)


def _fox_kernel(k_ref, f_ref, qt_ref, vt_ref, o_ref, s_ref, acc_ref):
    i = pl.program_id(1)
    Q = o_ref.shape[1]
    qts = [qt_ref[0, hh * HEAD_DIM:(hh + 1) * HEAD_DIM, :] for hh in range(N_HEADS)]

    def scores(c, mx, bias):
        off = pl.multiple_of(c * KC, KC)
        new = []
        for hh in range(N_HEADS):
            s = jnp.dot(k_ref[0, hh, pl.ds(off, KC), :], qts[hh], preferred_element_type=F32)
            s = s - f_ref[0, pl.ds(off, KC), hh:hh + 1]
            if bias is not None:
                s = s + bias
            s_ref[hh, pl.ds(off, KC), :] = s
            new.append(jnp.maximum(mx[hh], _fold_rows(s, jnp.maximum)))
        return tuple(new)

    mx = tuple(jnp.full((SUB, Q), MASKED, F32) for _ in range(N_HEADS))
    mx = lax.fori_loop(0, i, lambda c, m: scores(c, m, None), mx)
    kk = lax.broadcasted_iota(jnp.int32, (KC, Q), 0)
    qq = lax.broadcasted_iota(jnp.int32, (KC, Q), 1)
    mx = scores(i, mx, jnp.where(kk <= qq, 0.0, MASKED))
    m_all = [jnp.max(m, axis=0, keepdims=True) for m in mx]
    _softmax_pv(i + 1, s_ref, acc_ref, lambda hh, off: vt_ref[0, hh * HEAD_DIM:(hh + 1) * HEAD_DIM, pl.ds(off, KC)],
                m_all, o_ref)


def _fox_attention(k_heads, f_tok, qt, vt):
    B, H, S, Dh = k_heads.shape
    return pl.pallas_call(
        _fox_kernel,
        grid=(B, S // KC),
        in_specs=[pl.BlockSpec((1, H, S, Dh), lambda b, i: (b, 0, 0, 0)),
                  pl.BlockSpec((1, S, H), lambda b, i: (b, 0, 0)),
                  pl.BlockSpec((1, W_HEADS, KC), lambda b, i: (b, 0, i)),
                  pl.BlockSpec((1, W_HEADS, S), lambda b, i: (b, 0, 0))],
        out_specs=pl.BlockSpec((1, KC, W_HEADS), lambda b, i: (b, i, 0)),
        out_shape=jax.ShapeDtypeStruct((B, S, W_HEADS), BF16),
        scratch_shapes=[pltpu.VMEM((H, S, KC), F32), pltpu.VMEM((H, HEAD_DIM, KC), F32)],
        compiler_params=_cparams(("parallel", "arbitrary")),
        name="fox_attention",
    )(k_heads, f_tok, qt, vt)


INT_MIN = -(2 ** 31)
KEY_NEG_INF = INT_MIN + 0x7FFFFF
HI16 = -(2 ** 16)
PACK = 16


def _dsa_kernel(kkt_ref, dvt_ref, iqt_ref, dqt_ref, iwt_ref, o_ref, key_ref, hi_ref, s_ref, acc_ref, dk_ref, ik_ref,
                *, topk):
    i = pl.program_id(1)
    Q = o_ref.shape[1]
    nch = i + 1

    @pl.when(i == 0)
    def _():
        def to_rows(c, _):
            off = pl.multiple_of(c * KC, KC)
            rows = kkt_ref[0, :, pl.ds(off, KC)].astype(F32).T
            dk_ref[pl.ds(off, KC), :] = rows[:, :HEAD_DIM].astype(BF16)
            ik_ref[pl.ds(off, KC), :] = rows[:, HEAD_DIM:].astype(BF16)
            return 0
        lax.fori_loop(0, kkt_ref.shape[-1] // KC, to_rows, 0)

    sub_k = lax.broadcasted_iota(jnp.int32, (KC, Q), 0)
    sub_r = lax.broadcasted_iota(jnp.int32, (CHUNK, Q), 0)
    q_chunk = (i * Q + lax.broadcasted_iota(jnp.int32, (CHUNK, Q), 1)) >> CHUNK_SHIFT
    iqts = [iqt_ref[0, hh * HEAD_DIM:(hh + 1) * HEAD_DIM, :] for hh in range(N_HEADS)]
    iws = [iwt_ref[0, hh:hh + 1, :] for hh in range(N_HEADS)]

    def score_chunk(c, _):
        for r in range(KC // CHUNK):
            off = pl.multiple_of(c * KC + r * CHUNK, CHUNK)
            ik = ik_ref[pl.ds(off, CHUNK), :]
            sc = jnp.zeros((CHUNK, Q), F32)
            for hh in range(N_HEADS):
                d = jnp.dot(ik, iqts[hh], preferred_element_type=F32)
                sc = sc + iws[hh] * jnp.maximum(d, 0.0)
            sc = sc + 0.0
            allowed = ((off + sub_r) >> CHUNK_SHIFT) <= q_chunk
            bits = pltpu.bitcast(jnp.where(allowed, sc, -jnp.inf), jnp.int32)
            key = bits ^ ((bits >> 31) & 0x7FFFFFFF)
            key_ref[pl.ds(off, CHUNK), :] = key
            hi_ref[pl.ds(off, CHUNK), :] = (key >> 16).astype(jnp.int16)
        return 0

    lax.fori_loop(0, nch, score_chunk, 0)

    def sweep(n, body, init):
        if isinstance(n, int):
            acc = init
            for c in range(n):
                acc = body(c * KC, acc)
            return acc
        return lax.fori_loop(0, n, lambda c, acc: body(pl.multiple_of(c * KC, KC), acc), init)

    def count(pred, n=nch):
        def body(off, acc):
            hit = pred(key_ref[pl.ds(off, KC), :], off + sub_k)
            return acc + _fold_rows(jnp.where(hit, 1.0, 0.0), jnp.add, ways=4)
        return jnp.sum(sweep(n, body, jnp.zeros((SUB, Q), F32)), axis=0, keepdims=True)

    one, zero = jnp.ones((), BF16), jnp.zeros((), BF16)

    def count_hi(cand, n):
        c16 = jnp.broadcast_to(cand >> 16, (PACK, Q)).astype(jnp.int16)

        def body(off, acc):
            kb = hi_ref[pl.ds(off, KC), :].reshape(KC // PACK, PACK, Q)
            hit = jnp.where(kb >= c16[None], one, zero)
            parts = [hit[w] for w in range(4)]
            for j in range(4, KC // PACK):
                parts[j % 4] = parts[j % 4] + hit[j]
            return acc + ((parts[0] + parts[1]) + (parts[2] + parts[3]))
        acc = sweep(n, body, jnp.zeros((PACK, Q), BF16))
        return jnp.sum(acc.astype(F32), axis=0, keepdims=True)

    assert key_ref.shape[0] // PACK <= 256
    kf = jnp.float32(topk)

    def descent(n):
        n_nonneg = count_hi(jnp.zeros((1, Q), jnp.int32), n)
        top_half = n_nonneg >= kf
        thr = jnp.where(top_half, 0, INT_MIN).astype(jnp.int32)
        n_ge = jnp.where(top_half, n_nonneg, jnp.float32(n * KC))

        def descend(counter, top_bit):
            def step(j, carry):
                thr, n_ge = carry
                cand = thr + (jnp.int32(1) << (top_bit - j))
                cnt = counter(cand)
                take = cnt >= kf
                return jnp.where(take, cand, thr), jnp.where(take, cnt, n_ge)
            return step

        carry = lax.fori_loop(0, 15, descend(lambda c: count_hi(c, n), 30), (thr, n_ge))
        return lax.fori_loop(0, 16, descend(lambda c: count(lambda k, _: k >= c, n), 15), carry)

    thr, n_ge = lax.switch(i, [functools.partial(descent, n) for n in range(1, key_ref.shape[0] // KC + 1)])

    excess = (n_ge > kf) & (thr > KEY_NEG_INF)

    @pl.when(jnp.max(jnp.where(excess, 1.0, 0.0)) > 0.0)
    def _():
        need = kf - count(lambda k, _: k > thr)
        nbits = int(np.ceil(np.log2(key_ref.shape[0]))) + 1

        def bound(j, last):
            cand = last + (jnp.int32(1) << (nbits - 1 - j))
            n = count(lambda k, idx: (k == thr) & (idx < cand))
            return jnp.where(n < need, cand, last)

        last = lax.fori_loop(0, nbits, bound, jnp.zeros((1, Q), jnp.int32))

        def demote(c, _):
            off = pl.multiple_of(c * KC, KC)
            k = key_ref[pl.ds(off, KC), :]
            drop = excess & (k == thr) & (off + sub_k > last)
            key_ref[pl.ds(off, KC), :] = jnp.where(drop, thr - 1, k)
            return 0

        lax.fori_loop(0, nch, demote, 0)

    keep_from = jnp.maximum(thr, KEY_NEG_INF + 1)

    qts = [dqt_ref[0, hh * HEAD_DIM:(hh + 1) * HEAD_DIM, :] for hh in range(N_HEADS)]

    def scores(c, mx):
        off = pl.multiple_of(c * KC, KC)
        bias = jnp.where(key_ref[pl.ds(off, KC), :] >= keep_from, 0.0, MASKED)
        dk = dk_ref[pl.ds(off, KC), :]
        new = []
        for hh in range(N_HEADS):
            s = jnp.dot(dk, qts[hh], preferred_element_type=F32) + bias
            s_ref[hh, pl.ds(off, KC), :] = s
            new.append(jnp.maximum(mx[hh], _fold_rows(s, jnp.maximum)))
        return tuple(new)

    mx = lax.fori_loop(0, nch, scores, tuple(jnp.full((SUB, Q), MASKED, F32) for _ in range(N_HEADS)))
    m_all = [jnp.max(m, axis=0, keepdims=True) for m in mx]
    _softmax_pv(nch, s_ref, acc_ref, lambda hh, off: dvt_ref[0, :, pl.ds(off, KC)], m_all, o_ref)


def _dsa_attention(kkt, dvt, iqt, dqt, iwt, topk):
    B, Dh, S = dvt.shape
    rows = lambda r: pl.BlockSpec((1, r, S), lambda b, i: (b, 0, 0))
    qcols = lambda r: pl.BlockSpec((1, r, KC), lambda b, i: (b, 0, i))
    return pl.pallas_call(
        functools.partial(_dsa_kernel, topk=topk),
        grid=(B, S // KC),
        in_specs=[rows(2 * Dh), rows(Dh), qcols(W_HEADS), qcols(W_HEADS), qcols(N_HEADS)],
        out_specs=pl.BlockSpec((1, KC, W_HEADS), lambda b, i: (b, i, 0)),
        out_shape=jax.ShapeDtypeStruct((B, S, W_HEADS), BF16),
        scratch_shapes=[pltpu.VMEM((S, KC), jnp.int32), pltpu.VMEM((S, KC), jnp.int16),
                        pltpu.VMEM((N_HEADS, S, KC), F32), pltpu.VMEM((N_HEADS, HEAD_DIM, KC), F32),
                        pltpu.VMEM((S, Dh), BF16), pltpu.VMEM((S, Dh), BF16)],
        compiler_params=_cparams(("parallel", "arbitrary")),
        name="dsa_attention",
    )(kkt, dvt, iqt, dqt, iwt)


PACKED = jnp.int32


def _pack_halves(a):
    half = a.shape[-1] // 2
    rounded = a.astype(BF16).astype(F32)
    lo = pltpu.bitcast(rounded[:, :half], jnp.int32)
    hi = pltpu.bitcast(rounded[:, half:], jnp.int32)
    return hi | ((lo >> 16) & 0xFFFF)


def _unpack_halves(p):
    lo = pltpu.bitcast(p << 16, F32)
    hi = pltpu.bitcast(p & HI16, F32)
    return jnp.concatenate([lo, hi], axis=-1)


def _first(mask, lane):
    return jnp.min(jnp.where(mask, lane, LANES), axis=-1, keepdims=True)


def _post_kernel(of_ref, od_ref, gate_ref, x_ref, mod_ref, wpf_ref, wpd_ref, wo_ref, g2_ref, wr_ref, br_ref,
                 x1_ref, h2_ref, route_ref):
    D = x_ref.shape[-1]
    pf = jnp.dot(of_ref[0], wpf_ref[...], preferred_element_type=F32)
    pd = jnp.dot(od_ref[0], wpd_ref[...], preferred_element_type=F32)
    merged = gate_ref[0, :, :D].astype(F32) * pf + gate_ref[0, :, D:].astype(F32) * pd
    y = jnp.dot(merged.astype(BF16), wo_ref[...], preferred_element_type=F32)
    x1 = x_ref[0] + mod_ref[0, 2:3, :] * y
    x1_ref[0] = x1

    ms = jnp.mean(x1 * x1, axis=-1, keepdims=True)
    h2 = x1 * lax.rsqrt(ms + EPS) * g2_ref[...]
    h2 = h2 * (1.0 + mod_ref[0, 4:5, :]) + mod_ref[0, 3:4, :]
    hb = h2.astype(BF16)
    h2_ref[0] = _pack_halves(h2)

    logits = jnp.dot(hb, wr_ref[...], preferred_element_type=F32) + br_ref[...]
    lane = lax.broadcasted_iota(jnp.int32, logits.shape, 1)
    is_grp = lane < N_GROUPS
    gl = jnp.where(is_grp, logits, -jnp.inf)
    gmax = jnp.max(gl, axis=-1, keepdims=True)
    g_idx = _first(gl == gmax, lane)
    g_w = 1.0 / jnp.sum(jnp.exp(gl - gmax), axis=-1, keepdims=True)

    e_lo = N_GROUPS + g_idx * EXPERTS_PER_GROUP
    in_grp = (lane >= e_lo) & (lane < e_lo + EXPERTS_PER_GROUP)
    el = jnp.where(in_grp, logits, -jnp.inf)
    emax = jnp.max(el, axis=-1, keepdims=True)
    ee = jnp.exp(el - emax)
    prob = ee / jnp.sum(ee, axis=-1, keepdims=True)
    prob = jnp.where(in_grp, prob, -1.0)
    p0 = jnp.max(prob, axis=-1, keepdims=True)
    l0 = _first(prob == p0, lane)
    rest = jnp.where(lane == l0, -1.0, prob)
    p1 = jnp.max(rest, axis=-1, keepdims=True)
    l1 = _first(rest == p1, lane)
    psum = p0 + p1
    w0 = g_w * (p0 / psum)
    w1 = g_w * (p1 / psum)
    e0 = (l0 - N_GROUPS).astype(F32)
    e1 = (l1 - N_GROUPS).astype(F32)
    route_ref[0] = jnp.where(lane == 0, e0, jnp.where(lane == 1, e1, jnp.where(lane == 2, w0,
                             jnp.where(lane == 3, w1, 0.0))))


def _post_attention(of, od, gates, x, mod3, wpf, wpd, wo, g2, wr, br, tm):
    B, S, D = x.shape
    tok = lambda w: pl.BlockSpec((1, tm, w), lambda b, i: (b, i, 0))
    const = lambda shape: pl.BlockSpec(shape, lambda b, i: (0,) * len(shape))
    return pl.pallas_call(
        _post_kernel,
        grid=(B, S // tm),
        in_specs=[tok(W_HEADS), tok(W_HEADS), tok(2 * D), tok(D),
                  pl.BlockSpec((1, 6, D), lambda b, i: (b, 0, 0)),
                  const(wpf.shape), const(wpd.shape), const(wo.shape),
                  const((1, D)), const((D, LANES)), const((1, LANES))],
        out_specs=[tok(D), tok(D // 2), tok(LANES)],
        out_shape=[jax.ShapeDtypeStruct((B, S, D), F32),
                   jax.ShapeDtypeStruct((B, S, D // 2), PACKED),
                   jax.ShapeDtypeStruct((B, S, LANES), F32)],
        compiler_params=_cparams(("parallel", "parallel")),
        name="merge_out_router",
    )(of, od, gates, x, mod3, wpf, wpd, wo, g2, wr, br)


def _rank_kernel(route_ref, tri_ref, rank_ref, count_ref, carry_ref):
    @pl.when(pl.program_id(0) == 0)
    def _():
        carry_ref[...] = jnp.zeros_like(carry_ref)

    r = route_ref[...]
    lane = lax.broadcasted_iota(jnp.int32, r.shape, 1).astype(F32)
    hot0 = lane == r[:, 0:1]
    hot1 = lane == r[:, 1:2]
    hits = jnp.where(hot0 | hot1, 1.0, 0.0)
    incl = jnp.dot(tri_ref[...], hits.astype(BF16), preferred_element_type=F32)
    before = incl - hits + carry_ref[...]
    r0 = jnp.sum(jnp.where(hot0, before, 0.0), axis=-1, keepdims=True)
    r1 = jnp.sum(jnp.where(hot1, before, 0.0), axis=-1, keepdims=True)
    rank_ref[...] = jnp.where(lane == 0.0, r0, jnp.where(lane == 1.0, r1, 0.0))
    carry_ref[...] = carry_ref[...] + jnp.sum(hits, axis=0, keepdims=True)
    count_ref[...] = carry_ref[...]


def _expert_ranks(route, tm):
    N = route.shape[0]
    tri = jnp.asarray(np.tril(np.ones((tm, tm), np.float32)), BF16)
    return pl.pallas_call(
        _rank_kernel,
        grid=(N // tm,),
        in_specs=[pl.BlockSpec((tm, LANES), lambda i: (i, 0)),
                  pl.BlockSpec((tm, tm), lambda i: (0, 0))],
        out_specs=[pl.BlockSpec((tm, LANES), lambda i: (i, 0)),
                   pl.BlockSpec((1, LANES), lambda i: (0, 0))],
        out_shape=[jax.ShapeDtypeStruct((N, LANES), F32), jax.ShapeDtypeStruct((1, LANES), F32)],
        scratch_shapes=[pltpu.VMEM((1, LANES), F32)],
        compiler_params=_cparams(("arbitrary",)),
        name="expert_ranks",
    )(route, tri)


def _dispatch_kernel(zstart_ref, zon_ref, nt_ref, pos_ref, h_ref, xs_ref, zbuf, sem, zsem, *, tm, tg):
    @pl.when(pl.program_id(0) == 0)
    def _():
        zbuf[...] = jnp.zeros_like(zbuf)

        def zero_tile(start):
            return pltpu.make_async_copy(zbuf, xs_ref.at[pl.ds(pl.multiple_of(start, tg), tg), :], zsem)

        n_tiles = xs_ref.shape[0] // tg
        for e in range(N_EXPERTS):
            pl.when(zon_ref[e] > 0)(lambda e=e: zero_tile(zstart_ref[e]).start())
        lax.fori_loop(nt_ref[0], n_tiles, lambda t, _: (zero_tile(t * tg).start(), 0)[1], 0)
        for e in range(N_EXPERTS):
            pl.when(zon_ref[e] > 0)(lambda e=e: zero_tile(zstart_ref[e]).wait())
        lax.fori_loop(nt_ref[0], n_tiles, lambda t, _: (zero_tile(t * tg).wait(), 0)[1], 0)

    def copy(r, slot):
        return pltpu.make_async_copy(h_ref.at[pl.ds(r, 1), :],
                                     xs_ref.at[pl.ds(pos_ref[0, slot, r], 1), :], sem)

    def issue(r, _):
        copy(r, 0).start()
        copy(r, 1).start()
        return 0

    lax.fori_loop(0, tm, issue, 0, unroll=DMA_UNROLL)
    for _ in range(2):
        pltpu.make_async_copy(h_ref, xs_ref.at[pl.ds(0, tm), :], sem).wait()


def _dispatch(h2, pos3, last_tile_start, has_rows, n_tiles_used, n_rows, tm, tg):
    N, D = h2.shape
    grid_spec = pltpu.PrefetchScalarGridSpec(
        num_scalar_prefetch=3,
        grid=(N // tm,),
        in_specs=[pl.BlockSpec((1, 2, tm), lambda i, zs, zo, nt: (i, 0, 0), memory_space=pltpu.SMEM),
                  pl.BlockSpec((tm, D), lambda i, zs, zo, nt: (i, 0))],
        out_specs=pl.BlockSpec(memory_space=pl.ANY),
        scratch_shapes=[pltpu.VMEM((tg, D), F32), pltpu.SemaphoreType.DMA(()), pltpu.SemaphoreType.DMA(())],
    )
    return pl.pallas_call(
        functools.partial(_dispatch_kernel, tm=tm, tg=tg),
        grid_spec=grid_spec,
        out_shape=jax.ShapeDtypeStruct((n_rows, D), F32),
        compiler_params=_cparams(("arbitrary",)),
        name="moe_dispatch",
    )(last_tile_start, has_rows, n_tiles_used, pos3, h2)


def _expert_kernel(te_ref, nt_ref, xs_ref, w1_ref, w3_ref, w2_ref, y_ref, w1b, w3b, w2b):
    g = pl.program_id(0)
    used = g < nt_ref[0]
    new_expert = (g == 0) | (te_ref[g] != te_ref[jnp.maximum(g - 1, 0)])

    @pl.when(used & new_expert)
    def _():
        w1b[...] = w1_ref[0].astype(BF16)
        w3b[...] = w3_ref[0].astype(BF16)
        w2b[...] = w2_ref[0].astype(BF16)

    @pl.when(used)
    def _():
        xb = _unpack_halves(xs_ref[...]).astype(BF16)
        a = jnp.dot(xb, w1b[...], preferred_element_type=F32)
        b = jnp.dot(xb, w3b[...], preferred_element_type=F32)
        hmid = (a * jax.nn.sigmoid(a) * b).astype(BF16)
        y_ref[...] = _pack_halves(jnp.dot(hmid, w2b[...], preferred_element_type=F32))

    @pl.when(jnp.logical_not(used))
    def _():
        y_ref[...] = jnp.zeros_like(y_ref)


def _experts(tile_expert, n_tiles_used, xs, w1, w3, w2, tg):
    P, Dp = xs.shape
    E, D, De = w1.shape
    row_tile = lambda g, te, nt: (jnp.minimum(g, nt[0] - 1), 0)
    grid_spec = pltpu.PrefetchScalarGridSpec(
        num_scalar_prefetch=2,
        grid=(P // tg,),
        in_specs=[pl.BlockSpec((tg, Dp), row_tile),
                  pl.BlockSpec((1, D, De), lambda g, te, nt: (te[g], 0, 0)),
                  pl.BlockSpec((1, D, De), lambda g, te, nt: (te[g], 0, 0)),
                  pl.BlockSpec((1, De, D), lambda g, te, nt: (te[g], 0, 0))],
        out_specs=pl.BlockSpec((tg, Dp), lambda g, te, nt: (g, 0)),
        scratch_shapes=[pltpu.VMEM((D, De), BF16), pltpu.VMEM((D, De), BF16), pltpu.VMEM((De, D), BF16)],
    )
    return pl.pallas_call(
        _expert_kernel,
        grid_spec=grid_spec,
        out_shape=jax.ShapeDtypeStruct((P, Dp), PACKED),
        compiler_params=_cparams(("arbitrary",)),
        name="moe_experts",
    )(tile_expert, n_tiles_used, xs, w1, w3, w2)


def _combine_kernel(pos_ref, y_ref, x1_ref, route_ref, gt_ref, o_ref, buf0, buf1, sem, *, tm):
    def copy(r, slot, buf):
        return pltpu.make_async_copy(y_ref.at[pl.ds(pos_ref[0, slot, r], 1), :],
                                     buf.at[pl.ds(r, 1), :], sem)

    def issue(r, _):
        copy(r, 0, buf0).start()
        copy(r, 1, buf1).start()
        return 0

    lax.fori_loop(0, tm, issue, 0, unroll=DMA_UNROLL)
    for buf in (buf0, buf1):
        pltpu.make_async_copy(y_ref.at[pl.ds(0, tm), :], buf, sem).wait()
    w0 = route_ref[:, 2:3]
    w1 = route_ref[:, 3:4]
    y = buf0[...] * w0 + buf1[...] * w1
    o_ref[...] = x1_ref[...] + gt_ref[0] * y


def _combine(pos3, y, x1, route, gt2, tm, S):
    N, D = x1.shape
    per_b = S // tm
    return pl.pallas_call(
        functools.partial(_combine_kernel, tm=tm),
        grid=(N // tm,),
        in_specs=[pl.BlockSpec((1, 2, tm), lambda i: (i, 0, 0), memory_space=pltpu.SMEM),
                  pl.BlockSpec(memory_space=pl.ANY),
                  pl.BlockSpec((tm, D), lambda i: (i, 0)),
                  pl.BlockSpec((tm, LANES), lambda i: (i, 0)),
                  pl.BlockSpec((1, 1, D), lambda i: (i // per_b, 0, 0))],
        out_specs=pl.BlockSpec((tm, D), lambda i: (i, 0)),
        out_shape=jax.ShapeDtypeStruct((N, D), F32),
        scratch_shapes=[pltpu.VMEM((tm, D), F32), pltpu.VMEM((tm, D), F32), pltpu.SemaphoreType.DMA(())],
        compiler_params=_cparams(("arbitrary",)),
        name="moe_combine",
    )(pos3, y, x1, route, gt2)


SC_CORES = 2
SC_SUBCORES = 16
SC_WINDOW = 64


def _sc_row_gather(table, idx):
    M, = idx.shape
    D = table.shape[1]
    workers = SC_CORES * SC_SUBCORES
    per_worker = M // workers
    assert per_worker * workers == M and per_worker % SC_WINDOW == 0
    mesh = plsc.VectorSubcoreMesh(core_axis_name="c", subcore_axis_name="s",
                                  num_cores=SC_CORES, num_subcores=SC_SUBCORES)

    @functools.partial(
        pl.kernel, mesh=mesh,
        out_type=jax.ShapeDtypeStruct((M, D), table.dtype),
        scratch_types=[pltpu.VMEM((SC_WINDOW,), jnp.int32), pltpu.VMEM((SC_WINDOW, D), table.dtype),
                       pltpu.SemaphoreType.DMA],
        name="sc_row_gather")
    def gather(table_hbm, idx_hbm, out_hbm, idx_v, rows_v, sem):
        wid = lax.axis_index("s") * SC_CORES + lax.axis_index("c")

        def window(j, _):
            base = pl.multiple_of(wid * per_worker + j * SC_WINDOW, SC_WINDOW)
            pltpu.sync_copy(idx_hbm.at[pl.ds(base, SC_WINDOW)], idx_v)
            pltpu.async_copy(table_hbm.at[idx_v], rows_v, sem).wait()
            pltpu.sync_copy(rows_v, out_hbm.at[pl.ds(base, SC_WINDOW)])
            return 0

        lax.fori_loop(0, per_worker // SC_WINDOW, window, 0)

    return gather(table, idx)


def _combine_rows_kernel(y0_ref, y1_ref, x1_ref, route_ref, gt_ref, o_ref):
    y = _unpack_halves(y0_ref[...]) * route_ref[:, 2:3] + _unpack_halves(y1_ref[...]) * route_ref[:, 3:4]
    o_ref[...] = x1_ref[...] + gt_ref[0] * y


def _combine_rows(yg, x1, route, gt2, tm, S):
    N, D = x1.shape
    per_b = S // tm
    n_blocks = N // tm
    return pl.pallas_call(
        _combine_rows_kernel,
        grid=(n_blocks,),
        in_specs=[pl.BlockSpec((tm, D // 2), lambda i: (i, 0)),
                  pl.BlockSpec((tm, D // 2), lambda i: (i + n_blocks, 0)),
                  pl.BlockSpec((tm, D), lambda i: (i, 0)),
                  pl.BlockSpec((tm, LANES), lambda i: (i, 0)),
                  pl.BlockSpec((1, 1, D), lambda i: (i // per_b, 0, 0))],
        out_specs=pl.BlockSpec((tm, D), lambda i: (i, 0)),
        out_shape=jax.ShapeDtypeStruct((N, D), F32),
        compiler_params=_cparams(("parallel",)),
        name="moe_combine",
    )(yg, yg, x1, route, gt2)


def _rope_tables(positions):
    inv = ROPE_THETA ** (-jnp.arange(HALF, dtype=F32) / HALF)
    ang = positions.astype(F32)[:, None, :] * inv[None, :, None]
    return jnp.cos(ang), jnp.sin(ang)


def _block_diag_mean(width):
    blk = np.kron(np.eye(width // HEAD_DIM, dtype=np.float32), np.full((HEAD_DIM, HEAD_DIM), 1.0 / HEAD_DIM, np.float32))
    return jnp.asarray(blk, BF16)


def _layer(x, c_mod, positions, norm1_g, norm2_g, w_in, b_fgt, b_gate, qn_fox, kn_fox, qn_dsa, kn_dsa,
           w_proj_fox, w_proj_dsa, w_out, r_w_grp, r_b_grp, r_w_exp, r_b_exp, w1, w3, w2):
    B, S, D = x.shape
    N = B * S
    topk = min(TOPK_MAX, S // 4)
    tm = min(512, S)
    scale = HEAD_DIM ** -0.5
    mod3 = c_mod.reshape(B, 6, D)

    o = np.cumsum([0, 512, 512, 512, 8, 512, 64, 64, 512, 64, 8, D, D])
    seg = lambda k: w_in[:, o[k]:o[k + 1]]
    zpad = jnp.zeros((D, LANES - HEAD_DIM - 2 * N_HEADS), F32)
    w_tok = jnp.concatenate([seg(1), seg(10), seg(11)], axis=1).astype(BF16)
    w_t = jnp.concatenate([seg(0), seg(4), seg(7), seg(2),
                           seg(5), seg(8),
                           seg(6), seg(3), seg(9), zpad], axis=1).T.astype(BF16)
    gcol = jnp.stack([qn_fox * (scale * LOG2E), qn_dsa * (scale * LOG2E), kn_dsa]).reshape(3, HEAD_DIM, 1)
    cos_t, sin_t = _rope_tables(positions)

    k_heads, gates, fqt, dqt, iqt, fvt, kkt, dvt, logf_t, iwt = _in_projection(
        x, mod3, norm1_g.reshape(1, D), w_tok, w_t, _block_diag_mean(W_HEADS), cos_t, sin_t,
        jnp.tile(kn_fox, N_HEADS).reshape(1, W_HEADS), gcol, b_fgt.reshape(N_HEADS, 1),
        b_gate.reshape(1, 2 * D), tm)

    f_tok = jnp.transpose(_seq_cumsum(logf_t), (0, 2, 1)) * LOG2E
    of = _fox_attention(k_heads, f_tok, fqt, fvt)
    od = _dsa_attention(kkt, dvt, iqt, dqt, iwt, topk)

    wr = jnp.concatenate([r_w_grp, r_w_exp, jnp.zeros((D, LANES - N_GROUPS - N_EXPERTS), F32)], axis=1).astype(BF16)
    br = jnp.concatenate([r_b_grp, r_b_exp, jnp.zeros((LANES - N_GROUPS - N_EXPERTS,), F32)]).reshape(1, LANES)
    x1, h2, route = _post_attention(of, od, gates, x, mod3, w_proj_fox.astype(BF16), w_proj_dsa.astype(BF16),
                                    w_out.astype(BF16), norm2_g.reshape(1, D), wr, br, tm)
    x1, h2, route = x1.reshape(N, D), h2.reshape(N, D // 2), route.reshape(N, LANES)

    tg = 512 if N * 2 >= 512 * N_EXPERTS else 128
    ranks, counts = _expert_ranks(route, tm)
    counts = counts[0, :N_EXPERTS].astype(jnp.int32)
    padded = ((counts + tg - 1) // tg) * tg
    ends = jnp.cumsum(padded)
    starts = ends - padded
    e01 = route[:, :2].astype(jnp.int32)
    start_of = jnp.sum(jnp.where(e01[..., None] == jnp.arange(N_EXPERTS, dtype=jnp.int32), starts, 0), axis=-1)
    pos = start_of + ranks[:, :2].astype(jnp.int32)
    n_rows = N * 2 + N_EXPERTS * tg
    n_tiles = n_rows // tg
    tile_start = jnp.arange(n_tiles, dtype=jnp.int32) * tg
    tile_expert = jnp.minimum(jnp.sum((ends[None, :] <= tile_start[:, None]).astype(jnp.int32), axis=1),
                              N_EXPERTS - 1)
    n_used = (ends[-1] // tg).astype(jnp.int32).reshape(1)

    token_of_row = jnp.zeros((n_rows,), jnp.int32).at[pos.reshape(N * 2)].set(
        jnp.repeat(jnp.arange(N, dtype=jnp.int32), 2), unique_indices=True)
    xs = _sc_row_gather(h2, token_of_row)
    y = _experts(tile_expert, n_used, xs, w1, w3, w2, tg)
    yg = _sc_row_gather(y, jnp.concatenate([pos[:, 0], pos[:, 1]]))
    out = _combine_rows(yg, x1, route, mod3[:, 5:6, :], min(512, S), S)
    return out.reshape(B, S, D)


def kernel(x, c, positions, ada_w, ada_b, norm1_g, norm2_g, w_in, b_fgt, b_gate, qn_fox, kn_fox, qn_dsa, kn_dsa, w_proj_fox, w_proj_dsa, w_out, router_w_grp, router_b_grp, router_w_exp, router_b_exp, exp_w1, exp_w3, exp_w2):
    for l in range(ada_w.shape[0]):
        c_mod = _modulation(c, ada_w[l], ada_b[l])
        x = _layer(x, c_mod, positions, norm1_g[l], norm2_g[l], w_in[l], b_fgt[l], b_gate[l],
                   qn_fox[l], kn_fox[l], qn_dsa[l], kn_dsa[l], w_proj_fox[l], w_proj_dsa[l], w_out[l],
                   router_w_grp[l], router_b_grp[l], router_w_exp[l], router_b_exp[l],
                   exp_w1[l], exp_w3[l], exp_w2[l])
    return x
```

```python
import functools

import jax
import jax.numpy as jnp
import numpy as np
from jax import lax
from jax.experimental import pallas as pl
from jax.experimental.pallas import tpu as pltpu
from jax.experimental.pallas import tpu_sc as plsc

F32 = jnp.float32
BF16 = jnp.bfloat16

CHUNK = 64
CHUNK_SHIFT = 6
DMA_UNROLL = 8
HEAD_DIM = 64
N_HEADS = 8
W_HEADS = N_HEADS * HEAD_DIM
TOPK_MAX = 256
ROPE_THETA = 10000.0
N_GROUPS = 4
EXPERTS_PER_GROUP = 8
N_EXPERTS = N_GROUPS * EXPERTS_PER_GROUP
EPS = 1e-6
LOG2E = 1.4426950408889634
MASKED = -1e30

LANES = 128
VMEM_LIMIT = 56 * 1024 * 1024


def _cparams(sem):
    return pltpu.CompilerParams(dimension_semantics=sem, vmem_limit_bytes=VMEM_LIMIT)


def _mod_kernel(c_ref, w_ref, b_ref, o_ref):
    c = c_ref[...]
    ca = (c * jax.nn.sigmoid(c)).astype(BF16)
    o_ref[...] = jnp.dot(ca, w_ref[...].astype(BF16), preferred_element_type=F32) + b_ref[...]


def _modulation(c, ada_w, ada_b):
    B, D = c.shape
    n = ada_w.shape[1] // D
    return pl.pallas_call(
        _mod_kernel,
        grid=(n,),
        in_specs=[pl.BlockSpec((B, D), lambda j: (0, 0)),
                  pl.BlockSpec((D, D), lambda j: (0, j)),
                  pl.BlockSpec((1, D), lambda j: (0, j))],
        out_specs=pl.BlockSpec((B, D), lambda j: (0, j)),
        out_shape=jax.ShapeDtypeStruct((B, n * D), F32),
        compiler_params=_cparams(("arbitrary",)),
        name="adaln_mod",
    )(c, ada_w, ada_b.reshape(1, -1))


R_FQ, R_DQ, R_IQ, R_FV = 0, 512, 1024, 1536
R_KK = 2048
R_S2 = 2176
R_END = 2304
HALF = HEAD_DIM // 2


def _inproj_kernel(x_ref, mod_ref, g1_ref, wtok_ref, wt_ref, bd512_ref, cos_ref, sin_ref,
                   gk_ref, gcol_ref, bf_ref, bg_ref,
                   fk_ref, gate_ref, fqt_ref, dqt_ref, iqt_ref, fvt_ref, kkt_ref, dvt_ref, lf_ref, iwt_ref):
    x = x_ref[0]
    ms = jnp.mean(x * x, axis=-1, keepdims=True)
    h = x * lax.rsqrt(ms + EPS) * g1_ref[...]
    h = h * (1.0 + mod_ref[0, 1:2, :]) + mod_ref[0, 0:1, :]
    hb = h.astype(BF16)
    D = x.shape[-1]
    cos, sin = cos_ref[0], sin_ref[0]

    def proj_t(lo, hi):
        return lax.dot_general(wt_ref[lo:hi, :], hb, (((1,), (1,)), ((), ())), preferred_element_type=F32)

    def norm_t(yh, gain):
        msq = jnp.mean(yh * yh, axis=0, keepdims=True)
        return yh * lax.rsqrt(msq + EPS) * gain

    def rope_store(ref, lo, yh):
        x1, x2 = yh[:HALF], yh[HALF:]
        ref[0, lo:lo + HALF, :] = (x1 * cos - x2 * sin).astype(ref.dtype)
        ref[0, lo + HALF:lo + HEAD_DIM, :] = (x2 * cos + x1 * sin).astype(ref.dtype)

    fq = proj_t(R_FQ, R_FQ + W_HEADS)
    dq = proj_t(R_DQ, R_DQ + W_HEADS)
    iq = proj_t(R_IQ, R_IQ + W_HEADS)
    for hh in range(N_HEADS):
        lo = hh * HEAD_DIM
        fqt_ref[0, lo:lo + HEAD_DIM, :] = norm_t(fq[lo:lo + HEAD_DIM], gcol_ref[0]).astype(BF16)
        rope_store(dqt_ref, lo, norm_t(dq[lo:lo + HEAD_DIM], gcol_ref[1]))
        rope_store(iqt_ref, lo, iq[lo:lo + HEAD_DIM])
    fvt_ref[0] = proj_t(R_FV, R_FV + W_HEADS).astype(BF16)

    kk = proj_t(R_KK, R_KK + 2 * HEAD_DIM)
    rope_store(kkt_ref, 0, norm_t(kk[:HEAD_DIM], gcol_ref[2]))
    rope_store(kkt_ref, HEAD_DIM, kk[HEAD_DIM:])

    s2 = proj_t(R_S2, R_S2 + LANES)
    dvt_ref[0] = s2[:HEAD_DIM].astype(BF16)
    z = s2[HEAD_DIM:HEAD_DIM + N_HEADS] + bf_ref[...]
    lf_ref[0] = jnp.minimum(z, 0.0) - jnp.log(1.0 + jnp.exp(-jnp.abs(z)))
    iwt_ref[0] = s2[HEAD_DIM + N_HEADS:HEAD_DIM + 2 * N_HEADS]

    fk = jnp.dot(hb, wtok_ref[:, :W_HEADS], preferred_element_type=F32)
    msq = jnp.dot((fk * fk).astype(BF16), bd512_ref[...], preferred_element_type=F32)
    fk = (fk * lax.rsqrt(msq + EPS) * gk_ref[...]).astype(BF16)
    for hh in range(N_HEADS):
        fk_ref[0, hh] = fk[:, hh * HEAD_DIM:(hh + 1) * HEAD_DIM]
    g = jnp.dot(hb, wtok_ref[:, W_HEADS:], preferred_element_type=F32)
    gate_ref[0] = jax.nn.sigmoid(g + bg_ref[...]).astype(BF16)


def _in_projection(x, mod3, norm1_g, w_tok, w_t, bd512, cos_t, sin_t, gk, gcol, bf, bg, tm):
    B, S, D = x.shape
    tok = lambda w: pl.BlockSpec((1, tm, w), lambda b, i: (b, i, 0))
    feat = lambda r: pl.BlockSpec((1, r, tm), lambda b, i: (b, 0, i))
    const = lambda shape: pl.BlockSpec(shape, lambda b, i: (0,) * len(shape))
    out_shapes = [jax.ShapeDtypeStruct((B, N_HEADS, S, HEAD_DIM), BF16),
                  jax.ShapeDtypeStruct((B, S, 2 * D), BF16)] + \
                 [jax.ShapeDtypeStruct((B, W_HEADS, S), BF16)] * 4 + \
                 [jax.ShapeDtypeStruct((B, 2 * HEAD_DIM, S), BF16),
                  jax.ShapeDtypeStruct((B, HEAD_DIM, S), BF16),
                  jax.ShapeDtypeStruct((B, N_HEADS, S), F32),
                  jax.ShapeDtypeStruct((B, N_HEADS, S), F32)]
    return pl.pallas_call(
        _inproj_kernel,
        grid=(B, S // tm),
        in_specs=[tok(D),
                  pl.BlockSpec((1, 6, D), lambda b, i: (b, 0, 0)),
                  const((1, D)),
                  const(w_tok.shape), const(w_t.shape), const((W_HEADS, W_HEADS)),
                  feat(HALF), feat(HALF),
                  const((1, W_HEADS)), const((3, HEAD_DIM, 1)), const((N_HEADS, 1)), const((1, 2 * D))],
        out_specs=[pl.BlockSpec((1, N_HEADS, tm, HEAD_DIM), lambda b, i: (b, 0, i, 0)), tok(2 * D),
                   feat(W_HEADS), feat(W_HEADS), feat(W_HEADS), feat(W_HEADS),
                   feat(2 * HEAD_DIM), feat(HEAD_DIM), feat(N_HEADS), feat(N_HEADS)],
        out_shape=out_shapes,
        compiler_params=_cparams(("parallel", "parallel")),
        name="in_projection",
    )(x, mod3, norm1_g, w_tok, w_t, bd512, cos_t, sin_t, gk, gcol, bf, bg)


def _cumsum_kernel(x_ref, o_ref):
    x = x_ref[0]
    n = x.shape[-1]
    pos = lax.broadcasted_iota(jnp.int32, x.shape, 1)
    shift = 1
    while shift < n:
        x = x + jnp.where(pos >= shift, pltpu.roll(x, shift, 1), 0.0)
        shift *= 2
    o_ref[0] = x


def _seq_cumsum(logf_t):
    B, H, S = logf_t.shape
    return pl.pallas_call(
        _cumsum_kernel,
        grid=(B,),
        in_specs=[pl.BlockSpec((1, H, S), lambda b: (b, 0, 0))],
        out_specs=pl.BlockSpec((1, H, S), lambda b: (b, 0, 0)),
        out_shape=jax.ShapeDtypeStruct((B, H, S), F32),
        compiler_params=_cparams(("parallel",)),
        name="forget_cumsum",
    )(logf_t)


KC = 256
SUB = 8


def _fold_rows(a, op, ways=1):
    n = a.shape[0] // SUB
    a = a.reshape(n, SUB, a.shape[1])
    chains = [a[w] for w in range(ways)]
    for j in range(ways, n):
        chains[j % ways] = op(chains[j % ways], a[j])
    while len(chains) > 1:
        chains = [op(chains[2 * j], chains[2 * j + 1]) for j in range(len(chains) // 2)]
    return chains[0]


def _softmax_pv(nch, s_ref, acc_ref, vt_at, m_all, o_ref):
    Q = o_ref.shape[1]
    acc_ref[...] = jnp.zeros_like(acc_ref)

    def body(c, lsum):
        off = pl.multiple_of(c * KC, KC)
        new = []
        for hh in range(N_HEADS):
            p = jnp.exp2(s_ref[hh, pl.ds(off, KC), :] - m_all[hh])
            new.append(lsum[hh] + _fold_rows(p, jnp.add))
            acc_ref[hh] += jnp.dot(vt_at(hh, off), p.astype(BF16), preferred_element_type=F32)
        return tuple(new)

    lsum = lax.fori_loop(0, nch, body, tuple(jnp.zeros((SUB, Q), F32) for _ in range(N_HEADS)))
    for hh in range(N_HEADS):
        acc_ref[hh] = acc_ref[hh] / jnp.sum(lsum[hh], axis=0, keepdims=True)
    out_t = acc_ref[...].reshape(N_HEADS * HEAD_DIM, Q)
    o_ref[0] = out_t.T.astype(BF16)


def _fox_kernel(k_ref, f_ref, qt_ref, vt_ref, o_ref, s_ref, acc_ref):
    i = pl.program_id(1)
    Q = o_ref.shape[1]
    qts = [qt_ref[0, hh * HEAD_DIM:(hh + 1) * HEAD_DIM, :] for hh in range(N_HEADS)]

    def scores(c, mx, bias):
        off = pl.multiple_of(c * KC, KC)
        new = []
        for hh in range(N_HEADS):
            s = jnp.dot(k_ref[0, hh, pl.ds(off, KC), :], qts[hh], preferred_element_type=F32)
            s = s - f_ref[0, pl.ds(off, KC), hh:hh + 1]
            if bias is not None:
                s = s + bias
            s_ref[hh, pl.ds(off, KC), :] = s
            new.append(jnp.maximum(mx[hh], _fold_rows(s, jnp.maximum)))
        return tuple(new)

    mx = tuple(jnp.full((SUB, Q), MASKED, F32) for _ in range(N_HEADS))
    mx = lax.fori_loop(0, i, lambda c, m: scores(c, m, None), mx)
    kk = lax.broadcasted_iota(jnp.int32, (KC, Q), 0)
    qq = lax.broadcasted_iota(jnp.int32, (KC, Q), 1)
    mx = scores(i, mx, jnp.where(kk <= qq, 0.0, MASKED))
    m_all = [jnp.max(m, axis=0, keepdims=True) for m in mx]
    _softmax_pv(i + 1, s_ref, acc_ref, lambda hh, off: vt_ref[0, hh * HEAD_DIM:(hh + 1) * HEAD_DIM, pl.ds(off, KC)],
                m_all, o_ref)


def _fox_attention(k_heads, f_tok, qt, vt):
    B, H, S, Dh = k_heads.shape
    return pl.pallas_call(
        _fox_kernel,
        grid=(B, S // KC),
        in_specs=[pl.BlockSpec((1, H, S, Dh), lambda b, i: (b, 0, 0, 0)),
                  pl.BlockSpec((1, S, H), lambda b, i: (b, 0, 0)),
                  pl.BlockSpec((1, W_HEADS, KC), lambda b, i: (b, 0, i)),
                  pl.BlockSpec((1, W_HEADS, S), lambda b, i: (b, 0, 0))],
        out_specs=pl.BlockSpec((1, KC, W_HEADS), lambda b, i: (b, i, 0)),
        out_shape=jax.ShapeDtypeStruct((B, S, W_HEADS), BF16),
        scratch_shapes=[pltpu.VMEM((H, S, KC), F32), pltpu.VMEM((H, HEAD_DIM, KC), F32)],
        compiler_params=_cparams(("parallel", "arbitrary")),
        name="fox_attention",
    )(k_heads, f_tok, qt, vt)


INT_MIN = -(2 ** 31)
KEY_NEG_INF = INT_MIN + 0x7FFFFF
HI16 = -(2 ** 16)
PACK = 16


def _dsa_kernel(kkt_ref, dvt_ref, iqt_ref, dqt_ref, iwt_ref, o_ref, key_ref, hi_ref, s_ref, acc_ref, dk_ref, ik_ref,
                *, topk):
    i = pl.program_id(1)
    Q = o_ref.shape[1]
    nch = i + 1

    @pl.when(i == 0)
    def _():
        def to_rows(c, _):
            off = pl.multiple_of(c * KC, KC)
            rows = kkt_ref[0, :, pl.ds(off, KC)].astype(F32).T
            dk_ref[pl.ds(off, KC), :] = rows[:, :HEAD_DIM].astype(BF16)
            ik_ref[pl.ds(off, KC), :] = rows[:, HEAD_DIM:].astype(BF16)
            return 0
        lax.fori_loop(0, kkt_ref.shape[-1] // KC, to_rows, 0)

    sub_k = lax.broadcasted_iota(jnp.int32, (KC, Q), 0)
    sub_r = lax.broadcasted_iota(jnp.int32, (CHUNK, Q), 0)
    q_chunk = (i * Q + lax.broadcasted_iota(jnp.int32, (CHUNK, Q), 1)) >> CHUNK_SHIFT
    iqts = [iqt_ref[0, hh * HEAD_DIM:(hh + 1) * HEAD_DIM, :] for hh in range(N_HEADS)]
    iws = [iwt_ref[0, hh:hh + 1, :] for hh in range(N_HEADS)]

    def score_chunk(c, _):
        for r in range(KC // CHUNK):
            off = pl.multiple_of(c * KC + r * CHUNK, CHUNK)
            ik = ik_ref[pl.ds(off, CHUNK), :]
            sc = jnp.zeros((CHUNK, Q), F32)
            for hh in range(N_HEADS):
                d = jnp.dot(ik, iqts[hh], preferred_element_type=F32)
                sc = sc + iws[hh] * jnp.maximum(d, 0.0)
            sc = sc + 0.0
            allowed = ((off + sub_r) >> CHUNK_SHIFT) <= q_chunk
            bits = pltpu.bitcast(jnp.where(allowed, sc, -jnp.inf), jnp.int32)
            key = bits ^ ((bits >> 31) & 0x7FFFFFFF)
            key_ref[pl.ds(off, CHUNK), :] = key
            hi_ref[pl.ds(off, CHUNK), :] = (key >> 16).astype(jnp.int16)
        return 0

    lax.fori_loop(0, nch, score_chunk, 0)

    def sweep(n, body, init):
        if isinstance(n, int):
            acc = init
            for c in range(n):
                acc = body(c * KC, acc)
            return acc
        return lax.fori_loop(0, n, lambda c, acc: body(pl.multiple_of(c * KC, KC), acc), init)

    def count(pred, n=nch):
        def body(off, acc):
            hit = pred(key_ref[pl.ds(off, KC), :], off + sub_k)
            return acc + _fold_rows(jnp.where(hit, 1.0, 0.0), jnp.add, ways=4)
        return jnp.sum(sweep(n, body, jnp.zeros((SUB, Q), F32)), axis=0, keepdims=True)

    one, zero = jnp.ones((), BF16), jnp.zeros((), BF16)

    def count_hi(cand, n):
        c16 = jnp.broadcast_to(cand >> 16, (PACK, Q)).astype(jnp.int16)

        def body(off, acc):
            kb = hi_ref[pl.ds(off, KC), :].reshape(KC // PACK, PACK, Q)
            hit = jnp.where(kb >= c16[None], one, zero)
            parts = [hit[w] for w in range(4)]
            for j in range(4, KC // PACK):
                parts[j % 4] = parts[j % 4] + hit[j]
            return acc + ((parts[0] + parts[1]) + (parts[2] + parts[3]))
        acc = sweep(n, body, jnp.zeros((PACK, Q), BF16))
        return jnp.sum(acc.astype(F32), axis=0, keepdims=True)

    assert key_ref.shape[0] // PACK <= 256
    kf = jnp.float32(topk)

    def descent(n):
        n_nonneg = count_hi(jnp.zeros((1, Q), jnp.int32), n)
        top_half = n_nonneg >= kf
        thr = jnp.where(top_half, 0, INT_MIN).astype(jnp.int32)
        n_ge = jnp.where(top_half, n_nonneg, jnp.float32(n * KC))

        def descend(counter, top_bit):
            def step(j, carry):
                thr, n_ge = carry
                cand = thr + (jnp.int32(1) << (top_bit - j))
                cnt = counter(cand)
                take = cnt >= kf
                return jnp.where(take, cand, thr), jnp.where(take, cnt, n_ge)
            return step

        carry = lax.fori_loop(0, 15, descend(lambda c: count_hi(c, n), 30), (thr, n_ge))
        return lax.fori_loop(0, 16, descend(lambda c: count(lambda k, _: k >= c, n), 15), carry)

    thr, n_ge = lax.switch(i, [functools.partial(descent, n) for n in range(1, key_ref.shape[0] // KC + 1)])

    excess = (n_ge > kf) & (thr > KEY_NEG_INF)

    @pl.when(jnp.max(jnp.where(excess, 1.0, 0.0)) > 0.0)
    def _():
        need = kf - count(lambda k, _: k > thr)
        nbits = int(np.ceil(np.log2(key_ref.shape[0]))) + 1

        def bound(j, last):
            cand = last + (jnp.int32(1) << (nbits - 1 - j))
            n = count(lambda k, idx: (k == thr) & (idx < cand))
            return jnp.where(n < need, cand, last)

        last = lax.fori_loop(0, nbits, bound, jnp.zeros((1, Q), jnp.int32))

        def demote(c, _):
            off = pl.multiple_of(c * KC, KC)
            k = key_ref[pl.ds(off, KC), :]
            drop = excess & (k == thr) & (off + sub_k > last)
            key_ref[pl.ds(off, KC), :] = jnp.where(drop, thr - 1, k)
            return 0

        lax.fori_loop(0, nch, demote, 0)

    keep_from = jnp.maximum(thr, KEY_NEG_INF + 1)

    qts = [dqt_ref[0, hh * HEAD_DIM:(hh + 1) * HEAD_DIM, :] for hh in range(N_HEADS)]

    def scores(c, mx):
        off = pl.multiple_of(c * KC, KC)
        bias = jnp.where(key_ref[pl.ds(off, KC), :] >= keep_from, 0.0, MASKED)
        dk = dk_ref[pl.ds(off, KC), :]
        new = []
        for hh in range(N_HEADS):
            s = jnp.dot(dk, qts[hh], preferred_element_type=F32) + bias
            s_ref[hh, pl.ds(off, KC), :] = s
            new.append(jnp.maximum(mx[hh], _fold_rows(s, jnp.maximum)))
        return tuple(new)

    mx = lax.fori_loop(0, nch, scores, tuple(jnp.full((SUB, Q), MASKED, F32) for _ in range(N_HEADS)))
    m_all = [jnp.max(m, axis=0, keepdims=True) for m in mx]
    _softmax_pv(nch, s_ref, acc_ref, lambda hh, off: dvt_ref[0, :, pl.ds(off, KC)], m_all, o_ref)


def _dsa_attention(kkt, dvt, iqt, dqt, iwt, topk):
    B, Dh, S = dvt.shape
    rows = lambda r: pl.BlockSpec((1, r, S), lambda b, i: (b, 0, 0))
    qcols = lambda r: pl.BlockSpec((1, r, KC), lambda b, i: (b, 0, i))
    return pl.pallas_call(
        functools.partial(_dsa_kernel, topk=topk),
        grid=(B, S // KC),
        in_specs=[rows(2 * Dh), rows(Dh), qcols(W_HEADS), qcols(W_HEADS), qcols(N_HEADS)],
        out_specs=pl.BlockSpec((1, KC, W_HEADS), lambda b, i: (b, i, 0)),
        out_shape=jax.ShapeDtypeStruct((B, S, W_HEADS), BF16),
        scratch_shapes=[pltpu.VMEM((S, KC), jnp.int32), pltpu.VMEM((S, KC), jnp.int16),
                        pltpu.VMEM((N_HEADS, S, KC), F32), pltpu.VMEM((N_HEADS, HEAD_DIM, KC), F32),
                        pltpu.VMEM((S, Dh), BF16), pltpu.VMEM((S, Dh), BF16)],
        compiler_params=_cparams(("parallel", "arbitrary")),
        name="dsa_attention",
    )(kkt, dvt, iqt, dqt, iwt)


PACKED = jnp.int32


def _pack_halves(a):
    half = a.shape[-1] // 2
    rounded = a.astype(BF16).astype(F32)
    lo = pltpu.bitcast(rounded[:, :half], jnp.int32)
    hi = pltpu.bitcast(rounded[:, half:], jnp.int32)
    return hi | ((lo >> 16) & 0xFFFF)


def _unpack_halves(p):
    lo = pltpu.bitcast(p << 16, F32)
    hi = pltpu.bitcast(p & HI16, F32)
    return jnp.concatenate([lo, hi], axis=-1)


def _first(mask, lane):
    return jnp.min(jnp.where(mask, lane, LANES), axis=-1, keepdims=True)


def _post_kernel(of_ref, od_ref, gate_ref, x_ref, mod_ref, wpf_ref, wpd_ref, wo_ref, g2_ref, wr_ref, br_ref,
                 x1_ref, h2_ref, route_ref):
    D = x_ref.shape[-1]
    pf = jnp.dot(of_ref[0], wpf_ref[...], preferred_element_type=F32)
    pd = jnp.dot(od_ref[0], wpd_ref[...], preferred_element_type=F32)
    merged = gate_ref[0, :, :D].astype(F32) * pf + gate_ref[0, :, D:].astype(F32) * pd
    y = jnp.dot(merged.astype(BF16), wo_ref[...], preferred_element_type=F32)
    x1 = x_ref[0] + mod_ref[0, 2:3, :] * y
    x1_ref[0] = x1

    ms = jnp.mean(x1 * x1, axis=-1, keepdims=True)
    h2 = x1 * lax.rsqrt(ms + EPS) * g2_ref[...]
    h2 = h2 * (1.0 + mod_ref[0, 4:5, :]) + mod_ref[0, 3:4, :]
    hb = h2.astype(BF16)
    h2_ref[0] = _pack_halves(h2)

    logits = jnp.dot(hb, wr_ref[...], preferred_element_type=F32) + br_ref[...]
    lane = lax.broadcasted_iota(jnp.int32, logits.shape, 1)
    is_grp = lane < N_GROUPS
    gl = jnp.where(is_grp, logits, -jnp.inf)
    gmax = jnp.max(gl, axis=-1, keepdims=True)
    g_idx = _first(gl == gmax, lane)
    g_w = 1.0 / jnp.sum(jnp.exp(gl - gmax), axis=-1, keepdims=True)

    e_lo = N_GROUPS + g_idx * EXPERTS_PER_GROUP
    in_grp = (lane >= e_lo) & (lane < e_lo + EXPERTS_PER_GROUP)
    el = jnp.where(in_grp, logits, -jnp.inf)
    emax = jnp.max(el, axis=-1, keepdims=True)
    ee = jnp.exp(el - emax)
    prob = ee / jnp.sum(ee, axis=-1, keepdims=True)
    prob = jnp.where(in_grp, prob, -1.0)
    p0 = jnp.max(prob, axis=-1, keepdims=True)
    l0 = _first(prob == p0, lane)
    rest = jnp.where(lane == l0, -1.0, prob)
    p1 = jnp.max(rest, axis=-1, keepdims=True)
    l1 = _first(rest == p1, lane)
    psum = p0 + p1
    w0 = g_w * (p0 / psum)
    w1 = g_w * (p1 / psum)
    e0 = (l0 - N_GROUPS).astype(F32)
    e1 = (l1 - N_GROUPS).astype(F32)
    route_ref[0] = jnp.where(lane == 0, e0, jnp.where(lane == 1, e1, jnp.where(lane == 2, w0,
                             jnp.where(lane == 3, w1, 0.0))))


def _post_attention(of, od, gates, x, mod3, wpf, wpd, wo, g2, wr, br, tm):
    B, S, D = x.shape
    tok = lambda w: pl.BlockSpec((1, tm, w), lambda b, i: (b, i, 0))
    const = lambda shape: pl.BlockSpec(shape, lambda b, i: (0,) * len(shape))
    return pl.pallas_call(
        _post_kernel,
        grid=(B, S // tm),
        in_specs=[tok(W_HEADS), tok(W_HEADS), tok(2 * D), tok(D),
                  pl.BlockSpec((1, 6, D), lambda b, i: (b, 0, 0)),
                  const(wpf.shape), const(wpd.shape), const(wo.shape),
                  const((1, D)), const((D, LANES)), const((1, LANES))],
        out_specs=[tok(D), tok(D // 2), tok(LANES)],
        out_shape=[jax.ShapeDtypeStruct((B, S, D), F32),
                   jax.ShapeDtypeStruct((B, S, D // 2), PACKED),
                   jax.ShapeDtypeStruct((B, S, LANES), F32)],
        compiler_params=_cparams(("parallel", "parallel")),
        name="merge_out_router",
    )(of, od, gates, x, mod3, wpf, wpd, wo, g2, wr, br)


def _rank_kernel(route_ref, tri_ref, rank_ref, count_ref, carry_ref):
    @pl.when(pl.program_id(0) == 0)
    def _():
        carry_ref[...] = jnp.zeros_like(carry_ref)

    r = route_ref[...]
    lane = lax.broadcasted_iota(jnp.int32, r.shape, 1).astype(F32)
    hot0 = lane == r[:, 0:1]
    hot1 = lane == r[:, 1:2]
    hits = jnp.where(hot0 | hot1, 1.0, 0.0)
    incl = jnp.dot(tri_ref[...], hits.astype(BF16), preferred_element_type=F32)
    before = incl - hits + carry_ref[...]
    r0 = jnp.sum(jnp.where(hot0, before, 0.0), axis=-1, keepdims=True)
    r1 = jnp.sum(jnp.where(hot1, before, 0.0), axis=-1, keepdims=True)
    rank_ref[...] = jnp.where(lane == 0.0, r0, jnp.where(lane == 1.0, r1, 0.0))
    carry_ref[...] = carry_ref[...] + jnp.sum(hits, axis=0, keepdims=True)
    count_ref[...] = carry_ref[...]


def _expert_ranks(route, tm):
    N = route.shape[0]
    tri = jnp.asarray(np.tril(np.ones((tm, tm), np.float32)), BF16)
    return pl.pallas_call(
        _rank_kernel,
        grid=(N // tm,),
        in_specs=[pl.BlockSpec((tm, LANES), lambda i: (i, 0)),
                  pl.BlockSpec((tm, tm), lambda i: (0, 0))],
        out_specs=[pl.BlockSpec((tm, LANES), lambda i: (i, 0)),
                   pl.BlockSpec((1, LANES), lambda i: (0, 0))],
        out_shape=[jax.ShapeDtypeStruct((N, LANES), F32), jax.ShapeDtypeStruct((1, LANES), F32)],
        scratch_shapes=[pltpu.VMEM((1, LANES), F32)],
        compiler_params=_cparams(("arbitrary",)),
        name="expert_ranks",
    )(route, tri)


def _dispatch_kernel(zstart_ref, zon_ref, nt_ref, pos_ref, h_ref, xs_ref, zbuf, sem, zsem, *, tm, tg):
    @pl.when(pl.program_id(0) == 0)
    def _():
        zbuf[...] = jnp.zeros_like(zbuf)

        def zero_tile(start):
            return pltpu.make_async_copy(zbuf, xs_ref.at[pl.ds(pl.multiple_of(start, tg), tg), :], zsem)

        n_tiles = xs_ref.shape[0] // tg
        for e in range(N_EXPERTS):
            pl.when(zon_ref[e] > 0)(lambda e=e: zero_tile(zstart_ref[e]).start())
        lax.fori_loop(nt_ref[0], n_tiles, lambda t, _: (zero_tile(t * tg).start(), 0)[1], 0)
        for e in range(N_EXPERTS):
            pl.when(zon_ref[e] > 0)(lambda e=e: zero_tile(zstart_ref[e]).wait())
        lax.fori_loop(nt_ref[0], n_tiles, lambda t, _: (zero_tile(t * tg).wait(), 0)[1], 0)

    def copy(r, slot):
        return pltpu.make_async_copy(h_ref.at[pl.ds(r, 1), :],
                                     xs_ref.at[pl.ds(pos_ref[0, slot, r], 1), :], sem)

    def issue(r, _):
        copy(r, 0).start()
        copy(r, 1).start()
        return 0

    lax.fori_loop(0, tm, issue, 0, unroll=DMA_UNROLL)
    for _ in range(2):
        pltpu.make_async_copy(h_ref, xs_ref.at[pl.ds(0, tm), :], sem).wait()


def _dispatch(h2, pos3, last_tile_start, has_rows, n_tiles_used, n_rows, tm, tg):
    N, D = h2.shape
    grid_spec = pltpu.PrefetchScalarGridSpec(
        num_scalar_prefetch=3,
        grid=(N // tm,),
        in_specs=[pl.BlockSpec((1, 2, tm), lambda i, zs, zo, nt: (i, 0, 0), memory_space=pltpu.SMEM),
                  pl.BlockSpec((tm, D), lambda i, zs, zo, nt: (i, 0))],
        out_specs=pl.BlockSpec(memory_space=pl.ANY),
        scratch_shapes=[pltpu.VMEM((tg, D), h2.dtype), pltpu.SemaphoreType.DMA(()), pltpu.SemaphoreType.DMA(())],
    )
    return pl.pallas_call(
        functools.partial(_dispatch_kernel, tm=tm, tg=tg),
        grid_spec=grid_spec,
        out_shape=jax.ShapeDtypeStruct((n_rows, D), h2.dtype),
        compiler_params=_cparams(("arbitrary",)),
        name="moe_dispatch",
    )(last_tile_start, has_rows, n_tiles_used, pos3, h2)


def _expert_kernel(te_ref, nt_ref, xs_ref, w1_ref, w3_ref, w2_ref, y_ref, w1b, w3b, w2b):
    g = pl.program_id(0)
    used = g < nt_ref[0]
    new_expert = (g == 0) | (te_ref[g] != te_ref[jnp.maximum(g - 1, 0)])

    @pl.when(used & new_expert)
    def _():
        w1b[...] = w1_ref[0].astype(BF16)
        w3b[...] = w3_ref[0].astype(BF16)
        w2b[...] = w2_ref[0].astype(BF16)

    @pl.when(used)
    def _():
        xb = _unpack_halves(xs_ref[...]).astype(BF16)
        a = jnp.dot(xb, w1b[...], preferred_element_type=F32)
        b = jnp.dot(xb, w3b[...], preferred_element_type=F32)
        hmid = (a * jax.nn.sigmoid(a) * b).astype(BF16)
        y_ref[...] = _pack_halves(jnp.dot(hmid, w2b[...], preferred_element_type=F32))

    @pl.when(jnp.logical_not(used))
    def _():
        y_ref[...] = jnp.zeros_like(y_ref)


def _experts(tile_expert, n_tiles_used, xs, w1, w3, w2, tg):
    P, Dp = xs.shape
    E, D, De = w1.shape
    row_tile = lambda g, te, nt: (jnp.minimum(g, nt[0] - 1), 0)
    grid_spec = pltpu.PrefetchScalarGridSpec(
        num_scalar_prefetch=2,
        grid=(P // tg,),
        in_specs=[pl.BlockSpec((tg, Dp), row_tile),
                  pl.BlockSpec((1, D, De), lambda g, te, nt: (te[g], 0, 0)),
                  pl.BlockSpec((1, D, De), lambda g, te, nt: (te[g], 0, 0)),
                  pl.BlockSpec((1, De, D), lambda g, te, nt: (te[g], 0, 0))],
        out_specs=pl.BlockSpec((tg, Dp), lambda g, te, nt: (g, 0)),
        scratch_shapes=[pltpu.VMEM((D, De), BF16), pltpu.VMEM((D, De), BF16), pltpu.VMEM((De, D), BF16)],
    )
    return pl.pallas_call(
        _expert_kernel,
        grid_spec=grid_spec,
        out_shape=jax.ShapeDtypeStruct((P, Dp), PACKED),
        compiler_params=_cparams(("arbitrary",)),
        name="moe_experts",
    )(tile_expert, n_tiles_used, xs, w1, w3, w2)


def _combine_kernel(pos_ref, y_ref, x1_ref, route_ref, gt_ref, o_ref, buf0, buf1, sem, *, tm):
    def copy(r, slot, buf):
        return pltpu.make_async_copy(y_ref.at[pl.ds(pos_ref[0, slot, r], 1), :],
                                     buf.at[pl.ds(r, 1), :], sem)

    def issue(r, _):
        copy(r, 0, buf0).start()
        copy(r, 1, buf1).start()
        return 0

    lax.fori_loop(0, tm, issue, 0, unroll=DMA_UNROLL)
    for buf in (buf0, buf1):
        pltpu.make_async_copy(y_ref.at[pl.ds(0, tm), :], buf, sem).wait()
    w0 = route_ref[:, 2:3]
    w1 = route_ref[:, 3:4]
    y = buf0[...] * w0 + buf1[...] * w1
    o_ref[...] = x1_ref[...] + gt_ref[0] * y


def _combine(pos3, y, x1, route, gt2, tm, S):
    N, D = x1.shape
    per_b = S // tm
    return pl.pallas_call(
        functools.partial(_combine_kernel, tm=tm),
        grid=(N // tm,),
        in_specs=[pl.BlockSpec((1, 2, tm), lambda i: (i, 0, 0), memory_space=pltpu.SMEM),
                  pl.BlockSpec(memory_space=pl.ANY),
                  pl.BlockSpec((tm, D), lambda i: (i, 0)),
                  pl.BlockSpec((tm, LANES), lambda i: (i, 0)),
                  pl.BlockSpec((1, 1, D), lambda i: (i // per_b, 0, 0))],
        out_specs=pl.BlockSpec((tm, D), lambda i: (i, 0)),
        out_shape=jax.ShapeDtypeStruct((N, D), F32),
        scratch_shapes=[pltpu.VMEM((tm, D), F32), pltpu.VMEM((tm, D), F32), pltpu.SemaphoreType.DMA(())],
        compiler_params=_cparams(("arbitrary",)),
        name="moe_combine",
    )(pos3, y, x1, route, gt2)


SC_CORES = 2
SC_SUBCORES = 16
SC_WINDOW = 64


def _sc_row_gather(table, idx):
    M, = idx.shape
    D = table.shape[1]
    workers = SC_CORES * SC_SUBCORES
    per_worker = M // workers
    assert per_worker * workers == M and per_worker % SC_WINDOW == 0
    mesh = plsc.VectorSubcoreMesh(core_axis_name="c", subcore_axis_name="s",
                                  num_cores=SC_CORES, num_subcores=SC_SUBCORES)

    @functools.partial(
        pl.kernel, mesh=mesh,
        out_type=jax.ShapeDtypeStruct((M, D), table.dtype),
        scratch_types=[pltpu.VMEM((SC_WINDOW,), jnp.int32), pltpu.VMEM((SC_WINDOW, D), table.dtype),
                       pltpu.SemaphoreType.DMA],
        name="sc_row_gather")
    def gather(table_hbm, idx_hbm, out_hbm, idx_v, rows_v, sem):
        wid = lax.axis_index("s") * SC_CORES + lax.axis_index("c")

        def window(j, _):
            base = pl.multiple_of(wid * per_worker + j * SC_WINDOW, SC_WINDOW)
            pltpu.sync_copy(idx_hbm.at[pl.ds(base, SC_WINDOW)], idx_v)
            pltpu.async_copy(table_hbm.at[idx_v], rows_v, sem).wait()
            pltpu.sync_copy(rows_v, out_hbm.at[pl.ds(base, SC_WINDOW)])
            return 0

        lax.fori_loop(0, per_worker // SC_WINDOW, window, 0)

    return gather(table, idx)


def _combine_rows_kernel(y0_ref, y1_ref, x1_ref, route_ref, gt_ref, o_ref):
    y = _unpack_halves(y0_ref[...]) * route_ref[:, 2:3] + _unpack_halves(y1_ref[...]) * route_ref[:, 3:4]
    o_ref[...] = x1_ref[...] + gt_ref[0] * y


def _combine_rows(yg, x1, route, gt2, tm, S):
    N, D = x1.shape
    per_b = S // tm
    n_blocks = N // tm
    return pl.pallas_call(
        _combine_rows_kernel,
        grid=(n_blocks,),
        in_specs=[pl.BlockSpec((tm, D // 2), lambda i: (i, 0)),
                  pl.BlockSpec((tm, D // 2), lambda i: (i + n_blocks, 0)),
                  pl.BlockSpec((tm, D), lambda i: (i, 0)),
                  pl.BlockSpec((tm, LANES), lambda i: (i, 0)),
                  pl.BlockSpec((1, 1, D), lambda i: (i // per_b, 0, 0))],
        out_specs=pl.BlockSpec((tm, D), lambda i: (i, 0)),
        out_shape=jax.ShapeDtypeStruct((N, D), F32),
        compiler_params=_cparams(("parallel",)),
        name="moe_combine",
    )(yg, yg, x1, route, gt2)


def _rope_tables(positions):
    inv = ROPE_THETA ** (-jnp.arange(HALF, dtype=F32) / HALF)
    ang = positions.astype(F32)[:, None, :] * inv[None, :, None]
    return jnp.cos(ang), jnp.sin(ang)


def _block_diag_mean(width):
    blk = np.kron(np.eye(width // HEAD_DIM, dtype=np.float32), np.full((HEAD_DIM, HEAD_DIM), 1.0 / HEAD_DIM, np.float32))
    return jnp.asarray(blk, BF16)


def _layer(x, c_mod, positions, norm1_g, norm2_g, w_in, b_fgt, b_gate, qn_fox, kn_fox, qn_dsa, kn_dsa,
           w_proj_fox, w_proj_dsa, w_out, r_w_grp, r_b_grp, r_w_exp, r_b_exp, w1, w3, w2):
    B, S, D = x.shape
    N = B * S
    topk = min(TOPK_MAX, S // 4)
    tm = min(512, S)
    scale = HEAD_DIM ** -0.5
    mod3 = c_mod.reshape(B, 6, D)

    o = np.cumsum([0, 512, 512, 512, 8, 512, 64, 64, 512, 64, 8, D, D])
    seg = lambda k: w_in[:, o[k]:o[k + 1]]
    zpad = jnp.zeros((D, LANES - HEAD_DIM - 2 * N_HEADS), F32)
    w_tok = jnp.concatenate([seg(1), seg(10), seg(11)], axis=1).astype(BF16)
    w_t = jnp.concatenate([seg(0), seg(4), seg(7), seg(2),
                           seg(5), seg(8),
                           seg(6), seg(3), seg(9), zpad], axis=1).T.astype(BF16)
    gcol = jnp.stack([qn_fox * (scale * LOG2E), qn_dsa * (scale * LOG2E), kn_dsa]).reshape(3, HEAD_DIM, 1)
    cos_t, sin_t = _rope_tables(positions)

    k_heads, gates, fqt, dqt, iqt, fvt, kkt, dvt, logf_t, iwt = _in_projection(
        x, mod3, norm1_g.reshape(1, D), w_tok, w_t, _block_diag_mean(W_HEADS), cos_t, sin_t,
        jnp.tile(kn_fox, N_HEADS).reshape(1, W_HEADS), gcol, b_fgt.reshape(N_HEADS, 1),
        b_gate.reshape(1, 2 * D), tm)

    f_tok = jnp.transpose(_seq_cumsum(logf_t), (0, 2, 1)) * LOG2E
    of = _fox_attention(k_heads, f_tok, fqt, fvt)
    od = _dsa_attention(kkt, dvt, iqt, dqt, iwt, topk)

    wr = jnp.concatenate([r_w_grp, r_w_exp, jnp.zeros((D, LANES - N_GROUPS - N_EXPERTS), F32)], axis=1).astype(BF16)
    br = jnp.concatenate([r_b_grp, r_b_exp, jnp.zeros((LANES - N_GROUPS - N_EXPERTS,), F32)]).reshape(1, LANES)
    x1, h2, route = _post_attention(of, od, gates, x, mod3, w_proj_fox.astype(BF16), w_proj_dsa.astype(BF16),
                                    w_out.astype(BF16), norm2_g.reshape(1, D), wr, br, tm)
    x1, h2, route = x1.reshape(N, D), h2.reshape(N, D // 2), route.reshape(N, LANES)

    tg = 512 if N * 2 >= 512 * N_EXPERTS else 128
    ranks, counts = _expert_ranks(route, tm)
    counts = counts[0, :N_EXPERTS].astype(jnp.int32)
    padded = ((counts + tg - 1) // tg) * tg
    ends = jnp.cumsum(padded)
    starts = ends - padded
    e01 = jnp.transpose(route[:, :2]).astype(jnp.int32)
    start_of = jnp.sum(jnp.where(e01[..., None] == jnp.arange(N_EXPERTS, dtype=jnp.int32), starts, 0), axis=-1)
    pos = start_of + jnp.transpose(ranks[:, :2]).astype(jnp.int32)
    n_rows = N * 2 + N_EXPERTS * tg
    n_tiles = n_rows // tg
    tile_start = jnp.arange(n_tiles, dtype=jnp.int32) * tg
    tile_expert = jnp.minimum(jnp.sum((ends[None, :] <= tile_start[:, None]).astype(jnp.int32), axis=1),
                              N_EXPERTS - 1)
    n_used = (ends[-1] // tg).astype(jnp.int32).reshape(1)

    td = min(256, S)
    pos3 = jnp.transpose(pos.reshape(2, N // td, td), (1, 0, 2))
    xs = _dispatch(h2, pos3, jnp.maximum(ends - tg, 0).astype(jnp.int32), (padded > 0).astype(jnp.int32),
                   n_used, n_rows, td, tg)
    y = _experts(tile_expert, n_used, xs, w1, w3, w2, tg)
    yg = _sc_row_gather(y, pos.reshape(2 * N))
    out = _combine_rows(yg, x1, route, mod3[:, 5:6, :], min(512, S), S)
    return out.reshape(B, S, D)


def kernel(x, c, positions, ada_w, ada_b, norm1_g, norm2_g, w_in, b_fgt, b_gate, qn_fox, kn_fox, qn_dsa, kn_dsa, w_proj_fox, w_proj_dsa, w_out, router_w_grp, router_b_grp, router_w_exp, router_b_exp, exp_w1, exp_w3, exp_w2):
    for l in range(ada_w.shape[0]):
        c_mod = _modulation(c, ada_w[l], ada_b[l])
        x = _layer(x, c_mod, positions, norm1_g[l], norm2_g[l], w_in[l], b_fgt[l], b_gate[l],
                   qn_fox[l], kn_fox[l], qn_dsa[l], kn_dsa[l], w_proj_fox[l], w_proj_dsa[l], w_out[l],
                   router_w_grp[l], router_b_grp[l], router_w_exp[l], router_b_exp[l],
                   exp_w1[l], exp_w3[l], exp_w2[l])
    return x
```

```python
import functools

import jax
import jax.numpy as jnp
import numpy as np
from jax import lax
from jax.experimental import pallas as pl
from jax.experimental.pallas import tpu as pltpu
from jax.experimental.pallas import tpu_sc as plsc

F32 = jnp.float32
BF16 = jnp.bfloat16

CHUNK = 64
CHUNK_SHIFT = 6
DMA_UNROLL = 8
HEAD_DIM = 64
N_HEADS = 8
W_HEADS = N_HEADS * HEAD_DIM
TOPK_MAX = 256
ROPE_THETA = 10000.0
N_GROUPS = 4
EXPERTS_PER_GROUP = 8
N_EXPERTS = N_GROUPS * EXPERTS_PER_GROUP
EPS = 1e-6
LOG2E = 1.4426950408889634
MASKED = -1e30

LANES = 128
VMEM_LIMIT = 56 * 1024 * 1024


def _cparams(sem):
    return pltpu.CompilerParams(dimension_semantics=sem, vmem_limit_bytes=VMEM_LIMIT)


def _mod_kernel(c_ref, w_ref, b_ref, o_ref):
    c = c_ref[...]
    ca = (c * jax.nn.sigmoid(c)).astype(BF16)
    o_ref[...] = jnp.dot(ca, w_ref[...].astype(BF16), preferred_element_type=F32) + b_ref[...]


def _modulation(c, ada_w, ada_b):
    B, D = c.shape
    n = ada_w.shape[1] // D
    return pl.pallas_call(
        _mod_kernel,
        grid=(n,),
        in_specs=[pl.BlockSpec((B, D), lambda j: (0, 0)),
                  pl.BlockSpec((D, D), lambda j: (0, j)),
                  pl.BlockSpec((1, D), lambda j: (0, j))],
        out_specs=pl.BlockSpec((B, D), lambda j: (0, j)),
        out_shape=jax.ShapeDtypeStruct((B, n * D), F32),
        compiler_params=_cparams(("arbitrary",)),
        name="adaln_mod",
    )(c, ada_w, ada_b.reshape(1, -1))


R_FQ, R_DQ, R_IQ, R_FV = 0, 512, 1024, 1536
R_KK = 2048
R_S2 = 2176
R_END = 2304
HALF = HEAD_DIM // 2


def _inproj_kernel(x_ref, mod_ref, g1_ref, wtok_ref, wt_ref, bd512_ref, cos_ref, sin_ref,
                   gk_ref, gcol_ref, bf_ref, bg_ref,
                   fk_ref, gate_ref, fqt_ref, dqt_ref, iqt_ref, fvt_ref, kkt_ref, dvt_ref, lf_ref, iwt_ref):
    x = x_ref[0]
    ms = jnp.mean(x * x, axis=-1, keepdims=True)
    h = x * lax.rsqrt(ms + EPS) * g1_ref[...]
    h = h * (1.0 + mod_ref[0, 1:2, :]) + mod_ref[0, 0:1, :]
    hb = h.astype(BF16)
    D = x.shape[-1]
    cos, sin = cos_ref[0], sin_ref[0]

    def proj_t(lo, hi):
        return lax.dot_general(wt_ref[lo:hi, :], hb, (((1,), (1,)), ((), ())), preferred_element_type=F32)

    def norm_t(yh, gain):
        msq = jnp.mean(yh * yh, axis=0, keepdims=True)
        return yh * lax.rsqrt(msq + EPS) * gain

    def rope_store(ref, lo, yh):
        x1, x2 = yh[:HALF], yh[HALF:]
        ref[0, lo:lo + HALF, :] = (x1 * cos - x2 * sin).astype(ref.dtype)
        ref[0, lo + HALF:lo + HEAD_DIM, :] = (x2 * cos + x1 * sin).astype(ref.dtype)

    fq = proj_t(R_FQ, R_FQ + W_HEADS)
    dq = proj_t(R_DQ, R_DQ + W_HEADS)
    iq = proj_t(R_IQ, R_IQ + W_HEADS)
    for hh in range(N_HEADS):
        lo = hh * HEAD_DIM
        fqt_ref[0, lo:lo + HEAD_DIM, :] = norm_t(fq[lo:lo + HEAD_DIM], gcol_ref[0]).astype(BF16)
        rope_store(dqt_ref, lo, norm_t(dq[lo:lo + HEAD_DIM], gcol_ref[1]))
        rope_store(iqt_ref, lo, iq[lo:lo + HEAD_DIM])
    fvt_ref[0] = proj_t(R_FV, R_FV + W_HEADS).astype(BF16)

    kk = proj_t(R_KK, R_KK + 2 * HEAD_DIM)
    rope_store(kkt_ref, 0, norm_t(kk[:HEAD_DIM], gcol_ref[2]))
    rope_store(kkt_ref, HEAD_DIM, kk[HEAD_DIM:])

    s2 = proj_t(R_S2, R_S2 + LANES)
    dvt_ref[0] = s2[:HEAD_DIM].astype(BF16)
    z = s2[HEAD_DIM:HEAD_DIM + N_HEADS] + bf_ref[...]
    lf_ref[0] = jnp.minimum(z, 0.0) - jnp.log(1.0 + jnp.exp(-jnp.abs(z)))
    iwt_ref[0] = s2[HEAD_DIM + N_HEADS:HEAD_DIM + 2 * N_HEADS]

    fk = jnp.dot(hb, wtok_ref[:, :W_HEADS], preferred_element_type=F32)
    msq = jnp.dot((fk * fk).astype(BF16), bd512_ref[...], preferred_element_type=F32)
    fk = (fk * lax.rsqrt(msq + EPS) * gk_ref[...]).astype(BF16)
    for hh in range(N_HEADS):
        fk_ref[0, hh] = fk[:, hh * HEAD_DIM:(hh + 1) * HEAD_DIM]
    g = jnp.dot(hb, wtok_ref[:, W_HEADS:], preferred_element_type=F32)
    gate_ref[0] = jax.nn.sigmoid(g + bg_ref[...]).astype(BF16)


def _in_projection(x, mod3, norm1_g, w_tok, w_t, bd512, cos_t, sin_t, gk, gcol, bf, bg, tm):
    B, S, D = x.shape
    tok = lambda w: pl.BlockSpec((1, tm, w), lambda b, i: (b, i, 0))
    feat = lambda r: pl.BlockSpec((1, r, tm), lambda b, i: (b, 0, i))
    const = lambda shape: pl.BlockSpec(shape, lambda b, i: (0,) * len(shape))
    out_shapes = [jax.ShapeDtypeStruct((B, N_HEADS, S, HEAD_DIM), BF16),
                  jax.ShapeDtypeStruct((B, S, 2 * D), BF16)] + \
                 [jax.ShapeDtypeStruct((B, W_HEADS, S), BF16)] * 4 + \
                 [jax.ShapeDtypeStruct((B, 2 * HEAD_DIM, S), BF16),
                  jax.ShapeDtypeStruct((B, HEAD_DIM, S), BF16),
                  jax.ShapeDtypeStruct((B, N_HEADS, S), F32),
                  jax.ShapeDtypeStruct((B, N_HEADS, S), F32)]
    return pl.pallas_call(
        _inproj_kernel,
        grid=(B, S // tm),
        in_specs=[tok(D),
                  pl.BlockSpec((1, 6, D), lambda b, i: (b, 0, 0)),
                  const((1, D)),
                  const(w_tok.shape), const(w_t.shape), const((W_HEADS, W_HEADS)),
                  feat(HALF), feat(HALF),
                  const((1, W_HEADS)), const((3, HEAD_DIM, 1)), const((N_HEADS, 1)), const((1, 2 * D))],
        out_specs=[pl.BlockSpec((1, N_HEADS, tm, HEAD_DIM), lambda b, i: (b, 0, i, 0)), tok(2 * D),
                   feat(W_HEADS), feat(W_HEADS), feat(W_HEADS), feat(W_HEADS),
                   feat(2 * HEAD_DIM), feat(HEAD_DIM), feat(N_HEADS), feat(N_HEADS)],
        out_shape=out_shapes,
        compiler_params=_cparams(("parallel", "parallel")),
        name="in_projection",
    )(x, mod3, norm1_g, w_tok, w_t, bd512, cos_t, sin_t, gk, gcol, bf, bg)


def _cumsum_kernel(x_ref, o_ref):
    x = x_ref[0]
    n = x.shape[-1]
    pos = lax.broadcasted_iota(jnp.int32, x.shape, 1)
    shift = 1
    while shift < n:
        x = x + jnp.where(pos >= shift, pltpu.roll(x, shift, 1), 0.0)
        shift *= 2
    o_ref[0] = x


def _seq_cumsum(logf_t):
    B, H, S = logf_t.shape
    return pl.pallas_call(
        _cumsum_kernel,
        grid=(B,),
        in_specs=[pl.BlockSpec((1, H, S), lambda b: (b, 0, 0))],
        out_specs=pl.BlockSpec((1, H, S), lambda b: (b, 0, 0)),
        out_shape=jax.ShapeDtypeStruct((B, H, S), F32),
        compiler_params=_cparams(("parallel",)),
        name="forget_cumsum",
    )(logf_t)


KC = 256
SUB = 8


def _fold_rows(a, op, ways=1):
    n = a.shape[0] // SUB
    a = a.reshape(n, SUB, a.shape[1])
    chains = [a[w] for w in range(ways)]
    for j in range(ways, n):
        chains[j % ways] = op(chains[j % ways], a[j])
    while len(chains) > 1:
        chains = [op(chains[2 * j], chains[2 * j + 1]) for j in range(len(chains) // 2)]
    return chains[0]


def _softmax_pv(nch, s_ref, acc_ref, vt_at, m_all, o_ref):
    Q = o_ref.shape[1]
    acc_ref[...] = jnp.zeros_like(acc_ref)

    def body(c, lsum):
        off = pl.multiple_of(c * KC, KC)
        new = []
        for hh in range(N_HEADS):
            p = jnp.exp2(s_ref[hh, pl.ds(off, KC), :] - m_all[hh])
            new.append(lsum[hh] + _fold_rows(p, jnp.add))
            acc_ref[hh] += jnp.dot(vt_at(hh, off), p.astype(BF16), preferred_element_type=F32)
        return tuple(new)

    lsum = lax.fori_loop(0, nch, body, tuple(jnp.zeros((SUB, Q), F32) for _ in range(N_HEADS)))
    for hh in range(N_HEADS):
        acc_ref[hh] = acc_ref[hh] / jnp.sum(lsum[hh], axis=0, keepdims=True)
    out_t = acc_ref[...].reshape(N_HEADS * HEAD_DIM, Q)
    o_ref[0] = out_t.T.astype(BF16)


def _fox_kernel(k_ref, f_ref, qt_ref, vt_ref, o_ref, s_ref, acc_ref):
    i = pl.program_id(1)
    Q = o_ref.shape[1]
    qts = [qt_ref[0, hh * HEAD_DIM:(hh + 1) * HEAD_DIM, :] for hh in range(N_HEADS)]

    def scores(c, mx, bias):
        off = pl.multiple_of(c * KC, KC)
        new = []
        for hh in range(N_HEADS):
            s = jnp.dot(k_ref[0, hh, pl.ds(off, KC), :], qts[hh], preferred_element_type=F32)
            s = s - f_ref[0, pl.ds(off, KC), hh:hh + 1]
            if bias is not None:
                s = s + bias
            s_ref[hh, pl.ds(off, KC), :] = s
            new.append(jnp.maximum(mx[hh], _fold_rows(s, jnp.maximum)))
        return tuple(new)

    mx = tuple(jnp.full((SUB, Q), MASKED, F32) for _ in range(N_HEADS))
    mx = lax.fori_loop(0, i, lambda c, m: scores(c, m, None), mx)
    kk = lax.broadcasted_iota(jnp.int32, (KC, Q), 0)
    qq = lax.broadcasted_iota(jnp.int32, (KC, Q), 1)
    mx = scores(i, mx, jnp.where(kk <= qq, 0.0, MASKED))
    m_all = [jnp.max(m, axis=0, keepdims=True) for m in mx]
    _softmax_pv(i + 1, s_ref, acc_ref, lambda hh, off: vt_ref[0, hh * HEAD_DIM:(hh + 1) * HEAD_DIM, pl.ds(off, KC)],
                m_all, o_ref)


def _fox_attention(k_heads, f_tok, qt, vt):
    B, H, S, Dh = k_heads.shape
    return pl.pallas_call(
        _fox_kernel,
        grid=(B, S // KC),
        in_specs=[pl.BlockSpec((1, H, S, Dh), lambda b, i: (b, 0, 0, 0)),
                  pl.BlockSpec((1, S, H), lambda b, i: (b, 0, 0)),
                  pl.BlockSpec((1, W_HEADS, KC), lambda b, i: (b, 0, i)),
                  pl.BlockSpec((1, W_HEADS, S), lambda b, i: (b, 0, 0))],
        out_specs=pl.BlockSpec((1, KC, W_HEADS), lambda b, i: (b, i, 0)),
        out_shape=jax.ShapeDtypeStruct((B, S, W_HEADS), BF16),
        scratch_shapes=[pltpu.VMEM((H, S, KC), F32), pltpu.VMEM((H, HEAD_DIM, KC), F32)],
        compiler_params=_cparams(("parallel", "arbitrary")),
        name="fox_attention",
    )(k_heads, f_tok, qt, vt)


INT_MIN = -(2 ** 31)
KEY_NEG_INF = INT_MIN + 0x7FFFFF
HI16 = -(2 ** 16)
PACK = 16


def _dsa_kernel(kkt_ref, dvt_ref, iqt_ref, dqt_ref, iwt_ref, o_ref, key_ref, hi_ref, s_ref, acc_ref, dk_ref, ik_ref,
                *, topk):
    i = pl.program_id(1)
    Q = o_ref.shape[1]
    nch = i + 1

    @pl.when(i == 0)
    def _():
        def to_rows(c, _):
            off = pl.multiple_of(c * KC, KC)
            rows = kkt_ref[0, :, pl.ds(off, KC)].astype(F32).T
            dk_ref[pl.ds(off, KC), :] = rows[:, :HEAD_DIM].astype(BF16)
            ik_ref[pl.ds(off, KC), :] = rows[:, HEAD_DIM:].astype(BF16)
            return 0
        lax.fori_loop(0, kkt_ref.shape[-1] // KC, to_rows, 0)

    sub_k = lax.broadcasted_iota(jnp.int32, (KC, Q), 0)
    sub_r = lax.broadcasted_iota(jnp.int32, (CHUNK, Q), 0)
    q_chunk = (i * Q + lax.broadcasted_iota(jnp.int32, (CHUNK, Q), 1)) >> CHUNK_SHIFT
    iqts = [iqt_ref[0, hh * HEAD_DIM:(hh + 1) * HEAD_DIM, :] for hh in range(N_HEADS)]
    iws = [iwt_ref[0, hh:hh + 1, :] for hh in range(N_HEADS)]

    def score_chunk(c, _):
        for r in range(KC // CHUNK):
            off = pl.multiple_of(c * KC + r * CHUNK, CHUNK)
            ik = ik_ref[pl.ds(off, CHUNK), :]
            sc = jnp.zeros((CHUNK, Q), F32)
            for hh in range(N_HEADS):
                d = jnp.dot(ik, iqts[hh], preferred_element_type=F32)
                sc = sc + iws[hh] * jnp.maximum(d, 0.0)
            sc = sc + 0.0
            allowed = ((off + sub_r) >> CHUNK_SHIFT) <= q_chunk
            bits = pltpu.bitcast(jnp.where(allowed, sc, -jnp.inf), jnp.int32)
            key = bits ^ ((bits >> 31) & 0x7FFFFFFF)
            key_ref[pl.ds(off, CHUNK), :] = key
            hi_ref[pl.ds(off, CHUNK), :] = (key >> 16).astype(jnp.int16)
        return 0

    lax.fori_loop(0, nch, score_chunk, 0)

    def sweep(n, body, init):
        if isinstance(n, int):
            acc = init
            for c in range(n):
                acc = body(c * KC, acc)
            return acc
        return lax.fori_loop(0, n, lambda c, acc: body(pl.multiple_of(c * KC, KC), acc), init)

    def count(pred, n=nch):
        def body(off, acc):
            hit = pred(key_ref[pl.ds(off, KC), :], off + sub_k)
            return acc + _fold_rows(jnp.where(hit, 1.0, 0.0), jnp.add, ways=4)
        return jnp.sum(sweep(n, body, jnp.zeros((SUB, Q), F32)), axis=0, keepdims=True)

    one, zero = jnp.ones((), BF16), jnp.zeros((), BF16)

    def count_hi(cand, n):
        c16 = jnp.broadcast_to(cand >> 16, (PACK, Q)).astype(jnp.int16)

        def body(off, acc):
            kb = hi_ref[pl.ds(off, KC), :].reshape(KC // PACK, PACK, Q)
            hit = jnp.where(kb >= c16[None], one, zero)
            parts = [hit[w] for w in range(4)]
            for j in range(4, KC // PACK):
                parts[j % 4] = parts[j % 4] + hit[j]
            return acc + ((parts[0] + parts[1]) + (parts[2] + parts[3]))
        acc = sweep(n, body, jnp.zeros((PACK, Q), BF16))
        return jnp.sum(acc.astype(F32), axis=0, keepdims=True)

    assert key_ref.shape[0] // PACK <= 256
    kf = jnp.float32(topk)

    def descent(n):
        n_nonneg = count_hi(jnp.zeros((1, Q), jnp.int32), n)
        top_half = n_nonneg >= kf
        thr = jnp.where(top_half, 0, INT_MIN).astype(jnp.int32)
        n_ge = jnp.where(top_half, n_nonneg, jnp.float32(n * KC))

        def descend(counter, top_bit):
            def step(j, carry):
                thr, n_ge = carry
                cand = thr + (jnp.int32(1) << (top_bit - j))
                cnt = counter(cand)
                take = cnt >= kf
                return jnp.where(take, cand, thr), jnp.where(take, cnt, n_ge)
            return step

        carry = lax.fori_loop(0, 15, descend(lambda c: count_hi(c, n), 30), (thr, n_ge))
        return lax.fori_loop(0, 16, descend(lambda c: count(lambda k, _: k >= c, n), 15), carry)

    thr, n_ge = lax.switch(i, [functools.partial(descent, n) for n in range(1, key_ref.shape[0] // KC + 1)])

    excess = (n_ge > kf) & (thr > KEY_NEG_INF)

    @pl.when(jnp.max(jnp.where(excess, 1.0, 0.0)) > 0.0)
    def _():
        need = kf - count(lambda k, _: k > thr)
        nbits = int(np.ceil(np.log2(key_ref.shape[0]))) + 1

        def bound(j, last):
            cand = last + (jnp.int32(1) << (nbits - 1 - j))
            n = count(lambda k, idx: (k == thr) & (idx < cand))
            return jnp.where(n < need, cand, last)

        last = lax.fori_loop(0, nbits, bound, jnp.zeros((1, Q), jnp.int32))

        def demote(c, _):
            off = pl.multiple_of(c * KC, KC)
            k = key_ref[pl.ds(off, KC), :]
            drop = excess & (k == thr) & (off + sub_k > last)
            key_ref[pl.ds(off, KC), :] = jnp.where(drop, thr - 1, k)
            return 0

        lax.fori_loop(0, nch, demote, 0)

    keep_from = jnp.maximum(thr, KEY_NEG_INF + 1)

    qts = [dqt_ref[0, hh * HEAD_DIM:(hh + 1) * HEAD_DIM, :] for hh in range(N_HEADS)]

    def scores(c, mx):
        off = pl.multiple_of(c * KC, KC)
        bias = jnp.where(key_ref[pl.ds(off, KC), :] >= keep_from, 0.0, MASKED)
        dk = dk_ref[pl.ds(off, KC), :]
        new = []
        for hh in range(N_HEADS):
            s = jnp.dot(dk, qts[hh], preferred_element_type=F32) + bias
            s_ref[hh, pl.ds(off, KC), :] = s
            new.append(jnp.maximum(mx[hh], _fold_rows(s, jnp.maximum)))
        return tuple(new)

    mx = lax.fori_loop(0, nch, scores, tuple(jnp.full((SUB, Q), MASKED, F32) for _ in range(N_HEADS)))
    m_all = [jnp.max(m, axis=0, keepdims=True) for m in mx]
    _softmax_pv(nch, s_ref, acc_ref, lambda hh, off: dvt_ref[0, :, pl.ds(off, KC)], m_all, o_ref)


def _dsa_attention(kkt, dvt, iqt, dqt, iwt, topk):
    B, Dh, S = dvt.shape
    rows = lambda r: pl.BlockSpec((1, r, S), lambda b, i: (b, 0, 0))
    qcols = lambda r: pl.BlockSpec((1, r, KC), lambda b, i: (b, 0, i))
    return pl.pallas_call(
        functools.partial(_dsa_kernel, topk=topk),
        grid=(B, S // KC),
        in_specs=[rows(2 * Dh), rows(Dh), qcols(W_HEADS), qcols(W_HEADS), qcols(N_HEADS)],
        out_specs=pl.BlockSpec((1, KC, W_HEADS), lambda b, i: (b, i, 0)),
        out_shape=jax.ShapeDtypeStruct((B, S, W_HEADS), BF16),
        scratch_shapes=[pltpu.VMEM((S, KC), jnp.int32), pltpu.VMEM((S, KC), jnp.int16),
                        pltpu.VMEM((N_HEADS, S, KC), F32), pltpu.VMEM((N_HEADS, HEAD_DIM, KC), F32),
                        pltpu.VMEM((S, Dh), BF16), pltpu.VMEM((S, Dh), BF16)],
        compiler_params=_cparams(("parallel", "arbitrary")),
        name="dsa_attention",
    )(kkt, dvt, iqt, dqt, iwt)


PACKED = jnp.int32


def _pack_halves(a):
    half = a.shape[-1] // 2
    rounded = a.astype(BF16).astype(F32)
    lo = pltpu.bitcast(rounded[:, :half], jnp.int32)
    hi = pltpu.bitcast(rounded[:, half:], jnp.int32)
    return hi | ((lo >> 16) & 0xFFFF)


def _unpack_halves(p):
    lo = pltpu.bitcast(p << 16, F32)
    hi = pltpu.bitcast(p & HI16, F32)
    return jnp.concatenate([lo, hi], axis=-1)


def _first(mask, lane):
    return jnp.min(jnp.where(mask, lane, LANES), axis=-1, keepdims=True)


def _post_kernel(of_ref, od_ref, gate_ref, x_ref, mod_ref, wpf_ref, wpd_ref, wo_ref, g2_ref, wr_ref, br_ref,
                 x1_ref, h2_ref, route_ref):
    D = x_ref.shape[-1]
    pf = jnp.dot(of_ref[0], wpf_ref[...], preferred_element_type=F32)
    pd = jnp.dot(od_ref[0], wpd_ref[...], preferred_element_type=F32)
    merged = gate_ref[0, :, :D].astype(F32) * pf + gate_ref[0, :, D:].astype(F32) * pd
    y = jnp.dot(merged.astype(BF16), wo_ref[...], preferred_element_type=F32)
    x1 = x_ref[0] + mod_ref[0, 2:3, :] * y
    x1_ref[0] = x1

    ms = jnp.mean(x1 * x1, axis=-1, keepdims=True)
    h2 = x1 * lax.rsqrt(ms + EPS) * g2_ref[...]
    h2 = h2 * (1.0 + mod_ref[0, 4:5, :]) + mod_ref[0, 3:4, :]
    hb = h2.astype(BF16)
    h2_ref[0] = _pack_halves(h2)

    logits = jnp.dot(hb, wr_ref[...], preferred_element_type=F32) + br_ref[...]
    lane = lax.broadcasted_iota(jnp.int32, logits.shape, 1)
    is_grp = lane < N_GROUPS
    gl = jnp.where(is_grp, logits, -jnp.inf)
    gmax = jnp.max(gl, axis=-1, keepdims=True)
    g_idx = _first(gl == gmax, lane)
    g_w = 1.0 / jnp.sum(jnp.exp(gl - gmax), axis=-1, keepdims=True)

    e_lo = N_GROUPS + g_idx * EXPERTS_PER_GROUP
    in_grp = (lane >= e_lo) & (lane < e_lo + EXPERTS_PER_GROUP)
    el = jnp.where(in_grp, logits, -jnp.inf)
    emax = jnp.max(el, axis=-1, keepdims=True)
    ee = jnp.exp(el - emax)
    prob = ee / jnp.sum(ee, axis=-1, keepdims=True)
    prob = jnp.where(in_grp, prob, -1.0)
    p0 = jnp.max(prob, axis=-1, keepdims=True)
    l0 = _first(prob == p0, lane)
    rest = jnp.where(lane == l0, -1.0, prob)
    p1 = jnp.max(rest, axis=-1, keepdims=True)
    l1 = _first(rest == p1, lane)
    psum = p0 + p1
    w0 = g_w * (p0 / psum)
    w1 = g_w * (p1 / psum)
    e0 = (l0 - N_GROUPS).astype(F32)
    e1 = (l1 - N_GROUPS).astype(F32)
    route_ref[0] = jnp.where(lane == 0, e0, jnp.where(lane == 1, e1, jnp.where(lane == 2, w0,
                             jnp.where(lane == 3, w1, 0.0))))


def _post_attention(of, od, gates, x, mod3, wpf, wpd, wo, g2, wr, br, tm):
    B, S, D = x.shape
    tok = lambda w: pl.BlockSpec((1, tm, w), lambda b, i: (b, i, 0))
    const = lambda shape: pl.BlockSpec(shape, lambda b, i: (0,) * len(shape))
    return pl.pallas_call(
        _post_kernel,
        grid=(B, S // tm),
        in_specs=[tok(W_HEADS), tok(W_HEADS), tok(2 * D), tok(D),
                  pl.BlockSpec((1, 6, D), lambda b, i: (b, 0, 0)),
                  const(wpf.shape), const(wpd.shape), const(wo.shape),
                  const((1, D)), const((D, LANES)), const((1, LANES))],
        out_specs=[tok(D), tok(D // 2), tok(LANES)],
        out_shape=[jax.ShapeDtypeStruct((B, S, D), F32),
                   jax.ShapeDtypeStruct((B, S, D // 2), PACKED),
                   jax.ShapeDtypeStruct((B, S, LANES), F32)],
        compiler_params=_cparams(("parallel", "parallel")),
        name="merge_out_router",
    )(of, od, gates, x, mod3, wpf, wpd, wo, g2, wr, br)


def _rank_kernel(route_ref, tri_ref, rank_ref, count_ref, carry_ref):
    @pl.when(pl.program_id(0) == 0)
    def _():
        carry_ref[...] = jnp.zeros_like(carry_ref)

    r = route_ref[...]
    lane = lax.broadcasted_iota(jnp.int32, r.shape, 1).astype(F32)
    hot0 = lane == r[:, 0:1]
    hot1 = lane == r[:, 1:2]
    hits = jnp.where(hot0 | hot1, 1.0, 0.0)
    incl = jnp.dot(tri_ref[...], hits.astype(BF16), preferred_element_type=F32)
    before = incl - hits + carry_ref[...]
    r0 = jnp.sum(jnp.where(hot0, before, 0.0), axis=-1, keepdims=True)
    r1 = jnp.sum(jnp.where(hot1, before, 0.0), axis=-1, keepdims=True)
    rank_ref[...] = jnp.where(lane == 0.0, r0, jnp.where(lane == 1.0, r1, 0.0))
    carry_ref[...] = carry_ref[...] + jnp.sum(hits, axis=0, keepdims=True)
    count_ref[...] = carry_ref[...]


def _expert_ranks(route, tm):
    N = route.shape[0]
    tri = jnp.asarray(np.tril(np.ones((tm, tm), np.float32)), BF16)
    return pl.pallas_call(
        _rank_kernel,
        grid=(N // tm,),
        in_specs=[pl.BlockSpec((tm, LANES), lambda i: (i, 0)),
                  pl.BlockSpec((tm, tm), lambda i: (0, 0))],
        out_specs=[pl.BlockSpec((tm, LANES), lambda i: (i, 0)),
                   pl.BlockSpec((1, LANES), lambda i: (0, 0))],
        out_shape=[jax.ShapeDtypeStruct((N, LANES), F32), jax.ShapeDtypeStruct((1, LANES), F32)],
        scratch_shapes=[pltpu.VMEM((1, LANES), F32)],
        compiler_params=_cparams(("arbitrary",)),
        name="expert_ranks",
    )(route, tri)


def _dispatch_kernel(zstart_ref, zon_ref, nt_ref, pos_ref, h_ref, xs_ref, zbuf, sem, zsem, *, tm, tg):
    @pl.when(pl.program_id(0) == 0)
    def _():
        zbuf[...] = jnp.zeros_like(zbuf)

        def zero_tile(start):
            return pltpu.make_async_copy(zbuf, xs_ref.at[pl.ds(pl.multiple_of(start, tg), tg), :], zsem)

        n_tiles = xs_ref.shape[0] // tg
        for e in range(N_EXPERTS):
            pl.when(zon_ref[e] > 0)(lambda e=e: zero_tile(zstart_ref[e]).start())
        lax.fori_loop(nt_ref[0], n_tiles, lambda t, _: (zero_tile(t * tg).start(), 0)[1], 0)
        for e in range(N_EXPERTS):
            pl.when(zon_ref[e] > 0)(lambda e=e: zero_tile(zstart_ref[e]).wait())
        lax.fori_loop(nt_ref[0], n_tiles, lambda t, _: (zero_tile(t * tg).wait(), 0)[1], 0)

    def copy(r, slot):
        return pltpu.make_async_copy(h_ref.at[pl.ds(r, 1), :],
                                     xs_ref.at[pl.ds(pos_ref[0, slot, r], 1), :], sem)

    def issue(r, _):
        copy(r, 0).start()
        copy(r, 1).start()
        return 0

    lax.fori_loop(0, tm, issue, 0, unroll=DMA_UNROLL)
    for _ in range(2):
        pltpu.make_async_copy(h_ref, xs_ref.at[pl.ds(0, tm), :], sem).wait()


def _dispatch(h2, pos3, last_tile_start, has_rows, n_tiles_used, n_rows, tm, tg):
    N, D = h2.shape
    grid_spec = pltpu.PrefetchScalarGridSpec(
        num_scalar_prefetch=3,
        grid=(N // tm,),
        in_specs=[pl.BlockSpec((1, 2, tm), lambda i, zs, zo, nt: (i, 0, 0), memory_space=pltpu.SMEM),
                  pl.BlockSpec((tm, D), lambda i, zs, zo, nt: (i, 0))],
        out_specs=pl.BlockSpec(memory_space=pl.ANY),
        scratch_shapes=[pltpu.VMEM((tg, D), h2.dtype), pltpu.SemaphoreType.DMA(()), pltpu.SemaphoreType.DMA(())],
    )
    return pl.pallas_call(
        functools.partial(_dispatch_kernel, tm=tm, tg=tg),
        grid_spec=grid_spec,
        out_shape=jax.ShapeDtypeStruct((n_rows, D), h2.dtype),
        compiler_params=_cparams(("arbitrary",)),
        name="moe_dispatch",
    )(last_tile_start, has_rows, n_tiles_used, pos3, h2)


def _expert_kernel(te_ref, nt_ref, nv_ref, xs_ref, w1_ref, w3_ref, w2_ref, y_ref, w1b, w3b, w2b):
    g = pl.program_id(0)
    used = g < nt_ref[0]
    new_expert = (g == 0) | (te_ref[g] != te_ref[jnp.maximum(g - 1, 0)])

    @pl.when(used & new_expert)
    def _():
        w1b[...] = w1_ref[0].astype(BF16)
        w3b[...] = w3_ref[0].astype(BF16)
        w2b[...] = w2_ref[0].astype(BF16)

    @pl.when(used)
    def _():
        row = lax.broadcasted_iota(jnp.int32, xs_ref.shape, 0)
        xb = _unpack_halves(jnp.where(row < nv_ref[g], xs_ref[...], 0)).astype(BF16)
        a = jnp.dot(xb, w1b[...], preferred_element_type=F32)
        b = jnp.dot(xb, w3b[...], preferred_element_type=F32)
        hmid = (a * jax.nn.sigmoid(a) * b).astype(BF16)
        y_ref[...] = _pack_halves(jnp.dot(hmid, w2b[...], preferred_element_type=F32))

    @pl.when(jnp.logical_not(used))
    def _():
        y_ref[...] = jnp.zeros_like(y_ref)


def _experts(tile_expert, n_tiles_used, rows_valid, xs, w1, w3, w2, tg):
    P, Dp = xs.shape
    E, D, De = w1.shape
    row_tile = lambda g, te, nt, nv: (jnp.minimum(g, nt[0] - 1), 0)
    grid_spec = pltpu.PrefetchScalarGridSpec(
        num_scalar_prefetch=3,
        grid=(P // tg,),
        in_specs=[pl.BlockSpec((tg, Dp), row_tile),
                  pl.BlockSpec((1, D, De), lambda g, te, nt, nv: (te[g], 0, 0)),
                  pl.BlockSpec((1, D, De), lambda g, te, nt, nv: (te[g], 0, 0)),
                  pl.BlockSpec((1, De, D), lambda g, te, nt, nv: (te[g], 0, 0))],
        out_specs=pl.BlockSpec((tg, Dp), lambda g, te, nt, nv: (g, 0)),
        scratch_shapes=[pltpu.VMEM((D, De), BF16), pltpu.VMEM((D, De), BF16), pltpu.VMEM((De, D), BF16)],
    )
    return pl.pallas_call(
        _expert_kernel,
        grid_spec=grid_spec,
        out_shape=jax.ShapeDtypeStruct((P, Dp), PACKED),
        compiler_params=_cparams(("arbitrary",)),
        name="moe_experts",
    )(tile_expert, n_tiles_used, rows_valid, xs, w1, w3, w2)


def _combine_kernel(pos_ref, y_ref, x1_ref, route_ref, gt_ref, o_ref, buf0, buf1, sem, *, tm):
    def copy(r, slot, buf):
        return pltpu.make_async_copy(y_ref.at[pl.ds(pos_ref[0, slot, r], 1), :],
                                     buf.at[pl.ds(r, 1), :], sem)

    def issue(r, _):
        copy(r, 0, buf0).start()
        copy(r, 1, buf1).start()
        return 0

    lax.fori_loop(0, tm, issue, 0, unroll=DMA_UNROLL)
    for buf in (buf0, buf1):
        pltpu.make_async_copy(y_ref.at[pl.ds(0, tm), :], buf, sem).wait()
    w0 = route_ref[:, 2:3]
    w1 = route_ref[:, 3:4]
    y = buf0[...] * w0 + buf1[...] * w1
    o_ref[...] = x1_ref[...] + gt_ref[0] * y


def _combine(pos3, y, x1, route, gt2, tm, S):
    N, D = x1.shape
    per_b = S // tm
    return pl.pallas_call(
        functools.partial(_combine_kernel, tm=tm),
        grid=(N // tm,),
        in_specs=[pl.BlockSpec((1, 2, tm), lambda i: (i, 0, 0), memory_space=pltpu.SMEM),
                  pl.BlockSpec(memory_space=pl.ANY),
                  pl.BlockSpec((tm, D), lambda i: (i, 0)),
                  pl.BlockSpec((tm, LANES), lambda i: (i, 0)),
                  pl.BlockSpec((1, 1, D), lambda i: (i // per_b, 0, 0))],
        out_specs=pl.BlockSpec((tm, D), lambda i: (i, 0)),
        out_shape=jax.ShapeDtypeStruct((N, D), F32),
        scratch_shapes=[pltpu.VMEM((tm, D), F32), pltpu.VMEM((tm, D), F32), pltpu.SemaphoreType.DMA(())],
        compiler_params=_cparams(("arbitrary",)),
        name="moe_combine",
    )(pos3, y, x1, route, gt2)


SC_CORES = 2
SC_SUBCORES = 16
SC_WINDOW = 64


def _sc_row_gather(table, idx):
    M, = idx.shape
    D = table.shape[1]
    workers = SC_CORES * SC_SUBCORES
    per_worker = M // workers
    assert per_worker * workers == M and per_worker % SC_WINDOW == 0
    mesh = plsc.VectorSubcoreMesh(core_axis_name="c", subcore_axis_name="s",
                                  num_cores=SC_CORES, num_subcores=SC_SUBCORES)

    @functools.partial(
        pl.kernel, mesh=mesh,
        out_type=jax.ShapeDtypeStruct((M, D), table.dtype),
        scratch_types=[pltpu.VMEM((SC_WINDOW,), jnp.int32), pltpu.VMEM((SC_WINDOW, D), table.dtype),
                       pltpu.SemaphoreType.DMA],
        name="sc_row_gather")
    def gather(table_hbm, idx_hbm, out_hbm, idx_v, rows_v, sem):
        wid = lax.axis_index("s") * SC_CORES + lax.axis_index("c")

        def window(j, _):
            base = pl.multiple_of(wid * per_worker + j * SC_WINDOW, SC_WINDOW)
            pltpu.sync_copy(idx_hbm.at[pl.ds(base, SC_WINDOW)], idx_v)
            pltpu.async_copy(table_hbm.at[idx_v], rows_v, sem).wait()
            pltpu.sync_copy(rows_v, out_hbm.at[pl.ds(base, SC_WINDOW)])
            return 0

        lax.fori_loop(0, per_worker // SC_WINDOW, window, 0)

    return gather(table, idx)


def _sc_row_scatter(rows, idx0, idx1, n_out):
    N, D = rows.shape
    workers = SC_CORES * SC_SUBCORES
    per_worker = N // workers
    assert per_worker * workers == N and per_worker % SC_WINDOW == 0
    mesh = plsc.VectorSubcoreMesh(core_axis_name="c", subcore_axis_name="s",
                                  num_cores=SC_CORES, num_subcores=SC_SUBCORES)

    @functools.partial(
        pl.kernel, mesh=mesh,
        out_type=jax.ShapeDtypeStruct((n_out, D), rows.dtype),
        scratch_types=[pltpu.VMEM((SC_WINDOW,), jnp.int32), pltpu.VMEM((SC_WINDOW,), jnp.int32),
                       pltpu.VMEM((SC_WINDOW, D), rows.dtype)],
        name="sc_row_scatter")
    def scatter(rows_hbm, idx0_hbm, idx1_hbm, out_hbm, i0_v, i1_v, rows_v):
        wid = lax.axis_index("s") * SC_CORES + lax.axis_index("c")

        def window(j, _):
            base = pl.multiple_of(wid * per_worker + j * SC_WINDOW, SC_WINDOW)
            pltpu.sync_copy(rows_hbm.at[pl.ds(base, SC_WINDOW)], rows_v)
            pltpu.sync_copy(idx0_hbm.at[pl.ds(base, SC_WINDOW)], i0_v)
            pltpu.sync_copy(idx1_hbm.at[pl.ds(base, SC_WINDOW)], i1_v)
            pltpu.sync_copy(rows_v, out_hbm.at[i0_v])
            pltpu.sync_copy(rows_v, out_hbm.at[i1_v])
            return 0

        lax.fori_loop(0, per_worker // SC_WINDOW, window, 0)

    return scatter(rows, idx0, idx1)


def _combine_rows_kernel(y0_ref, y1_ref, x1_ref, route_ref, gt_ref, o_ref):
    y = _unpack_halves(y0_ref[...]) * route_ref[:, 2:3] + _unpack_halves(y1_ref[...]) * route_ref[:, 3:4]
    o_ref[...] = x1_ref[...] + gt_ref[0] * y


def _combine_rows(yg, x1, route, gt2, tm, S):
    N, D = x1.shape
    per_b = S // tm
    n_blocks = N // tm
    return pl.pallas_call(
        _combine_rows_kernel,
        grid=(n_blocks,),
        in_specs=[pl.BlockSpec((tm, D // 2), lambda i: (i, 0)),
                  pl.BlockSpec((tm, D // 2), lambda i: (i + n_blocks, 0)),
                  pl.BlockSpec((tm, D), lambda i: (i, 0)),
                  pl.BlockSpec((tm, LANES), lambda i: (i, 0)),
                  pl.BlockSpec((1, 1, D), lambda i: (i // per_b, 0, 0))],
        out_specs=pl.BlockSpec((tm, D), lambda i: (i, 0)),
        out_shape=jax.ShapeDtypeStruct((N, D), F32),
        compiler_params=_cparams(("parallel",)),
        name="moe_combine",
    )(yg, yg, x1, route, gt2)


def _rope_tables(positions):
    inv = ROPE_THETA ** (-jnp.arange(HALF, dtype=F32) / HALF)
    ang = positions.astype(F32)[:, None, :] * inv[None, :, None]
    return jnp.cos(ang), jnp.sin(ang)


def _block_diag_mean(width):
    blk = np.kron(np.eye(width // HEAD_DIM, dtype=np.float32), np.full((HEAD_DIM, HEAD_DIM), 1.0 / HEAD_DIM, np.float32))
    return jnp.asarray(blk, BF16)


def _layer(x, c_mod, positions, norm1_g, norm2_g, w_in, b_fgt, b_gate, qn_fox, kn_fox, qn_dsa, kn_dsa,
           w_proj_fox, w_proj_dsa, w_out, r_w_grp, r_b_grp, r_w_exp, r_b_exp, w1, w3, w2):
    B, S, D = x.shape
    N = B * S
    topk = min(TOPK_MAX, S // 4)
    tm = min(512, S)
    scale = HEAD_DIM ** -0.5
    mod3 = c_mod.reshape(B, 6, D)

    o = np.cumsum([0, 512, 512, 512, 8, 512, 64, 64, 512, 64, 8, D, D])
    seg = lambda k: w_in[:, o[k]:o[k + 1]]
    zpad = jnp.zeros((D, LANES - HEAD_DIM - 2 * N_HEADS), F32)
    w_tok = jnp.concatenate([seg(1), seg(10), seg(11)], axis=1).astype(BF16)
    w_t = jnp.concatenate([seg(0), seg(4), seg(7), seg(2),
                           seg(5), seg(8),
                           seg(6), seg(3), seg(9), zpad], axis=1).T.astype(BF16)
    gcol = jnp.stack([qn_fox * (scale * LOG2E), qn_dsa * (scale * LOG2E), kn_dsa]).reshape(3, HEAD_DIM, 1)
    cos_t, sin_t = _rope_tables(positions)

    k_heads, gates, fqt, dqt, iqt, fvt, kkt, dvt, logf_t, iwt = _in_projection(
        x, mod3, norm1_g.reshape(1, D), w_tok, w_t, _block_diag_mean(W_HEADS), cos_t, sin_t,
        jnp.tile(kn_fox, N_HEADS).reshape(1, W_HEADS), gcol, b_fgt.reshape(N_HEADS, 1),
        b_gate.reshape(1, 2 * D), tm)

    f_tok = jnp.transpose(_seq_cumsum(logf_t), (0, 2, 1)) * LOG2E
    of = _fox_attention(k_heads, f_tok, fqt, fvt)
    od = _dsa_attention(kkt, dvt, iqt, dqt, iwt, topk)

    wr = jnp.concatenate([r_w_grp, r_w_exp, jnp.zeros((D, LANES - N_GROUPS - N_EXPERTS), F32)], axis=1).astype(BF16)
    br = jnp.concatenate([r_b_grp, r_b_exp, jnp.zeros((LANES - N_GROUPS - N_EXPERTS,), F32)]).reshape(1, LANES)
    x1, h2, route = _post_attention(of, od, gates, x, mod3, w_proj_fox.astype(BF16), w_proj_dsa.astype(BF16),
                                    w_out.astype(BF16), norm2_g.reshape(1, D), wr, br, tm)
    x1, h2, route = x1.reshape(N, D), h2.reshape(N, D // 2), route.reshape(N, LANES)

    tg = 512 if N * 2 >= 512 * N_EXPERTS else 128
    ranks, counts = _expert_ranks(route, tm)
    counts = counts[0, :N_EXPERTS].astype(jnp.int32)
    padded = ((counts + tg - 1) // tg) * tg
    ends = jnp.cumsum(padded)
    starts = ends - padded
    e01 = jnp.transpose(route[:, :2]).astype(jnp.int32)
    start_of = jnp.sum(jnp.where(e01[..., None] == jnp.arange(N_EXPERTS, dtype=jnp.int32), starts, 0), axis=-1)
    pos = start_of + jnp.transpose(ranks[:, :2]).astype(jnp.int32)
    n_rows = N * 2 + N_EXPERTS * tg
    n_tiles = n_rows // tg
    tile_start = jnp.arange(n_tiles, dtype=jnp.int32) * tg
    tile_expert = jnp.minimum(jnp.sum((ends[None, :] <= tile_start[:, None]).astype(jnp.int32), axis=1),
                              N_EXPERTS - 1)
    n_used = (ends[-1] // tg).astype(jnp.int32).reshape(1)

    real_end = jnp.sum(jnp.where(tile_expert[:, None] == jnp.arange(N_EXPERTS, dtype=jnp.int32),
                                 starts + counts, 0), axis=-1)
    rows_valid = jnp.clip(real_end - tile_start, 0, tg).astype(jnp.int32)

    xs = _sc_row_scatter(h2, pos[0], pos[1], n_rows)
    y = _experts(tile_expert, n_used, rows_valid, xs, w1, w3, w2, tg)
    yg = _sc_row_gather(y, pos.reshape(2 * N))
    out = _combine_rows(yg, x1, route, mod3[:, 5:6, :], min(512, S), S)
    return out.reshape(B, S, D)


def kernel(x, c, positions, ada_w, ada_b, norm1_g, norm2_g, w_in, b_fgt, b_gate, qn_fox, kn_fox, qn_dsa, kn_dsa, w_proj_fox, w_proj_dsa, w_out, router_w_grp, router_b_grp, router_w_exp, router_b_exp, exp_w1, exp_w3, exp_w2):
    for l in range(ada_w.shape[0]):
        c_mod = _modulation(c, ada_w[l], ada_b[l])
        x = _layer(x, c_mod, positions, norm1_g[l], norm2_g[l], w_in[l], b_fgt[l], b_gate[l],
                   qn_fox[l], kn_fox[l], qn_dsa[l], kn_dsa[l], w_proj_fox[l], w_proj_dsa[l], w_out[l],
                   router_w_grp[l], router_b_grp[l], router_w_exp[l], router_b_exp[l],
                   exp_w1[l], exp_w3[l], exp_w2[l])
    return x
```

```python
import functools

import jax
import jax.numpy as jnp
import numpy as np
from jax import lax
from jax.experimental import pallas as pl
from jax.experimental.pallas import tpu as pltpu
from jax.experimental.pallas import tpu_sc as plsc

F32 = jnp.float32
BF16 = jnp.bfloat16

CHUNK = 64
CHUNK_SHIFT = 6
DMA_UNROLL = 8
HEAD_DIM = 64
N_HEADS = 8
W_HEADS = N_HEADS * HEAD_DIM
TOPK_MAX = 256
ROPE_THETA = 10000.0
N_GROUPS = 4
EXPERTS_PER_GROUP = 8
N_EXPERTS = N_GROUPS * EXPERTS_PER_GROUP
EPS = 1e-6
LOG2E = 1.4426950408889634
MASKED = -1e30

LANES = 128
VMEM_LIMIT = 56 * 1024 * 1024


def _cparams(sem):
    return pltpu.CompilerParams(dimension_semantics=sem, vmem_limit_bytes=VMEM_LIMIT)


def _mod_kernel(c_ref, w_ref, b_ref, o_ref):
    c = c_ref[...]
    ca = (c * jax.nn.sigmoid(c)).astype(BF16)
    o_ref[...] = jnp.dot(ca, w_ref[...].astype(BF16), preferred_element_type=F32) + b_ref[...]


def _modulation(c, ada_w, ada_b):
    B, D = c.shape
    n = ada_w.shape[1] // D
    return pl.pallas_call(
        _mod_kernel,
        grid=(n,),
        in_specs=[pl.BlockSpec((B, D), lambda j: (0, 0)),
                  pl.BlockSpec((D, D), lambda j: (0, j)),
                  pl.BlockSpec((1, D), lambda j: (0, j))],
        out_specs=pl.BlockSpec((B, D), lambda j: (0, j)),
        out_shape=jax.ShapeDtypeStruct((B, n * D), F32),
        compiler_params=_cparams(("arbitrary",)),
        name="adaln_mod",
    )(c, ada_w, ada_b.reshape(1, -1))


R_FQ, R_DQ, R_IQ, R_FV = 0, 512, 1024, 1536
R_KK = 2048
R_S2 = 2176
R_END = 2304
HALF = HEAD_DIM // 2


def _inproj_kernel(x_ref, mod_ref, g1_ref, wtok_ref, wt_ref, bd512_ref, cos_ref, sin_ref,
                   gk_ref, gcol_ref, bf_ref, bg_ref,
                   fk_ref, gate_ref, fqt_ref, dqt_ref, iqt_ref, fvt_ref, kkt_ref, dvt_ref, lf_ref, iwt_ref):
    x = x_ref[0]
    ms = jnp.mean(x * x, axis=-1, keepdims=True)
    h = x * lax.rsqrt(ms + EPS) * g1_ref[...]
    h = h * (1.0 + mod_ref[0, 1:2, :]) + mod_ref[0, 0:1, :]
    hb = h.astype(BF16)
    D = x.shape[-1]
    cos, sin = cos_ref[0], sin_ref[0]

    def proj_t(lo, hi):
        return lax.dot_general(wt_ref[lo:hi, :], hb, (((1,), (1,)), ((), ())), preferred_element_type=F32)

    def norm_t(yh, gain):
        msq = jnp.mean(yh * yh, axis=0, keepdims=True)
        return yh * lax.rsqrt(msq + EPS) * gain

    def rope_store(ref, lo, yh):
        x1, x2 = yh[:HALF], yh[HALF:]
        ref[0, lo:lo + HALF, :] = (x1 * cos - x2 * sin).astype(ref.dtype)
        ref[0, lo + HALF:lo + HEAD_DIM, :] = (x2 * cos + x1 * sin).astype(ref.dtype)

    fq = proj_t(R_FQ, R_FQ + W_HEADS)
    dq = proj_t(R_DQ, R_DQ + W_HEADS)
    iq = proj_t(R_IQ, R_IQ + W_HEADS)
    for hh in range(N_HEADS):
        lo = hh * HEAD_DIM
        fqt_ref[0, lo:lo + HEAD_DIM, :] = norm_t(fq[lo:lo + HEAD_DIM], gcol_ref[0]).astype(BF16)
        rope_store(dqt_ref, lo, norm_t(dq[lo:lo + HEAD_DIM], gcol_ref[1]))
        rope_store(iqt_ref, lo, iq[lo:lo + HEAD_DIM])
    fvt_ref[0] = proj_t(R_FV, R_FV + W_HEADS).astype(BF16)

    kk = proj_t(R_KK, R_KK + 2 * HEAD_DIM)
    rope_store(kkt_ref, 0, norm_t(kk[:HEAD_DIM], gcol_ref[2]))
    rope_store(kkt_ref, HEAD_DIM, kk[HEAD_DIM:])

    s2 = proj_t(R_S2, R_S2 + LANES)
    dvt_ref[0] = s2[:HEAD_DIM].astype(BF16)
    z = s2[HEAD_DIM:HEAD_DIM + N_HEADS] + bf_ref[...]
    lf_ref[0] = jnp.minimum(z, 0.0) - jnp.log(1.0 + jnp.exp(-jnp.abs(z)))
    iwt_ref[0] = s2[HEAD_DIM + N_HEADS:HEAD_DIM + 2 * N_HEADS]

    fk = jnp.dot(hb, wtok_ref[:, :W_HEADS], preferred_element_type=F32)
    msq = jnp.dot((fk * fk).astype(BF16), bd512_ref[...], preferred_element_type=F32)
    fk = (fk * lax.rsqrt(msq + EPS) * gk_ref[...]).astype(BF16)
    for hh in range(N_HEADS):
        fk_ref[0, hh] = fk[:, hh * HEAD_DIM:(hh + 1) * HEAD_DIM]
    g = jnp.dot(hb, wtok_ref[:, W_HEADS:], preferred_element_type=F32)
    gate_ref[0] = jax.nn.sigmoid(g + bg_ref[...]).astype(BF16)


def _in_projection(x, mod3, norm1_g, w_tok, w_t, bd512, cos_t, sin_t, gk, gcol, bf, bg, tm):
    B, S, D = x.shape
    tok = lambda w: pl.BlockSpec((1, tm, w), lambda b, i: (b, i, 0))
    feat = lambda r: pl.BlockSpec((1, r, tm), lambda b, i: (b, 0, i))
    const = lambda shape: pl.BlockSpec(shape, lambda b, i: (0,) * len(shape))
    out_shapes = [jax.ShapeDtypeStruct((B, N_HEADS, S, HEAD_DIM), BF16),
                  jax.ShapeDtypeStruct((B, S, 2 * D), BF16)] + \
                 [jax.ShapeDtypeStruct((B, W_HEADS, S), BF16)] * 4 + \
                 [jax.ShapeDtypeStruct((B, 2 * HEAD_DIM, S), BF16),
                  jax.ShapeDtypeStruct((B, HEAD_DIM, S), BF16),
                  jax.ShapeDtypeStruct((B, N_HEADS, S), F32),
                  jax.ShapeDtypeStruct((B, N_HEADS, S), F32)]
    return pl.pallas_call(
        _inproj_kernel,
        grid=(B, S // tm),
        in_specs=[tok(D),
                  pl.BlockSpec((1, 6, D), lambda b, i: (b, 0, 0)),
                  const((1, D)),
                  const(w_tok.shape), const(w_t.shape), const((W_HEADS, W_HEADS)),
                  feat(HALF), feat(HALF),
                  const((1, W_HEADS)), const((3, HEAD_DIM, 1)), const((N_HEADS, 1)), const((1, 2 * D))],
        out_specs=[pl.BlockSpec((1, N_HEADS, tm, HEAD_DIM), lambda b, i: (b, 0, i, 0)), tok(2 * D),
                   feat(W_HEADS), feat(W_HEADS), feat(W_HEADS), feat(W_HEADS),
                   feat(2 * HEAD_DIM), feat(HEAD_DIM), feat(N_HEADS), feat(N_HEADS)],
        out_shape=out_shapes,
        compiler_params=_cparams(("parallel", "parallel")),
        name="in_projection",
    )(x, mod3, norm1_g, w_tok, w_t, bd512, cos_t, sin_t, gk, gcol, bf, bg)


def _cumsum_kernel(x_ref, o_ref):
    x = x_ref[0]
    n = x.shape[-1]
    pos = lax.broadcasted_iota(jnp.int32, x.shape, 1)
    shift = 1
    while shift < n:
        x = x + jnp.where(pos >= shift, pltpu.roll(x, shift, 1), 0.0)
        shift *= 2
    o_ref[0] = x


def _seq_cumsum(logf_t):
    B, H, S = logf_t.shape
    return pl.pallas_call(
        _cumsum_kernel,
        grid=(B,),
        in_specs=[pl.BlockSpec((1, H, S), lambda b: (b, 0, 0))],
        out_specs=pl.BlockSpec((1, H, S), lambda b: (b, 0, 0)),
        out_shape=jax.ShapeDtypeStruct((B, H, S), F32),
        compiler_params=_cparams(("parallel",)),
        name="forget_cumsum",
    )(logf_t)


KC = 256
SUB = 8


def _fold_rows(a, op, ways=1):
    n = a.shape[0] // SUB
    a = a.reshape(n, SUB, a.shape[1])
    chains = [a[w] for w in range(ways)]
    for j in range(ways, n):
        chains[j % ways] = op(chains[j % ways], a[j])
    while len(chains) > 1:
        chains = [op(chains[2 * j], chains[2 * j + 1]) for j in range(len(chains) // 2)]
    return chains[0]


def _softmax_pv(nch, s_ref, acc_ref, vt_at, m_all, o_ref):
    Q = o_ref.shape[1]
    acc_ref[...] = jnp.zeros_like(acc_ref)

    def body(c, lsum):
        off = pl.multiple_of(c * KC, KC)
        new = []
        for hh in range(N_HEADS):
            p = jnp.exp2(s_ref[hh, pl.ds(off, KC), :] - m_all[hh])
            new.append(lsum[hh] + _fold_rows(p, jnp.add))
            acc_ref[hh] += jnp.dot(vt_at(hh, off), p.astype(BF16), preferred_element_type=F32)
        return tuple(new)

    lsum = lax.fori_loop(0, nch, body, tuple(jnp.zeros((SUB, Q), F32) for _ in range(N_HEADS)))
    for hh in range(N_HEADS):
        acc_ref[hh] = acc_ref[hh] / jnp.sum(lsum[hh], axis=0, keepdims=True)
    out_t = acc_ref[...].reshape(N_HEADS * HEAD_DIM, Q)
    o_ref[0] = out_t.T.astype(BF16)


def _fox_kernel(k_ref, f_ref, qt_ref, vt_ref, o_ref, s_ref, acc_ref):
    i = pl.program_id(1)
    Q = o_ref.shape[1]
    qts = [qt_ref[0, hh * HEAD_DIM:(hh + 1) * HEAD_DIM, :] for hh in range(N_HEADS)]

    def scores(c, mx, bias):
        off = pl.multiple_of(c * KC, KC)
        new = []
        for hh in range(N_HEADS):
            s = jnp.dot(k_ref[0, hh, pl.ds(off, KC), :], qts[hh], preferred_element_type=F32)
            s = s - f_ref[0, pl.ds(off, KC), hh:hh + 1]
            if bias is not None:
                s = s + bias
            s_ref[hh, pl.ds(off, KC), :] = s
            new.append(jnp.maximum(mx[hh], _fold_rows(s, jnp.maximum)))
        return tuple(new)

    mx = tuple(jnp.full((SUB, Q), MASKED, F32) for _ in range(N_HEADS))
    mx = lax.fori_loop(0, i, lambda c, m: scores(c, m, None), mx)
    kk = lax.broadcasted_iota(jnp.int32, (KC, Q), 0)
    qq = lax.broadcasted_iota(jnp.int32, (KC, Q), 1)
    mx = scores(i, mx, jnp.where(kk <= qq, 0.0, MASKED))
    m_all = [jnp.max(m, axis=0, keepdims=True) for m in mx]
    _softmax_pv(i + 1, s_ref, acc_ref, lambda hh, off: vt_ref[0, hh * HEAD_DIM:(hh + 1) * HEAD_DIM, pl.ds(off, KC)],
                m_all, o_ref)


def _fox_attention(k_heads, f_tok, qt, vt):
    B, H, S, Dh = k_heads.shape
    return pl.pallas_call(
        _fox_kernel,
        grid=(B, S // KC),
        in_specs=[pl.BlockSpec((1, H, S, Dh), lambda b, i: (b, 0, 0, 0)),
                  pl.BlockSpec((1, S, H), lambda b, i: (b, 0, 0)),
                  pl.BlockSpec((1, W_HEADS, KC), lambda b, i: (b, 0, i)),
                  pl.BlockSpec((1, W_HEADS, S), lambda b, i: (b, 0, 0))],
        out_specs=pl.BlockSpec((1, KC, W_HEADS), lambda b, i: (b, i, 0)),
        out_shape=jax.ShapeDtypeStruct((B, S, W_HEADS), BF16),
        scratch_shapes=[pltpu.VMEM((H, S, KC), F32), pltpu.VMEM((H, HEAD_DIM, KC), F32)],
        compiler_params=_cparams(("parallel", "arbitrary")),
        name="fox_attention",
    )(k_heads, f_tok, qt, vt)


INT_MIN = -(2 ** 31)
KEY_NEG_INF = INT_MIN + 0x7FFFFF
HI16 = -(2 ** 16)
PACK = 16


def _dsa_kernel(kkt_ref, dvt_ref, iqt_ref, dqt_ref, iwt_ref, o_ref, key_ref, hi_ref, s_ref, acc_ref, dk_ref, ik_ref,
                *, topk):
    i = pl.program_id(1)
    Q = o_ref.shape[1]
    nch = i + 1

    @pl.when(i == 0)
    def _():
        def to_rows(c, _):
            off = pl.multiple_of(c * KC, KC)
            rows = kkt_ref[0, :, pl.ds(off, KC)].astype(F32).T
            dk_ref[pl.ds(off, KC), :] = rows[:, :HEAD_DIM].astype(BF16)
            ik_ref[pl.ds(off, KC), :] = rows[:, HEAD_DIM:].astype(BF16)
            return 0
        lax.fori_loop(0, kkt_ref.shape[-1] // KC, to_rows, 0)

    sub_k = lax.broadcasted_iota(jnp.int32, (KC, Q), 0)
    sub_r = lax.broadcasted_iota(jnp.int32, (CHUNK, Q), 0)
    q_chunk = (i * Q + lax.broadcasted_iota(jnp.int32, (CHUNK, Q), 1)) >> CHUNK_SHIFT
    iqts = [iqt_ref[0, hh * HEAD_DIM:(hh + 1) * HEAD_DIM, :] for hh in range(N_HEADS)]
    iws = [iwt_ref[0, hh:hh + 1, :] for hh in range(N_HEADS)]

    def score_chunk(c, _):
        for r in range(KC // CHUNK):
            off = pl.multiple_of(c * KC + r * CHUNK, CHUNK)
            ik = ik_ref[pl.ds(off, CHUNK), :]
            sc = jnp.zeros((CHUNK, Q), F32)
            for hh in range(N_HEADS):
                d = jnp.dot(ik, iqts[hh], preferred_element_type=F32)
                sc = sc + iws[hh] * jnp.maximum(d, 0.0)
            sc = sc + 0.0
            allowed = ((off + sub_r) >> CHUNK_SHIFT) <= q_chunk
            bits = pltpu.bitcast(jnp.where(allowed, sc, -jnp.inf), jnp.int32)
            key = bits ^ ((bits >> 31) & 0x7FFFFFFF)
            key_ref[pl.ds(off, CHUNK), :] = key
            hi_ref[pl.ds(off, CHUNK), :] = (key >> 16).astype(jnp.int16)
        return 0

    lax.fori_loop(0, nch, score_chunk, 0)

    def sweep(n, body, init):
        if isinstance(n, int):
            acc = init
            for c in range(n):
                acc = body(c * KC, acc)
            return acc
        return lax.fori_loop(0, n, lambda c, acc: body(pl.multiple_of(c * KC, KC), acc), init)

    def count(pred, n=nch):
        def body(off, acc):
            hit = pred(key_ref[pl.ds(off, KC), :], off + sub_k)
            return acc + _fold_rows(jnp.where(hit, 1.0, 0.0), jnp.add, ways=4)
        return jnp.sum(sweep(n, body, jnp.zeros((SUB, Q), F32)), axis=0, keepdims=True)

    one, zero = jnp.ones((), BF16), jnp.zeros((), BF16)

    def count_hi(cand, n):
        c16 = jnp.broadcast_to(cand >> 16, (PACK, Q)).astype(jnp.int16)

        def body(off, acc):
            kb = hi_ref[pl.ds(off, KC), :].reshape(KC // PACK, PACK, Q)
            hit = jnp.where(kb >= c16[None], one, zero)
            parts = [hit[w] for w in range(4)]
            for j in range(4, KC // PACK):
                parts[j % 4] = parts[j % 4] + hit[j]
            return acc + ((parts[0] + parts[1]) + (parts[2] + parts[3]))
        acc = sweep(n, body, jnp.zeros((PACK, Q), BF16))
        return jnp.sum(acc.astype(F32), axis=0, keepdims=True)

    assert key_ref.shape[0] // PACK <= 256
    kf = jnp.float32(topk)

    def descent(n):
        n_nonneg = count_hi(jnp.zeros((1, Q), jnp.int32), n)
        top_half = n_nonneg >= kf
        thr = jnp.where(top_half, 0, INT_MIN).astype(jnp.int32)
        n_ge = jnp.where(top_half, n_nonneg, jnp.float32(n * KC))

        def descend(counter, top_bit):
            def step(j, carry):
                thr, n_ge = carry
                cand = thr + (jnp.int32(1) << (top_bit - j))
                cnt = counter(cand)
                take = cnt >= kf
                return jnp.where(take, cand, thr), jnp.where(take, cnt, n_ge)
            return step

        carry = lax.fori_loop(0, 15, descend(lambda c: count_hi(c, n), 30), (thr, n_ge))
        return lax.fori_loop(0, 16, descend(lambda c: count(lambda k, _: k >= c, n), 15), carry)

    thr, n_ge = lax.switch(i, [functools.partial(descent, n) for n in range(1, key_ref.shape[0] // KC + 1)])

    excess = (n_ge > kf) & (thr > KEY_NEG_INF)

    @pl.when(jnp.max(jnp.where(excess, 1.0, 0.0)) > 0.0)
    def _():
        need = kf - count(lambda k, _: k > thr)
        nbits = int(np.ceil(np.log2(key_ref.shape[0]))) + 1

        def bound(j, last):
            cand = last + (jnp.int32(1) << (nbits - 1 - j))
            n = count(lambda k, idx: (k == thr) & (idx < cand))
            return jnp.where(n < need, cand, last)

        last = lax.fori_loop(0, nbits, bound, jnp.zeros((1, Q), jnp.int32))

        def demote(c, _):
            off = pl.multiple_of(c * KC, KC)
            k = key_ref[pl.ds(off, KC), :]
            drop = excess & (k == thr) & (off + sub_k > last)
            key_ref[pl.ds(off, KC), :] = jnp.where(drop, thr - 1, k)
            return 0

        lax.fori_loop(0, nch, demote, 0)

    keep_from = jnp.maximum(thr, KEY_NEG_INF + 1)

    qts = [dqt_ref[0, hh * HEAD_DIM:(hh + 1) * HEAD_DIM, :] for hh in range(N_HEADS)]

    def scores(c, mx):
        off = pl.multiple_of(c * KC, KC)
        bias = jnp.where(key_ref[pl.ds(off, KC), :] >= keep_from, 0.0, MASKED)
        dk = dk_ref[pl.ds(off, KC), :]
        new = []
        for hh in range(N_HEADS):
            s = jnp.dot(dk, qts[hh], preferred_element_type=F32) + bias
            s_ref[hh, pl.ds(off, KC), :] = s
            new.append(jnp.maximum(mx[hh], _fold_rows(s, jnp.maximum)))
        return tuple(new)

    mx = lax.fori_loop(0, nch, scores, tuple(jnp.full((SUB, Q), MASKED, F32) for _ in range(N_HEADS)))
    m_all = [jnp.max(m, axis=0, keepdims=True) for m in mx]
    _softmax_pv(nch, s_ref, acc_ref, lambda hh, off: dvt_ref[0, :, pl.ds(off, KC)], m_all, o_ref)


def _dsa_attention(kkt, dvt, iqt, dqt, iwt, topk):
    B, Dh, S = dvt.shape
    rows = lambda r: pl.BlockSpec((1, r, S), lambda b, i: (b, 0, 0))
    qcols = lambda r: pl.BlockSpec((1, r, KC), lambda b, i: (b, 0, i))
    return pl.pallas_call(
        functools.partial(_dsa_kernel, topk=topk),
        grid=(B, S // KC),
        in_specs=[rows(2 * Dh), rows(Dh), qcols(W_HEADS), qcols(W_HEADS), qcols(N_HEADS)],
        out_specs=pl.BlockSpec((1, KC, W_HEADS), lambda b, i: (b, i, 0)),
        out_shape=jax.ShapeDtypeStruct((B, S, W_HEADS), BF16),
        scratch_shapes=[pltpu.VMEM((S, KC), jnp.int32), pltpu.VMEM((S, KC), jnp.int16),
                        pltpu.VMEM((N_HEADS, S, KC), F32), pltpu.VMEM((N_HEADS, HEAD_DIM, KC), F32),
                        pltpu.VMEM((S, Dh), BF16), pltpu.VMEM((S, Dh), BF16)],
        compiler_params=_cparams(("parallel", "arbitrary")),
        name="dsa_attention",
    )(kkt, dvt, iqt, dqt, iwt)


PACKED = jnp.int32


def _pack_halves(a):
    half = a.shape[-1] // 2
    rounded = a.astype(BF16).astype(F32)
    lo = pltpu.bitcast(rounded[:, :half], jnp.int32)
    hi = pltpu.bitcast(rounded[:, half:], jnp.int32)
    return hi | ((lo >> 16) & 0xFFFF)


def _unpack_halves(p):
    lo = pltpu.bitcast(p << 16, F32)
    hi = pltpu.bitcast(p & HI16, F32)
    return jnp.concatenate([lo, hi], axis=-1)


def _first(mask, lane):
    return jnp.min(jnp.where(mask, lane, LANES), axis=-1, keepdims=True)


def _post_kernel(of_ref, od_ref, gate_ref, x_ref, mod_ref, wpf_ref, wpd_ref, wo_ref, g2_ref, wr_ref, br_ref,
                 x1_ref, h2_ref, route_ref):
    D = x_ref.shape[-1]
    pf = jnp.dot(of_ref[0], wpf_ref[...], preferred_element_type=F32)
    pd = jnp.dot(od_ref[0], wpd_ref[...], preferred_element_type=F32)
    merged = gate_ref[0, :, :D].astype(F32) * pf + gate_ref[0, :, D:].astype(F32) * pd
    y = jnp.dot(merged.astype(BF16), wo_ref[...], preferred_element_type=F32)
    x1 = x_ref[0] + mod_ref[0, 2:3, :] * y
    x1_ref[0] = x1

    ms = jnp.mean(x1 * x1, axis=-1, keepdims=True)
    h2 = x1 * lax.rsqrt(ms + EPS) * g2_ref[...]
    h2 = h2 * (1.0 + mod_ref[0, 4:5, :]) + mod_ref[0, 3:4, :]
    hb = h2.astype(BF16)
    h2_ref[0] = _pack_halves(h2)

    logits = jnp.dot(hb, wr_ref[...], preferred_element_type=F32) + br_ref[...]
    lane = lax.broadcasted_iota(jnp.int32, logits.shape, 1)
    is_grp = lane < N_GROUPS
    gl = jnp.where(is_grp, logits, -jnp.inf)
    gmax = jnp.max(gl, axis=-1, keepdims=True)
    g_idx = _first(gl == gmax, lane)
    g_w = 1.0 / jnp.sum(jnp.exp(gl - gmax), axis=-1, keepdims=True)

    e_lo = N_GROUPS + g_idx * EXPERTS_PER_GROUP
    in_grp = (lane >= e_lo) & (lane < e_lo + EXPERTS_PER_GROUP)
    el = jnp.where(in_grp, logits, -jnp.inf)
    emax = jnp.max(el, axis=-1, keepdims=True)
    ee = jnp.exp(el - emax)
    prob = ee / jnp.sum(ee, axis=-1, keepdims=True)
    prob = jnp.where(in_grp, prob, -1.0)
    p0 = jnp.max(prob, axis=-1, keepdims=True)
    l0 = _first(prob == p0, lane)
    rest = jnp.where(lane == l0, -1.0, prob)
    p1 = jnp.max(rest, axis=-1, keepdims=True)
    l1 = _first(rest == p1, lane)
    psum = p0 + p1
    w0 = g_w * (p0 / psum)
    w1 = g_w * (p1 / psum)
    e0 = (l0 - N_GROUPS).astype(F32)
    e1 = (l1 - N_GROUPS).astype(F32)
    route_ref[0] = jnp.where(lane == 0, e0, jnp.where(lane == 1, e1, jnp.where(lane == 2, w0,
                             jnp.where(lane == 3, w1, 0.0))))


def _post_attention(of, od, gates, x, mod3, wpf, wpd, wo, g2, wr, br, tm):
    B, S, D = x.shape
    tok = lambda w: pl.BlockSpec((1, tm, w), lambda b, i: (b, i, 0))
    const = lambda shape: pl.BlockSpec(shape, lambda b, i: (0,) * len(shape))
    return pl.pallas_call(
        _post_kernel,
        grid=(B, S // tm),
        in_specs=[tok(W_HEADS), tok(W_HEADS), tok(2 * D), tok(D),
                  pl.BlockSpec((1, 6, D), lambda b, i: (b, 0, 0)),
                  const(wpf.shape), const(wpd.shape), const(wo.shape),
                  const((1, D)), const((D, LANES)), const((1, LANES))],
        out_specs=[tok(D), tok(D // 2), tok(LANES)],
        out_shape=[jax.ShapeDtypeStruct((B, S, D), F32),
                   jax.ShapeDtypeStruct((B, S, D // 2), PACKED),
                   jax.ShapeDtypeStruct((B, S, LANES), F32)],
        compiler_params=_cparams(("parallel", "parallel")),
        name="merge_out_router",
    )(of, od, gates, x, mod3, wpf, wpd, wo, g2, wr, br)


def _rank_kernel(route_ref, tri_ref, rank_ref, count_ref, carry_ref):
    @pl.when(pl.program_id(0) == 0)
    def _():
        carry_ref[...] = jnp.zeros_like(carry_ref)

    r = route_ref[...]
    lane = lax.broadcasted_iota(jnp.int32, r.shape, 1).astype(F32)
    hot0 = lane == r[:, 0:1]
    hot1 = lane == r[:, 1:2]
    hits = jnp.where(hot0 | hot1, 1.0, 0.0)
    incl = jnp.dot(tri_ref[...], hits.astype(BF16), preferred_element_type=F32)
    before = incl - hits + carry_ref[...]
    r0 = jnp.sum(jnp.where(hot0, before, 0.0), axis=-1, keepdims=True)
    r1 = jnp.sum(jnp.where(hot1, before, 0.0), axis=-1, keepdims=True)
    rank_ref[...] = jnp.where(lane == 0.0, r0, jnp.where(lane == 1.0, r1, 0.0))
    carry_ref[...] = carry_ref[...] + jnp.sum(hits, axis=0, keepdims=True)
    count_ref[...] = carry_ref[...]


def _expert_ranks(route, tm):
    N = route.shape[0]
    tri = jnp.asarray(np.tril(np.ones((tm, tm), np.float32)), BF16)
    return pl.pallas_call(
        _rank_kernel,
        grid=(N // tm,),
        in_specs=[pl.BlockSpec((tm, LANES), lambda i: (i, 0)),
                  pl.BlockSpec((tm, tm), lambda i: (0, 0))],
        out_specs=[pl.BlockSpec((tm, LANES), lambda i: (i, 0)),
                   pl.BlockSpec((1, LANES), lambda i: (0, 0))],
        out_shape=[jax.ShapeDtypeStruct((N, LANES), F32), jax.ShapeDtypeStruct((1, LANES), F32)],
        scratch_shapes=[pltpu.VMEM((1, LANES), F32)],
        compiler_params=_cparams(("arbitrary",)),
        name="expert_ranks",
    )(route, tri)


def _dispatch_kernel(zstart_ref, zon_ref, nt_ref, pos_ref, h_ref, xs_ref, zbuf, sem, zsem, *, tm, tg):
    @pl.when(pl.program_id(0) == 0)
    def _():
        zbuf[...] = jnp.zeros_like(zbuf)

        def zero_tile(start):
            return pltpu.make_async_copy(zbuf, xs_ref.at[pl.ds(pl.multiple_of(start, tg), tg), :], zsem)

        n_tiles = xs_ref.shape[0] // tg
        for e in range(N_EXPERTS):
            pl.when(zon_ref[e] > 0)(lambda e=e: zero_tile(zstart_ref[e]).start())
        lax.fori_loop(nt_ref[0], n_tiles, lambda t, _: (zero_tile(t * tg).start(), 0)[1], 0)
        for e in range(N_EXPERTS):
            pl.when(zon_ref[e] > 0)(lambda e=e: zero_tile(zstart_ref[e]).wait())
        lax.fori_loop(nt_ref[0], n_tiles, lambda t, _: (zero_tile(t * tg).wait(), 0)[1], 0)

    def copy(r, slot):
        return pltpu.make_async_copy(h_ref.at[pl.ds(r, 1), :],
                                     xs_ref.at[pl.ds(pos_ref[0, slot, r], 1), :], sem)

    def issue(r, _):
        copy(r, 0).start()
        copy(r, 1).start()
        return 0

    lax.fori_loop(0, tm, issue, 0, unroll=DMA_UNROLL)
    for _ in range(2):
        pltpu.make_async_copy(h_ref, xs_ref.at[pl.ds(0, tm), :], sem).wait()


def _dispatch(h2, pos3, last_tile_start, has_rows, n_tiles_used, n_rows, tm, tg):
    N, D = h2.shape
    grid_spec = pltpu.PrefetchScalarGridSpec(
        num_scalar_prefetch=3,
        grid=(N // tm,),
        in_specs=[pl.BlockSpec((1, 2, tm), lambda i, zs, zo, nt: (i, 0, 0), memory_space=pltpu.SMEM),
                  pl.BlockSpec((tm, D), lambda i, zs, zo, nt: (i, 0))],
        out_specs=pl.BlockSpec(memory_space=pl.ANY),
        scratch_shapes=[pltpu.VMEM((tg, D), h2.dtype), pltpu.SemaphoreType.DMA(()), pltpu.SemaphoreType.DMA(())],
    )
    return pl.pallas_call(
        functools.partial(_dispatch_kernel, tm=tm, tg=tg),
        grid_spec=grid_spec,
        out_shape=jax.ShapeDtypeStruct((n_rows, D), h2.dtype),
        compiler_params=_cparams(("arbitrary",)),
        name="moe_dispatch",
    )(last_tile_start, has_rows, n_tiles_used, pos3, h2)


def _expert_kernel(te_ref, nt_ref, nv_ref, xs_ref, w1_ref, w3_ref, w2_ref, y_ref, w1b, w3b, w2b):
    g = pl.program_id(0)
    used = g < nt_ref[0]
    new_expert = (g == 0) | (te_ref[g] != te_ref[jnp.maximum(g - 1, 0)])

    @pl.when(used & new_expert)
    def _():
        w1b[...] = w1_ref[0].astype(BF16)
        w3b[...] = w3_ref[0].astype(BF16)
        w2b[...] = w2_ref[0].astype(BF16)

    @pl.when(used)
    def _():
        row = lax.broadcasted_iota(jnp.int32, xs_ref.shape, 0)
        xb = _unpack_halves(jnp.where(row < nv_ref[g], xs_ref[...], 0)).astype(BF16)
        a = jnp.dot(xb, w1b[...], preferred_element_type=F32)
        b = jnp.dot(xb, w3b[...], preferred_element_type=F32)
        hmid = (a * jax.nn.sigmoid(a) * b).astype(BF16)
        y_ref[...] = _pack_halves(jnp.dot(hmid, w2b[...], preferred_element_type=F32))

    @pl.when(jnp.logical_not(used))
    def _():
        y_ref[...] = jnp.zeros_like(y_ref)


def _experts(tile_expert, n_tiles_used, rows_valid, xs, w1, w3, w2, tg):
    P, Dp = xs.shape
    E, D, De = w1.shape
    row_tile = lambda g, te, nt, nv: (jnp.minimum(g, nt[0] - 1), 0)
    grid_spec = pltpu.PrefetchScalarGridSpec(
        num_scalar_prefetch=3,
        grid=(P // tg,),
        in_specs=[pl.BlockSpec((tg, Dp), row_tile),
                  pl.BlockSpec((1, D, De), lambda g, te, nt, nv: (te[g], 0, 0)),
                  pl.BlockSpec((1, D, De), lambda g, te, nt, nv: (te[g], 0, 0)),
                  pl.BlockSpec((1, De, D), lambda g, te, nt, nv: (te[g], 0, 0))],
        out_specs=pl.BlockSpec((tg, Dp), lambda g, te, nt, nv: (g, 0)),
        scratch_shapes=[pltpu.VMEM((D, De), BF16), pltpu.VMEM((D, De), BF16), pltpu.VMEM((De, D), BF16)],
    )
    return pl.pallas_call(
        _expert_kernel,
        grid_spec=grid_spec,
        out_shape=jax.ShapeDtypeStruct((P, Dp), PACKED),
        compiler_params=_cparams(("arbitrary",)),
        name="moe_experts",
    )(tile_expert, n_tiles_used, rows_valid, xs, w1, w3, w2)


def _combine_kernel(pos_ref, y_ref, x1_ref, route_ref, gt_ref, o_ref, buf0, buf1, sem, *, tm):
    def copy(r, slot, buf):
        return pltpu.make_async_copy(y_ref.at[pl.ds(pos_ref[0, slot, r], 1), :],
                                     buf.at[pl.ds(r, 1), :], sem)

    def issue(r, _):
        copy(r, 0, buf0).start()
        copy(r, 1, buf1).start()
        return 0

    lax.fori_loop(0, tm, issue, 0, unroll=DMA_UNROLL)
    for buf in (buf0, buf1):
        pltpu.make_async_copy(y_ref.at[pl.ds(0, tm), :], buf, sem).wait()
    w0 = route_ref[:, 2:3]
    w1 = route_ref[:, 3:4]
    y = buf0[...] * w0 + buf1[...] * w1
    o_ref[...] = x1_ref[...] + gt_ref[0] * y


def _combine(pos3, y, x1, route, gt2, tm, S):
    N, D = x1.shape
    per_b = S // tm
    return pl.pallas_call(
        functools.partial(_combine_kernel, tm=tm),
        grid=(N // tm,),
        in_specs=[pl.BlockSpec((1, 2, tm), lambda i: (i, 0, 0), memory_space=pltpu.SMEM),
                  pl.BlockSpec(memory_space=pl.ANY),
                  pl.BlockSpec((tm, D), lambda i: (i, 0)),
                  pl.BlockSpec((tm, LANES), lambda i: (i, 0)),
                  pl.BlockSpec((1, 1, D), lambda i: (i // per_b, 0, 0))],
        out_specs=pl.BlockSpec((tm, D), lambda i: (i, 0)),
        out_shape=jax.ShapeDtypeStruct((N, D), F32),
        scratch_shapes=[pltpu.VMEM((tm, D), F32), pltpu.VMEM((tm, D), F32), pltpu.SemaphoreType.DMA(())],
        compiler_params=_cparams(("arbitrary",)),
        name="moe_combine",
    )(pos3, y, x1, route, gt2)


SC_CORES = 2
SC_SUBCORES = 16
SC_WINDOW = 128


def _sc_row_gather(table, idx):
    M, = idx.shape
    D = table.shape[1]
    workers = SC_CORES * SC_SUBCORES
    per_worker = M // workers
    assert per_worker * workers == M and per_worker % SC_WINDOW == 0
    mesh = plsc.VectorSubcoreMesh(core_axis_name="c", subcore_axis_name="s",
                                  num_cores=SC_CORES, num_subcores=SC_SUBCORES)

    @functools.partial(
        pl.kernel, mesh=mesh,
        out_type=jax.ShapeDtypeStruct((M, D), table.dtype),
        scratch_types=[pltpu.VMEM((SC_WINDOW,), jnp.int32), pltpu.VMEM((SC_WINDOW, D), table.dtype),
                       pltpu.SemaphoreType.DMA],
        name="sc_row_gather")
    def gather(table_hbm, idx_hbm, out_hbm, idx_v, rows_v, sem):
        wid = lax.axis_index("s") * SC_CORES + lax.axis_index("c")

        def window(j, _):
            base = pl.multiple_of(wid * per_worker + j * SC_WINDOW, SC_WINDOW)
            pltpu.sync_copy(idx_hbm.at[pl.ds(base, SC_WINDOW)], idx_v)
            pltpu.async_copy(table_hbm.at[idx_v], rows_v, sem).wait()
            pltpu.sync_copy(rows_v, out_hbm.at[pl.ds(base, SC_WINDOW)])
            return 0

        lax.fori_loop(0, per_worker // SC_WINDOW, window, 0)

    return gather(table, idx)


def _sc_row_scatter(rows, idx0, idx1, n_out):
    N, D = rows.shape
    workers = SC_CORES * SC_SUBCORES
    per_worker = N // workers
    assert per_worker * workers == N and per_worker % SC_WINDOW == 0
    mesh = plsc.VectorSubcoreMesh(core_axis_name="c", subcore_axis_name="s",
                                  num_cores=SC_CORES, num_subcores=SC_SUBCORES)

    @functools.partial(
        pl.kernel, mesh=mesh,
        out_type=jax.ShapeDtypeStruct((n_out, D), rows.dtype),
        scratch_types=[pltpu.VMEM((SC_WINDOW,), jnp.int32), pltpu.VMEM((SC_WINDOW,), jnp.int32),
                       pltpu.VMEM((SC_WINDOW, D), rows.dtype)],
        name="sc_row_scatter")
    def scatter(rows_hbm, idx0_hbm, idx1_hbm, out_hbm, i0_v, i1_v, rows_v):
        wid = lax.axis_index("s") * SC_CORES + lax.axis_index("c")

        def window(j, _):
            base = pl.multiple_of(wid * per_worker + j * SC_WINDOW, SC_WINDOW)
            pltpu.sync_copy(rows_hbm.at[pl.ds(base, SC_WINDOW)], rows_v)
            pltpu.sync_copy(idx0_hbm.at[pl.ds(base, SC_WINDOW)], i0_v)
            pltpu.sync_copy(idx1_hbm.at[pl.ds(base, SC_WINDOW)], i1_v)
            pltpu.sync_copy(rows_v, out_hbm.at[i0_v])
            pltpu.sync_copy(rows_v, out_hbm.at[i1_v])
            return 0

        lax.fori_loop(0, per_worker // SC_WINDOW, window, 0)

    return scatter(rows, idx0, idx1)


def _combine_rows_kernel(y0_ref, y1_ref, x1_ref, route_ref, gt_ref, o_ref):
    y = _unpack_halves(y0_ref[...]) * route_ref[:, 2:3] + _unpack_halves(y1_ref[...]) * route_ref[:, 3:4]
    o_ref[...] = x1_ref[...] + gt_ref[0] * y


def _combine_rows(yg, x1, route, gt2, tm, S):
    N, D = x1.shape
    per_b = S // tm
    n_blocks = N // tm
    return pl.pallas_call(
        _combine_rows_kernel,
        grid=(n_blocks,),
        in_specs=[pl.BlockSpec((tm, D // 2), lambda i: (i, 0)),
                  pl.BlockSpec((tm, D // 2), lambda i: (i + n_blocks, 0)),
                  pl.BlockSpec((tm, D), lambda i: (i, 0)),
                  pl.BlockSpec((tm, LANES), lambda i: (i, 0)),
                  pl.BlockSpec((1, 1, D), lambda i: (i // per_b, 0, 0))],
        out_specs=pl.BlockSpec((tm, D), lambda i: (i, 0)),
        out_shape=jax.ShapeDtypeStruct((N, D), F32),
        compiler_params=_cparams(("parallel",)),
        name="moe_combine",
    )(yg, yg, x1, route, gt2)


def _rope_tables(positions):
    inv = ROPE_THETA ** (-jnp.arange(HALF, dtype=F32) / HALF)
    ang = positions.astype(F32)[:, None, :] * inv[None, :, None]
    return jnp.cos(ang), jnp.sin(ang)


def _block_diag_mean(width):
    blk = np.kron(np.eye(width // HEAD_DIM, dtype=np.float32), np.full((HEAD_DIM, HEAD_DIM), 1.0 / HEAD_DIM, np.float32))
    return jnp.asarray(blk, BF16)


def _layer(x, c_mod, positions, norm1_g, norm2_g, w_in, b_fgt, b_gate, qn_fox, kn_fox, qn_dsa, kn_dsa,
           w_proj_fox, w_proj_dsa, w_out, r_w_grp, r_b_grp, r_w_exp, r_b_exp, w1, w3, w2):
    B, S, D = x.shape
    N = B * S
    topk = min(TOPK_MAX, S // 4)
    tm = min(512, S)
    scale = HEAD_DIM ** -0.5
    mod3 = c_mod.reshape(B, 6, D)

    o = np.cumsum([0, 512, 512, 512, 8, 512, 64, 64, 512, 64, 8, D, D])
    seg = lambda k: w_in[:, o[k]:o[k + 1]]
    zpad = jnp.zeros((D, LANES - HEAD_DIM - 2 * N_HEADS), F32)
    w_tok = jnp.concatenate([seg(1), seg(10), seg(11)], axis=1).astype(BF16)
    w_t = jnp.concatenate([seg(0), seg(4), seg(7), seg(2),
                           seg(5), seg(8),
                           seg(6), seg(3), seg(9), zpad], axis=1).T.astype(BF16)
    gcol = jnp.stack([qn_fox * (scale * LOG2E), qn_dsa * (scale * LOG2E), kn_dsa]).reshape(3, HEAD_DIM, 1)
    cos_t, sin_t = _rope_tables(positions)

    k_heads, gates, fqt, dqt, iqt, fvt, kkt, dvt, logf_t, iwt = _in_projection(
        x, mod3, norm1_g.reshape(1, D), w_tok, w_t, _block_diag_mean(W_HEADS), cos_t, sin_t,
        jnp.tile(kn_fox, N_HEADS).reshape(1, W_HEADS), gcol, b_fgt.reshape(N_HEADS, 1),
        b_gate.reshape(1, 2 * D), tm)

    f_tok = jnp.transpose(_seq_cumsum(logf_t), (0, 2, 1)) * LOG2E
    of = _fox_attention(k_heads, f_tok, fqt, fvt)
    od = _dsa_attention(kkt, dvt, iqt, dqt, iwt, topk)

    wr = jnp.concatenate([r_w_grp, r_w_exp, jnp.zeros((D, LANES - N_GROUPS - N_EXPERTS), F32)], axis=1).astype(BF16)
    br = jnp.concatenate([r_b_grp, r_b_exp, jnp.zeros((LANES - N_GROUPS - N_EXPERTS,), F32)]).reshape(1, LANES)
    x1, h2, route = _post_attention(of, od, gates, x, mod3, w_proj_fox.astype(BF16), w_proj_dsa.astype(BF16),
                                    w_out.astype(BF16), norm2_g.reshape(1, D), wr, br, tm)
    x1, h2, route = x1.reshape(N, D), h2.reshape(N, D // 2), route.reshape(N, LANES)

    tg = 512 if N * 2 >= 512 * N_EXPERTS else 128
    ranks, counts = _expert_ranks(route, tm)
    counts = counts[0, :N_EXPERTS].astype(jnp.int32)
    padded = ((counts + tg - 1) // tg) * tg
    ends = jnp.cumsum(padded)
    starts = ends - padded
    e01 = jnp.transpose(route[:, :2]).astype(jnp.int32)
    start_of = jnp.sum(jnp.where(e01[..., None] == jnp.arange(N_EXPERTS, dtype=jnp.int32), starts, 0), axis=-1)
    pos = start_of + jnp.transpose(ranks[:, :2]).astype(jnp.int32)
    n_rows = N * 2 + N_EXPERTS * tg
    n_tiles = n_rows // tg
    tile_start = jnp.arange(n_tiles, dtype=jnp.int32) * tg
    tile_expert = jnp.minimum(jnp.sum((ends[None, :] <= tile_start[:, None]).astype(jnp.int32), axis=1),
                              N_EXPERTS - 1)
    n_used = (ends[-1] // tg).astype(jnp.int32).reshape(1)

    real_end = jnp.sum(jnp.where(tile_expert[:, None] == jnp.arange(N_EXPERTS, dtype=jnp.int32),
                                 starts + counts, 0), axis=-1)
    rows_valid = jnp.clip(real_end - tile_start, 0, tg).astype(jnp.int32)

    xs = _sc_row_scatter(h2, pos[0], pos[1], n_rows)
    y = _experts(tile_expert, n_used, rows_valid, xs, w1, w3, w2, tg)
    yg = _sc_row_gather(y, pos.reshape(2 * N))
    out = _combine_rows(yg, x1, route, mod3[:, 5:6, :], min(512, S), S)
    return out.reshape(B, S, D)


def kernel(x, c, positions, ada_w, ada_b, norm1_g, norm2_g, w_in, b_fgt, b_gate, qn_fox, kn_fox, qn_dsa, kn_dsa, w_proj_fox, w_proj_dsa, w_out, router_w_grp, router_b_grp, router_w_exp, router_b_exp, exp_w1, exp_w3, exp_w2):
    for l in range(ada_w.shape[0]):
        c_mod = _modulation(c, ada_w[l], ada_b[l])
        x = _layer(x, c_mod, positions, norm1_g[l], norm2_g[l], w_in[l], b_fgt[l], b_gate[l],
                   qn_fox[l], kn_fox[l], qn_dsa[l], kn_dsa[l], w_proj_fox[l], w_proj_dsa[l], w_out[l],
                   router_w_grp[l], router_b_grp[l], router_w_exp[l], router_b_exp[l],
                   exp_w1[l], exp_w3[l], exp_w2[l])
    return x
```

```python
import functools

import jax
import jax.numpy as jnp
import numpy as np
from jax import lax
from jax.experimental import pallas as pl
from jax.experimental.pallas import tpu as pltpu
from jax.experimental.pallas import tpu_sc as plsc

F32 = jnp.float32
BF16 = jnp.bfloat16

CHUNK = 64
CHUNK_SHIFT = 6
HEAD_DIM = 64
N_HEADS = 8
W_HEADS = N_HEADS * HEAD_DIM
TOPK_MAX = 256
ROPE_THETA = 10000.0
N_GROUPS = 4
EXPERTS_PER_GROUP = 8
N_EXPERTS = N_GROUPS * EXPERTS_PER_GROUP
EPS = 1e-6
LOG2E = 1.4426950408889634
MASKED = -1e30

LANES = 128
VMEM_LIMIT = 56 * 1024 * 1024


def _cparams(sem):
    return pltpu.CompilerParams(dimension_semantics=sem, vmem_limit_bytes=VMEM_LIMIT)


def _mod_kernel(c_ref, w_ref, b_ref, o_ref):
    c = c_ref[...]
    ca = (c * jax.nn.sigmoid(c)).astype(BF16)
    o_ref[...] = jnp.dot(ca, w_ref[...].astype(BF16), preferred_element_type=F32) + b_ref[...]


def _modulation(c, ada_w, ada_b):
    B, D = c.shape
    n = ada_w.shape[1] // D
    return pl.pallas_call(
        _mod_kernel,
        grid=(n,),
        in_specs=[pl.BlockSpec((B, D), lambda j: (0, 0)),
                  pl.BlockSpec((D, D), lambda j: (0, j)),
                  pl.BlockSpec((1, D), lambda j: (0, j))],
        out_specs=pl.BlockSpec((B, D), lambda j: (0, j)),
        out_shape=jax.ShapeDtypeStruct((B, n * D), F32),
        compiler_params=_cparams(("arbitrary",)),
        name="adaln_mod",
    )(c, ada_w, ada_b.reshape(1, -1))


R_FQ, R_DQ, R_IQ, R_FV = 0, 512, 1024, 1536
R_KK = 2048
R_S2 = 2176
R_END = 2304
HALF = HEAD_DIM // 2


def _inproj_kernel(x_ref, mod_ref, g1_ref, wtok_ref, wt_ref, bd512_ref, cos_ref, sin_ref,
                   gk_ref, gcol_ref, bf_ref, bg_ref,
                   fk_ref, gate_ref, fqt_ref, dqt_ref, iqt_ref, fvt_ref, kkt_ref, dvt_ref, lf_ref, iwt_ref):
    x = x_ref[0]
    ms = jnp.mean(x * x, axis=-1, keepdims=True)
    h = x * lax.rsqrt(ms + EPS) * g1_ref[...]
    h = h * (1.0 + mod_ref[0, 1:2, :]) + mod_ref[0, 0:1, :]
    hb = h.astype(BF16)
    D = x.shape[-1]
    cos, sin = cos_ref[0], sin_ref[0]

    def proj_t(lo, hi):
        return lax.dot_general(wt_ref[lo:hi, :], hb, (((1,), (1,)), ((), ())), preferred_element_type=F32)

    def norm_t(yh, gain):
        msq = jnp.mean(yh * yh, axis=0, keepdims=True)
        return yh * lax.rsqrt(msq + EPS) * gain

    def rope_store(ref, lo, yh):
        x1, x2 = yh[:HALF], yh[HALF:]
        ref[0, lo:lo + HALF, :] = (x1 * cos - x2 * sin).astype(ref.dtype)
        ref[0, lo + HALF:lo + HEAD_DIM, :] = (x2 * cos + x1 * sin).astype(ref.dtype)

    fq = proj_t(R_FQ, R_FQ + W_HEADS)
    dq = proj_t(R_DQ, R_DQ + W_HEADS)
    iq = proj_t(R_IQ, R_IQ + W_HEADS)
    for hh in range(N_HEADS):
        lo = hh * HEAD_DIM
        fqt_ref[0, lo:lo + HEAD_DIM, :] = norm_t(fq[lo:lo + HEAD_DIM], gcol_ref[0]).astype(BF16)
        rope_store(dqt_ref, lo, norm_t(dq[lo:lo + HEAD_DIM], gcol_ref[1]))
        rope_store(iqt_ref, lo, iq[lo:lo + HEAD_DIM])
    fvt_ref[0] = proj_t(R_FV, R_FV + W_HEADS).astype(BF16)

    kk = proj_t(R_KK, R_KK + 2 * HEAD_DIM)
    rope_store(kkt_ref, 0, norm_t(kk[:HEAD_DIM], gcol_ref[2]))
    rope_store(kkt_ref, HEAD_DIM, kk[HEAD_DIM:])

    s2 = proj_t(R_S2, R_S2 + LANES)
    dvt_ref[0] = s2[:HEAD_DIM].astype(BF16)
    z = s2[HEAD_DIM:HEAD_DIM + N_HEADS] + bf_ref[...]
    lf_ref[0] = jnp.minimum(z, 0.0) - jnp.log(1.0 + jnp.exp(-jnp.abs(z)))
    iwt_ref[0] = s2[HEAD_DIM + N_HEADS:HEAD_DIM + 2 * N_HEADS]

    fk = jnp.dot(hb, wtok_ref[:, :W_HEADS], preferred_element_type=F32)
    msq = jnp.dot((fk * fk).astype(BF16), bd512_ref[...], preferred_element_type=F32)
    fk = (fk * lax.rsqrt(msq + EPS) * gk_ref[...]).astype(BF16)
    for hh in range(N_HEADS):
        fk_ref[0, hh] = fk[:, hh * HEAD_DIM:(hh + 1) * HEAD_DIM]
    g = jnp.dot(hb, wtok_ref[:, W_HEADS:], preferred_element_type=F32)
    gate_ref[0] = jax.nn.sigmoid(g + bg_ref[...]).astype(BF16)


def _in_projection(x, mod3, norm1_g, w_tok, w_t, bd512, cos_t, sin_t, gk, gcol, bf, bg, tm):
    B, S, D = x.shape
    tok = lambda w: pl.BlockSpec((1, tm, w), lambda b, i: (b, i, 0))
    feat = lambda r: pl.BlockSpec((1, r, tm), lambda b, i: (b, 0, i))
    const = lambda shape: pl.BlockSpec(shape, lambda b, i: (0,) * len(shape))
    out_shapes = [jax.ShapeDtypeStruct((B, N_HEADS, S, HEAD_DIM), BF16),
                  jax.ShapeDtypeStruct((B, S, 2 * D), BF16)] + \
                 [jax.ShapeDtypeStruct((B, W_HEADS, S), BF16)] * 4 + \
                 [jax.ShapeDtypeStruct((B, 2 * HEAD_DIM, S), BF16),
                  jax.ShapeDtypeStruct((B, HEAD_DIM, S), BF16),
                  jax.ShapeDtypeStruct((B, N_HEADS, S), F32),
                  jax.ShapeDtypeStruct((B, N_HEADS, S), F32)]
    return pl.pallas_call(
        _inproj_kernel,
        grid=(B, S // tm),
        in_specs=[tok(D),
                  pl.BlockSpec((1, 6, D), lambda b, i: (b, 0, 0)),
                  const((1, D)),
                  const(w_tok.shape), const(w_t.shape), const((W_HEADS, W_HEADS)),
                  feat(HALF), feat(HALF),
                  const((1, W_HEADS)), const((3, HEAD_DIM, 1)), const((N_HEADS, 1)), const((1, 2 * D))],
        out_specs=[pl.BlockSpec((1, N_HEADS, tm, HEAD_DIM), lambda b, i: (b, 0, i, 0)), tok(2 * D),
                   feat(W_HEADS), feat(W_HEADS), feat(W_HEADS), feat(W_HEADS),
                   feat(2 * HEAD_DIM), feat(HEAD_DIM), feat(N_HEADS), feat(N_HEADS)],
        out_shape=out_shapes,
        compiler_params=_cparams(("parallel", "parallel")),
        name="in_projection",
    )(x, mod3, norm1_g, w_tok, w_t, bd512, cos_t, sin_t, gk, gcol, bf, bg)


def _cumsum_kernel(x_ref, o_ref):
    x = x_ref[0]
    n = x.shape[-1]
    pos = lax.broadcasted_iota(jnp.int32, x.shape, 1)
    shift = 1
    while shift < n:
        x = x + jnp.where(pos >= shift, pltpu.roll(x, shift, 1), 0.0)
        shift *= 2
    o_ref[0] = x


def _seq_cumsum(logf_t):
    B, H, S = logf_t.shape
    return pl.pallas_call(
        _cumsum_kernel,
        grid=(B,),
        in_specs=[pl.BlockSpec((1, H, S), lambda b: (b, 0, 0))],
        out_specs=pl.BlockSpec((1, H, S), lambda b: (b, 0, 0)),
        out_shape=jax.ShapeDtypeStruct((B, H, S), F32),
        compiler_params=_cparams(("parallel",)),
        name="forget_cumsum",
    )(logf_t)


KC = 256
SUB = 8


def _fold_rows(a, op, ways=1):
    n = a.shape[0] // SUB
    a = a.reshape(n, SUB, a.shape[1])
    chains = [a[w] for w in range(ways)]
    for j in range(ways, n):
        chains[j % ways] = op(chains[j % ways], a[j])
    while len(chains) > 1:
        chains = [op(chains[2 * j], chains[2 * j + 1]) for j in range(len(chains) // 2)]
    return chains[0]


def _softmax_pv(nch, s_ref, acc_ref, vt_at, m_all, o_ref):
    Q = o_ref.shape[1]
    acc_ref[...] = jnp.zeros_like(acc_ref)

    def body(c, lsum):
        off = pl.multiple_of(c * KC, KC)
        new = []
        for hh in range(N_HEADS):
            p = jnp.exp2(s_ref[hh, pl.ds(off, KC), :] - m_all[hh])
            new.append(lsum[hh] + _fold_rows(p, jnp.add))
            acc_ref[hh] += jnp.dot(vt_at(hh, off), p.astype(BF16), preferred_element_type=F32)
        return tuple(new)

    lsum = lax.fori_loop(0, nch, body, tuple(jnp.zeros((SUB, Q), F32) for _ in range(N_HEADS)))
    for hh in range(N_HEADS):
        acc_ref[hh] = acc_ref[hh] / jnp.sum(lsum[hh], axis=0, keepdims=True)
    out_t = acc_ref[...].reshape(N_HEADS * HEAD_DIM, Q)
    o_ref[0] = out_t.T.astype(BF16)


def _fox_kernel(k_ref, f_ref, qt_ref, vt_ref, o_ref, s_ref, acc_ref):
    i = pl.program_id(1)
    Q = o_ref.shape[1]
    qts = [qt_ref[0, hh * HEAD_DIM:(hh + 1) * HEAD_DIM, :] for hh in range(N_HEADS)]

    def scores(c, mx, bias):
        off = pl.multiple_of(c * KC, KC)
        new = []
        for hh in range(N_HEADS):
            s = jnp.dot(k_ref[0, hh, pl.ds(off, KC), :], qts[hh], preferred_element_type=F32)
            s = s - f_ref[0, pl.ds(off, KC), hh:hh + 1]
            if bias is not None:
                s = s + bias
            s_ref[hh, pl.ds(off, KC), :] = s
            new.append(jnp.maximum(mx[hh], _fold_rows(s, jnp.maximum)))
        return tuple(new)

    mx = tuple(jnp.full((SUB, Q), MASKED, F32) for _ in range(N_HEADS))
    mx = lax.fori_loop(0, i, lambda c, m: scores(c, m, None), mx)
    kk = lax.broadcasted_iota(jnp.int32, (KC, Q), 0)
    qq = lax.broadcasted_iota(jnp.int32, (KC, Q), 1)
    mx = scores(i, mx, jnp.where(kk <= qq, 0.0, MASKED))
    m_all = [jnp.max(m, axis=0, keepdims=True) for m in mx]
    _softmax_pv(i + 1, s_ref, acc_ref, lambda hh, off: vt_ref[0, hh * HEAD_DIM:(hh + 1) * HEAD_DIM, pl.ds(off, KC)],
                m_all, o_ref)


def _fox_attention(k_heads, f_tok, qt, vt):
    B, H, S, Dh = k_heads.shape
    return pl.pallas_call(
        _fox_kernel,
        grid=(B, S // KC),
        in_specs=[pl.BlockSpec((1, H, S, Dh), lambda b, i: (b, 0, 0, 0)),
                  pl.BlockSpec((1, S, H), lambda b, i: (b, 0, 0)),
                  pl.BlockSpec((1, W_HEADS, KC), lambda b, i: (b, 0, i)),
                  pl.BlockSpec((1, W_HEADS, S), lambda b, i: (b, 0, 0))],
        out_specs=pl.BlockSpec((1, KC, W_HEADS), lambda b, i: (b, i, 0)),
        out_shape=jax.ShapeDtypeStruct((B, S, W_HEADS), BF16),
        scratch_shapes=[pltpu.VMEM((H, S, KC), F32), pltpu.VMEM((H, HEAD_DIM, KC), F32)],
        compiler_params=_cparams(("parallel", "arbitrary")),
        name="fox_attention",
    )(k_heads, f_tok, qt, vt)


INT_MIN = -(2 ** 31)
KEY_NEG_INF = INT_MIN + 0x7FFFFF
HI16 = -(2 ** 16)
PACK = 16


def _dsa_kernel(kkt_ref, dvt_ref, iqt_ref, dqt_ref, iwt_ref, o_ref, key_ref, hi_ref, s_ref, acc_ref, dk_ref, ik_ref,
                *, topk):
    i = pl.program_id(1)
    Q = o_ref.shape[1]
    nch = i + 1

    @pl.when(i == 0)
    def _():
        def to_rows(c, _):
            off = pl.multiple_of(c * KC, KC)
            rows = kkt_ref[0, :, pl.ds(off, KC)].astype(F32).T
            dk_ref[pl.ds(off, KC), :] = rows[:, :HEAD_DIM].astype(BF16)
            ik_ref[pl.ds(off, KC), :] = rows[:, HEAD_DIM:].astype(BF16)
            return 0
        lax.fori_loop(0, kkt_ref.shape[-1] // KC, to_rows, 0)

    sub_k = lax.broadcasted_iota(jnp.int32, (KC, Q), 0)
    sub_r = lax.broadcasted_iota(jnp.int32, (CHUNK, Q), 0)
    q_chunk = (i * Q + lax.broadcasted_iota(jnp.int32, (CHUNK, Q), 1)) >> CHUNK_SHIFT
    iqts = [iqt_ref[0, hh * HEAD_DIM:(hh + 1) * HEAD_DIM, :] for hh in range(N_HEADS)]
    iws = [iwt_ref[0, hh:hh + 1, :] for hh in range(N_HEADS)]

    def score_chunk(c, _):
        for r in range(KC // CHUNK):
            off = pl.multiple_of(c * KC + r * CHUNK, CHUNK)
            ik = ik_ref[pl.ds(off, CHUNK), :]
            sc = jnp.zeros((CHUNK, Q), F32)
            for hh in range(N_HEADS):
                d = jnp.dot(ik, iqts[hh], preferred_element_type=F32)
                sc = sc + iws[hh] * jnp.maximum(d, 0.0)
            sc = sc + 0.0
            allowed = ((off + sub_r) >> CHUNK_SHIFT) <= q_chunk
            bits = pltpu.bitcast(jnp.where(allowed, sc, -jnp.inf), jnp.int32)
            key = bits ^ ((bits >> 31) & 0x7FFFFFFF)
            key_ref[pl.ds(off, CHUNK), :] = key
            hi_ref[pl.ds(off, CHUNK), :] = (key >> 16).astype(jnp.int16)
        return 0

    lax.fori_loop(0, nch, score_chunk, 0)

    def sweep(n, body, init):
        if isinstance(n, int):
            acc = init
            for c in range(n):
                acc = body(c * KC, acc)
            return acc
        return lax.fori_loop(0, n, lambda c, acc: body(pl.multiple_of(c * KC, KC), acc), init)

    def count(pred, n=nch):
        def body(off, acc):
            hit = pred(key_ref[pl.ds(off, KC), :], off + sub_k)
            return acc + _fold_rows(jnp.where(hit, 1.0, 0.0), jnp.add, ways=4)
        return jnp.sum(sweep(n, body, jnp.zeros((SUB, Q), F32)), axis=0, keepdims=True)

    one, zero = jnp.ones((), BF16), jnp.zeros((), BF16)

    def count_hi(cand, n):
        c16 = jnp.broadcast_to(cand >> 16, (PACK, Q)).astype(jnp.int16)

        def body(off, acc):
            kb = hi_ref[pl.ds(off, KC), :].reshape(KC // PACK, PACK, Q)
            hit = jnp.where(kb >= c16[None], one, zero)
            parts = [hit[w] for w in range(4)]
            for j in range(4, KC // PACK):
                parts[j % 4] = parts[j % 4] + hit[j]
            return acc + ((parts[0] + parts[1]) + (parts[2] + parts[3]))
        acc = sweep(n, body, jnp.zeros((PACK, Q), BF16))
        return jnp.sum(acc.astype(F32), axis=0, keepdims=True)

    assert key_ref.shape[0] // PACK <= 256
    kf = jnp.float32(topk)

    def descent(n):
        n_nonneg = count_hi(jnp.zeros((1, Q), jnp.int32), n)
        top_half = n_nonneg >= kf
        thr = jnp.where(top_half, 0, INT_MIN).astype(jnp.int32)
        n_ge = jnp.where(top_half, n_nonneg, jnp.float32(n * KC))

        def descend(counter, top_bit):
            def step(j, carry):
                thr, n_ge = carry
                cand = thr + (jnp.int32(1) << (top_bit - j))
                cnt = counter(cand)
                take = cnt >= kf
                return jnp.where(take, cand, thr), jnp.where(take, cnt, n_ge)
            return step

        carry = lax.fori_loop(0, 15, descend(lambda c: count_hi(c, n), 30), (thr, n_ge))
        return lax.fori_loop(0, 16, descend(lambda c: count(lambda k, _: k >= c, n), 15), carry)

    thr, n_ge = lax.switch(i, [functools.partial(descent, n) for n in range(1, key_ref.shape[0] // KC + 1)])

    excess = (n_ge > kf) & (thr > KEY_NEG_INF)

    @pl.when(jnp.max(jnp.where(excess, 1.0, 0.0)) > 0.0)
    def _():
        need = kf - count(lambda k, _: k > thr)
        nbits = int(np.ceil(np.log2(key_ref.shape[0]))) + 1

        def bound(j, last):
            cand = last + (jnp.int32(1) << (nbits - 1 - j))
            n = count(lambda k, idx: (k == thr) & (idx < cand))
            return jnp.where(n < need, cand, last)

        last = lax.fori_loop(0, nbits, bound, jnp.zeros((1, Q), jnp.int32))

        def demote(c, _):
            off = pl.multiple_of(c * KC, KC)
            k = key_ref[pl.ds(off, KC), :]
            drop = excess & (k == thr) & (off + sub_k > last)
            key_ref[pl.ds(off, KC), :] = jnp.where(drop, thr - 1, k)
            return 0

        lax.fori_loop(0, nch, demote, 0)

    keep_from = jnp.maximum(thr, KEY_NEG_INF + 1)

    qts = [dqt_ref[0, hh * HEAD_DIM:(hh + 1) * HEAD_DIM, :] for hh in range(N_HEADS)]

    def scores(c, mx):
        off = pl.multiple_of(c * KC, KC)
        bias = jnp.where(key_ref[pl.ds(off, KC), :] >= keep_from, 0.0, MASKED)
        dk = dk_ref[pl.ds(off, KC), :]
        new = []
        for hh in range(N_HEADS):
            s = jnp.dot(dk, qts[hh], preferred_element_type=F32) + bias
            s_ref[hh, pl.ds(off, KC), :] = s
            new.append(jnp.maximum(mx[hh], _fold_rows(s, jnp.maximum)))
        return tuple(new)

    mx = lax.fori_loop(0, nch, scores, tuple(jnp.full((SUB, Q), MASKED, F32) for _ in range(N_HEADS)))
    m_all = [jnp.max(m, axis=0, keepdims=True) for m in mx]
    _softmax_pv(nch, s_ref, acc_ref, lambda hh, off: dvt_ref[0, :, pl.ds(off, KC)], m_all, o_ref)


def _dsa_attention(kkt, dvt, iqt, dqt, iwt, topk):
    B, Dh, S = dvt.shape
    rows = lambda r: pl.BlockSpec((1, r, S), lambda b, i: (b, 0, 0))
    qcols = lambda r: pl.BlockSpec((1, r, KC), lambda b, i: (b, 0, i))
    return pl.pallas_call(
        functools.partial(_dsa_kernel, topk=topk),
        grid=(B, S // KC),
        in_specs=[rows(2 * Dh), rows(Dh), qcols(W_HEADS), qcols(W_HEADS), qcols(N_HEADS)],
        out_specs=pl.BlockSpec((1, KC, W_HEADS), lambda b, i: (b, i, 0)),
        out_shape=jax.ShapeDtypeStruct((B, S, W_HEADS), BF16),
        scratch_shapes=[pltpu.VMEM((S, KC), jnp.int32), pltpu.VMEM((S, KC), jnp.int16),
                        pltpu.VMEM((N_HEADS, S, KC), F32), pltpu.VMEM((N_HEADS, HEAD_DIM, KC), F32),
                        pltpu.VMEM((S, Dh), BF16), pltpu.VMEM((S, Dh), BF16)],
        compiler_params=_cparams(("parallel", "arbitrary")),
        name="dsa_attention",
    )(kkt, dvt, iqt, dqt, iwt)


PACKED = jnp.int32


def _pack_halves(a):
    half = a.shape[-1] // 2
    rounded = a.astype(BF16).astype(F32)
    lo = pltpu.bitcast(rounded[:, :half], jnp.int32)
    hi = pltpu.bitcast(rounded[:, half:], jnp.int32)
    return hi | ((lo >> 16) & 0xFFFF)


def _unpack_halves(p):
    lo = pltpu.bitcast(p << 16, F32)
    hi = pltpu.bitcast(p & HI16, F32)
    return jnp.concatenate([lo, hi], axis=-1)


def _first(mask, lane):
    return jnp.min(jnp.where(mask, lane, LANES), axis=-1, keepdims=True)


def _post_kernel(of_ref, od_ref, gate_ref, x_ref, mod_ref, wpf_ref, wpd_ref, wo_ref, g2_ref, wr_ref, br_ref,
                 x1_ref, h2_ref, route_ref):
    D = x_ref.shape[-1]
    pf = jnp.dot(of_ref[0], wpf_ref[...], preferred_element_type=F32)
    pd = jnp.dot(od_ref[0], wpd_ref[...], preferred_element_type=F32)
    merged = gate_ref[0, :, :D].astype(F32) * pf + gate_ref[0, :, D:].astype(F32) * pd
    y = jnp.dot(merged.astype(BF16), wo_ref[...], preferred_element_type=F32)
    x1 = x_ref[0] + mod_ref[0, 2:3, :] * y
    x1_ref[0] = x1

    ms = jnp.mean(x1 * x1, axis=-1, keepdims=True)
    h2 = x1 * lax.rsqrt(ms + EPS) * g2_ref[...]
    h2 = h2 * (1.0 + mod_ref[0, 4:5, :]) + mod_ref[0, 3:4, :]
    hb = h2.astype(BF16)
    h2_ref[0] = _pack_halves(h2)

    logits = jnp.dot(hb, wr_ref[...], preferred_element_type=F32) + br_ref[...]
    lane = lax.broadcasted_iota(jnp.int32, logits.shape, 1)
    is_grp = lane < N_GROUPS
    gl = jnp.where(is_grp, logits, -jnp.inf)
    gmax = jnp.max(gl, axis=-1, keepdims=True)
    g_idx = _first(gl == gmax, lane)
    g_w = 1.0 / jnp.sum(jnp.exp(gl - gmax), axis=-1, keepdims=True)

    e_lo = N_GROUPS + g_idx * EXPERTS_PER_GROUP
    in_grp = (lane >= e_lo) & (lane < e_lo + EXPERTS_PER_GROUP)
    el = jnp.where(in_grp, logits, -jnp.inf)
    emax = jnp.max(el, axis=-1, keepdims=True)
    ee = jnp.exp(el - emax)
    prob = ee / jnp.sum(ee, axis=-1, keepdims=True)
    prob = jnp.where(in_grp, prob, -1.0)
    p0 = jnp.max(prob, axis=-1, keepdims=True)
    l0 = _first(prob == p0, lane)
    rest = jnp.where(lane == l0, -1.0, prob)
    p1 = jnp.max(rest, axis=-1, keepdims=True)
    l1 = _first(rest == p1, lane)
    psum = p0 + p1
    w0 = g_w * (p0 / psum)
    w1 = g_w * (p1 / psum)
    e0 = (l0 - N_GROUPS).astype(F32)
    e1 = (l1 - N_GROUPS).astype(F32)
    route_ref[0] = jnp.where(lane == 0, e0, jnp.where(lane == 1, e1, jnp.where(lane == 2, w0,
                             jnp.where(lane == 3, w1, 0.0))))


def _post_attention(of, od, gates, x, mod3, wpf, wpd, wo, g2, wr, br, tm):
    B, S, D = x.shape
    tok = lambda w: pl.BlockSpec((1, tm, w), lambda b, i: (b, i, 0))
    const = lambda shape: pl.BlockSpec(shape, lambda b, i: (0,) * len(shape))
    return pl.pallas_call(
        _post_kernel,
        grid=(B, S // tm),
        in_specs=[tok(W_HEADS), tok(W_HEADS), tok(2 * D), tok(D),
                  pl.BlockSpec((1, 6, D), lambda b, i: (b, 0, 0)),
                  const(wpf.shape), const(wpd.shape), const(wo.shape),
                  const((1, D)), const((D, LANES)), const((1, LANES))],
        out_specs=[tok(D), tok(D // 2), tok(LANES)],
        out_shape=[jax.ShapeDtypeStruct((B, S, D), F32),
                   jax.ShapeDtypeStruct((B, S, D // 2), PACKED),
                   jax.ShapeDtypeStruct((B, S, LANES), F32)],
        compiler_params=_cparams(("parallel", "parallel")),
        name="merge_out_router",
    )(of, od, gates, x, mod3, wpf, wpd, wo, g2, wr, br)


def _rank_kernel(route_ref, tri_ref, rank_ref, count_ref, carry_ref):
    @pl.when(pl.program_id(0) == 0)
    def _():
        carry_ref[...] = jnp.zeros_like(carry_ref)

    r = route_ref[...]
    lane = lax.broadcasted_iota(jnp.int32, r.shape, 1).astype(F32)
    hot0 = lane == r[:, 0:1]
    hot1 = lane == r[:, 1:2]
    hits = jnp.where(hot0 | hot1, 1.0, 0.0)
    incl = jnp.dot(tri_ref[...], hits.astype(BF16), preferred_element_type=F32)
    before = incl - hits + carry_ref[...]
    r0 = jnp.sum(jnp.where(hot0, before, 0.0), axis=-1, keepdims=True)
    r1 = jnp.sum(jnp.where(hot1, before, 0.0), axis=-1, keepdims=True)
    rank_ref[...] = jnp.where(lane == 0.0, r0, jnp.where(lane == 1.0, r1, 0.0))
    carry_ref[...] = carry_ref[...] + jnp.sum(hits, axis=0, keepdims=True)
    count_ref[...] = carry_ref[...]


def _expert_ranks(route, tm):
    N = route.shape[0]
    tri = jnp.asarray(np.tril(np.ones((tm, tm), np.float32)), BF16)
    return pl.pallas_call(
        _rank_kernel,
        grid=(N // tm,),
        in_specs=[pl.BlockSpec((tm, LANES), lambda i: (i, 0)),
                  pl.BlockSpec((tm, tm), lambda i: (0, 0))],
        out_specs=[pl.BlockSpec((tm, LANES), lambda i: (i, 0)),
                   pl.BlockSpec((1, LANES), lambda i: (0, 0))],
        out_shape=[jax.ShapeDtypeStruct((N, LANES), F32), jax.ShapeDtypeStruct((1, LANES), F32)],
        scratch_shapes=[pltpu.VMEM((1, LANES), F32)],
        compiler_params=_cparams(("arbitrary",)),
        name="expert_ranks",
    )(route, tri)


def _expert_kernel(te_ref, nt_ref, nv_ref, xs_ref, w1_ref, w3_ref, w2_ref, y_ref, w1b, w3b, w2b):
    g = pl.program_id(0)
    used = g < nt_ref[0]
    new_expert = (g == 0) | (te_ref[g] != te_ref[jnp.maximum(g - 1, 0)])

    @pl.when(used & new_expert)
    def _():
        w1b[...] = w1_ref[0].astype(BF16)
        w3b[...] = w3_ref[0].astype(BF16)
        w2b[...] = w2_ref[0].astype(BF16)

    @pl.when(used)
    def _():
        row = lax.broadcasted_iota(jnp.int32, xs_ref.shape, 0)
        xb = _unpack_halves(jnp.where(row < nv_ref[g], xs_ref[...], 0)).astype(BF16)
        a = jnp.dot(xb, w1b[...], preferred_element_type=F32)
        b = jnp.dot(xb, w3b[...], preferred_element_type=F32)
        hmid = (a * jax.nn.sigmoid(a) * b).astype(BF16)
        y_ref[...] = _pack_halves(jnp.dot(hmid, w2b[...], preferred_element_type=F32))

    @pl.when(jnp.logical_not(used))
    def _():
        y_ref[...] = jnp.zeros_like(y_ref)


def _experts(tile_expert, n_tiles_used, rows_valid, xs, w1, w3, w2, tg):
    P, Dp = xs.shape
    E, D, De = w1.shape
    row_tile = lambda g, te, nt, nv: (jnp.minimum(g, nt[0] - 1), 0)
    grid_spec = pltpu.PrefetchScalarGridSpec(
        num_scalar_prefetch=3,
        grid=(P // tg,),
        in_specs=[pl.BlockSpec((tg, Dp), row_tile),
                  pl.BlockSpec((1, D, De), lambda g, te, nt, nv: (te[g], 0, 0)),
                  pl.BlockSpec((1, D, De), lambda g, te, nt, nv: (te[g], 0, 0)),
                  pl.BlockSpec((1, De, D), lambda g, te, nt, nv: (te[g], 0, 0))],
        out_specs=pl.BlockSpec((tg, Dp), lambda g, te, nt, nv: (g, 0)),
        scratch_shapes=[pltpu.VMEM((D, De), BF16), pltpu.VMEM((D, De), BF16), pltpu.VMEM((De, D), BF16)],
    )
    return pl.pallas_call(
        _expert_kernel,
        grid_spec=grid_spec,
        out_shape=jax.ShapeDtypeStruct((P, Dp), PACKED),
        compiler_params=_cparams(("arbitrary",)),
        name="moe_experts",
    )(tile_expert, n_tiles_used, rows_valid, xs, w1, w3, w2)


SC_CORES = 2
SC_SUBCORES = 16
SC_WINDOW = 128


def _sc_row_gather(table, idx):
    M, = idx.shape
    D = table.shape[1]
    workers = SC_CORES * SC_SUBCORES
    per_worker = M // workers
    assert per_worker * workers == M and per_worker % SC_WINDOW == 0
    mesh = plsc.VectorSubcoreMesh(core_axis_name="c", subcore_axis_name="s",
                                  num_cores=SC_CORES, num_subcores=SC_SUBCORES)

    @functools.partial(
        pl.kernel, mesh=mesh,
        out_type=jax.ShapeDtypeStruct((M, D), table.dtype),
        scratch_types=[pltpu.VMEM((SC_WINDOW,), jnp.int32), pltpu.VMEM((SC_WINDOW, D), table.dtype),
                       pltpu.SemaphoreType.DMA],
        name="sc_row_gather")
    def gather(table_hbm, idx_hbm, out_hbm, idx_v, rows_v, sem):
        wid = lax.axis_index("s") * SC_CORES + lax.axis_index("c")

        def window(j, _):
            base = pl.multiple_of(wid * per_worker + j * SC_WINDOW, SC_WINDOW)
            pltpu.sync_copy(idx_hbm.at[pl.ds(base, SC_WINDOW)], idx_v)
            pltpu.async_copy(table_hbm.at[idx_v], rows_v, sem).wait()
            pltpu.sync_copy(rows_v, out_hbm.at[pl.ds(base, SC_WINDOW)])
            return 0

        lax.fori_loop(0, per_worker // SC_WINDOW, window, 0)

    return gather(table, idx)


def _sc_row_scatter(rows, idx0, idx1, n_out):
    N, D = rows.shape
    workers = SC_CORES * SC_SUBCORES
    per_worker = N // workers
    assert per_worker * workers == N and per_worker % SC_WINDOW == 0
    mesh = plsc.VectorSubcoreMesh(core_axis_name="c", subcore_axis_name="s",
                                  num_cores=SC_CORES, num_subcores=SC_SUBCORES)

    @functools.partial(
        pl.kernel, mesh=mesh,
        out_type=jax.ShapeDtypeStruct((n_out, D), rows.dtype),
        scratch_types=[pltpu.VMEM((SC_WINDOW,), jnp.int32), pltpu.VMEM((SC_WINDOW,), jnp.int32),
                       pltpu.VMEM((SC_WINDOW, D), rows.dtype)],
        name="sc_row_scatter")
    def scatter(rows_hbm, idx0_hbm, idx1_hbm, out_hbm, i0_v, i1_v, rows_v):
        wid = lax.axis_index("s") * SC_CORES + lax.axis_index("c")

        def window(j, _):
            base = pl.multiple_of(wid * per_worker + j * SC_WINDOW, SC_WINDOW)
            pltpu.sync_copy(rows_hbm.at[pl.ds(base, SC_WINDOW)], rows_v)
            pltpu.sync_copy(idx0_hbm.at[pl.ds(base, SC_WINDOW)], i0_v)
            pltpu.sync_copy(idx1_hbm.at[pl.ds(base, SC_WINDOW)], i1_v)
            pltpu.sync_copy(rows_v, out_hbm.at[i0_v])
            pltpu.sync_copy(rows_v, out_hbm.at[i1_v])
            return 0

        lax.fori_loop(0, per_worker // SC_WINDOW, window, 0)

    return scatter(rows, idx0, idx1)


def _combine_rows_kernel(y0_ref, y1_ref, x1_ref, route_ref, gt_ref, o_ref):
    y = _unpack_halves(y0_ref[...]) * route_ref[:, 2:3] + _unpack_halves(y1_ref[...]) * route_ref[:, 3:4]
    o_ref[...] = x1_ref[...] + gt_ref[0] * y


def _combine_rows(yg, x1, route, gt2, tm, S):
    N, D = x1.shape
    per_b = S // tm
    n_blocks = N // tm
    return pl.pallas_call(
        _combine_rows_kernel,
        grid=(n_blocks,),
        in_specs=[pl.BlockSpec((tm, D // 2), lambda i: (i, 0)),
                  pl.BlockSpec((tm, D // 2), lambda i: (i + n_blocks, 0)),
                  pl.BlockSpec((tm, D), lambda i: (i, 0)),
                  pl.BlockSpec((tm, LANES), lambda i: (i, 0)),
                  pl.BlockSpec((1, 1, D), lambda i: (i // per_b, 0, 0))],
        out_specs=pl.BlockSpec((tm, D), lambda i: (i, 0)),
        out_shape=jax.ShapeDtypeStruct((N, D), F32),
        compiler_params=_cparams(("parallel",)),
        name="moe_combine",
    )(yg, yg, x1, route, gt2)


def _rope_tables(positions):
    inv = ROPE_THETA ** (-jnp.arange(HALF, dtype=F32) / HALF)
    ang = positions.astype(F32)[:, None, :] * inv[None, :, None]
    return jnp.cos(ang), jnp.sin(ang)


def _block_diag_mean(width):
    blk = np.kron(np.eye(width // HEAD_DIM, dtype=np.float32), np.full((HEAD_DIM, HEAD_DIM), 1.0 / HEAD_DIM, np.float32))
    return jnp.asarray(blk, BF16)


def _layer(x, c_mod, positions, norm1_g, norm2_g, w_in, b_fgt, b_gate, qn_fox, kn_fox, qn_dsa, kn_dsa,
           w_proj_fox, w_proj_dsa, w_out, r_w_grp, r_b_grp, r_w_exp, r_b_exp, w1, w3, w2):
    B, S, D = x.shape
    N = B * S
    topk = min(TOPK_MAX, S // 4)
    tm = min(512, S)
    scale = HEAD_DIM ** -0.5
    mod3 = c_mod.reshape(B, 6, D)

    o = np.cumsum([0, 512, 512, 512, 8, 512, 64, 64, 512, 64, 8, D, D])
    seg = lambda k: w_in[:, o[k]:o[k + 1]]
    zpad = jnp.zeros((D, LANES - HEAD_DIM - 2 * N_HEADS), F32)
    w_tok = jnp.concatenate([seg(1), seg(10), seg(11)], axis=1).astype(BF16)
    w_t = jnp.concatenate([seg(0), seg(4), seg(7), seg(2),
                           seg(5), seg(8),
                           seg(6), seg(3), seg(9), zpad], axis=1).T.astype(BF16)
    gcol = jnp.stack([qn_fox * (scale * LOG2E), qn_dsa * (scale * LOG2E), kn_dsa]).reshape(3, HEAD_DIM, 1)
    cos_t, sin_t = _rope_tables(positions)

    k_heads, gates, fqt, dqt, iqt, fvt, kkt, dvt, logf_t, iwt = _in_projection(
        x, mod3, norm1_g.reshape(1, D), w_tok, w_t, _block_diag_mean(W_HEADS), cos_t, sin_t,
        jnp.tile(kn_fox, N_HEADS).reshape(1, W_HEADS), gcol, b_fgt.reshape(N_HEADS, 1),
        b_gate.reshape(1, 2 * D), tm)

    f_tok = jnp.transpose(_seq_cumsum(logf_t), (0, 2, 1)) * LOG2E
    of = _fox_attention(k_heads, f_tok, fqt, fvt)
    od = _dsa_attention(kkt, dvt, iqt, dqt, iwt, topk)

    wr = jnp.concatenate([r_w_grp, r_w_exp, jnp.zeros((D, LANES - N_GROUPS - N_EXPERTS), F32)], axis=1).astype(BF16)
    br = jnp.concatenate([r_b_grp, r_b_exp, jnp.zeros((LANES - N_GROUPS - N_EXPERTS,), F32)]).reshape(1, LANES)
    x1, h2, route = _post_attention(of, od, gates, x, mod3, w_proj_fox.astype(BF16), w_proj_dsa.astype(BF16),
                                    w_out.astype(BF16), norm2_g.reshape(1, D), wr, br, tm)
    x1, h2, route = x1.reshape(N, D), h2.reshape(N, D // 2), route.reshape(N, LANES)

    tg = 512 if N * 2 >= 512 * N_EXPERTS else 128
    ranks, counts = _expert_ranks(route, tm)
    counts = counts[0, :N_EXPERTS].astype(jnp.int32)
    padded = ((counts + tg - 1) // tg) * tg
    ends = jnp.cumsum(padded)
    starts = ends - padded
    e01 = jnp.transpose(route[:, :2]).astype(jnp.int32)
    start_of = jnp.sum(jnp.where(e01[..., None] == jnp.arange(N_EXPERTS, dtype=jnp.int32), starts, 0), axis=-1)
    pos = start_of + jnp.transpose(ranks[:, :2]).astype(jnp.int32)
    n_rows = N * 2 + N_EXPERTS * tg
    n_tiles = n_rows // tg
    tile_start = jnp.arange(n_tiles, dtype=jnp.int32) * tg
    tile_expert = jnp.minimum(jnp.sum((ends[None, :] <= tile_start[:, None]).astype(jnp.int32), axis=1),
                              N_EXPERTS - 1)
    n_used = (ends[-1] // tg).astype(jnp.int32).reshape(1)

    real_end = jnp.sum(jnp.where(tile_expert[:, None] == jnp.arange(N_EXPERTS, dtype=jnp.int32),
                                 starts + counts, 0), axis=-1)
    rows_valid = jnp.clip(real_end - tile_start, 0, tg).astype(jnp.int32)

    xs = _sc_row_scatter(h2, pos[0], pos[1], n_rows)
    y = _experts(tile_expert, n_used, rows_valid, xs, w1, w3, w2, tg)
    yg = _sc_row_gather(y, pos.reshape(2 * N))
    out = _combine_rows(yg, x1, route, mod3[:, 5:6, :], min(512, S), S)
    return out.reshape(B, S, D)


def kernel(x, c, positions, ada_w, ada_b, norm1_g, norm2_g, w_in, b_fgt, b_gate, qn_fox, kn_fox, qn_dsa, kn_dsa, w_proj_fox, w_proj_dsa, w_out, router_w_grp, router_b_grp, router_w_exp, router_b_exp, exp_w1, exp_w3, exp_w2):
    for l in range(ada_w.shape[0]):
        c_mod = _modulation(c, ada_w[l], ada_b[l])
        x = _layer(x, c_mod, positions, norm1_g[l], norm2_g[l], w_in[l], b_fgt[l], b_gate[l],
                   qn_fox[l], kn_fox[l], qn_dsa[l], kn_dsa[l], w_proj_fox[l], w_proj_dsa[l], w_out[l],
                   router_w_grp[l], router_b_grp[l], router_w_exp[l], router_b_exp[l],
                   exp_w1[l], exp_w3[l], exp_w2[l])
    return x
```

```python
import functools

import jax
import jax.numpy as jnp
import numpy as np
from jax import lax
from jax.experimental import pallas as pl
from jax.experimental.pallas import tpu as pltpu
from jax.experimental.pallas import tpu_sc as plsc

F32 = jnp.float32
BF16 = jnp.bfloat16

CHUNK = 64
CHUNK_SHIFT = 6
HEAD_DIM = 64
N_HEADS = 8
W_HEADS = N_HEADS * HEAD_DIM
TOPK_MAX = 256
ROPE_THETA = 10000.0
N_GROUPS = 4
EXPERTS_PER_GROUP = 8
N_EXPERTS = N_GROUPS * EXPERTS_PER_GROUP
EPS = 1e-6
LOG2E = 1.4426950408889634
MASKED = -1e30

LANES = 128
VMEM_LIMIT = 56 * 1024 * 1024


def _cparams(sem):
    return pltpu.CompilerParams(dimension_semantics=sem, vmem_limit_bytes=VMEM_LIMIT)


def _mod_kernel(c_ref, w_ref, b_ref, o_ref):
    c = c_ref[...]
    ca = (c * jax.nn.sigmoid(c)).astype(BF16)
    o_ref[...] = jnp.dot(ca, w_ref[...].astype(BF16), preferred_element_type=F32) + b_ref[...]


def _modulation(c, ada_w, ada_b):
    B, D = c.shape
    n = ada_w.shape[1] // D
    return pl.pallas_call(
        _mod_kernel,
        grid=(n,),
        in_specs=[pl.BlockSpec((B, D), lambda j: (0, 0)),
                  pl.BlockSpec((D, D), lambda j: (0, j)),
                  pl.BlockSpec((1, D), lambda j: (0, j))],
        out_specs=pl.BlockSpec((B, D), lambda j: (0, j)),
        out_shape=jax.ShapeDtypeStruct((B, n * D), F32),
        compiler_params=_cparams(("arbitrary",)),
        name="adaln_mod",
    )(c, ada_w, ada_b.reshape(1, -1))


R_FQ, R_DQ, R_IQ, R_FV = 0, 512, 1024, 1536
R_KK = 2048
R_S2 = 2176
R_END = 2304
HALF = HEAD_DIM // 2


def _inproj_kernel(x_ref, mod_ref, g1_ref, wtok_ref, wt_ref, bd512_ref, cos_ref, sin_ref,
                   gk_ref, gcol_ref, bf_ref, bg_ref,
                   fk_ref, gate_ref, fqt_ref, dqt_ref, iqt_ref, fvt_ref, kkt_ref, dvt_ref, lf_ref, iwt_ref):
    x = x_ref[0]
    ms = jnp.mean(x * x, axis=-1, keepdims=True)
    h = x * lax.rsqrt(ms + EPS) * g1_ref[...]
    h = h * (1.0 + mod_ref[0, 1:2, :]) + mod_ref[0, 0:1, :]
    hb = h.astype(BF16)
    D = x.shape[-1]
    cos, sin = cos_ref[0], sin_ref[0]

    def proj_t(lo, hi):
        return lax.dot_general(wt_ref[lo:hi, :], hb, (((1,), (1,)), ((), ())), preferred_element_type=F32)

    def norm_t(yh, gain):
        msq = jnp.mean(yh * yh, axis=0, keepdims=True)
        return yh * lax.rsqrt(msq + EPS) * gain

    def rope_store(ref, lo, yh):
        x1, x2 = yh[:HALF], yh[HALF:]
        ref[0, lo:lo + HALF, :] = (x1 * cos - x2 * sin).astype(ref.dtype)
        ref[0, lo + HALF:lo + HEAD_DIM, :] = (x2 * cos + x1 * sin).astype(ref.dtype)

    fq = proj_t(R_FQ, R_FQ + W_HEADS)
    dq = proj_t(R_DQ, R_DQ + W_HEADS)
    iq = proj_t(R_IQ, R_IQ + W_HEADS)
    for hh in range(N_HEADS):
        lo = hh * HEAD_DIM
        fqt_ref[0, lo:lo + HEAD_DIM, :] = norm_t(fq[lo:lo + HEAD_DIM], gcol_ref[0]).astype(BF16)
        rope_store(dqt_ref, lo, norm_t(dq[lo:lo + HEAD_DIM], gcol_ref[1]))
        rope_store(iqt_ref, lo, iq[lo:lo + HEAD_DIM])
    fvt_ref[0] = proj_t(R_FV, R_FV + W_HEADS).astype(BF16)

    kk = proj_t(R_KK, R_KK + 2 * HEAD_DIM)
    rope_store(kkt_ref, 0, norm_t(kk[:HEAD_DIM], gcol_ref[2]))
    rope_store(kkt_ref, HEAD_DIM, kk[HEAD_DIM:])

    s2 = proj_t(R_S2, R_S2 + LANES)
    dvt_ref[0] = s2[:HEAD_DIM].astype(BF16)
    z = s2[HEAD_DIM:HEAD_DIM + N_HEADS] + bf_ref[...]
    lf_ref[0] = jnp.minimum(z, 0.0) - jnp.log(1.0 + jnp.exp(-jnp.abs(z)))
    iwt_ref[0] = s2[HEAD_DIM + N_HEADS:HEAD_DIM + 2 * N_HEADS]

    fk = jnp.dot(hb, wtok_ref[:, :W_HEADS], preferred_element_type=F32)
    msq = jnp.dot((fk * fk).astype(BF16), bd512_ref[...], preferred_element_type=F32)
    fk = (fk * lax.rsqrt(msq + EPS) * gk_ref[...]).astype(BF16)
    for hh in range(N_HEADS):
        fk_ref[0, hh] = fk[:, hh * HEAD_DIM:(hh + 1) * HEAD_DIM]
    g = jnp.dot(hb, wtok_ref[:, W_HEADS:], preferred_element_type=F32)
    gate_ref[0] = jax.nn.sigmoid(g + bg_ref[...]).astype(BF16)


def _in_projection(x, mod3, norm1_g, w_tok, w_t, bd512, cos_t, sin_t, gk, gcol, bf, bg, tm):
    B, S, D = x.shape
    tok = lambda w: pl.BlockSpec((1, tm, w), lambda b, i: (b, i, 0))
    feat = lambda r: pl.BlockSpec((1, r, tm), lambda b, i: (b, 0, i))
    const = lambda shape: pl.BlockSpec(shape, lambda b, i: (0,) * len(shape))
    out_shapes = [jax.ShapeDtypeStruct((B, N_HEADS, S, HEAD_DIM), BF16),
                  jax.ShapeDtypeStruct((B, S, 2 * D), BF16)] + \
                 [jax.ShapeDtypeStruct((B, W_HEADS, S), BF16)] * 4 + \
                 [jax.ShapeDtypeStruct((B, 2 * HEAD_DIM, S), BF16),
                  jax.ShapeDtypeStruct((B, HEAD_DIM, S), BF16),
                  jax.ShapeDtypeStruct((B, N_HEADS, S), F32),
                  jax.ShapeDtypeStruct((B, N_HEADS, S), F32)]
    return pl.pallas_call(
        _inproj_kernel,
        grid=(B, S // tm),
        in_specs=[tok(D),
                  pl.BlockSpec((1, 6, D), lambda b, i: (b, 0, 0)),
                  const((1, D)),
                  const(w_tok.shape), const(w_t.shape), const((W_HEADS, W_HEADS)),
                  feat(HALF), feat(HALF),
                  const((1, W_HEADS)), const((3, HEAD_DIM, 1)), const((N_HEADS, 1)), const((1, 2 * D))],
        out_specs=[pl.BlockSpec((1, N_HEADS, tm, HEAD_DIM), lambda b, i: (b, 0, i, 0)), tok(2 * D),
                   feat(W_HEADS), feat(W_HEADS), feat(W_HEADS), feat(W_HEADS),
                   feat(2 * HEAD_DIM), feat(HEAD_DIM), feat(N_HEADS), feat(N_HEADS)],
        out_shape=out_shapes,
        compiler_params=_cparams(("parallel", "parallel")),
        name="in_projection",
    )(x, mod3, norm1_g, w_tok, w_t, bd512, cos_t, sin_t, gk, gcol, bf, bg)


def _cumsum_kernel(x_ref, o_ref):
    x = x_ref[0]
    n = x.shape[-1]
    pos = lax.broadcasted_iota(jnp.int32, x.shape, 1)
    shift = 1
    while shift < n:
        x = x + jnp.where(pos >= shift, pltpu.roll(x, shift, 1), 0.0)
        shift *= 2
    o_ref[0] = x


def _seq_cumsum(logf_t):
    B, H, S = logf_t.shape
    return pl.pallas_call(
        _cumsum_kernel,
        grid=(B,),
        in_specs=[pl.BlockSpec((1, H, S), lambda b: (b, 0, 0))],
        out_specs=pl.BlockSpec((1, H, S), lambda b: (b, 0, 0)),
        out_shape=jax.ShapeDtypeStruct((B, H, S), F32),
        compiler_params=_cparams(("parallel",)),
        name="forget_cumsum",
    )(logf_t)


KC = 256
SUB = 8


def _fold_rows(a, op, ways=1):
    n = a.shape[0] // SUB
    a = a.reshape(n, SUB, a.shape[1])
    chains = [a[w] for w in range(ways)]
    for j in range(ways, n):
        chains[j % ways] = op(chains[j % ways], a[j])
    while len(chains) > 1:
        chains = [op(chains[2 * j], chains[2 * j + 1]) for j in range(len(chains) // 2)]
    return chains[0]


def _softmax_pv(nch, s_ref, acc_ref, vt_at, m_all, o_ref):
    Q = o_ref.shape[1]
    acc_ref[...] = jnp.zeros_like(acc_ref)

    def body(c, lsum):
        off = pl.multiple_of(c * KC, KC)
        new = []
        for hh in range(N_HEADS):
            p = jnp.exp2(s_ref[hh, pl.ds(off, KC), :] - m_all[hh])
            new.append(lsum[hh] + _fold_rows(p, jnp.add))
            acc_ref[hh] += jnp.dot(vt_at(hh, off), p.astype(BF16), preferred_element_type=F32)
        return tuple(new)

    lsum = lax.fori_loop(0, nch, body, tuple(jnp.zeros((SUB, Q), F32) for _ in range(N_HEADS)))
    for hh in range(N_HEADS):
        acc_ref[hh] = acc_ref[hh] / jnp.sum(lsum[hh], axis=0, keepdims=True)
    out_t = acc_ref[...].reshape(N_HEADS * HEAD_DIM, Q)
    o_ref[0] = out_t.T.astype(BF16)


def _fox_kernel(k_ref, f_ref, qt_ref, vt_ref, o_ref, s_ref, acc_ref):
    i = pl.program_id(1)
    Q = o_ref.shape[1]
    qts = [qt_ref[0, hh * HEAD_DIM:(hh + 1) * HEAD_DIM, :] for hh in range(N_HEADS)]

    def scores(c, mx, bias):
        off = pl.multiple_of(c * KC, KC)
        new = []
        for hh in range(N_HEADS):
            s = jnp.dot(k_ref[0, hh, pl.ds(off, KC), :], qts[hh], preferred_element_type=F32)
            s = s - f_ref[0, pl.ds(off, KC), hh:hh + 1]
            if bias is not None:
                s = s + bias
            s_ref[hh, pl.ds(off, KC), :] = s
            new.append(jnp.maximum(mx[hh], _fold_rows(s, jnp.maximum)))
        return tuple(new)

    mx = tuple(jnp.full((SUB, Q), MASKED, F32) for _ in range(N_HEADS))
    mx = lax.fori_loop(0, i, lambda c, m: scores(c, m, None), mx)
    kk = lax.broadcasted_iota(jnp.int32, (KC, Q), 0)
    qq = lax.broadcasted_iota(jnp.int32, (KC, Q), 1)
    mx = scores(i, mx, jnp.where(kk <= qq, 0.0, MASKED))
    m_all = [jnp.max(m, axis=0, keepdims=True) for m in mx]
    _softmax_pv(i + 1, s_ref, acc_ref, lambda hh, off: vt_ref[0, hh * HEAD_DIM:(hh + 1) * HEAD_DIM, pl.ds(off, KC)],
                m_all, o_ref)


def _fox_attention(k_heads, f_tok, qt, vt):
    B, H, S, Dh = k_heads.shape
    return pl.pallas_call(
        _fox_kernel,
        grid=(B, S // KC),
        in_specs=[pl.BlockSpec((1, H, S, Dh), lambda b, i: (b, 0, 0, 0)),
                  pl.BlockSpec((1, S, H), lambda b, i: (b, 0, 0)),
                  pl.BlockSpec((1, W_HEADS, KC), lambda b, i: (b, 0, i)),
                  pl.BlockSpec((1, W_HEADS, S), lambda b, i: (b, 0, 0))],
        out_specs=pl.BlockSpec((1, KC, W_HEADS), lambda b, i: (b, i, 0)),
        out_shape=jax.ShapeDtypeStruct((B, S, W_HEADS), BF16),
        scratch_shapes=[pltpu.VMEM((H, S, KC), F32), pltpu.VMEM((H, HEAD_DIM, KC), F32)],
        compiler_params=_cparams(("parallel", "arbitrary")),
        name="fox_attention",
    )(k_heads, f_tok, qt, vt)


INT_MIN = -(2 ** 31)
KEY_NEG_INF = INT_MIN + 0x7FFFFF
HI16 = -(2 ** 16)
PACK = 16


def _dsa_kernel(kkt_ref, dvt_ref, iqt_ref, dqt_ref, iwt_ref, o_ref, key_ref, hi_ref, s_ref, acc_ref, dk_ref, ik_ref,
                *, topk):
    i = pl.program_id(1)
    Q = o_ref.shape[1]
    nch = i + 1

    @pl.when(i == 0)
    def _():
        def to_rows(c, _):
            off = pl.multiple_of(c * KC, KC)
            rows = kkt_ref[0, :, pl.ds(off, KC)].astype(F32).T
            dk_ref[pl.ds(off, KC), :] = rows[:, :HEAD_DIM].astype(BF16)
            ik_ref[pl.ds(off, KC), :] = rows[:, HEAD_DIM:].astype(BF16)
            return 0
        lax.fori_loop(0, kkt_ref.shape[-1] // KC, to_rows, 0)

    sub_k = lax.broadcasted_iota(jnp.int32, (KC, Q), 0)
    sub_r = lax.broadcasted_iota(jnp.int32, (CHUNK, Q), 0)
    q_chunk = (i * Q + lax.broadcasted_iota(jnp.int32, (CHUNK, Q), 1)) >> CHUNK_SHIFT
    iqts = [iqt_ref[0, hh * HEAD_DIM:(hh + 1) * HEAD_DIM, :] for hh in range(N_HEADS)]
    iws = [iwt_ref[0, hh:hh + 1, :] for hh in range(N_HEADS)]

    def score_chunk(c, _):
        for r in range(KC // CHUNK):
            off = pl.multiple_of(c * KC + r * CHUNK, CHUNK)
            ik = ik_ref[pl.ds(off, CHUNK), :]
            sc = jnp.zeros((CHUNK, Q), F32)
            for hh in range(N_HEADS):
                d = jnp.dot(ik, iqts[hh], preferred_element_type=F32)
                sc = sc + iws[hh] * jnp.maximum(d, 0.0)
            sc = sc + 0.0
            allowed = ((off + sub_r) >> CHUNK_SHIFT) <= q_chunk
            bits = pltpu.bitcast(jnp.where(allowed, sc, -jnp.inf), jnp.int32)
            key = bits ^ ((bits >> 31) & 0x7FFFFFFF)
            key_ref[pl.ds(off, CHUNK), :] = key
            hi_ref[pl.ds(off, CHUNK), :] = (key >> 16).astype(jnp.int16)
        return 0

    lax.fori_loop(0, nch, score_chunk, 0)

    def sweep(n, body, init):
        if isinstance(n, int):
            acc = init
            for c in range(n):
                acc = body(c * KC, acc)
            return acc
        return lax.fori_loop(0, n, lambda c, acc: body(pl.multiple_of(c * KC, KC), acc), init)

    def count(pred, n=nch):
        def body(off, acc):
            hit = pred(key_ref[pl.ds(off, KC), :], off + sub_k)
            return acc + _fold_rows(jnp.where(hit, 1.0, 0.0), jnp.add, ways=4)
        return jnp.sum(sweep(n, body, jnp.zeros((SUB, Q), F32)), axis=0, keepdims=True)

    one, zero = jnp.ones((), BF16), jnp.zeros((), BF16)

    def count_hi(cand, n):
        c16 = jnp.broadcast_to(cand >> 16, (PACK, Q)).astype(jnp.int16)

        def body(off, acc):
            kb = hi_ref[pl.ds(off, KC), :].reshape(KC // PACK, PACK, Q)
            hit = jnp.where(kb >= c16[None], one, zero)
            parts = [hit[w] for w in range(4)]
            for j in range(4, KC // PACK):
                parts[j % 4] = parts[j % 4] + hit[j]
            return acc + ((parts[0] + parts[1]) + (parts[2] + parts[3]))
        acc = sweep(n, body, jnp.zeros((PACK, Q), BF16))
        return jnp.sum(acc.astype(F32), axis=0, keepdims=True)

    assert key_ref.shape[0] // PACK <= 256
    kf = jnp.float32(topk)

    def descent(n):
        n_nonneg = count_hi(jnp.zeros((1, Q), jnp.int32), n)
        top_half = n_nonneg >= kf
        thr = jnp.where(top_half, 0, INT_MIN).astype(jnp.int32)
        n_ge = jnp.where(top_half, n_nonneg, jnp.float32(n * KC))

        def descend(counter, top_bit):
            def step(j, carry):
                thr, n_ge = carry
                cand = thr + (jnp.int32(1) << (top_bit - j))
                cnt = counter(cand)
                take = cnt >= kf
                return jnp.where(take, cand, thr), jnp.where(take, cnt, n_ge)
            return step

        carry = lax.fori_loop(0, 15, descend(lambda c: count_hi(c, n), 30), (thr, n_ge))
        return lax.fori_loop(0, 16, descend(lambda c: count(lambda k, _: k >= c, n), 15), carry)

    thr, n_ge = lax.switch(i, [functools.partial(descent, n) for n in range(1, key_ref.shape[0] // KC + 1)])

    excess = (n_ge > kf) & (thr > KEY_NEG_INF)

    @pl.when(jnp.max(jnp.where(excess, 1.0, 0.0)) > 0.0)
    def _():
        need = kf - count(lambda k, _: k > thr)
        nbits = int(np.ceil(np.log2(key_ref.shape[0]))) + 1

        def bound(j, last):
            cand = last + (jnp.int32(1) << (nbits - 1 - j))
            n = count(lambda k, idx: (k == thr) & (idx < cand))
            return jnp.where(n < need, cand, last)

        last = lax.fori_loop(0, nbits, bound, jnp.zeros((1, Q), jnp.int32))

        def demote(c, _):
            off = pl.multiple_of(c * KC, KC)
            k = key_ref[pl.ds(off, KC), :]
            drop = excess & (k == thr) & (off + sub_k > last)
            key_ref[pl.ds(off, KC), :] = jnp.where(drop, thr - 1, k)
            return 0

        lax.fori_loop(0, nch, demote, 0)

    keep_from = jnp.maximum(thr, KEY_NEG_INF + 1)

    qts = [dqt_ref[0, hh * HEAD_DIM:(hh + 1) * HEAD_DIM, :] for hh in range(N_HEADS)]

    def scores(c, mx):
        off = pl.multiple_of(c * KC, KC)
        bias = jnp.where(key_ref[pl.ds(off, KC), :] >= keep_from, 0.0, MASKED)
        dk = dk_ref[pl.ds(off, KC), :]
        new = []
        for hh in range(N_HEADS):
            s = jnp.dot(dk, qts[hh], preferred_element_type=F32) + bias
            s_ref[hh, pl.ds(off, KC), :] = s
            new.append(jnp.maximum(mx[hh], _fold_rows(s, jnp.maximum)))
        return tuple(new)

    mx = lax.fori_loop(0, nch, scores, tuple(jnp.full((SUB, Q), MASKED, F32) for _ in range(N_HEADS)))
    m_all = [jnp.max(m, axis=0, keepdims=True) for m in mx]
    _softmax_pv(nch, s_ref, acc_ref, lambda hh, off: dvt_ref[0, :, pl.ds(off, KC)], m_all, o_ref)


def _dsa_attention(kkt, dvt, iqt, dqt, iwt, topk):
    B, Dh, S = dvt.shape
    rows = lambda r: pl.BlockSpec((1, r, S), lambda b, i: (b, 0, 0))
    qcols = lambda r: pl.BlockSpec((1, r, KC), lambda b, i: (b, 0, i))
    return pl.pallas_call(
        functools.partial(_dsa_kernel, topk=topk),
        grid=(B, S // KC),
        in_specs=[rows(2 * Dh), rows(Dh), qcols(W_HEADS), qcols(W_HEADS), qcols(N_HEADS)],
        out_specs=pl.BlockSpec((1, KC, W_HEADS), lambda b, i: (b, i, 0)),
        out_shape=jax.ShapeDtypeStruct((B, S, W_HEADS), BF16),
        scratch_shapes=[pltpu.VMEM((S, KC), jnp.int32), pltpu.VMEM((S, KC), jnp.int16),
                        pltpu.VMEM((N_HEADS, S, KC), F32), pltpu.VMEM((N_HEADS, HEAD_DIM, KC), F32),
                        pltpu.VMEM((S, Dh), BF16), pltpu.VMEM((S, Dh), BF16)],
        compiler_params=_cparams(("parallel", "arbitrary")),
        name="dsa_attention",
    )(kkt, dvt, iqt, dqt, iwt)


PACKED = jnp.int32


def _pack_halves(a):
    half = a.shape[-1] // 2
    rounded = a.astype(BF16).astype(F32)
    lo = pltpu.bitcast(rounded[:, :half], jnp.int32)
    hi = pltpu.bitcast(rounded[:, half:], jnp.int32)
    return hi | ((lo >> 16) & 0xFFFF)


def _unpack_halves(p):
    lo = pltpu.bitcast(p << 16, F32)
    hi = pltpu.bitcast(p & HI16, F32)
    return jnp.concatenate([lo, hi], axis=-1)


def _first(mask, lane):
    return jnp.min(jnp.where(mask, lane, LANES), axis=-1, keepdims=True)


def _post_kernel(of_ref, od_ref, gate_ref, x_ref, mod_ref, wpf_ref, wpd_ref, wo_ref, g2_ref, wr_ref, br_ref,
                 x1_ref, h2_ref, route_ref):
    D = x_ref.shape[-1]
    pf = jnp.dot(of_ref[0], wpf_ref[...], preferred_element_type=F32)
    pd = jnp.dot(od_ref[0], wpd_ref[...], preferred_element_type=F32)
    merged = gate_ref[0, :, :D].astype(F32) * pf + gate_ref[0, :, D:].astype(F32) * pd
    y = jnp.dot(merged.astype(BF16), wo_ref[...], preferred_element_type=F32)
    x1 = x_ref[0] + mod_ref[0, 2:3, :] * y
    x1_ref[0] = x1

    ms = jnp.mean(x1 * x1, axis=-1, keepdims=True)
    h2 = x1 * lax.rsqrt(ms + EPS) * g2_ref[...]
    h2 = h2 * (1.0 + mod_ref[0, 4:5, :]) + mod_ref[0, 3:4, :]
    hb = h2.astype(BF16)
    h2_ref[0] = _pack_halves(h2)

    logits = jnp.dot(hb, wr_ref[...], preferred_element_type=F32) + br_ref[...]
    lane = lax.broadcasted_iota(jnp.int32, logits.shape, 1)
    is_grp = lane < N_GROUPS
    gl = jnp.where(is_grp, logits, -jnp.inf)
    gmax = jnp.max(gl, axis=-1, keepdims=True)
    g_idx = _first(gl == gmax, lane)
    g_w = 1.0 / jnp.sum(jnp.exp(gl - gmax), axis=-1, keepdims=True)

    e_lo = N_GROUPS + g_idx * EXPERTS_PER_GROUP
    in_grp = (lane >= e_lo) & (lane < e_lo + EXPERTS_PER_GROUP)
    el = jnp.where(in_grp, logits, -jnp.inf)
    emax = jnp.max(el, axis=-1, keepdims=True)
    ee = jnp.exp(el - emax)
    prob = ee / jnp.sum(ee, axis=-1, keepdims=True)
    prob = jnp.where(in_grp, prob, -1.0)
    p0 = jnp.max(prob, axis=-1, keepdims=True)
    l0 = _first(prob == p0, lane)
    rest = jnp.where(lane == l0, -1.0, prob)
    p1 = jnp.max(rest, axis=-1, keepdims=True)
    l1 = _first(rest == p1, lane)
    psum = p0 + p1
    w0 = g_w * (p0 / psum)
    w1 = g_w * (p1 / psum)
    e0 = (l0 - N_GROUPS).astype(F32)
    e1 = (l1 - N_GROUPS).astype(F32)
    route_ref[0] = jnp.where(lane == 0, e0, jnp.where(lane == 1, e1, jnp.where(lane == 2, w0,
                             jnp.where(lane == 3, w1, 0.0))))


def _post_attention(of, od, gates, x, mod3, wpf, wpd, wo, g2, wr, br, tm):
    B, S, D = x.shape
    tok = lambda w: pl.BlockSpec((1, tm, w), lambda b, i: (b, i, 0))
    const = lambda shape: pl.BlockSpec(shape, lambda b, i: (0,) * len(shape))
    return pl.pallas_call(
        _post_kernel,
        grid=(B, S // tm),
        in_specs=[tok(W_HEADS), tok(W_HEADS), tok(2 * D), tok(D),
                  pl.BlockSpec((1, 6, D), lambda b, i: (b, 0, 0)),
                  const(wpf.shape), const(wpd.shape), const(wo.shape),
                  const((1, D)), const((D, LANES)), const((1, LANES))],
        out_specs=[tok(D), tok(D // 2), tok(LANES)],
        out_shape=[jax.ShapeDtypeStruct((B, S, D), F32),
                   jax.ShapeDtypeStruct((B, S, D // 2), PACKED),
                   jax.ShapeDtypeStruct((B, S, LANES), F32)],
        compiler_params=_cparams(("parallel", "parallel")),
        name="merge_out_router",
    )(of, od, gates, x, mod3, wpf, wpd, wo, g2, wr, br)


def _rank_kernel(route_ref, tri_ref, rank_ref, count_ref, carry_ref):
    @pl.when(pl.program_id(0) == 0)
    def _():
        carry_ref[...] = jnp.zeros_like(carry_ref)

    r = route_ref[...]
    lane = lax.broadcasted_iota(jnp.int32, r.shape, 1).astype(F32)
    hot0 = lane == r[:, 0:1]
    hot1 = lane == r[:, 1:2]
    hits = jnp.where(hot0 | hot1, 1.0, 0.0)
    incl = jnp.dot(tri_ref[...], hits.astype(BF16), preferred_element_type=F32)
    before = incl - hits + carry_ref[...]
    r0 = jnp.sum(jnp.where(hot0, before, 0.0), axis=-1, keepdims=True)
    r1 = jnp.sum(jnp.where(hot1, before, 0.0), axis=-1, keepdims=True)
    rank_ref[...] = jnp.where(lane == 0.0, r0, jnp.where(lane == 1.0, r1, 0.0))
    carry_ref[...] = carry_ref[...] + jnp.sum(hits, axis=0, keepdims=True)
    count_ref[...] = carry_ref[...]


def _expert_ranks(route, tm):
    N = route.shape[0]
    tri = jnp.asarray(np.tril(np.ones((tm, tm), np.float32)), BF16)
    return pl.pallas_call(
        _rank_kernel,
        grid=(N // tm,),
        in_specs=[pl.BlockSpec((tm, LANES), lambda i: (i, 0)),
                  pl.BlockSpec((tm, tm), lambda i: (0, 0))],
        out_specs=[pl.BlockSpec((tm, LANES), lambda i: (i, 0)),
                   pl.BlockSpec((1, LANES), lambda i: (0, 0))],
        out_shape=[jax.ShapeDtypeStruct((N, LANES), F32), jax.ShapeDtypeStruct((1, LANES), F32)],
        scratch_shapes=[pltpu.VMEM((1, LANES), F32)],
        compiler_params=_cparams(("arbitrary",)),
        name="expert_ranks",
    )(route, tri)


def _expert_kernel(te_ref, nt_ref, nv_ref, xs_ref, w1_ref, w3_ref, w2_ref, y_ref, w1b, w3b, w2b):
    g = pl.program_id(0)
    used = g < nt_ref[0]
    new_expert = (g == 0) | (te_ref[g] != te_ref[jnp.maximum(g - 1, 0)])

    @pl.when(used & new_expert)
    def _():
        w1b[...] = w1_ref[0].astype(BF16)
        w3b[...] = w3_ref[0].astype(BF16)
        w2b[...] = w2_ref[0].astype(BF16)

    @pl.when(used)
    def _():
        row = lax.broadcasted_iota(jnp.int32, xs_ref.shape, 0)
        xb = _unpack_halves(jnp.where(row < nv_ref[g], xs_ref[...], 0)).astype(BF16)
        a = jnp.dot(xb, w1b[...], preferred_element_type=F32)
        b = jnp.dot(xb, w3b[...], preferred_element_type=F32)
        hmid = (a * jax.nn.sigmoid(a) * b).astype(BF16)
        y_ref[...] = _pack_halves(jnp.dot(hmid, w2b[...], preferred_element_type=F32))

    @pl.when(jnp.logical_not(used))
    def _():
        y_ref[...] = jnp.zeros_like(y_ref)


def _experts(tile_expert, n_tiles_used, rows_valid, xs, w1, w3, w2, tg):
    P, Dp = xs.shape
    E, D, De = w1.shape
    row_tile = lambda g, te, nt, nv: (jnp.minimum(g, nt[0] - 1), 0)
    grid_spec = pltpu.PrefetchScalarGridSpec(
        num_scalar_prefetch=3,
        grid=(P // tg,),
        in_specs=[pl.BlockSpec((tg, Dp), row_tile),
                  pl.BlockSpec((1, D, De), lambda g, te, nt, nv: (te[g], 0, 0)),
                  pl.BlockSpec((1, D, De), lambda g, te, nt, nv: (te[g], 0, 0)),
                  pl.BlockSpec((1, De, D), lambda g, te, nt, nv: (te[g], 0, 0))],
        out_specs=pl.BlockSpec((tg, Dp), lambda g, te, nt, nv: (g, 0)),
        scratch_shapes=[pltpu.VMEM((D, De), BF16), pltpu.VMEM((D, De), BF16), pltpu.VMEM((De, D), BF16)],
    )
    return pl.pallas_call(
        _expert_kernel,
        grid_spec=grid_spec,
        out_shape=jax.ShapeDtypeStruct((P, Dp), PACKED),
        compiler_params=_cparams(("arbitrary",)),
        name="moe_experts",
    )(tile_expert, n_tiles_used, rows_valid, xs, w1, w3, w2)


SC_CORES = 2
SC_SUBCORES = 16
SC_WINDOW = 128


def _sc_row_gather(table, idx):
    M, = idx.shape
    D = table.shape[1]
    workers = SC_CORES * SC_SUBCORES
    per_worker = M // workers
    win = SC_WINDOW // 2
    n_win = per_worker // win
    assert per_worker * workers == M and n_win * win == per_worker and n_win % 2 == 0
    mesh = plsc.VectorSubcoreMesh(core_axis_name="c", subcore_axis_name="s",
                                  num_cores=SC_CORES, num_subcores=SC_SUBCORES)

    @functools.partial(
        pl.kernel, mesh=mesh,
        out_type=jax.ShapeDtypeStruct((M, D), table.dtype),
        scratch_types=[pltpu.VMEM((win,), jnp.int32), pltpu.VMEM((win,), jnp.int32),
                       pltpu.VMEM((win, D), table.dtype), pltpu.VMEM((win, D), table.dtype),
                       pltpu.SemaphoreType.DMA, pltpu.SemaphoreType.DMA],
        name="sc_row_gather")
    def gather(table_hbm, idx_hbm, out_hbm, idx_a, idx_b, rows_a, rows_b, sem_a, sem_b):
        wid = lax.axis_index("s") * SC_CORES + lax.axis_index("c")

        def base_of(j):
            return pl.multiple_of(wid * per_worker + j * win, win)

        def stream(idx_v, rows_v, sem):
            return pltpu.make_async_copy(table_hbm.at[idx_v], rows_v, sem)

        def start(j, idx_v, rows_v, sem):
            pltpu.sync_copy(idx_hbm.at[pl.ds(base_of(j), win)], idx_v)
            stream(idx_v, rows_v, sem).start()

        def finish(j, idx_v, rows_v, sem):
            stream(idx_v, rows_v, sem).wait()
            pltpu.sync_copy(rows_v, out_hbm.at[pl.ds(base_of(j), win)])

        start(0, idx_a, rows_a, sem_a)

        def pair(k, _):
            j = 2 * k
            start(j + 1, idx_b, rows_b, sem_b)
            finish(j, idx_a, rows_a, sem_a)
            pl.when(j + 2 < n_win)(lambda: start(j + 2, idx_a, rows_a, sem_a))
            finish(j + 1, idx_b, rows_b, sem_b)
            return 0

        lax.fori_loop(0, n_win // 2, pair, 0)

    return gather(table, idx)


def _sc_row_scatter(rows, idx0, idx1, n_out):
    N, D = rows.shape
    workers = SC_CORES * SC_SUBCORES
    per_worker = N // workers
    assert per_worker * workers == N and per_worker % SC_WINDOW == 0
    mesh = plsc.VectorSubcoreMesh(core_axis_name="c", subcore_axis_name="s",
                                  num_cores=SC_CORES, num_subcores=SC_SUBCORES)

    @functools.partial(
        pl.kernel, mesh=mesh,
        out_type=jax.ShapeDtypeStruct((n_out, D), rows.dtype),
        scratch_types=[pltpu.VMEM((SC_WINDOW,), jnp.int32), pltpu.VMEM((SC_WINDOW,), jnp.int32),
                       pltpu.VMEM((SC_WINDOW, D), rows.dtype)],
        name="sc_row_scatter")
    def scatter(rows_hbm, idx0_hbm, idx1_hbm, out_hbm, i0_v, i1_v, rows_v):
        wid = lax.axis_index("s") * SC_CORES + lax.axis_index("c")

        def window(j, _):
            base = pl.multiple_of(wid * per_worker + j * SC_WINDOW, SC_WINDOW)
            pltpu.sync_copy(rows_hbm.at[pl.ds(base, SC_WINDOW)], rows_v)
            pltpu.sync_copy(idx0_hbm.at[pl.ds(base, SC_WINDOW)], i0_v)
            pltpu.sync_copy(idx1_hbm.at[pl.ds(base, SC_WINDOW)], i1_v)
            pltpu.sync_copy(rows_v, out_hbm.at[i0_v])
            pltpu.sync_copy(rows_v, out_hbm.at[i1_v])
            return 0

        lax.fori_loop(0, per_worker // SC_WINDOW, window, 0)

    return scatter(rows, idx0, idx1)


def _combine_rows_kernel(y0_ref, y1_ref, x1_ref, route_ref, gt_ref, o_ref):
    y = _unpack_halves(y0_ref[...]) * route_ref[:, 2:3] + _unpack_halves(y1_ref[...]) * route_ref[:, 3:4]
    o_ref[...] = x1_ref[...] + gt_ref[0] * y


def _combine_rows(yg, x1, route, gt2, tm, S):
    N, D = x1.shape
    per_b = S // tm
    n_blocks = N // tm
    return pl.pallas_call(
        _combine_rows_kernel,
        grid=(n_blocks,),
        in_specs=[pl.BlockSpec((tm, D // 2), lambda i: (i, 0)),
                  pl.BlockSpec((tm, D // 2), lambda i: (i + n_blocks, 0)),
                  pl.BlockSpec((tm, D), lambda i: (i, 0)),
                  pl.BlockSpec((tm, LANES), lambda i: (i, 0)),
                  pl.BlockSpec((1, 1, D), lambda i: (i // per_b, 0, 0))],
        out_specs=pl.BlockSpec((tm, D), lambda i: (i, 0)),
        out_shape=jax.ShapeDtypeStruct((N, D), F32),
        compiler_params=_cparams(("parallel",)),
        name="moe_combine",
    )(yg, yg, x1, route, gt2)


def _rope_tables(positions):
    inv = ROPE_THETA ** (-jnp.arange(HALF, dtype=F32) / HALF)
    ang = positions.astype(F32)[:, None, :] * inv[None, :, None]
    return jnp.cos(ang), jnp.sin(ang)


def _block_diag_mean(width):
    blk = np.kron(np.eye(width // HEAD_DIM, dtype=np.float32), np.full((HEAD_DIM, HEAD_DIM), 1.0 / HEAD_DIM, np.float32))
    return jnp.asarray(blk, BF16)


def _layer(x, c_mod, positions, norm1_g, norm2_g, w_in, b_fgt, b_gate, qn_fox, kn_fox, qn_dsa, kn_dsa,
           w_proj_fox, w_proj_dsa, w_out, r_w_grp, r_b_grp, r_w_exp, r_b_exp, w1, w3, w2):
    B, S, D = x.shape
    N = B * S
    topk = min(TOPK_MAX, S // 4)
    tm = min(512, S)
    scale = HEAD_DIM ** -0.5
    mod3 = c_mod.reshape(B, 6, D)

    o = np.cumsum([0, 512, 512, 512, 8, 512, 64, 64, 512, 64, 8, D, D])
    seg = lambda k: w_in[:, o[k]:o[k + 1]]
    zpad = jnp.zeros((D, LANES - HEAD_DIM - 2 * N_HEADS), F32)
    w_tok = jnp.concatenate([seg(1), seg(10), seg(11)], axis=1).astype(BF16)
    w_t = jnp.concatenate([seg(0), seg(4), seg(7), seg(2),
                           seg(5), seg(8),
                           seg(6), seg(3), seg(9), zpad], axis=1).T.astype(BF16)
    gcol = jnp.stack([qn_fox * (scale * LOG2E), qn_dsa * (scale * LOG2E), kn_dsa]).reshape(3, HEAD_DIM, 1)
    cos_t, sin_t = _rope_tables(positions)

    k_heads, gates, fqt, dqt, iqt, fvt, kkt, dvt, logf_t, iwt = _in_projection(
        x, mod3, norm1_g.reshape(1, D), w_tok, w_t, _block_diag_mean(W_HEADS), cos_t, sin_t,
        jnp.tile(kn_fox, N_HEADS).reshape(1, W_HEADS), gcol, b_fgt.reshape(N_HEADS, 1),
        b_gate.reshape(1, 2 * D), tm)

    f_tok = jnp.transpose(_seq_cumsum(logf_t), (0, 2, 1)) * LOG2E
    of = _fox_attention(k_heads, f_tok, fqt, fvt)
    od = _dsa_attention(kkt, dvt, iqt, dqt, iwt, topk)

    wr = jnp.concatenate([r_w_grp, r_w_exp, jnp.zeros((D, LANES - N_GROUPS - N_EXPERTS), F32)], axis=1).astype(BF16)
    br = jnp.concatenate([r_b_grp, r_b_exp, jnp.zeros((LANES - N_GROUPS - N_EXPERTS,), F32)]).reshape(1, LANES)
    x1, h2, route = _post_attention(of, od, gates, x, mod3, w_proj_fox.astype(BF16), w_proj_dsa.astype(BF16),
                                    w_out.astype(BF16), norm2_g.reshape(1, D), wr, br, tm)
    x1, h2, route = x1.reshape(N, D), h2.reshape(N, D // 2), route.reshape(N, LANES)

    tg = 512 if N * 2 >= 512 * N_EXPERTS else 128
    ranks, counts = _expert_ranks(route, tm)
    counts = counts[0, :N_EXPERTS].astype(jnp.int32)
    padded = ((counts + tg - 1) // tg) * tg
    ends = jnp.cumsum(padded)
    starts = ends - padded
    e01 = jnp.transpose(route[:, :2]).astype(jnp.int32)
    start_of = jnp.sum(jnp.where(e01[..., None] == jnp.arange(N_EXPERTS, dtype=jnp.int32), starts, 0), axis=-1)
    pos = start_of + jnp.transpose(ranks[:, :2]).astype(jnp.int32)
    n_rows = N * 2 + N_EXPERTS * tg
    n_tiles = n_rows // tg
    tile_start = jnp.arange(n_tiles, dtype=jnp.int32) * tg
    tile_expert = jnp.minimum(jnp.sum((ends[None, :] <= tile_start[:, None]).astype(jnp.int32), axis=1),
                              N_EXPERTS - 1)
    n_used = (ends[-1] // tg).astype(jnp.int32).reshape(1)

    real_end = jnp.sum(jnp.where(tile_expert[:, None] == jnp.arange(N_EXPERTS, dtype=jnp.int32),
                                 starts + counts, 0), axis=-1)
    rows_valid = jnp.clip(real_end - tile_start, 0, tg).astype(jnp.int32)

    xs = _sc_row_scatter(h2, pos[0], pos[1], n_rows)
    y = _experts(tile_expert, n_used, rows_valid, xs, w1, w3, w2, tg)
    yg = _sc_row_gather(y, pos.reshape(2 * N))
    out = _combine_rows(yg, x1, route, mod3[:, 5:6, :], min(512, S), S)
    return out.reshape(B, S, D)


def kernel(x, c, positions, ada_w, ada_b, norm1_g, norm2_g, w_in, b_fgt, b_gate, qn_fox, kn_fox, qn_dsa, kn_dsa, w_proj_fox, w_proj_dsa, w_out, router_w_grp, router_b_grp, router_w_exp, router_b_exp, exp_w1, exp_w3, exp_w2):
    for l in range(ada_w.shape[0]):
        c_mod = _modulation(c, ada_w[l], ada_b[l])
        x = _layer(x, c_mod, positions, norm1_g[l], norm2_g[l], w_in[l], b_fgt[l], b_gate[l],
                   qn_fox[l], kn_fox[l], qn_dsa[l], kn_dsa[l], w_proj_fox[l], w_proj_dsa[l], w_out[l],
                   router_w_grp[l], router_b_grp[l], router_w_exp[l], router_b_exp[l],
                   exp_w1[l], exp_w3[l], exp_w2[l])
    return x
```
